```python
import math
import jax
import jax.numpy as jnp
from jax import lax
import numpy as np

D_MODEL = 2048
BATCH = 8
SEQ = 4096
DEPTH = 4

HEAD_DIM = 128
DN_HEADS = 6
DN_CONV = 4
DN_CHUNK = 64
POOL_WINDOWS = (2, 4, 8, 16)
POOL_GROUPS = 4
POOL_GROUP_DIM = 128
SWA_HEADS = 6
SWA_KV_HEADS = 2
SWA_WINDOW = 128
SWA_BLOCK = 128
ROPE_THETA = 10000.0
D_FF = 5632
FFN_CONV = 3
NORM_EPS = 1e-6

DN_W = DN_HEADS * HEAD_DIM
POOL_W = POOL_GROUPS * POOL_GROUP_DIM
SWA_W = SWA_HEADS * HEAD_DIM
SWA_KV_W = SWA_KV_HEADS * HEAD_DIM
MIX_W = DN_W + POOL_W + SWA_W
IN_SIZES = (3 * DN_W, DN_W, DN_HEADS, DN_HEADS, POOL_W, SWA_W, SWA_KV_W, SWA_KV_W)
IN_W = int(sum(IN_SIZES))
IN_OFFSETS = tuple(int(v) for v in np.cumsum(IN_SIZES)[:-1])

kernel_name = "hybrid_parallel_heads_deltanet_pool_swa"


def rms_norm(x, w):
    xf = x.astype(jnp.float32)
    y = xf * lax.rsqrt(jnp.mean(xf * xf, axis=-1, keepdims=True) + NORM_EPS)
    return (y * w.astype(jnp.float32)).astype(x.dtype)


def l2_norm(x):
    return x * lax.rsqrt(jnp.sum(x * x, axis=-1, keepdims=True) + NORM_EPS)


def causal_dwconv(x, w):
    K = w.shape[0]
    T = x.shape[1]
    xp = jnp.pad(x, ((0, 0), (K - 1, 0), (0, 0)))
    return sum(xp[:, k:k + T] * w[k] for k in range(K))


def rope(x, cos, sin):
    xf = x.astype(jnp.float32)
    x1, x2 = jnp.split(xf, 2, axis=-1)
    return jnp.concatenate([x1 * cos - x2 * sin, x2 * cos + x1 * sin], axis=-1).astype(x.dtype)


def unit_lower_inverse(lmat):
    C = lmat.shape[-1]
    eye = jnp.eye(C, dtype=jnp.float32)
    nil = -lmat
    inv = eye + nil
    powk = nil
    for _ in range(int(math.log2(C)) - 1):
        powk = jnp.matmul(powk, powk)
        inv = jnp.matmul(inv, eye + powk)
    return inv


def chunk_gated_delta_rule(q, k, v, g, beta):
    B, T, H, D = q.shape
    C = DN_CHUNK
    N = T // C
    q, k, v = (t.reshape(B, N, C, H, D) for t in (q, k, v))
    g = g.reshape(B, N, C, H)
    beta = beta.reshape(B, N, C, H)
    gc = jnp.cumsum(g, axis=2)
    gct = jnp.moveaxis(gc, 2, 3)
    idx = jnp.arange(C)
    causal = idx[:, None] >= idx[None, :]
    strict = idx[:, None] > idx[None, :]
    decay = jnp.exp(jnp.where(causal, gct[..., :, None] - gct[..., None, :], -jnp.inf))
    kb = k * beta[..., None]
    vb = v * beta[..., None]
    lmat = jnp.where(strict, jnp.einsum('bnihd,bnjhd->bnhij', kb, k) * decay, 0.0)
    tinv = unit_lower_inverse(lmat)
    eg = jnp.exp(gc)[..., None]
    u = jnp.einsum('bnhij,bnjhd->bnihd', tinv, vb)
    w = jnp.einsum('bnhij,bnjhd->bnihd', tinv, kb * eg)
    a_intra = jnp.einsum('bnihd,bnjhd->bnhij', q, k) * decay
    g_last = gc[:, :, -1:, :]
    q_dec = q * eg
    k_dec = k * jnp.exp(g_last - gc)[..., None]
    cdec = jnp.exp(g_last[:, :, 0, :])

    def step(S, xs):
        u_n, w_n, q_n, k_n, a_n, c_n = xs
        v_new = u_n - jnp.einsum('bchk,bhkv->bchv', w_n, S)
        o_n = jnp.einsum('bchk,bhkv->bchv', q_n, S) + jnp.einsum('bhij,bjhv->bihv', a_n, v_new)
        S = S * c_n[:, :, None, None] + jnp.einsum('bchk,bchv->bhkv', k_n, v_new)
        return S, o_n

    xs = tuple(jnp.moveaxis(t, 1, 0) for t in (u, w, q_dec, k_dec, a_intra, cdec))
    S0 = jnp.zeros((B, H, D, D), jnp.float32)
    _, o = lax.scan(step, S0, xs)
    return jnp.moveaxis(o, 0, 1).reshape(B, T, H, D)


def gated_deltanet(p_qkv, p_z, p_b, p_a, conv_w, a_log, dt_bias, norm_w):
    B, T, _ = p_qkv.shape
    f32 = jnp.float32
    qkv = jax.nn.silu(causal_dwconv(p_qkv, conv_w)).astype(f32)
    q, k, v = jnp.split(qkv, 3, axis=-1)
    shp = (B, T, DN_HEADS, HEAD_DIM)
    q = l2_norm(q.reshape(shp)) * (HEAD_DIM ** -0.5)
    k = l2_norm(k.reshape(shp))
    v = v.reshape(shp)
    beta = jax.nn.sigmoid(p_b.astype(f32))
    g = -jnp.exp(a_log.astype(f32)) * jax.nn.softplus(p_a.astype(f32) + dt_bias.astype(f32))
    o = chunk_gated_delta_rule(q, k, v, g, beta)
    o = rms_norm(o, norm_w) * jax.nn.silu(p_z.astype(f32).reshape(shp))
    return o.reshape(B, T, DN_W).astype(p_qkv.dtype)


def multiscale_pool(p_pool, pool_w, pool_scale):
    B, T, _ = p_pool.shape
    xg = p_pool.astype(jnp.float32).reshape(B, T, POOL_GROUPS, POOL_GROUP_DIM)
    cs = jnp.cumsum(xg, axis=1)
    t = jnp.arange(T)
    outs = []
    for gi, win in enumerate(POOL_WINDOWS):
        c = cs[:, :, gi]
        prev = jnp.pad(c, ((0, 0), (win, 0), (0, 0)))[:, :T]
        cnt = jnp.minimum(t + 1, win).astype(jnp.float32)
        outs.append((c - prev) / cnt[None, :, None] - xg[:, :, gi])
    y = jnp.stack(outs, axis=2)
    y = jnp.einsum('btgc,gcd->btgd', y, pool_w.astype(jnp.float32))
    return (y.reshape(B, T, POOL_W) * pool_scale.astype(jnp.float32)).astype(p_pool.dtype)


def band_blocks(t, blk):
    B, T = t.shape[:2]
    nb = T // blk
    tp = jnp.pad(t, ((0, 0), (blk, 0), (0, 0), (0, 0))).reshape(B, nb + 1, blk, *t.shape[2:])
    return jnp.concatenate([tp[:, :-1], tp[:, 1:]], axis=2)


def band_mask(nb, blk):
    ii = jnp.arange(blk)[:, None]
    jj = jnp.arange(2 * blk)[None, :]
    diff = blk + ii - jj
    in_win = (diff >= 0) & (diff < SWA_WINDOW)
    blk_idx = jnp.arange(nb)[:, None, None]
    return in_win[None] & ((blk_idx > 0) | (jj >= blk)[None])


def swa_sink_attention(p_q, p_k, p_v, sinks, cos, sin):
    B, T, _ = p_q.shape
    G = SWA_HEADS // SWA_KV_HEADS
    bq = SWA_BLOCK
    nb = T // bq
    q = rope(p_q.reshape(B, T, SWA_HEADS, HEAD_DIM), cos, sin)
    k = rope(p_k.reshape(B, T, SWA_KV_HEADS, HEAD_DIM), cos, sin)
    v = p_v.reshape(B, T, SWA_KV_HEADS, HEAD_DIM)
    qb = q.reshape(B, nb, bq, SWA_KV_HEADS, G, HEAD_DIM)
    kb = band_blocks(k, bq)
    vb = band_blocks(v, bq)
    s = jnp.einsum('bnihgd,bnjhd->bnhgij', qb, kb, preferred_element_type=jnp.float32) * (HEAD_DIM ** -0.5)
    s = jnp.where(band_mask(nb, bq)[None, :, None, None], s, -jnp.inf)
    sink = jnp.broadcast_to(
        sinks.astype(jnp.float32).reshape(SWA_KV_HEADS, G)[None, None, :, :, None, None],
        s.shape[:-1] + (1,))
    p = jax.nn.softmax(jnp.concatenate([s, sink], axis=-1), axis=-1)[..., :-1]
    o = jnp.einsum('bnhgij,bnjhd->bnihgd', p.astype(v.dtype), vb)
    return o.reshape(B, T, SWA_W)


def conv_glu_ffn(h, w_up, conv_w, conv_b, w_down):
    u = causal_dwconv(h @ w_up, conv_w) + conv_b
    a, b = jnp.split(u, 2, axis=-1)
    return (jax.nn.silu(a) * b) @ w_down


def _fwd_setup_inputs(seed: int = 0) -> dict:
    key = jax.random.key(seed)
    ks = jax.random.split(key, 24)
    f32 = jnp.float32

    def nrm(k, shape, scale):
        return jax.random.normal(k, shape, f32) * scale

    def gain(k, shape):
        return 1.0 + 0.02 * jax.random.normal(k, shape, f32)

    x = jax.random.normal(ks[0], (BATCH, SEQ, D_MODEL), f32)
    offs = jax.random.randint(ks[1], (BATCH, 1), 0, 1024)
    positions = (offs + jnp.arange(SEQ)[None, :]).astype(jnp.int32)
    dt = jnp.exp(jax.random.uniform(ks[2], (DEPTH, DN_HEADS), f32, math.log(1e-3), math.log(1e-1)))
    return {
        "x": x,
        "positions": positions,
        "norm_mix_pre": gain(ks[3], (DEPTH, D_MODEL)),
        "w_in": nrm(ks[4], (DEPTH, D_MODEL, IN_W), D_MODEL ** -0.5),
        "dn_conv_w": nrm(ks[5], (DEPTH, DN_CONV, 3 * DN_W), DN_CONV ** -0.5),
        "dn_a_log": jnp.log(jax.random.uniform(ks[6], (DEPTH, DN_HEADS), f32, 1.0, 16.0)),
        "dn_dt_bias": dt + jnp.log(-jnp.expm1(-dt)),
        "dn_norm_w": gain(ks[7], (DEPTH, HEAD_DIM)),
        "pool_w": nrm(ks[8], (DEPTH, POOL_GROUPS, POOL_GROUP_DIM, POOL_GROUP_DIM), POOL_GROUP_DIM ** -0.5),
        "pool_scale": gain(ks[9], (DEPTH, POOL_W)),
        "swa_sinks": nrm(ks[10], (DEPTH, SWA_HEADS), 1.0),
        "w_out": nrm(ks[11], (DEPTH, MIX_W, D_MODEL), MIX_W ** -0.5),
        "norm_mix_post": gain(ks[12], (DEPTH, D_MODEL)),
        "norm_ffn_pre": gain(ks[13], (DEPTH, D_MODEL)),
        "ffn_w_up": nrm(ks[14], (DEPTH, D_MODEL, 2 * D_FF), D_MODEL ** -0.5),
        "ffn_conv_w": nrm(ks[15], (DEPTH, FFN_CONV, 2 * D_FF), FFN_CONV ** -0.5),
        "ffn_conv_b": nrm(ks[16], (DEPTH, 2 * D_FF), 0.01),
        "ffn_w_down": nrm(ks[17], (DEPTH, D_FF, D_MODEL), D_FF ** -0.5),
        "norm_ffn_post": gain(ks[18], (DEPTH, D_MODEL)),
    }


def _fwd_reference(x, positions, norm_mix_pre, w_in, dn_conv_w, dn_a_log, dn_dt_bias, dn_norm_w,
              pool_w, pool_scale, swa_sinks, w_out, norm_mix_post, norm_ffn_pre,
              ffn_w_up, ffn_conv_w, ffn_conv_b, ffn_w_down, norm_ffn_post):
    inv_freq = 1.0 / (ROPE_THETA ** (jnp.arange(0, HEAD_DIM, 2, dtype=jnp.float32) / HEAD_DIM))
    ang = positions.astype(jnp.float32)[..., None] * inv_freq
    cos = jnp.cos(ang)[:, :, None, :]
    sin = jnp.sin(ang)[:, :, None, :]
    for l in range(DEPTH):
        h = rms_norm(x, norm_mix_pre[l])
        p = h @ w_in[l]
        dn_qkv, dn_z, dn_b, dn_a, p_pool, sq, sk, sv = jnp.split(p, IN_OFFSETS, axis=-1)
        y_dn = gated_deltanet(dn_qkv, dn_z, dn_b, dn_a, dn_conv_w[l], dn_a_log[l], dn_dt_bias[l], dn_norm_w[l])
        y_pool = multiscale_pool(p_pool, pool_w[l], pool_scale[l])
        y_swa = swa_sink_attention(sq, sk, sv, swa_sinks[l], cos, sin)
        mix = jnp.concatenate([y_dn, y_pool, y_swa], axis=-1) @ w_out[l]
        x = x + rms_norm(mix, norm_mix_post[l])
        h = rms_norm(x, norm_ffn_pre[l])
        f = conv_glu_ffn(h, ffn_w_up[l], ffn_conv_w[l], ffn_conv_b[l], ffn_w_down[l])
        x = x + rms_norm(f, norm_ffn_post[l])
    return x


import jax as _jax
import jax.numpy as _jnp

TWIN_FORMAT = 'train_step'
FWD_PARAMS = ['x', 'positions', 'norm_mix_pre', 'w_in', 'dn_conv_w', 'dn_a_log', 'dn_dt_bias', 'dn_norm_w', 'pool_w', 'pool_scale', 'swa_sinks', 'w_out', 'norm_mix_post', 'norm_ffn_pre', 'ffn_w_up', 'ffn_conv_w', 'ffn_conv_b', 'ffn_w_down', 'norm_ffn_post']
TWIN_WEIGHTS = ['norm_mix_pre', 'w_in', 'dn_conv_w', 'dn_a_log', 'dn_dt_bias', 'dn_norm_w', 'pool_w', 'pool_scale', 'swa_sinks', 'w_out', 'norm_mix_post', 'norm_ffn_pre', 'ffn_w_up', 'ffn_conv_w', 'ffn_conv_b', 'ffn_w_down', 'norm_ffn_post']
TWIN_DIFF_INPUT = 'x'
TWIN_INPUTS = ['x', 'positions', 'norm_mix_pre', 'w_in', 'dn_conv_w', 'dn_a_log', 'dn_dt_bias', 'dn_norm_w', 'pool_w', 'pool_scale', 'swa_sinks', 'w_out', 'norm_mix_post', 'norm_ffn_pre', 'ffn_w_up', 'ffn_conv_w', 'ffn_conv_b', 'ffn_w_down', 'norm_ffn_post', 'loss_target', 'm_norm_mix_pre', 'm_w_in', 'm_dn_conv_w', 'm_dn_a_log', 'm_dn_dt_bias', 'm_dn_norm_w', 'm_pool_w', 'm_pool_scale', 'm_swa_sinks', 'm_w_out', 'm_norm_mix_post', 'm_norm_ffn_pre', 'm_ffn_w_up', 'm_ffn_conv_w', 'm_ffn_conv_b', 'm_ffn_w_down', 'm_norm_ffn_post', 'v_norm_mix_pre', 'v_w_in', 'v_dn_conv_w', 'v_dn_a_log', 'v_dn_dt_bias', 'v_dn_norm_w', 'v_pool_w', 'v_pool_scale', 'v_swa_sinks', 'v_w_out', 'v_norm_mix_post', 'v_norm_ffn_pre', 'v_ffn_w_up', 'v_ffn_conv_w', 'v_ffn_conv_b', 'v_ffn_w_down', 'v_norm_ffn_post']
TWIN_OUTPUTS = ['loss', 'grad_x', 'grad_norm_mix_pre', 'grad_w_in', 'grad_dn_conv_w', 'grad_dn_a_log', 'grad_dn_dt_bias', 'grad_dn_norm_w', 'grad_pool_w', 'grad_pool_scale', 'grad_swa_sinks', 'grad_w_out', 'grad_norm_mix_post', 'grad_norm_ffn_pre', 'grad_ffn_w_up', 'grad_ffn_conv_w', 'grad_ffn_conv_b', 'grad_ffn_w_down', 'grad_norm_ffn_post', 'delta_norm_mix_pre', 'delta_w_in', 'delta_dn_conv_w', 'delta_dn_a_log', 'delta_dn_dt_bias', 'delta_dn_norm_w', 'delta_pool_w', 'delta_pool_scale', 'delta_swa_sinks', 'delta_w_out', 'delta_norm_mix_post', 'delta_norm_ffn_pre', 'delta_ffn_w_up', 'delta_ffn_conv_w', 'delta_ffn_conv_b', 'delta_ffn_w_down', 'delta_norm_ffn_post', 'new_m_norm_mix_pre', 'new_m_w_in', 'new_m_dn_conv_w', 'new_m_dn_a_log', 'new_m_dn_dt_bias', 'new_m_dn_norm_w', 'new_m_pool_w', 'new_m_pool_scale', 'new_m_swa_sinks', 'new_m_w_out', 'new_m_norm_mix_post', 'new_m_norm_ffn_pre', 'new_m_ffn_w_up', 'new_m_ffn_conv_w', 'new_m_ffn_conv_b', 'new_m_ffn_w_down', 'new_m_norm_ffn_post', 'new_v_norm_mix_pre', 'new_v_w_in', 'new_v_dn_conv_w', 'new_v_dn_a_log', 'new_v_dn_dt_bias', 'new_v_dn_norm_w', 'new_v_pool_w', 'new_v_pool_scale', 'new_v_swa_sinks', 'new_v_w_out', 'new_v_norm_mix_post', 'new_v_norm_ffn_pre', 'new_v_ffn_w_up', 'new_v_ffn_conv_w', 'new_v_ffn_conv_b', 'new_v_ffn_w_down', 'new_v_norm_ffn_post']
TWIN_LEAF_KINDS = {'loss': 'loss', 'grad_x': 'grad_x', 'grad_norm_mix_pre': 'grad_w', 'grad_w_in': 'grad_w', 'grad_dn_conv_w': 'grad_w', 'grad_dn_a_log': 'grad_w', 'grad_dn_dt_bias': 'grad_w', 'grad_dn_norm_w': 'grad_w', 'grad_pool_w': 'grad_w', 'grad_pool_scale': 'grad_w', 'grad_swa_sinks': 'grad_w', 'grad_w_out': 'grad_w', 'grad_norm_mix_post': 'grad_w', 'grad_norm_ffn_pre': 'grad_w', 'grad_ffn_w_up': 'grad_w', 'grad_ffn_conv_w': 'grad_w', 'grad_ffn_conv_b': 'grad_w', 'grad_ffn_w_down': 'grad_w', 'grad_norm_ffn_post': 'grad_w', 'delta_norm_mix_pre': 'delta_w', 'delta_w_in': 'delta_w', 'delta_dn_conv_w': 'delta_w', 'delta_dn_a_log': 'delta_w', 'delta_dn_dt_bias': 'delta_w', 'delta_dn_norm_w': 'delta_w', 'delta_pool_w': 'delta_w', 'delta_pool_scale': 'delta_w', 'delta_swa_sinks': 'delta_w', 'delta_w_out': 'delta_w', 'delta_norm_mix_post': 'delta_w', 'delta_norm_ffn_pre': 'delta_w', 'delta_ffn_w_up': 'delta_w', 'delta_ffn_conv_w': 'delta_w', 'delta_ffn_conv_b': 'delta_w', 'delta_ffn_w_down': 'delta_w', 'delta_norm_ffn_post': 'delta_w', 'new_m_norm_mix_pre': 'new_m', 'new_m_w_in': 'new_m', 'new_m_dn_conv_w': 'new_m', 'new_m_dn_a_log': 'new_m', 'new_m_dn_dt_bias': 'new_m', 'new_m_dn_norm_w': 'new_m', 'new_m_pool_w': 'new_m', 'new_m_pool_scale': 'new_m', 'new_m_swa_sinks': 'new_m', 'new_m_w_out': 'new_m', 'new_m_norm_mix_post': 'new_m', 'new_m_norm_ffn_pre': 'new_m', 'new_m_ffn_w_up': 'new_m', 'new_m_ffn_conv_w': 'new_m', 'new_m_ffn_conv_b': 'new_m', 'new_m_ffn_w_down': 'new_m', 'new_m_norm_ffn_post': 'new_m', 'new_v_norm_mix_pre': 'new_v', 'new_v_w_in': 'new_v', 'new_v_dn_conv_w': 'new_v', 'new_v_dn_a_log': 'new_v', 'new_v_dn_dt_bias': 'new_v', 'new_v_dn_norm_w': 'new_v', 'new_v_pool_w': 'new_v', 'new_v_pool_scale': 'new_v', 'new_v_swa_sinks': 'new_v', 'new_v_w_out': 'new_v', 'new_v_norm_mix_post': 'new_v', 'new_v_norm_ffn_pre': 'new_v', 'new_v_ffn_w_up': 'new_v', 'new_v_ffn_conv_w': 'new_v', 'new_v_ffn_conv_b': 'new_v', 'new_v_ffn_w_down': 'new_v', 'new_v_norm_ffn_post': 'new_v'}


def _forward(args):
    return _fwd_reference(*[args[k] for k in FWD_PARAMS])


def _output_shape():
    def fwd():
        inp = _fwd_setup_inputs(0)
        return _fwd_reference(*[inp[k] for k in FWD_PARAMS])
    out = _jax.eval_shape(fwd)
    return out.shape, out.dtype

N_MICROBATCH = 1
ADAM_LR = 0.001
ADAM_B1 = 0.9
ADAM_B2 = 0.999
ADAM_EPS = 1e-08
ADAM_WD = 0.01
ADAM_STEP = 10
PER_EXAMPLE_BATCH_AXIS = {'x': 0, 'positions': 0, 'loss_target': 0}
SHARED_INPUTS = []
_WEIGHT_DTYPES = {'norm_mix_pre': _jnp.float32, 'w_in': _jnp.float32, 'dn_conv_w': _jnp.float32, 'dn_a_log': _jnp.float32, 'dn_dt_bias': _jnp.float32, 'dn_norm_w': _jnp.float32, 'pool_w': _jnp.float32, 'pool_scale': _jnp.float32, 'swa_sinks': _jnp.float32, 'w_out': _jnp.float32, 'norm_mix_post': _jnp.float32, 'norm_ffn_pre': _jnp.float32, 'ffn_w_up': _jnp.float32, 'ffn_conv_w': _jnp.float32, 'ffn_conv_b': _jnp.float32, 'ffn_w_down': _jnp.float32, 'norm_ffn_post': _jnp.float32}
MOMENT_SCALE = {'norm_mix_pre': 1.054597e+00, 'w_in': 6.824017e-01, 'dn_conv_w': 6.764770e-01, 'dn_a_log': 2.222635e+00, 'dn_dt_bias': 2.192130e+00, 'dn_norm_w': 3.762861e+00, 'pool_w': 1.531795e+00, 'pool_scale': 1.729502e+00, 'swa_sinks': 7.596879e-01, 'w_out': 1.192735e+00, 'norm_mix_post': 1.599943e+01, 'norm_ffn_pre': 7.012520e-01, 'ffn_w_up': 3.132049e-01, 'ffn_conv_w': 3.332263e-01, 'ffn_conv_b': 1.024686e+00, 'ffn_w_down': 5.758864e-01, 'norm_ffn_post': 1.594680e+01}


def _to_microbatches(a, axis):
    t = _jnp.moveaxis(a, axis, 0)
    t = t.reshape((N_MICROBATCH, t.shape[0] // N_MICROBATCH) + t.shape[1:])
    return _jnp.moveaxis(t, 1, axis + 1)


def setup_inputs(seed: int = 0) -> dict:
    inp = _fwd_setup_inputs(seed)
    key = _jax.random.fold_in(_jax.random.key(seed), 7919)
    shape, _ = _output_shape()
    out = dict(inp)
    out["loss_target"] = _jax.random.normal(_jax.random.fold_in(key, 0), shape, _jnp.float32)
    for i, name in enumerate(TWIN_WEIGHTS):
        w = inp[name].astype(_jnp.float32)
        if MOMENT_SCALE is None:
            s = _jnp.sqrt(_jnp.mean(_jnp.square(w)) + 1e-30)
        else:
            s = MOMENT_SCALE[name]
        km, kv = _jax.random.split(_jax.random.fold_in(key, i + 1))
        out[name] = w
        out["m_" + name] = s * _jax.random.normal(km, w.shape, _jnp.float32)
        out["v_" + name] = (s * s) * _jax.random.uniform(kv, w.shape, _jnp.float32, 0.5, 1.5)
    if N_MICROBATCH > 1:
        for name, axis in PER_EXAMPLE_BATCH_AXIS.items():
            out[name] = _to_microbatches(out[name], axis)
    return {'x': out['x'], 'positions': out['positions'], 'norm_mix_pre': out['norm_mix_pre'], 'w_in': out['w_in'], 'dn_conv_w': out['dn_conv_w'], 'dn_a_log': out['dn_a_log'], 'dn_dt_bias': out['dn_dt_bias'], 'dn_norm_w': out['dn_norm_w'], 'pool_w': out['pool_w'], 'pool_scale': out['pool_scale'], 'swa_sinks': out['swa_sinks'], 'w_out': out['w_out'], 'norm_mix_post': out['norm_mix_post'], 'norm_ffn_pre': out['norm_ffn_pre'], 'ffn_w_up': out['ffn_w_up'], 'ffn_conv_w': out['ffn_conv_w'], 'ffn_conv_b': out['ffn_conv_b'], 'ffn_w_down': out['ffn_w_down'], 'norm_ffn_post': out['norm_ffn_post'], 'loss_target': out['loss_target'], 'm_norm_mix_pre': out['m_norm_mix_pre'], 'm_w_in': out['m_w_in'], 'm_dn_conv_w': out['m_dn_conv_w'], 'm_dn_a_log': out['m_dn_a_log'], 'm_dn_dt_bias': out['m_dn_dt_bias'], 'm_dn_norm_w': out['m_dn_norm_w'], 'm_pool_w': out['m_pool_w'], 'm_pool_scale': out['m_pool_scale'], 'm_swa_sinks': out['m_swa_sinks'], 'm_w_out': out['m_w_out'], 'm_norm_mix_post': out['m_norm_mix_post'], 'm_norm_ffn_pre': out['m_norm_ffn_pre'], 'm_ffn_w_up': out['m_ffn_w_up'], 'm_ffn_conv_w': out['m_ffn_conv_w'], 'm_ffn_conv_b': out['m_ffn_conv_b'], 'm_ffn_w_down': out['m_ffn_w_down'], 'm_norm_ffn_post': out['m_norm_ffn_post'], 'v_norm_mix_pre': out['v_norm_mix_pre'], 'v_w_in': out['v_w_in'], 'v_dn_conv_w': out['v_dn_conv_w'], 'v_dn_a_log': out['v_dn_a_log'], 'v_dn_dt_bias': out['v_dn_dt_bias'], 'v_dn_norm_w': out['v_dn_norm_w'], 'v_pool_w': out['v_pool_w'], 'v_pool_scale': out['v_pool_scale'], 'v_swa_sinks': out['v_swa_sinks'], 'v_w_out': out['v_w_out'], 'v_norm_mix_post': out['v_norm_mix_post'], 'v_norm_ffn_pre': out['v_norm_ffn_pre'], 'v_ffn_w_up': out['v_ffn_w_up'], 'v_ffn_conv_w': out['v_ffn_conv_w'], 'v_ffn_conv_b': out['v_ffn_conv_b'], 'v_ffn_w_down': out['v_ffn_w_down'], 'v_norm_ffn_post': out['v_norm_ffn_post']}


def _loss(weights, diff, rest, loss_target):
    with _jax.named_scope("forward"):
        args = {**rest, TWIN_DIFF_INPUT: diff, **{k: w.astype(_WEIGHT_DTYPES[k]) for k, w in weights.items()}}
        y = _forward(args)
    with _jax.named_scope("loss_head"):
        err = _jnp.square(y.astype(_jnp.float32) - loss_target)
        return 0.5 * _jnp.sum(_jnp.mean(err, axis=-1)) if err.ndim else 0.5 * err


def _adamw(w, g, m, v):
    m = ADAM_B1 * m + (1.0 - ADAM_B1) * g
    v = ADAM_B2 * v + (1.0 - ADAM_B2) * _jnp.square(g)
    m_hat = m / (1.0 - ADAM_B1 ** ADAM_STEP)
    v_hat = v / (1.0 - ADAM_B2 ** ADAM_STEP)
    delta = -ADAM_LR * (m_hat / (_jnp.sqrt(v_hat) + ADAM_EPS) + ADAM_WD * w)
    return delta, m, v


def reference(x, positions, norm_mix_pre, w_in, dn_conv_w, dn_a_log, dn_dt_bias, dn_norm_w, pool_w, pool_scale, swa_sinks, w_out, norm_mix_post, norm_ffn_pre, ffn_w_up, ffn_conv_w, ffn_conv_b, ffn_w_down, norm_ffn_post, loss_target, m_norm_mix_pre, m_w_in, m_dn_conv_w, m_dn_a_log, m_dn_dt_bias, m_dn_norm_w, m_pool_w, m_pool_scale, m_swa_sinks, m_w_out, m_norm_mix_post, m_norm_ffn_pre, m_ffn_w_up, m_ffn_conv_w, m_ffn_conv_b, m_ffn_w_down, m_norm_ffn_post, v_norm_mix_pre, v_w_in, v_dn_conv_w, v_dn_a_log, v_dn_dt_bias, v_dn_norm_w, v_pool_w, v_pool_scale, v_swa_sinks, v_w_out, v_norm_mix_post, v_norm_ffn_pre, v_ffn_w_up, v_ffn_conv_w, v_ffn_conv_b, v_ffn_w_down, v_norm_ffn_post):
    given = dict(x=x, positions=positions, norm_mix_pre=norm_mix_pre, w_in=w_in, dn_conv_w=dn_conv_w, dn_a_log=dn_a_log, dn_dt_bias=dn_dt_bias, dn_norm_w=dn_norm_w, pool_w=pool_w, pool_scale=pool_scale, swa_sinks=swa_sinks, w_out=w_out, norm_mix_post=norm_mix_post, norm_ffn_pre=norm_ffn_pre, ffn_w_up=ffn_w_up, ffn_conv_w=ffn_conv_w, ffn_conv_b=ffn_conv_b, ffn_w_down=ffn_w_down, norm_ffn_post=norm_ffn_post, loss_target=loss_target, m_norm_mix_pre=m_norm_mix_pre, m_w_in=m_w_in, m_dn_conv_w=m_dn_conv_w, m_dn_a_log=m_dn_a_log, m_dn_dt_bias=m_dn_dt_bias, m_dn_norm_w=m_dn_norm_w, m_pool_w=m_pool_w, m_pool_scale=m_pool_scale, m_swa_sinks=m_swa_sinks, m_w_out=m_w_out, m_norm_mix_post=m_norm_mix_post, m_norm_ffn_pre=m_norm_ffn_pre, m_ffn_w_up=m_ffn_w_up, m_ffn_conv_w=m_ffn_conv_w, m_ffn_conv_b=m_ffn_conv_b, m_ffn_w_down=m_ffn_w_down, m_norm_ffn_post=m_norm_ffn_post, v_norm_mix_pre=v_norm_mix_pre, v_w_in=v_w_in, v_dn_conv_w=v_dn_conv_w, v_dn_a_log=v_dn_a_log, v_dn_dt_bias=v_dn_dt_bias, v_dn_norm_w=v_dn_norm_w, v_pool_w=v_pool_w, v_pool_scale=v_pool_scale, v_swa_sinks=v_swa_sinks, v_w_out=v_w_out, v_norm_mix_post=v_norm_mix_post, v_norm_ffn_pre=v_norm_ffn_pre, v_ffn_w_up=v_ffn_w_up, v_ffn_conv_w=v_ffn_conv_w, v_ffn_conv_b=v_ffn_conv_b, v_ffn_w_down=v_ffn_w_down, v_norm_ffn_post=v_norm_ffn_post)
    weights = {n: given[n] for n in TWIN_WEIGHTS}
    shared = {n: given[n] for n in SHARED_INPUTS}
    per_example = {n: given[n] for n in ['x', 'positions']}
    grad_fn = _jax.value_and_grad(_loss, argnums=(0, 1))

    def one_microbatch(ex, loss_target):
        ex = dict(ex)
        diff = ex.pop(TWIN_DIFF_INPUT)
        return grad_fn(weights, diff, {**shared, **ex}, loss_target)

    if N_MICROBATCH == 1:
        loss, (grad_w, grad_x) = one_microbatch(per_example, given["loss_target"])
    else:
        def body(carry, xs):
            loss_sum, grad_sum = carry
            l_k, (gw_k, gx_k) = one_microbatch(xs[0], xs[1])
            with _jax.named_scope("update"):
                return (loss_sum + l_k, _jax.tree.map(_jnp.add, grad_sum, gw_k)), gx_k

        init = (_jnp.zeros((), _jnp.float32), _jax.tree.map(_jnp.zeros_like, weights))
        (loss, grad_w), grad_x = _jax.lax.scan(body, init, (per_example, given["loss_target"]))
    with _jax.named_scope("update"):
        delta_w, new_m, new_v = {}, {}, {}
        for n in TWIN_WEIGHTS:
            delta_w[n], new_m[n], new_v[n] = _adamw(weights[n], grad_w[n], given["m_" + n], given["v_" + n])
    return (loss, grad_x, *[grad_w[n] for n in TWIN_WEIGHTS], *[delta_w[n] for n in TWIN_WEIGHTS],
            *[new_m[n] for n in TWIN_WEIGHTS], *[new_v[n] for n in TWIN_WEIGHTS])
```

```python
import functools
import math

import jax
import jax.numpy as jnp
from jax import lax
from jax.experimental import pallas as pl
from jax.experimental.pallas import tpu as pltpu

F32 = jnp.float32
BF16 = jnp.bfloat16
MXU_DT = jnp.bfloat16
HI = lax.Precision.HIGHEST

N_DEV = 8
LANE = 128
SUB = 8
VMEM_LIMIT = 56 * 1024 * 1024
ROW_TILE = 512
NORM_TILE = 256
MM_TM, MM_TN, MM_TK = 512, 1664, 1664

HD = 128
DN_H, DN_W, DN_K, CH = 6, 768, 4, 64
POOL_G = 4
SWA_H, SWA_KV, SWA_G, SWA_BLK = 6, 2, 3, 128
EPS = 1e-6
SCALE = HD ** -0.5
NEG = -1e30

O_QKV, O_Z, O_SQ, O_SK, O_SV, O_POOL, O_GATE, PW = 0, 2304, 3072, 3840, 4096, 4352, 4864, 4992
IN_W = 4876
MIX_W = 2048

ADAM_LR, ADAM_B1, ADAM_B2, ADAM_EPS, ADAM_WD, ADAM_STEP = 0.001, 0.9, 0.999, 1e-08, 0.01, 10


def _pick(n, cap, mult=LANE):
    best = None
    for d in range(mult, min(n, cap) + 1, mult):
        if n % d == 0:
            best = d
    return best if best is not None else n


def _cp(*sem):
    return pltpu.CompilerParams(dimension_semantics=sem, vmem_limit_bytes=VMEM_LIMIT)


def _dot(a, b, dims):
    return lax.dot_general(a.astype(MXU_DT), b.astype(MXU_DT), dims, preferred_element_type=F32)


_NN = (((1,), (0,)), ((), ()))
_NT = (((1,), (1,)), ((), ()))
_TN = (((0,), (0,)), ((), ()))


def _mm(a, b):
    return _dot(a, b, _NN)


def _mm_nt(a, b):
    return _dot(a, b, _NT)


def _mm_tn(a, b):
    return _dot(a, b, _TN)


def _mm_hi(a, b, dims=_NN):
    return lax.dot_general(a, b, dims, precision=HI, preferred_element_type=F32)


def _sigmoid(x):
    return jax.nn.sigmoid(x)


def _softplus(x):
    return jnp.maximum(x, 0.0) + jnp.log(1.0 + jnp.exp(-jnp.abs(x)))


def _align_in(w):
    pad = jnp.zeros(w.shape[:-1] + (PW - IN_W,), w.dtype)
    return jnp.concatenate([w[..., 0:3072], w[..., 3596:4364], w[..., 4364:4620], w[..., 4620:4876],
                            w[..., 3084:3596], w[..., 3072:3084], pad], axis=-1)


def _unalign_in(g):
    return jnp.concatenate([g[..., 0:3072], g[..., O_GATE:O_GATE + 12], g[..., O_POOL:O_POOL + 512],
                            g[..., O_SQ:O_SQ + 768], g[..., O_SK:O_SK + 256], g[..., O_SV:O_SV + 256]], axis=-1)


def _perm_mix_rows(w):
    return jnp.concatenate([w[0:768], w[1280:2048], w[768:1280]], axis=0)


def _unperm_mix_rows(w):
    return jnp.concatenate([w[0:768], w[1536:2048], w[768:1536]], axis=0)


def _mm_call(name, a, b, out_shape, grid, a_spec, b_spec, o_spec, dims, acc_shape):
    nk = grid[2]

    def body(a_ref, b_ref, o_ref, acc_ref):
        k = pl.program_id(2)

        @pl.when(k == 0)
        def _():
            acc_ref[...] = jnp.zeros_like(acc_ref)

        acc_ref[...] += _dot(a_ref[...], b_ref[...], dims)

        @pl.when(k == nk - 1)
        def _():
            o_ref[...] = acc_ref[...].astype(o_ref.dtype)

    return pl.pallas_call(
        body, grid=grid, in_specs=[a_spec, b_spec], out_specs=o_spec, out_shape=out_shape,
        scratch_shapes=[pltpu.VMEM(acc_shape, F32)],
        compiler_params=_cp("parallel", "parallel", "arbitrary"), name=name)(a, b)


def mm_nn(a, b, out_dtype, name):
    (m, k), n = a.shape, b.shape[1]
    tm, tn, tk = _pick(m, MM_TM, SUB), _pick(n, MM_TN), _pick(k, MM_TK)
    return _mm_call(name, a, b, jax.ShapeDtypeStruct((m, n), out_dtype), (m // tm, n // tn, k // tk),
                    pl.BlockSpec((tm, tk), lambda i, j, kk: (i, kk)),
                    pl.BlockSpec((tk, tn), lambda i, j, kk: (kk, j)),
                    pl.BlockSpec((tm, tn), lambda i, j, kk: (i, j)), _NN, (tm, tn))


def mm_nt(a, b, out_dtype, name):
    (m, k), n = a.shape, b.shape[0]
    tm, tn, tk = _pick(m, MM_TM, SUB), _pick(n, MM_TN), _pick(k, MM_TK)
    return _mm_call(name, a, b, jax.ShapeDtypeStruct((m, n), out_dtype), (m // tm, n // tn, k // tk),
                    pl.BlockSpec((tm, tk), lambda i, j, kk: (i, kk)),
                    pl.BlockSpec((tn, tk), lambda i, j, kk: (j, kk)),
                    pl.BlockSpec((tm, tn), lambda i, j, kk: (i, j)), _NT, (tm, tn))


def mm_tn(a, b, out_dtype, name):
    (k, m), n = a.shape, b.shape[1]
    tm, tn, tk = _pick(m, MM_TM), _pick(n, MM_TN), _pick(k, MM_TK, SUB)
    return _mm_call(name, a, b, jax.ShapeDtypeStruct((m, n), out_dtype), (m // tm, n // tn, k // tk),
                    pl.BlockSpec((tk, tm), lambda i, j, kk: (kk, i)),
                    pl.BlockSpec((tk, tn), lambda i, j, kk: (kk, j)),
                    pl.BlockSpec((tm, tn), lambda i, j, kk: (i, j)), _TN, (tm, tn))


def mm_up(h, wblk, name):
    (t, d), (nblk, _, nb) = h.shape, wblk.shape
    tm, tk = _pick(t, MM_TM, SUB), _pick(d, MM_TK)
    hb = nblk // 2
    return _mm_call(name, h, wblk, jax.ShapeDtypeStruct((2, t, hb * nb), F32), (t // tm, nblk, d // tk),
                    pl.BlockSpec((tm, tk), lambda i, j, kk: (i, kk)),
                    pl.BlockSpec((None, tk, nb), lambda i, j, kk: (j, kk, 0)),
                    pl.BlockSpec((None, tm, nb), lambda i, j, kk: (j // hb, i, j % hb)), _NN, (tm, nb))


def mm_up_dgrad(du0, wblk, name):
    (_, t, _), (nblk, d, nb) = du0.shape, wblk.shape
    tm, tn = _pick(t, MM_TM, SUB), _pick(d, MM_TN)
    hb = nblk // 2
    return _mm_call(name, du0, wblk, jax.ShapeDtypeStruct((t, d), F32), (t // tm, d // tn, nblk),
                    pl.BlockSpec((None, tm, nb), lambda i, j, kk: (kk // hb, i, kk % hb)),
                    pl.BlockSpec((None, tn, nb), lambda i, j, kk: (kk, j, 0)),
                    pl.BlockSpec((tm, tn), lambda i, j, kk: (i, j)), _NT, (tm, tn))


def mm_up_wgrad(h, du0, name):
    (t, d), (_, _, f) = h.shape, du0.shape
    nb = f // (N_DEV // 2)
    hb = N_DEV // 2
    tm, tk = _pick(d, MM_TM), _pick(t, MM_TK, SUB)
    return _mm_call(name, h, du0, jax.ShapeDtypeStruct((N_DEV, d, nb), BF16), (d // tm, N_DEV, t // tk),
                    pl.BlockSpec((tk, tm), lambda i, j, kk: (kk, i)),
                    pl.BlockSpec((None, tk, nb), lambda i, j, kk: (j // hb, kk, j % hb)),
                    pl.BlockSpec((None, tm, nb), lambda i, j, kk: (j, i, 0)), _TN, (tm, nb))


def _rms(x, w):
    r = lax.rsqrt(jnp.mean(x * x, axis=-1, keepdims=True) + EPS)
    return x * r * w


def _rms_bwd(dy, x, w):
    r = lax.rsqrt(jnp.mean(x * x, axis=-1, keepdims=True) + EPS)
    xh = x * r
    dxh = dy * w
    dx = r * (dxh - xh * jnp.mean(dxh * xh, axis=-1, keepdims=True))
    return dx, jnp.sum(dy * xh, axis=0, keepdims=True)


def _row_spec(tb, d):
    return pl.BlockSpec((tb, d), lambda i: (i, 0))


def _fix_spec(r, d):
    return pl.BlockSpec((r, d), lambda i: (0, 0))


def norm_first(x, w):
    t, d = x.shape
    tb = _pick(t, NORM_TILE, SUB)

    def body(x_ref, w_ref, h_ref):
        h_ref[...] = _rms(x_ref[...], w_ref[...]).astype(h_ref.dtype)

    return pl.pallas_call(body, grid=(t // tb,), in_specs=[_row_spec(tb, d), _fix_spec(1, d)],
                          out_specs=_row_spec(tb, d), out_shape=jax.ShapeDtypeStruct((t, d), BF16),
                          compiler_params=_cp("parallel"), name="norm_first")(x, w)


def post_pre(x, y, w_post, w_pre):
    t, d = x.shape
    tb = _pick(t, NORM_TILE, SUB)

    def body(x_ref, y_ref, wp_ref, wq_ref, xn_ref, h_ref):
        xn = x_ref[...] + _rms(y_ref[...], wp_ref[...])
        xn_ref[...] = xn
        h_ref[...] = _rms(xn, wq_ref[...]).astype(h_ref.dtype)

    return pl.pallas_call(
        body, grid=(t // tb,),
        in_specs=[_row_spec(tb, d), _row_spec(tb, d), _fix_spec(1, d), _fix_spec(1, d)],
        out_specs=[_row_spec(tb, d), _row_spec(tb, d)],
        out_shape=(jax.ShapeDtypeStruct((t, d), F32), jax.ShapeDtypeStruct((t, d), BF16)),
        compiler_params=_cp("parallel"), name="post_pre")(x, y, w_post, w_pre)


def post_loss(x, y, w_post, target):
    t, d = x.shape
    tb = _pick(t, NORM_TILE, SUB)

    def body(x_ref, y_ref, wp_ref, t_ref, g_ref, l_ref):
        err = x_ref[...] + _rms(y_ref[...], wp_ref[...]) - t_ref[...]
        g_ref[...] = err * (1.0 / d)

        @pl.when(pl.program_id(0) == 0)
        def _():
            l_ref[...] = jnp.zeros_like(l_ref)

        part = 0.5 * jnp.sum(jnp.mean(err * err, axis=-1, keepdims=True), axis=0, keepdims=True)
        l_ref[...] += jnp.broadcast_to(part, l_ref.shape)

    return pl.pallas_call(
        body, grid=(t // tb,),
        in_specs=[_row_spec(tb, d), _row_spec(tb, d), _fix_spec(1, d), _row_spec(tb, d)],
        out_specs=[_row_spec(tb, d), _fix_spec(1, LANE)],
        out_shape=(jax.ShapeDtypeStruct((t, d), F32), jax.ShapeDtypeStruct((1, LANE), F32)),
        compiler_params=_cp("arbitrary"), name="post_loss")(x, y, w_post, target)


def bwd_norms(dx_in, *, pre=None, post=None):
    t, d = dx_in.shape
    tb = _pick(t, NORM_TILE, SUB)
    has_pre, has_post = pre is not None, post is not None

    def body(*refs):
        refs = list(refs)
        dxi = refs.pop(0)
        if has_pre:
            dh, x, wq = refs.pop(0), refs.pop(0), refs.pop(0)
        if has_post:
            y, wp = refs.pop(0), refs.pop(0)
        first = pl.program_id(0) == 0
        dx = dxi[...]
        if has_pre:
            dxo, dwq = refs.pop(0), refs.pop(0)
            g, dw = _rms_bwd(dh[...], x[...], wq[...])
            dx = dx + g
            dxo[...] = dx

            @pl.when(first)
            def _():
                dwq[...] = jnp.zeros_like(dwq)

            dwq[...] += dw
        if has_post:
            dyo, dwp = refs.pop(0), refs.pop(0)
            g, dw = _rms_bwd(dx, y[...], wp[...])
            dyo[...] = g.astype(dyo.dtype)

            @pl.when(first)
            def _():
                dwp[...] = jnp.zeros_like(dwp)

            dwp[...] += dw

    ins, in_specs, outs, out_specs = [dx_in], [_row_spec(tb, d)], [], []
    if has_pre:
        ins += list(pre)
        in_specs += [_row_spec(tb, d), _row_spec(tb, d), _fix_spec(1, d)]
        outs += [jax.ShapeDtypeStruct((t, d), F32), jax.ShapeDtypeStruct((1, d), F32)]
        out_specs += [_row_spec(tb, d), _fix_spec(1, d)]
    if has_post:
        ins += list(post)
        in_specs += [_row_spec(tb, d), _fix_spec(1, d)]
        outs += [jax.ShapeDtypeStruct((t, d), BF16), jax.ShapeDtypeStruct((1, d), F32)]
        out_specs += [_row_spec(tb, d), _fix_spec(1, d)]
    name = "bwd_norms" + ("_pre" if has_pre else "") + ("_post" if has_post else "")
    return pl.pallas_call(body, grid=(t // tb,), in_specs=in_specs, out_specs=out_specs, out_shape=tuple(outs),
                          compiler_params=_cp("arbitrary"), name=name)(*ins)


def _ffn_conv(e_ref, cw, cb, tb):
    return (cw[:, 0:1, :] * e_ref[:, pl.ds(6, tb), :] + cw[:, 1:2, :] * e_ref[:, pl.ds(7, tb), :]
            + cw[:, 2:3, :] * e_ref[:, pl.ds(8, tb), :] + cb)


def _glu_specs(tb, nb, hpb, row_of):
    tile = pl.BlockSpec((2, tb, nb), lambda j, i: (0, row_of(i), j))
    halo = pl.BlockSpec((2, SUB, nb), lambda j, i: (0, jnp.maximum(row_of(i) * hpb - 1, 0), j))
    cw = pl.BlockSpec((2, None, 3, nb), lambda j, i: (0, j, 0, 0))
    cb = pl.BlockSpec((2, None, 1, nb), lambda j, i: (0, j, 0, 0))
    return tile, halo, cw, cb


def glu_fwd(u0, cw, cb):
    _, t, f = u0.shape
    nb = cw.shape[-1]
    tb = _pick(t, ROW_TILE, SUB)
    nt, hpb = t // tb, tb // SUB

    def body(u, h, cwr, cbr, o_ref, e):
        i = pl.program_id(1)
        e[:, 0:SUB, :] = jnp.where(i > 0, h[...], 0.0)
        e[:, SUB:, :] = u[...]
        ab = _ffn_conv(e, cwr[...], cbr[...], tb)
        a, b = ab[0], ab[1]
        o_ref[...] = (a * _sigmoid(a) * b).astype(o_ref.dtype)

    return pl.pallas_call(
        body, grid=(f // nb, nt), in_specs=list(_glu_specs(tb, nb, hpb, lambda i: i)),
        out_specs=pl.BlockSpec((tb, nb), lambda j, i: (i, j)),
        out_shape=jax.ShapeDtypeStruct((t, f), BF16),
        scratch_shapes=[pltpu.VMEM((2, tb + SUB, nb), F32)],
        compiler_params=_cp("parallel", "arbitrary"), name="glu_fwd")(u0, u0, cw, cb)


def glu_bwd(dact, u0, cw, cb):
    _, t, f = u0.shape
    nb = cw.shape[-1]
    tb = _pick(t, ROW_TILE, SUB)
    nt, hpb = t // tb, tb // SUB

    def body(d_ref, u, h, cwr, cbr, du_o, dc_o, e, x2):
        i = pl.program_id(1)
        r = nt - 1 - i
        e[:, 0:SUB, :] = jnp.where(r > 0, h[...], 0.0)
        e[:, SUB:, :] = u[...]
        w = cwr[...]
        ab = _ffn_conv(e, w, cbr[...], tb)
        a, b = ab[0], ab[1]
        sa = _sigmoid(a)
        d = d_ref[...]

        @pl.when(i == 0)
        def _():
            dc_o[...] = jnp.zeros_like(dc_o)
            x2[:, tb:, :] = jnp.zeros((2, SUB, nb), F32)

        x2[0, 0:tb, :] = d * b * (sa * (1.0 + a * (1.0 - sa)))
        x2[1, 0:tb, :] = d * (a * sa)
        du = x2[:, 0:tb, :]
        for k in range(3):
            dc_o[:, k:k + 1, :] += jnp.sum(du * e[:, pl.ds(6 + k, tb), :], axis=1, keepdims=True)
        dc_o[:, 3:4, :] += jnp.sum(du, axis=1, keepdims=True)
        du_o[...] = (w[:, 2:3, :] * du + w[:, 1:2, :] * x2[:, pl.ds(1, tb), :]
                     + w[:, 0:1, :] * x2[:, pl.ds(2, tb), :]).astype(du_o.dtype)
        x2[:, tb:, :] = du[:, 0:SUB, :]

    rev = lambda i: nt - 1 - i
    return pl.pallas_call(
        body, grid=(f // nb, nt),
        in_specs=[pl.BlockSpec((tb, nb), lambda j, i: (rev(i), j))] + list(_glu_specs(tb, nb, hpb, rev)),
        out_specs=[pl.BlockSpec((2, tb, nb), lambda j, i: (0, rev(i), j)),
                   pl.BlockSpec((2, None, SUB, nb), lambda j, i: (0, j, 0, 0))],
        out_shape=(jax.ShapeDtypeStruct((2, t, f), BF16), jax.ShapeDtypeStruct((2, f // nb, SUB, nb), F32)),
        scratch_shapes=[pltpu.VMEM((2, tb + SUB, nb), F32), pltpu.VMEM((2, tb + SUB, nb), F32)],
        compiler_params=_cp("arbitrary", "arbitrary"), name="glu_bwd")(dact, u0, u0, cw, cb)


POOL_HALO = 16
_PCOL = O_POOL // LANE
_CPOOL = 1536 // LANE


def _pool_sel(g, v2, v4, v8, v16):
    return jnp.where(g == 0, v2, jnp.where(g == 1, v4, jnp.where(g == 2, v8, v16)))


def _pool_cnt(g, t0, n):
    win = _pool_sel(g, 2, 4, 8, 16)
    tpos = t0 + lax.broadcasted_iota(jnp.int32, (n, 1), 0)
    return jnp.minimum(tpos + 1, win).astype(F32)


def _pool_core(e, g, t0, tb):
    s2 = e + pltpu.roll(e, 1, 0)
    s4 = s2 + pltpu.roll(s2, 2, 0)
    s8 = s4 + pltpu.roll(s4, 4, 0)
    s16 = s8 + pltpu.roll(s8, 8, 0)
    sw = _pool_sel(g, s2, s4, s8, s16)[POOL_HALO:]
    return sw / _pool_cnt(g, t0, tb) - e[POOL_HALO:]


def pool_fwd(p, pool_w, pool_scale):
    t = p.shape[0]
    tb = _pick(t, ROW_TILE, POOL_HALO)
    nt, hpb = t // tb, tb // POOL_HALO

    def body(x_ref, h_ref, w_ref, s_ref, o_ref):
        i, g = pl.program_id(0), pl.program_id(1)
        e = jnp.concatenate([jnp.where(i > 0, h_ref[...], 0.0), x_ref[...]], axis=0)
        yy = _pool_core(e, g, i * tb, tb)
        o_ref[...] = (_mm(yy, w_ref[...]) * s_ref[...]).astype(o_ref.dtype)

    return pl.pallas_call(
        body, grid=(nt, POOL_G),
        in_specs=[pl.BlockSpec((tb, LANE), lambda i, g: (i, _PCOL + g)),
                  pl.BlockSpec((POOL_HALO, LANE), lambda i, g: (jnp.maximum(i * hpb - 1, 0), _PCOL + g)),
                  pl.BlockSpec((None, LANE, LANE), lambda i, g: (g, 0, 0)),
                  pl.BlockSpec((1, LANE), lambda i, g: (0, g))],
        out_specs=pl.BlockSpec((tb, LANE), lambda i, g: (i, g)),
        out_shape=jax.ShapeDtypeStruct((t, POOL_G * LANE), BF16),
        compiler_params=_cp("parallel", "parallel"), name="pool_fwd")(p, p, pool_w, pool_scale)


def pool_bwd(p, dc, pool_w, pool_scale):
    t = p.shape[0]
    tb = _pick(t, ROW_TILE, POOL_HALO)
    nt, hpb = t // tb, tb // POOL_HALO
    n = tb + POOL_HALO

    def body(x_ref, h_ref, dy_ref, dn_ref, w_ref, s_ref, dx_o, dw_o, ds_o):
        g, i = pl.program_id(0), pl.program_id(1)
        e = jnp.concatenate([jnp.where(i > 0, h_ref[...], 0.0), x_ref[...]], axis=0)
        yy = _pool_core(e, g, i * tb, tb)
        w, sc, dy = w_ref[...], s_ref[...], dy_ref[...]

        @pl.when(i == 0)
        def _():
            dw_o[...] = jnp.zeros_like(dw_o)
            ds_o[...] = jnp.zeros_like(ds_o)

        ds_o[...] += jnp.sum(dy * _mm(yy, w), axis=0, keepdims=True)
        dw_o[...] += _mm_tn(yy, dy * sc)
        dye = jnp.concatenate([dy, jnp.where(i < nt - 1, dn_ref[...], 0.0)], axis=0) * sc
        dyy = _mm_nt(dye, w)
        z = dyy / _pool_cnt(g, i * tb, n)
        r2 = z + pltpu.roll(z, n - 1, 0)
        r4 = r2 + pltpu.roll(r2, n - 2, 0)
        r8 = r4 + pltpu.roll(r4, n - 4, 0)
        r16 = r8 + pltpu.roll(r8, n - 8, 0)
        dx_o[...] = (_pool_sel(g, r2, r4, r8, r16)[:tb] - dyy[:tb]).astype(dx_o.dtype)

    last = t // POOL_HALO - 1
    return pl.pallas_call(
        body, grid=(POOL_G, nt),
        in_specs=[pl.BlockSpec((tb, LANE), lambda g, i: (i, _PCOL + g)),
                  pl.BlockSpec((POOL_HALO, LANE), lambda g, i: (jnp.maximum(i * hpb - 1, 0), _PCOL + g)),
                  pl.BlockSpec((tb, LANE), lambda g, i: (i, _CPOOL + g)),
                  pl.BlockSpec((POOL_HALO, LANE), lambda g, i: (jnp.minimum((i + 1) * hpb, last), _CPOOL + g)),
                  pl.BlockSpec((None, LANE, LANE), lambda g, i: (g, 0, 0)),
                  pl.BlockSpec((1, LANE), lambda g, i: (0, g))],
        out_specs=[pl.BlockSpec((tb, LANE), lambda g, i: (i, g)),
                   pl.BlockSpec((None, LANE, LANE), lambda g, i: (g, 0, 0)),
                   pl.BlockSpec((1, LANE), lambda g, i: (0, g))],
        out_shape=(jax.ShapeDtypeStruct((t, POOL_G * LANE), BF16),
                   jax.ShapeDtypeStruct((POOL_G, LANE, LANE), F32),
                   jax.ShapeDtypeStruct((1, POOL_G * LANE), F32)),
        compiler_params=_cp("arbitrary", "arbitrary"), name="pool_bwd")(p, p, dc, dc, pool_w, pool_scale)


_QCOL, _KCOL, _VCOL = O_SQ // 768, O_SK // 256, O_SV // 256
_GQ = SWA_G * SWA_BLK


def _rope(x, c2, s2):
    return x * c2 + pltpu.roll(x, HD // 2, 1) * s2


def _rope_bwd(d, c2, s2):
    return d * c2 + pltpu.roll(d * s2, HD // 2, 1)


def _hs(x, h):
    return x[:, h * HD:(h + 1) * HD]


def _swa_group(q, kc, kp, vc, vp, c2c, s2c, c2p, s2p, sinks, h, blk):
    kcat = jnp.concatenate([_rope(_hs(kp, h), c2p, s2p), _rope(_hs(kc, h), c2c, s2c)], axis=0)
    vcat = jnp.concatenate([_hs(vp, h), _hs(vc, h)], axis=0)
    qs = jnp.concatenate([_rope(_hs(q, SWA_G * h + g), c2c, s2c) for g in range(SWA_G)], axis=0)
    s = _mm_nt(qs, kcat) * SCALE
    ii = lax.broadcasted_iota(jnp.int32, (_GQ, 2 * SWA_BLK), 0) & (SWA_BLK - 1)
    jj = lax.broadcasted_iota(jnp.int32, (_GQ, 2 * SWA_BLK), 1)
    lo = jnp.where(blk > 0, 0, SWA_BLK)
    s = jnp.where((jj > ii) & (jj <= ii + SWA_BLK) & (jj >= lo), s, NEG)
    sink = jnp.concatenate(
        [jnp.broadcast_to(sinks[:, SWA_G * h + g:SWA_G * h + g + 1], (SWA_BLK, 1)) for g in range(SWA_G)], axis=0)
    m = jnp.maximum(jnp.max(s, axis=1, keepdims=True), sink)
    p = jnp.exp(s - m)
    ps = jnp.exp(sink - m)
    l = jnp.sum(p, axis=1, keepdims=True) + ps
    return qs, kcat, vcat, p, ps, l


def _swa_specs(blk_of):
    cur = lambda w, c: pl.BlockSpec((SWA_BLK, w), lambda n: (blk_of(n), c))
    prev = lambda w, c: pl.BlockSpec((SWA_BLK, w), lambda n: (jnp.maximum(blk_of(n) - 1, 0), c))
    return [cur(768, _QCOL), cur(256, _KCOL), prev(256, _KCOL), cur(256, _VCOL), prev(256, _VCOL),
            cur(HD, 0), cur(HD, 0), prev(HD, 0), prev(HD, 0), pl.BlockSpec((1, LANE), lambda n: (0, 0))]


def swa_fwd(p, cos2, sin2, sinks):
    t = p.shape[0]

    def body(q_ref, kc, kp, vc, vp, c2c, s2c, c2p, s2p, sk_ref, o_ref):
        n = pl.program_id(0)
        for h in range(SWA_KV):
            _, _, vcat, pr, _, l = _swa_group(q_ref[...], kc[...], kp[...], vc[...], vp[...], c2c[...], s2c[...],
                                              c2p[...], s2p[...], sk_ref[...], h, n)
            o = _mm(pr, vcat) / l
            for g in range(SWA_G):
                hh = SWA_G * h + g
                o_ref[:, hh * HD:(hh + 1) * HD] = o[g * SWA_BLK:(g + 1) * SWA_BLK].astype(o_ref.dtype)

    return pl.pallas_call(
        body, grid=(t // SWA_BLK,), in_specs=_swa_specs(lambda n: n),
        out_specs=pl.BlockSpec((SWA_BLK, 768), lambda n: (n, 0)),
        out_shape=jax.ShapeDtypeStruct((t, 768), BF16),
        compiler_params=_cp("parallel"), name="swa_fwd")(p, p, p, p, p, cos2, sin2, cos2, sin2, sinks)


def swa_bwd(p, dc, cos2, sin2, sinks):
    t = p.shape[0]
    nb = t // SWA_BLK

    def body(q_ref, kc, kp, vc, vp, c2c, s2c, c2p, s2p, sk_ref, do_ref, dq_o, dk_o, dv_o, dsk_o, ck, cv):
        i = pl.program_id(0)
        r = nb - 1 - i

        @pl.when(i == 0)
        def _():
            ck[...] = jnp.zeros_like(ck)
            cv[...] = jnp.zeros_like(cv)
            dsk_o[...] = jnp.zeros_like(dsk_o)

        lane = lax.broadcasted_iota(jnp.int32, (1, LANE), 1)
        dsk = jnp.zeros((1, LANE), F32)
        do = do_ref[...]
        for h in range(SWA_KV):
            qs, kcat, vcat, pr, ps, l = _swa_group(q_ref[...], kc[...], kp[...], vc[...], vp[...], c2c[...],
                                                   s2c[...], c2p[...], s2p[...], sk_ref[...], h, r)
            pn = pr / l
            dos = jnp.concatenate([_hs(do, SWA_G * h + g) for g in range(SWA_G)], axis=0)
            dp = _mm_nt(dos, vcat)
            delta = jnp.sum(pn * dp, axis=1, keepdims=True)
            ds = pn * (dp - delta)
            dsr = -(ps / l) * delta
            for g in range(SWA_G):
                tot = jnp.sum(dsr[g * SWA_BLK:(g + 1) * SWA_BLK], axis=0, keepdims=True)
                dsk = dsk + jnp.where(lane == SWA_G * h + g, tot, 0.0)
            dqs = _mm(ds, kcat) * SCALE
            for g in range(SWA_G):
                hh = SWA_G * h + g
                dq_o[:, hh * HD:(hh + 1) * HD] = _rope_bwd(dqs[g * SWA_BLK:(g + 1) * SWA_BLK], c2c[...],
                                                          s2c[...]).astype(dq_o.dtype)
            dk = _mm_tn(ds, qs) * SCALE
            dv = _mm_tn(pn, dos)
            cs = slice(h * HD, (h + 1) * HD)
            dk_o[:, cs] = (_rope_bwd(dk[SWA_BLK:], c2c[...], s2c[...]) + ck[:, cs]).astype(dk_o.dtype)
            dv_o[:, cs] = (dv[SWA_BLK:] + cv[:, cs]).astype(dv_o.dtype)
            ck[:, cs] = _rope_bwd(dk[:SWA_BLK], c2p[...], s2p[...])
            cv[:, cs] = dv[:SWA_BLK]
        dsk_o[...] += dsk

    rev = lambda n: nb - 1 - n
    return pl.pallas_call(
        body, grid=(nb,),
        in_specs=_swa_specs(rev) + [pl.BlockSpec((SWA_BLK, 768), lambda n: (rev(n), 1))],
        out_specs=[pl.BlockSpec((SWA_BLK, 768), lambda n: (rev(n), 0)),
                   pl.BlockSpec((SWA_BLK, 256), lambda n: (rev(n), 0)),
                   pl.BlockSpec((SWA_BLK, 256), lambda n: (rev(n), 0)),
                   pl.BlockSpec((1, LANE), lambda n: (0, 0))],
        out_shape=(jax.ShapeDtypeStruct((t, 768), BF16), jax.ShapeDtypeStruct((t, 256), BF16),
                   jax.ShapeDtypeStruct((t, 256), BF16), jax.ShapeDtypeStruct((1, LANE), F32)),
        scratch_shapes=[pltpu.VMEM((SWA_BLK, 256), F32), pltpu.VMEM((SWA_BLK, 256), F32)],
        compiler_params=_cp("arbitrary"), name="swa_bwd")(p, p, p, p, p, cos2, sin2, cos2, sin2, sinks, dc)


_ZCOL, _GCOL = O_Z // DN_W, O_GATE // LANE
_QKV_W = 3 * DN_W
_INV_STEPS = int(math.log2(CH)) - 1


class _Bag(dict):
    __getattr__ = dict.__getitem__


def _dn_consts():
    ii = lax.broadcasted_iota(jnp.int32, (CH, CH), 0)
    jj = lax.broadcasted_iota(jnp.int32, (CH, CH), 1)
    return _Bag(lower=ii >= jj, strict=ii > jj, diag=ii == jj,
                eye=jnp.where(ii == jj, 1.0, 0.0).astype(F32),
                tril=jnp.where(ii >= jj, 1.0, 0.0).astype(F32),
                ones=jnp.ones((CH, CH), F32), ones_w=jnp.ones((CH, LANE), F32),
                rows=lax.broadcasted_iota(jnp.int32, (CH, 1), 0),
                lane=lax.broadcasted_iota(jnp.int32, (1, LANE), 1))


def _dn_conv(ext_ref, cw):
    return (cw[0:1, :] * ext_ref[pl.ds(5, CH), :] + cw[1:2, :] * ext_ref[pl.ds(6, CH), :]
            + cw[2:3, :] * ext_ref[pl.ds(7, CH), :] + cw[3:4, :] * ext_ref[pl.ds(8, CH), :])


def _dn_gates(gt, arow, drow, c):
    beta = _sigmoid(gt)
    ea = jnp.exp(arow)
    xa = gt + drow
    g = -ea * _softplus(xa)
    return beta, g, _mm_hi(c.tril, g), ea, _sigmoid(xa)


def _dn_head_fwd(qh, kh, vh, beta, gc, s0, c):
    rq = lax.rsqrt(jnp.sum(qh * qh, axis=1, keepdims=True) + EPS)
    rk = lax.rsqrt(jnp.sum(kh * kh, axis=1, keepdims=True) + EPS)
    qn = qh * rq * SCALE
    kn = kh * rk
    kb = kn * beta
    vb = vh * beta
    gcol = _mm_hi(c.ones, jnp.where(c.diag, gc, 0.0))
    gam = jnp.where(c.lower, jnp.exp(jnp.minimum(gc - gcol, 0.0)), 0.0)
    lmat = jnp.where(c.strict, _mm_nt(kb, kn) * gam, 0.0)
    nil = -lmat
    inv = c.eye + nil
    powk = nil
    for _ in range(_INV_STEPS):
        powk = _mm(powk, powk)
        inv = _mm(inv, c.eye + powk)
    eg = jnp.exp(gc)
    kbe = kb * eg
    u = _mm(inv, vb)
    w = _mm(inv, kbe)
    amat = _mm_nt(qn, kn) * gam
    gl = gc[CH - 1:CH, :]
    e2 = jnp.exp(gl - gc)
    cd = jnp.exp(gl)
    qd = qn * eg
    kd = kn * e2
    vnew = u - _mm(w, s0)
    o = _mm(qd, s0) + _mm(amat, vnew)
    s1 = s0 * cd + _mm_tn(kd, vnew)
    return _Bag(rq=rq, rk=rk, qn=qn, kn=kn, kb=kb, vb=vb, gam=gam, lmat=lmat, inv=inv, eg=eg, kbe=kbe, w=w,
                amat=amat, e2=e2, cd=cd, qd=qd, kd=kd, vnew=vnew, o=o, s1=s1)


def _dn_post(o, zh, nw):
    ro = lax.rsqrt(jnp.mean(o * o, axis=1, keepdims=True) + EPS)
    oh = o * ro
    sz = _sigmoid(zh)
    return ro, oh, sz, oh * nw * (zh * sz)


def _dn_head_bwd(f, dy, dsn, s0, qh, vh, zh, beta, nw, c):
    ro, oh, sz, _ = _dn_post(f.o, zh, nw)
    don = dy * (zh * sz)
    dz = dy * (oh * nw) * (sz * (1.0 + zh * (1.0 - sz)))
    dnw = jnp.sum(don * oh, axis=0, keepdims=True)
    doh = don * nw
    do = ro * (doh - oh * jnp.mean(doh * oh, axis=1, keepdims=True))

    dvnew = _mm_tn(f.amat, do) + _mm(f.kd, dsn)
    da = jnp.where(c.lower, _mm_nt(do, f.vnew), 0.0)
    dqd = _mm_nt(do, s0)
    ds0 = _mm_tn(f.qd, do) + f.cd * dsn - _mm_tn(f.w, dvnew)
    dcd = jnp.sum(jnp.sum(s0 * dsn, axis=1, keepdims=True), axis=0, keepdims=True)
    dkd = _mm_nt(f.vnew, dsn)
    dw = -_mm_nt(dvnew, s0)
    dt = _mm_nt(dvnew, f.vb) + _mm_nt(dw, f.kbe)
    dvb = _mm_tn(f.inv, dvnew)
    dkbe = _mm_tn(f.inv, dw)
    dl = -jnp.where(c.strict, _mm_tn(f.inv, _mm_nt(dt, f.inv)), 0.0)
    dm = dl * f.gam
    dn = da * f.gam
    dkb = _mm(dm, f.kn) + dkbe * f.eg
    dkn = _mm_tn(dm, f.kb) + _mm_tn(dn, f.qn) + dkd * f.e2 + beta * dkb
    dqn = _mm(dn, f.kn) + dqd * f.eg
    pm = dl * f.lmat + da * f.amat
    colsum = _mm_hi(pm, c.ones_w, _TN)[:, 0:1]
    tkd = jnp.sum(dkd * f.kn, axis=1, keepdims=True) * f.e2
    dgc = (jnp.sum(pm, axis=1, keepdims=True) - colsum - tkd
           + (jnp.sum(dqd * f.qn, axis=1, keepdims=True) + jnp.sum(dkbe * f.kb, axis=1, keepdims=True)) * f.eg)
    dgl = jnp.sum(tkd, axis=0, keepdims=True) + dcd * f.cd
    dgc = dgc + jnp.where(c.rows == CH - 1, dgl, 0.0)
    dbeta = jnp.sum(dkb * f.kn, axis=1, keepdims=True) + jnp.sum(dvb * vh, axis=1, keepdims=True)
    dvh = beta * dvb
    qhat = qh * f.rq
    dqs = dqn * SCALE
    dqh = f.rq * (dqs - qhat * jnp.sum(qhat * dqs, axis=1, keepdims=True))
    dkh = f.rk * (dkn - f.kn * jnp.sum(f.kn * dkn, axis=1, keepdims=True))
    return dqh, dkh, dvh, dz, dnw, dbeta, dgc, ds0


def _dn_in_specs(chunk_of):
    return [pl.BlockSpec((CH, _QKV_W), lambda n: (chunk_of(n), 0)),
            pl.BlockSpec((SUB, _QKV_W), lambda n: (jnp.maximum(chunk_of(n) * (CH // SUB) - 1, 0), 0)),
            pl.BlockSpec((CH, DN_W), lambda n: (chunk_of(n), _ZCOL)),
            pl.BlockSpec((CH, LANE), lambda n: (chunk_of(n), _GCOL)),
            pl.BlockSpec((DN_K, _QKV_W), lambda n: (0, 0)),
            pl.BlockSpec((SUB, LANE), lambda n: (0, 0))]


def dn_fwd(p, conv_w, par):
    t = p.shape[0]
    nc = t // CH

    def body(x_ref, h_ref, z_ref, g_ref, cw_ref, par_ref, y_o, s_o, ext, st):
        n = pl.program_id(0)
        c = _dn_consts()

        @pl.when(n == 0)
        def _():
            st[...] = jnp.zeros_like(st)

        ext[0:SUB, :] = jnp.where(n > 0, h_ref[...], 0.0)
        ext[SUB:, :] = x_ref[...]
        pre = _dn_conv(ext, cw_ref[...])
        qkv = pre * _sigmoid(pre)
        par = par_ref[...]
        beta_all, _, gc_all, _, _ = _dn_gates(g_ref[...], par[0:1, :], par[1:2, :], c)
        z = z_ref[...]
        for h in range(DN_H):
            s0 = st[h]
            s_o[h] = s0
            f = _dn_head_fwd(_hs(qkv, h), _hs(qkv, DN_H + h), _hs(qkv, 2 * DN_H + h),
                             beta_all[:, h:h + 1], gc_all[:, DN_H + h:DN_H + h + 1], s0, c)
            st[h] = f.s1
            y_o[:, h * HD:(h + 1) * HD] = _dn_post(f.o, _hs(z, h), par[2:3, :])[3].astype(y_o.dtype)

    return pl.pallas_call(
        body, grid=(nc,), in_specs=_dn_in_specs(lambda n: n),
        out_specs=[pl.BlockSpec((CH, DN_W), lambda n: (n, 0)),
                   pl.BlockSpec((None, DN_H, HD, HD), lambda n: (n, 0, 0, 0))],
        out_shape=(jax.ShapeDtypeStruct((t, DN_W), BF16), jax.ShapeDtypeStruct((nc, DN_H, HD, HD), F32)),
        scratch_shapes=[pltpu.VMEM((CH + SUB, _QKV_W), F32), pltpu.VMEM((DN_H, HD, HD), F32)],
        compiler_params=_cp("arbitrary"), name="dn_fwd")(p, p, p, p, conv_w, par)


def dn_bwd(p, dc, states, conv_w, par):
    t = p.shape[0]
    nc = t // CH

    def body(x_ref, h_ref, z_ref, g_ref, cw_ref, par_ref, s_ref, dy_ref,
             dx_o, dz_o, dg_o, dcw_o, dpar_o, ext, dst, dpost, x2):
        i = pl.program_id(0)
        r = nc - 1 - i
        c = _dn_consts()

        @pl.when(i == 0)
        def _():
            dst[...] = jnp.zeros_like(dst)
            dcw_o[...] = jnp.zeros_like(dcw_o)
            dpar_o[...] = jnp.zeros_like(dpar_o)
            x2[CH:, :] = jnp.zeros((SUB, _QKV_W), F32)

        ext[0:SUB, :] = jnp.where(r > 0, h_ref[...], 0.0)
        ext[SUB:, :] = x_ref[...]
        cw = cw_ref[...]
        pre = _dn_conv(ext, cw)
        sg = _sigmoid(pre)
        qkv = pre * sg
        par = par_ref[...]
        gt = g_ref[...]
        beta_all, g_all, gc_all, ea, sxa = _dn_gates(gt, par[0:1, :], par[1:2, :], c)
        z, dy = z_ref[...], dy_ref[...]
        nw = par[2:3, :]
        dnw = jnp.zeros((1, LANE), F32)
        dbeta_all = jnp.zeros((CH, LANE), F32)
        dgc_all = jnp.zeros((CH, LANE), F32)
        for h in range(DN_H):
            s0 = s_ref[h]
            qh, vh, zh = _hs(qkv, h), _hs(qkv, 2 * DN_H + h), _hs(z, h)
            beta = beta_all[:, h:h + 1]
            f = _dn_head_fwd(qh, _hs(qkv, DN_H + h), vh, beta, gc_all[:, DN_H + h:DN_H + h + 1], s0, c)
            dqh, dkh, dvh, dz, dnw_h, dbeta, dgc, ds0 = _dn_head_bwd(f, _hs(dy, h), dst[h], s0, qh, vh, zh,
                                                                     beta, nw, c)
            dst[h] = ds0
            dnw = dnw + dnw_h
            dbeta_all = dbeta_all + jnp.where(c.lane == h, dbeta, 0.0)
            dgc_all = dgc_all + jnp.where(c.lane == DN_H + h, dgc, 0.0)
            dpost[:, h * HD:(h + 1) * HD] = dqh
            dpost[:, (DN_H + h) * HD:(DN_H + h + 1) * HD] = dkh
            dpost[:, (2 * DN_H + h) * HD:(2 * DN_H + h + 1) * HD] = dvh
            dz_o[:, h * HD:(h + 1) * HD] = dz.astype(dz_o.dtype)
        dg_all = _mm_hi(c.tril, dgc_all, _TN)
        dpa = dg_all * (-ea) * sxa
        dpb = dbeta_all * beta_all * (1.0 - beta_all)
        is_b = c.lane < DN_H
        is_a = (c.lane >= DN_H) & (c.lane < 2 * DN_H)
        dg_o[...] = jnp.where(is_b, dpb, jnp.where(is_a, dpa, 0.0)).astype(dg_o.dtype)
        dpar_o[0:1, :] += jnp.where(is_a, jnp.sum(dg_all * g_all, axis=0, keepdims=True), 0.0)
        dpar_o[1:2, :] += jnp.where(is_a, jnp.sum(dpa, axis=0, keepdims=True), 0.0)
        dpar_o[2:3, :] += dnw
        dpre = dpost[...] * (sg * (1.0 + pre * (1.0 - sg)))
        for k in range(DN_K):
            dcw_o[k:k + 1, :] += jnp.sum(dpre * ext[pl.ds(5 + k, CH), :], axis=0, keepdims=True)
        x2[0:CH, :] = dpre
        dx_o[...] = (cw[3:4, :] * dpre + cw[2:3, :] * x2[pl.ds(1, CH), :] + cw[1:2, :] * x2[pl.ds(2, CH), :]
                     + cw[0:1, :] * x2[pl.ds(3, CH), :]).astype(dx_o.dtype)
        x2[CH:, :] = dpre[0:SUB, :]

    rev = lambda n: nc - 1 - n
    return pl.pallas_call(
        body, grid=(nc,),
        in_specs=_dn_in_specs(rev) + [pl.BlockSpec((None, DN_H, HD, HD), lambda n: (rev(n), 0, 0, 0)),
                                      pl.BlockSpec((CH, DN_W), lambda n: (rev(n), 0))],
        out_specs=[pl.BlockSpec((CH, _QKV_W), lambda n: (rev(n), 0)),
                   pl.BlockSpec((CH, DN_W), lambda n: (rev(n), 0)),
                   pl.BlockSpec((CH, LANE), lambda n: (rev(n), 0)),
                   pl.BlockSpec((SUB, _QKV_W), lambda n: (0, 0)),
                   pl.BlockSpec((SUB, LANE), lambda n: (0, 0))],
        out_shape=(jax.ShapeDtypeStruct((t, _QKV_W), BF16), jax.ShapeDtypeStruct((t, DN_W), BF16),
                   jax.ShapeDtypeStruct((t, LANE), BF16), jax.ShapeDtypeStruct((SUB, _QKV_W), F32),
                   jax.ShapeDtypeStruct((SUB, LANE), F32)),
        scratch_shapes=[pltpu.VMEM((CH + SUB, _QKV_W), F32), pltpu.VMEM((DN_H, HD, HD), F32),
                        pltpu.VMEM((CH, _QKV_W), F32), pltpu.VMEM((CH + SUB, _QKV_W), F32)],
        compiler_params=_cp("arbitrary"), name="dn_bwd")(p, p, p, p, conv_w, par, states, dc)


_ANY = pl.BlockSpec(memory_space=pl.ANY)
_MESH = pl.DeviceIdType.MESH


def _me():
    return lax.axis_index("x"), lax.axis_index("y"), lax.axis_index("c")


def all_gather(x, name):
    def body(x_ref, out_ref, send_sems, recv_sems, local_sem):
        mx, my, mc = _me()
        me, sibling = (mx, my, mc), (mx, my, 1 - mc)
        chips = [(1 - mx, my), (mx, 1 - my), (1 - mx, 1 - my)]

        def slot(px, py, pc):
            return out_ref.at[4 * px + 2 * py + pc]

        def copy(k, block, to, src=None):
            return pltpu.make_async_remote_copy(
                src_ref=slot(*block) if src is None else src, dst_ref=slot(*block),
                send_sem=send_sems.at[k], recv_sem=recv_sems.at[k], device_id=to, device_id_type=_MESH)

        mine = pltpu.make_async_copy(x_ref, slot(*me), local_sem)
        mine.start()
        first = [copy(0, me, sibling, src=x_ref)]
        first += [copy(1 + j, me, (*chip, mc), src=x_ref) for j, chip in enumerate(chips)]
        for cp in first:
            cp.start()
        passed = [copy(4 + j, (*chip, mc), sibling) for j, chip in enumerate(chips)]
        for j, chip in enumerate(chips):
            copy(1 + j, (*chip, mc), me).wait_recv()
            passed[j].start()
        copy(0, sibling, me).wait_recv()
        for j, chip in enumerate(chips):
            copy(4 + j, (*chip, 1 - mc), me).wait_recv()
        for cp in first + passed:
            cp.wait_send()
        mine.wait()

    return pl.pallas_call(
        body, out_shape=jax.ShapeDtypeStruct((N_DEV,) + x.shape, x.dtype), in_specs=[_ANY], out_specs=_ANY,
        scratch_shapes=[pltpu.SemaphoreType.DMA((7,)), pltpu.SemaphoreType.DMA((7,)), pltpu.SemaphoreType.DMA],
        name=name)(x)


def slab_exchange(x, name):
    def body(x_ref, out_ref, send_sems, recv_sems, local_sem):
        mx, my, mc = _me()
        me = 4 * mx + 2 * my + mc
        mine = pltpu.make_async_copy(x_ref.at[me], out_ref.at[me], local_sem)
        mine.start()
        copies = []
        for rel in range(1, N_DEV):
            px = 1 - mx if rel & 4 else mx
            py = 1 - my if rel & 2 else my
            pc = 1 - mc if rel & 1 else mc
            cp = pltpu.make_async_remote_copy(
                src_ref=x_ref.at[4 * px + 2 * py + pc], dst_ref=out_ref.at[me],
                send_sem=send_sems.at[rel - 1], recv_sem=recv_sems.at[rel - 1],
                device_id=(px, py, pc), device_id_type=_MESH)
            cp.start()
            copies.append(cp)
        for cp in copies:
            cp.wait_recv()
        for cp in copies:
            cp.wait_send()
        mine.wait()

    return pl.pallas_call(
        body, out_shape=jax.ShapeDtypeStruct(x.shape, x.dtype), in_specs=[_ANY], out_specs=_ANY,
        scratch_shapes=[pltpu.SemaphoreType.DMA((7,)), pltpu.SemaphoreType.DMA((7,)), pltpu.SemaphoreType.DMA],
        name=name)(x)


def sum_slabs(x, name):
    _, r, c = x.shape
    tr = _pick(r, max(SUB, (1 << 19) // c // SUB * SUB), SUB)

    def body(x_ref, o_ref):
        acc = x_ref[0].astype(F32)
        for s in range(1, N_DEV):
            acc = acc + x_ref[s].astype(F32)
        o_ref[...] = acc

    return pl.pallas_call(
        body, grid=(r // tr,), in_specs=[pl.BlockSpec((N_DEV, tr, c), lambda i: (0, i, 0))],
        out_specs=pl.BlockSpec((tr, c), lambda i: (i, 0)), out_shape=jax.ShapeDtypeStruct((r, c), F32),
        compiler_params=_cp("parallel"), name=name)(x)


def adamw(w, g, m, v, name):
    r, c = w.shape
    tr = _pick(r, max(SUB, (1 << 18) // c // SUB * SUB), SUB)
    c1 = 1.0 / (1.0 - ADAM_B1 ** ADAM_STEP)
    c2 = 1.0 / (1.0 - ADAM_B2 ** ADAM_STEP)

    def body(w_ref, g_ref, m_ref, v_ref, d_o, m_o, v_o):
        gg = g_ref[...]
        mn = ADAM_B1 * m_ref[...] + (1.0 - ADAM_B1) * gg
        vn = ADAM_B2 * v_ref[...] + (1.0 - ADAM_B2) * (gg * gg)
        m_o[...] = mn
        v_o[...] = vn
        d_o[...] = -ADAM_LR * ((mn * c1) / (jnp.sqrt(vn * c2) + ADAM_EPS) + ADAM_WD * w_ref[...])

    spec = pl.BlockSpec((tr, c), lambda i: (i, 0))
    sds = jax.ShapeDtypeStruct((r, c), F32)
    return pl.pallas_call(body, grid=(r // tr,), in_specs=[spec] * 4, out_specs=[spec] * 3, out_shape=(sds,) * 3,
                          compiler_params=_cp("parallel"), name=name)(w, g, m, v)


def _adamw_nd(w, g, m, v, name):
    shp = w.shape
    f = lambda a: a.reshape(-1, shp[-1])
    return tuple(o.reshape(shp) for o in adamw(f(w), f(g), f(m), f(v), name))


def _pack(parts):
    flat = jnp.concatenate([a.reshape(-1).astype(F32) for a in parts])
    n = flat.shape[0]
    npad = -n % (SUB * LANE)
    return jnp.pad(flat, (0, npad)).reshape(-1, LANE)


def _unpack(buf, shapes, lead=()):
    flat = buf.reshape(lead + (-1,))
    out, off = [], 0
    for s in shapes:
        n = math.prod(s)
        out.append(flat[..., off:off + n].reshape(lead + tuple(s)))
        off += n
    return out


ROPE_THETA = 10000.0

_SMALL = ("norm_mix_pre", "dn_conv_w", "dn_a_log", "dn_dt_bias", "dn_norm_w", "pool_w", "pool_scale", "swa_sinks",
          "norm_mix_post", "norm_ffn_pre", "ffn_conv_w", "ffn_conv_b", "norm_ffn_post")
_BIG = ("w_in", "w_out", "ffn_w_up", "ffn_w_down")
_ORDER = ("norm_mix_pre", "w_in", "dn_conv_w", "dn_a_log", "dn_dt_bias", "dn_norm_w", "pool_w", "pool_scale",
          "swa_sinks", "w_out", "norm_mix_post", "norm_ffn_pre", "ffn_w_up", "ffn_conv_w", "ffn_conv_b",
          "ffn_w_down", "norm_ffn_post")


def _step(x, positions, loss_target, w, m, v):
    nl = w["w_in"].shape[0]
    t, d = x.shape[1], x.shape[2]
    nb = w["ffn_w_up"].shape[2]
    f = nb * N_DEV // 2
    me = 4 * lax.axis_index("x") + 2 * lax.axis_index("y") + lax.axis_index("c")
    x_in, tgt = x[0], loss_target[0]

    inv_freq = 1.0 / (ROPE_THETA ** (jnp.arange(0, HD, 2, dtype=F32) / HD))
    ang = positions[0].astype(F32)[:, None] * inv_freq
    cos, sin = jnp.cos(ang), jnp.sin(ang)
    cos2 = jnp.concatenate([cos, cos], axis=1)
    sin2 = jnp.concatenate([-sin, sin], axis=1)

    conv_shapes = [w["dn_conv_w"].shape, w["ffn_conv_w"].shape]
    gathered = all_gather(_pack([w["dn_conv_w"], w["ffn_conv_w"]]), "ag_conv")
    dn_cw_g, ffn_cw_g = _unpack(gathered, conv_shapes, lead=(N_DEV,))
    dn_cw = jnp.moveaxis(dn_cw_g, 0, 2).reshape(nl, DN_K, _QKV_W)
    ffn_cw = jnp.moveaxis(ffn_cw_g, 0, 1).reshape(nl, 2, N_DEV // 2, 3, nb)
    ffn_cb = w["ffn_conv_b"].reshape(nl, 2, N_DEV // 2, 1, nb)

    def lane_row(vec, off):
        return jnp.zeros((LANE,), F32).at[off:off + vec.shape[0]].set(vec)

    dn_par = jnp.stack([
        jnp.zeros((SUB, LANE), F32).at[0].set(lane_row(w["dn_a_log"][l], DN_H))
        .at[1].set(lane_row(w["dn_dt_bias"][l], DN_H)).at[2].set(w["dn_norm_w"][l]) for l in range(nl)])
    sinks = jnp.stack([lane_row(w["swa_sinks"][l], 0)[None, :] for l in range(nl)])

    win, wout, wup, wdown = [], [], [], []
    for l in range(nl):
        win.append(all_gather(_align_in(w["w_in"][l]).astype(BF16), "ag_w_in").reshape(d, PW))
        wout.append(_perm_mix_rows(all_gather(w["w_out"][l].astype(BF16), "ag_w_out").reshape(MIX_W, d)))
        wup.append(all_gather(w["ffn_w_up"][l].astype(BF16), "ag_w_up"))
        wdown.append(all_gather(w["ffn_w_down"][l].astype(BF16), "ag_w_down").reshape(f, d))

    row = lambda a, l: a[l][None, :]
    g1, g2, g3, g4 = w["norm_mix_pre"], w["norm_mix_post"], w["norm_ffn_pre"], w["norm_ffn_post"]

    saved = []
    xl = x_in
    h1 = norm_first(xl, row(g1, 0))
    for l in range(nl):
        p = mm_nn(h1, win[l], F32, "mm_in")
        y_dn, states = dn_fwd(p, dn_cw[l], dn_par[l])
        y_pool = pool_fwd(p, w["pool_w"][l], row(w["pool_scale"], l))
        y_swa = swa_fwd(p, cos2, sin2, sinks[l])
        c = jnp.concatenate([y_dn, y_swa, y_pool], axis=1)
        mix = mm_nn(c, wout[l], F32, "mm_out")
        x1, h2 = post_pre(xl, mix, row(g2, l), row(g3, l))
        u0 = mm_up(h2, wup[l], "mm_up")
        act = glu_fwd(u0, ffn_cw[l], ffn_cb[l])
        fo = mm_nn(act, wdown[l], F32, "mm_down")
        saved.append(dict(x=xl, h1=h1, p=p, states=states, c=c, mix=mix, x1=x1, h2=h2, u0=u0, act=act, f=fo))
        if l < nl - 1:
            xl, h1 = post_pre(x1, fo, row(g4, l), row(g1, l + 1))
        else:
            dx, loss_part = post_loss(x1, fo, row(g4, l), tgt)

    small_g = [dict() for _ in range(nl)]
    big_g = [dict() for _ in range(nl)]
    df, small_g[nl - 1]["norm_ffn_post"] = bwd_norms(dx, post=(saved[-1]["f"], row(g4, nl - 1)))
    for l in range(nl - 1, -1, -1):
        s, sg = saved[l], small_g[l]
        dact = mm_nt(df, wdown[l], F32, "mm_down_d")
        big_g[l]["ffn_w_down"] = mm_tn(s["act"], df, BF16, "mm_down_w").reshape(N_DEV, f // N_DEV, d)
        du0, dcw = glu_bwd(dact, s["u0"], ffn_cw[l], ffn_cb[l])
        sg["ffn_conv"] = dcw
        dh2 = mm_up_dgrad(du0, wup[l], "mm_up_d")
        big_g[l]["ffn_w_up"] = mm_up_wgrad(s["h2"], du0, "mm_up_w")
        dx1, sg["norm_ffn_pre"], dmix, sg["norm_mix_post"] = bwd_norms(
            dx, pre=(dh2, s["x1"], row(g3, l)), post=(s["mix"], row(g2, l)))
        dc = mm_nt(dmix, wout[l], F32, "mm_out_d")
        big_g[l]["w_out"] = _unperm_mix_rows(mm_tn(s["c"], dmix, BF16, "mm_out_w")).reshape(N_DEV, MIX_W // N_DEV, d)
        dqkv, dz, dgate, sg["dn_conv_w"], sg["dn_par"] = dn_bwd(s["p"], dc, s["states"], dn_cw[l], dn_par[l])
        dpool, sg["pool_w"], sg["pool_scale"] = pool_bwd(s["p"], dc, w["pool_w"][l], row(w["pool_scale"], l))
        dsq, dsk, dsv, sg["swa_sinks"] = swa_bwd(s["p"], dc, cos2, sin2, sinks[l])
        dp = jnp.concatenate([dqkv, dz, dsq, dsk, dsv, dpool, dgate], axis=1)
        dh1 = mm_nt(dp, win[l], F32, "mm_in_d")
        big_g[l]["w_in"] = mm_tn(s["h1"], dp, BF16, "mm_in_w").reshape(N_DEV, d // N_DEV, PW)
        if l > 0:
            dx, sg["norm_mix_pre"], df, small_g[l - 1]["norm_ffn_post"] = bwd_norms(
                dx1, pre=(dh1, s["x"], row(g1, l)), post=(saved[l - 1]["f"], row(g4, l - 1)))
        else:
            grad_x, sg["norm_mix_pre"] = bwd_norms(dx1, pre=(dh1, s["x"], row(g1, 0)))

    grads = {}
    for name in _BIG:
        per_layer = []
        for l in range(nl):
            got = slab_exchange(big_g[l][name], "xch_" + name)
            per_layer.append(sum_slabs(got, "sum_" + name))
        grads[name] = jnp.stack(per_layer)
    grads["w_in"] = _unalign_in(grads["w_in"])

    keys = ("norm_mix_pre", "norm_mix_post", "norm_ffn_pre", "norm_ffn_post", "dn_conv_w", "dn_par", "pool_w",
            "pool_scale", "swa_sinks", "ffn_conv")
    parts = [small_g[l][k] for l in range(nl) for k in keys] + [loss_part]
    shapes = [a.shape for a in parts]
    summed = sum_slabs(all_gather(_pack(parts), "ag_small"), "sum_small")
    vals = _unpack(summed, shapes)
    loss = vals[-1][0, 0]
    sm = [dict(zip(keys, vals[l * len(keys):(l + 1) * len(keys)])) for l in range(nl)]
    st = lambda fn: jnp.stack([fn(sm[l]) for l in range(nl)])
    for k in ("norm_mix_pre", "norm_mix_post", "norm_ffn_pre", "norm_ffn_post"):
        grads[k] = st(lambda q: q[k][0])
    grads["dn_conv_w"] = lax.dynamic_slice_in_dim(st(lambda q: q["dn_conv_w"][0:DN_K]), me * (_QKV_W // N_DEV),
                                                  _QKV_W // N_DEV, axis=2)
    grads["dn_a_log"] = st(lambda q: q["dn_par"][0, DN_H:2 * DN_H])
    grads["dn_dt_bias"] = st(lambda q: q["dn_par"][1, DN_H:2 * DN_H])
    grads["dn_norm_w"] = st(lambda q: q["dn_par"][2])
    grads["pool_w"] = st(lambda q: q["pool_w"])
    grads["pool_scale"] = st(lambda q: q["pool_scale"][0])
    grads["swa_sinks"] = st(lambda q: q["swa_sinks"][0, 0:SWA_H])
    conv_all = st(lambda q: q["ffn_conv"].reshape(N_DEV, SUB, nb))
    grads["ffn_conv_w"] = lax.dynamic_index_in_dim(conv_all, me, axis=1, keepdims=False)[:, 0:3, :]
    grads["ffn_conv_b"] = conv_all[:, :, 3, :].reshape(nl, 2 * f)

    delta, new_m, new_v = {}, {}, {}
    for name in _BIG:
        delta[name], new_m[name], new_v[name] = _adamw_nd(w[name], grads[name], m[name], v[name], "adamw_" + name)
    shapes = [w[k].shape for k in _SMALL]
    pk = lambda tree: _pack([tree[k] for k in _SMALL])
    outs = adamw(pk(w), pk(grads), pk(m), pk(v), "adamw_small")
    for tree, buf in zip((delta, new_m, new_v), outs):
        for k, a in zip(_SMALL, _unpack(buf, shapes)):
            tree[k] = a

    return (loss, grad_x[None], *[grads[k] for k in _ORDER], *[delta[k] for k in _ORDER],
            *[new_m[k] for k in _ORDER], *[new_v[k] for k in _ORDER])


def kernel(x, positions, norm_mix_pre, w_in, dn_conv_w, dn_a_log, dn_dt_bias, dn_norm_w, pool_w, pool_scale, swa_sinks, w_out, norm_mix_post, norm_ffn_pre, ffn_w_up, ffn_conv_w, ffn_conv_b, ffn_w_down, norm_ffn_post, loss_target, m_norm_mix_pre, m_w_in, m_dn_conv_w, m_dn_a_log, m_dn_dt_bias, m_dn_norm_w, m_pool_w, m_pool_scale, m_swa_sinks, m_w_out, m_norm_mix_post, m_norm_ffn_pre, m_ffn_w_up, m_ffn_conv_w, m_ffn_conv_b, m_ffn_w_down, m_norm_ffn_post, v_norm_mix_pre, v_w_in, v_dn_conv_w, v_dn_a_log, v_dn_dt_bias, v_dn_norm_w, v_pool_w, v_pool_scale, v_swa_sinks, v_w_out, v_norm_mix_post, v_norm_ffn_pre, v_ffn_w_up, v_ffn_conv_w, v_ffn_conv_b, v_ffn_w_down, v_norm_ffn_post):
    args = locals()
    w = {k: args[k] for k in _ORDER}
    m = {k: args["m_" + k] for k in _ORDER}
    v = {k: args["v_" + k] for k in _ORDER}
    return _step(x, positions, loss_target, w, m, v)
```

```python
import functools
import math

import jax
import jax.numpy as jnp
from jax import lax
from jax.experimental import pallas as pl
from jax.experimental.pallas import tpu as pltpu

F32 = jnp.float32
BF16 = jnp.bfloat16
MXU_DT = jnp.bfloat16
HI = lax.Precision.HIGHEST

N_DEV = 8
LANE = 128
SUB = 8
VMEM_LIMIT = 56 * 1024 * 1024
ROW_TILE = 512
NORM_TILE = 256
MM_TM, MM_TN, MM_TK = 512, 1664, 2816
PACK_ROWS = 512

HD = 128
DN_H, DN_W, DN_K, CH = 6, 768, 4, 64
POOL_G = 4
SWA_H, SWA_KV, SWA_G, SWA_BLK = 6, 2, 3, 128
EPS = 1e-6
SCALE = HD ** -0.5
NEG = -1e30

O_QKV, O_Z, O_SQ, O_SK, O_SV, O_POOL, O_GATE, PW = 0, 2304, 3072, 3840, 4096, 4352, 4864, 4992
IN_W = 4876
MIX_W = 2048

ADAM_LR, ADAM_B1, ADAM_B2, ADAM_EPS, ADAM_WD, ADAM_STEP = 0.001, 0.9, 0.999, 1e-08, 0.01, 10


def _pick(n, cap, mult=LANE):
    best = None
    for d in range(mult, min(n, cap) + 1, mult):
        if n % d == 0:
            best = d
    return best if best is not None else n


def _cp(*sem):
    return pltpu.CompilerParams(dimension_semantics=sem, vmem_limit_bytes=VMEM_LIMIT)


def _dot(a, b, dims):
    return lax.dot_general(a.astype(MXU_DT), b.astype(MXU_DT), dims, preferred_element_type=F32)


_NN = (((1,), (0,)), ((), ()))
_NT = (((1,), (1,)), ((), ()))
_TN = (((0,), (0,)), ((), ()))


def _mm(a, b):
    return _dot(a, b, _NN)


def _mm_nt(a, b):
    return _dot(a, b, _NT)


def _mm_tn(a, b):
    return _dot(a, b, _TN)


def _mm_hi(a, b, dims=_NN):
    return lax.dot_general(a, b, dims, precision=HI, preferred_element_type=F32)


def _sigmoid(x):
    return jax.nn.sigmoid(x)


def _softplus(x):
    return jnp.maximum(x, 0.0) + jnp.log(1.0 + jnp.exp(-jnp.abs(x)))


def _align_in(w):
    pad = jnp.zeros(w.shape[:-1] + (PW - IN_W,), w.dtype)
    return jnp.concatenate([w[..., 0:3072], w[..., 3596:4364], w[..., 4364:4620], w[..., 4620:4876],
                            w[..., 3084:3596], w[..., 3072:3084], pad], axis=-1)


def _unalign_in(g):
    return jnp.concatenate([g[..., 0:3072], g[..., O_GATE:O_GATE + 12], g[..., O_POOL:O_POOL + 512],
                            g[..., O_SQ:O_SQ + 768], g[..., O_SK:O_SK + 256], g[..., O_SV:O_SV + 256]], axis=-1)


def _perm_mix_rows(w):
    return jnp.concatenate([w[0:768], w[1280:2048], w[768:1280]], axis=0)


def _unperm_mix_rows(w):
    return jnp.concatenate([w[0:768], w[1536:2048], w[768:1536]], axis=0)


def _mm_call(name, a, b, out_shape, grid, a_spec, b_spec, o_spec, dims, acc_shape):
    nk = grid[2]
    if nk == 1:
        def body_once(a_ref, b_ref, o_ref):
            o_ref[...] = _dot(a_ref[...], b_ref[...], dims).astype(o_ref.dtype)

        return pl.pallas_call(
            body_once, grid=grid, in_specs=[a_spec, b_spec], out_specs=o_spec, out_shape=out_shape,
            compiler_params=_cp("parallel", "parallel", "arbitrary"), name=name)(a, b)

    def body(a_ref, b_ref, o_ref, acc_ref):
        k = pl.program_id(2)

        @pl.when(k == 0)
        def _():
            acc_ref[...] = jnp.zeros_like(acc_ref)

        acc_ref[...] += _dot(a_ref[...], b_ref[...], dims)

        @pl.when(k == nk - 1)
        def _():
            o_ref[...] = acc_ref[...].astype(o_ref.dtype)

    return pl.pallas_call(
        body, grid=grid, in_specs=[a_spec, b_spec], out_specs=o_spec, out_shape=out_shape,
        scratch_shapes=[pltpu.VMEM(acc_shape, F32)],
        compiler_params=_cp("parallel", "parallel", "arbitrary"), name=name)(a, b)


def mm_nn(a, b, out_dtype, name):
    (m, k), n = a.shape, b.shape[1]
    tm, tn, tk = _pick(m, MM_TM, SUB), _pick(n, MM_TN), _pick(k, MM_TK)
    return _mm_call(name, a, b, jax.ShapeDtypeStruct((m, n), out_dtype), (m // tm, n // tn, k // tk),
                    pl.BlockSpec((tm, tk), lambda i, j, kk: (i, kk)),
                    pl.BlockSpec((tk, tn), lambda i, j, kk: (kk, j)),
                    pl.BlockSpec((tm, tn), lambda i, j, kk: (i, j)), _NN, (tm, tn))


def mm_nt(a, b, out_dtype, name):
    (m, k), n = a.shape, b.shape[0]
    tm, tn, tk = _pick(m, MM_TM, SUB), _pick(n, MM_TN), _pick(k, MM_TK)
    return _mm_call(name, a, b, jax.ShapeDtypeStruct((m, n), out_dtype), (m // tm, n // tn, k // tk),
                    pl.BlockSpec((tm, tk), lambda i, j, kk: (i, kk)),
                    pl.BlockSpec((tn, tk), lambda i, j, kk: (j, kk)),
                    pl.BlockSpec((tm, tn), lambda i, j, kk: (i, j)), _NT, (tm, tn))


def mm_tn(a, b, out_dtype, name):
    (k, m), n = a.shape, b.shape[1]
    tm, tn, tk = _pick(m, MM_TM), _pick(n, MM_TN), _pick(k, MM_TK, SUB)
    return _mm_call(name, a, b, jax.ShapeDtypeStruct((m, n), out_dtype), (m // tm, n // tn, k // tk),
                    pl.BlockSpec((tk, tm), lambda i, j, kk: (kk, i)),
                    pl.BlockSpec((tk, tn), lambda i, j, kk: (kk, j)),
                    pl.BlockSpec((tm, tn), lambda i, j, kk: (i, j)), _TN, (tm, tn))


def mm_up(h, wblk, name):
    (t, d), (nblk, _, nb) = h.shape, wblk.shape
    tm, tk = _pick(t, MM_TM, SUB), _pick(d, MM_TK)
    hb = nblk // 2
    return _mm_call(name, h, wblk, jax.ShapeDtypeStruct((2, t, hb * nb), F32), (t // tm, nblk, d // tk),
                    pl.BlockSpec((tm, tk), lambda i, j, kk: (i, kk)),
                    pl.BlockSpec((None, tk, nb), lambda i, j, kk: (j, kk, 0)),
                    pl.BlockSpec((None, tm, nb), lambda i, j, kk: (j // hb, i, j % hb)), _NN, (tm, nb))


def mm_up_dgrad(du0, wblk, name):
    (_, t, _), (nblk, d, nb) = du0.shape, wblk.shape
    tm, tn = _pick(t, MM_TM, SUB), _pick(d, MM_TN)
    hb = nblk // 2
    return _mm_call(name, du0, wblk, jax.ShapeDtypeStruct((t, d), F32), (t // tm, d // tn, nblk),
                    pl.BlockSpec((None, tm, nb), lambda i, j, kk: (kk // hb, i, kk % hb)),
                    pl.BlockSpec((None, tn, nb), lambda i, j, kk: (kk, j, 0)),
                    pl.BlockSpec((tm, tn), lambda i, j, kk: (i, j)), _NT, (tm, tn))


def mm_up_wgrad(h, du0, name):
    (t, d), (_, _, f) = h.shape, du0.shape
    nb = f // (N_DEV // 2)
    hb = N_DEV // 2
    tm, tk = _pick(d, MM_TM), _pick(t, MM_TK, SUB)
    return _mm_call(name, h, du0, jax.ShapeDtypeStruct((N_DEV, d, nb), BF16), (d // tm, N_DEV, t // tk),
                    pl.BlockSpec((tk, tm), lambda i, j, kk: (kk, i)),
                    pl.BlockSpec((None, tk, nb), lambda i, j, kk: (j // hb, kk, j % hb)),
                    pl.BlockSpec((None, tm, nb), lambda i, j, kk: (j, i, 0)), _TN, (tm, nb))


def _rms(x, w):
    r = lax.rsqrt(jnp.mean(x * x, axis=-1, keepdims=True) + EPS)
    return x * r * w


def _rms_bwd(dy, x, w):
    r = lax.rsqrt(jnp.mean(x * x, axis=-1, keepdims=True) + EPS)
    xh = x * r
    dxh = dy * w
    dx = r * (dxh - xh * jnp.mean(dxh * xh, axis=-1, keepdims=True))
    return dx, jnp.sum(dy * xh, axis=0, keepdims=True)


def _row_spec(tb, d):
    return pl.BlockSpec((tb, d), lambda i: (i, 0))


def _fix_spec(r, d):
    return pl.BlockSpec((r, d), lambda i: (0, 0))


def norm_first(x, w):
    t, d = x.shape
    tb = _pick(t, NORM_TILE, SUB)

    def body(x_ref, w_ref, h_ref):
        h_ref[...] = _rms(x_ref[...], w_ref[...]).astype(h_ref.dtype)

    return pl.pallas_call(body, grid=(t // tb,), in_specs=[_row_spec(tb, d), _fix_spec(1, d)],
                          out_specs=_row_spec(tb, d), out_shape=jax.ShapeDtypeStruct((t, d), BF16),
                          compiler_params=_cp("parallel"), name="norm_first")(x, w)


def post_pre(x, y, w_post, w_pre):
    t, d = x.shape
    tb = _pick(t, NORM_TILE, SUB)

    def body(x_ref, y_ref, wp_ref, wq_ref, xn_ref, h_ref):
        xn = x_ref[...] + _rms(y_ref[...], wp_ref[...])
        xn_ref[...] = xn
        h_ref[...] = _rms(xn, wq_ref[...]).astype(h_ref.dtype)

    return pl.pallas_call(
        body, grid=(t // tb,),
        in_specs=[_row_spec(tb, d), _row_spec(tb, d), _fix_spec(1, d), _fix_spec(1, d)],
        out_specs=[_row_spec(tb, d), _row_spec(tb, d)],
        out_shape=(jax.ShapeDtypeStruct((t, d), F32), jax.ShapeDtypeStruct((t, d), BF16)),
        compiler_params=_cp("parallel"), name="post_pre")(x, y, w_post, w_pre)


def post_loss(x, y, w_post, target):
    t, d = x.shape
    tb = _pick(t, NORM_TILE, SUB)

    def body(x_ref, y_ref, wp_ref, t_ref, g_ref, l_ref):
        err = x_ref[...] + _rms(y_ref[...], wp_ref[...]) - t_ref[...]
        g_ref[...] = err * (1.0 / d)

        @pl.when(pl.program_id(0) == 0)
        def _():
            l_ref[...] = jnp.zeros_like(l_ref)

        part = 0.5 * jnp.sum(jnp.mean(err * err, axis=-1, keepdims=True), axis=0, keepdims=True)
        l_ref[...] += jnp.broadcast_to(part, l_ref.shape)

    return pl.pallas_call(
        body, grid=(t // tb,),
        in_specs=[_row_spec(tb, d), _row_spec(tb, d), _fix_spec(1, d), _row_spec(tb, d)],
        out_specs=[_row_spec(tb, d), _fix_spec(1, LANE)],
        out_shape=(jax.ShapeDtypeStruct((t, d), F32), jax.ShapeDtypeStruct((1, LANE), F32)),
        compiler_params=_cp("arbitrary"), name="post_loss")(x, y, w_post, target)


def bwd_norms(dx_in, *, pre=None, post=None):
    t, d = dx_in.shape
    tb = _pick(t, NORM_TILE, SUB)
    has_pre, has_post = pre is not None, post is not None

    def body(*refs):
        refs = list(refs)
        dxi = refs.pop(0)
        if has_pre:
            dh, x, wq = refs.pop(0), refs.pop(0), refs.pop(0)
        if has_post:
            y, wp = refs.pop(0), refs.pop(0)
        first = pl.program_id(0) == 0
        dx = dxi[...]
        if has_pre:
            dxo, dwq = refs.pop(0), refs.pop(0)
            g, dw = _rms_bwd(dh[...], x[...], wq[...])
            dx = dx + g
            dxo[...] = dx

            @pl.when(first)
            def _():
                dwq[...] = jnp.zeros_like(dwq)

            dwq[...] += dw
        if has_post:
            dyo, dwp = refs.pop(0), refs.pop(0)
            g, dw = _rms_bwd(dx, y[...], wp[...])
            dyo[...] = g.astype(dyo.dtype)

            @pl.when(first)
            def _():
                dwp[...] = jnp.zeros_like(dwp)

            dwp[...] += dw

    ins, in_specs, outs, out_specs = [dx_in], [_row_spec(tb, d)], [], []
    if has_pre:
        ins += list(pre)
        in_specs += [_row_spec(tb, d), _row_spec(tb, d), _fix_spec(1, d)]
        outs += [jax.ShapeDtypeStruct((t, d), F32), jax.ShapeDtypeStruct((1, d), F32)]
        out_specs += [_row_spec(tb, d), _fix_spec(1, d)]
    if has_post:
        ins += list(post)
        in_specs += [_row_spec(tb, d), _fix_spec(1, d)]
        outs += [jax.ShapeDtypeStruct((t, d), BF16), jax.ShapeDtypeStruct((1, d), F32)]
        out_specs += [_row_spec(tb, d), _fix_spec(1, d)]
    name = "bwd_norms" + ("_pre" if has_pre else "") + ("_post" if has_post else "")
    return pl.pallas_call(body, grid=(t // tb,), in_specs=in_specs, out_specs=out_specs, out_shape=tuple(outs),
                          compiler_params=_cp("arbitrary"), name=name)(*ins)


def _ffn_conv(e_ref, cw, cb, tb):
    return (cw[:, 0:1, :] * e_ref[:, pl.ds(6, tb), :] + cw[:, 1:2, :] * e_ref[:, pl.ds(7, tb), :]
            + cw[:, 2:3, :] * e_ref[:, pl.ds(8, tb), :] + cb)


def _glu_specs(tb, nb, hpb, row_of):
    tile = pl.BlockSpec((2, tb, nb), lambda j, i: (0, row_of(i), j))
    halo = pl.BlockSpec((2, SUB, nb), lambda j, i: (0, jnp.maximum(row_of(i) * hpb - 1, 0), j))
    cw = pl.BlockSpec((2, None, 3, nb), lambda j, i: (0, j, 0, 0))
    cb = pl.BlockSpec((2, None, 1, nb), lambda j, i: (0, j, 0, 0))
    return tile, halo, cw, cb


def glu_fwd(u0, cw, cb):
    _, t, f = u0.shape
    nb = cw.shape[-1]
    tb = _pick(t, ROW_TILE, SUB)
    nt, hpb = t // tb, tb // SUB

    def body(u, h, cwr, cbr, o_ref, e):
        i = pl.program_id(1)
        e[:, 0:SUB, :] = jnp.where(i > 0, h[...], 0.0)
        e[:, SUB:, :] = u[...]
        ab = _ffn_conv(e, cwr[...], cbr[...], tb)
        a, b = ab[0], ab[1]
        o_ref[...] = (a * _sigmoid(a) * b).astype(o_ref.dtype)

    return pl.pallas_call(
        body, grid=(f // nb, nt), in_specs=list(_glu_specs(tb, nb, hpb, lambda i: i)),
        out_specs=pl.BlockSpec((tb, nb), lambda j, i: (i, j)),
        out_shape=jax.ShapeDtypeStruct((t, f), BF16),
        scratch_shapes=[pltpu.VMEM((2, tb + SUB, nb), F32)],
        compiler_params=_cp("parallel", "arbitrary"), name="glu_fwd")(u0, u0, cw, cb)


def glu_bwd(dact, u0, cw, cb):
    _, t, f = u0.shape
    nb = cw.shape[-1]
    tb = _pick(t, ROW_TILE, SUB)
    nt, hpb = t // tb, tb // SUB

    def body(d_ref, u, h, cwr, cbr, du_o, dc_o, e, x2):
        i = pl.program_id(1)
        r = nt - 1 - i
        e[:, 0:SUB, :] = jnp.where(r > 0, h[...], 0.0)
        e[:, SUB:, :] = u[...]
        w = cwr[...]
        ab = _ffn_conv(e, w, cbr[...], tb)
        a, b = ab[0], ab[1]
        sa = _sigmoid(a)
        d = d_ref[...]

        @pl.when(i == 0)
        def _():
            dc_o[...] = jnp.zeros_like(dc_o)
            x2[:, tb:, :] = jnp.zeros((2, SUB, nb), F32)

        x2[0, 0:tb, :] = d * b * (sa * (1.0 + a * (1.0 - sa)))
        x2[1, 0:tb, :] = d * (a * sa)
        du = x2[:, 0:tb, :]
        for k in range(3):
            dc_o[:, k:k + 1, :] += jnp.sum(du * e[:, pl.ds(6 + k, tb), :], axis=1, keepdims=True)
        dc_o[:, 3:4, :] += jnp.sum(du, axis=1, keepdims=True)
        du_o[...] = (w[:, 2:3, :] * du + w[:, 1:2, :] * x2[:, pl.ds(1, tb), :]
                     + w[:, 0:1, :] * x2[:, pl.ds(2, tb), :]).astype(du_o.dtype)
        x2[:, tb:, :] = du[:, 0:SUB, :]

    rev = lambda i: nt - 1 - i
    return pl.pallas_call(
        body, grid=(f // nb, nt),
        in_specs=[pl.BlockSpec((tb, nb), lambda j, i: (rev(i), j))] + list(_glu_specs(tb, nb, hpb, rev)),
        out_specs=[pl.BlockSpec((2, tb, nb), lambda j, i: (0, rev(i), j)),
                   pl.BlockSpec((2, None, SUB, nb), lambda j, i: (0, j, 0, 0))],
        out_shape=(jax.ShapeDtypeStruct((2, t, f), BF16), jax.ShapeDtypeStruct((2, f // nb, SUB, nb), F32)),
        scratch_shapes=[pltpu.VMEM((2, tb + SUB, nb), F32), pltpu.VMEM((2, tb + SUB, nb), F32)],
        compiler_params=_cp("arbitrary", "arbitrary"), name="glu_bwd")(dact, u0, u0, cw, cb)


POOL_HALO = 16
_PCOL = O_POOL // LANE
_CPOOL = 1536 // LANE


def _pool_sel(g, v2, v4, v8, v16):
    return jnp.where(g == 0, v2, jnp.where(g == 1, v4, jnp.where(g == 2, v8, v16)))


def _pool_cnt(g, t0, n):
    win = _pool_sel(g, 2, 4, 8, 16)
    tpos = t0 + lax.broadcasted_iota(jnp.int32, (n, 1), 0)
    return jnp.minimum(tpos + 1, win).astype(F32)


def _pool_core(e, g, t0, tb):
    s2 = e + pltpu.roll(e, 1, 0)
    s4 = s2 + pltpu.roll(s2, 2, 0)
    s8 = s4 + pltpu.roll(s4, 4, 0)
    s16 = s8 + pltpu.roll(s8, 8, 0)
    sw = _pool_sel(g, s2, s4, s8, s16)[POOL_HALO:]
    return sw / _pool_cnt(g, t0, tb) - e[POOL_HALO:]


def pool_fwd(p, pool_w, pool_scale):
    t = p.shape[0]
    tb = _pick(t, ROW_TILE, POOL_HALO)
    nt, hpb = t // tb, tb // POOL_HALO

    def body(x_ref, h_ref, w_ref, s_ref, o_ref):
        i, g = pl.program_id(0), pl.program_id(1)
        e = jnp.concatenate([jnp.where(i > 0, h_ref[...], 0.0), x_ref[...]], axis=0)
        yy = _pool_core(e, g, i * tb, tb)
        o_ref[...] = (_mm(yy, w_ref[...]) * s_ref[...]).astype(o_ref.dtype)

    return pl.pallas_call(
        body, grid=(nt, POOL_G),
        in_specs=[pl.BlockSpec((tb, LANE), lambda i, g: (i, _PCOL + g)),
                  pl.BlockSpec((POOL_HALO, LANE), lambda i, g: (jnp.maximum(i * hpb - 1, 0), _PCOL + g)),
                  pl.BlockSpec((None, LANE, LANE), lambda i, g: (g, 0, 0)),
                  pl.BlockSpec((1, LANE), lambda i, g: (0, g))],
        out_specs=pl.BlockSpec((tb, LANE), lambda i, g: (i, g)),
        out_shape=jax.ShapeDtypeStruct((t, POOL_G * LANE), BF16),
        compiler_params=_cp("parallel", "parallel"), name="pool_fwd")(p, p, pool_w, pool_scale)


def pool_bwd(p, dc, pool_w, pool_scale):
    t = p.shape[0]
    tb = _pick(t, ROW_TILE, POOL_HALO)
    nt, hpb = t // tb, tb // POOL_HALO
    n = tb + POOL_HALO

    def body(x_ref, h_ref, dy_ref, dn_ref, w_ref, s_ref, dx_o, dw_o, ds_o):
        g, i = pl.program_id(0), pl.program_id(1)
        e = jnp.concatenate([jnp.where(i > 0, h_ref[...], 0.0), x_ref[...]], axis=0)
        yy = _pool_core(e, g, i * tb, tb)
        w, sc, dy = w_ref[...], s_ref[...], dy_ref[...]

        @pl.when(i == 0)
        def _():
            dw_o[...] = jnp.zeros_like(dw_o)
            ds_o[...] = jnp.zeros_like(ds_o)

        ds_o[...] += jnp.sum(dy * _mm(yy, w), axis=0, keepdims=True)
        dw_o[...] += _mm_tn(yy, dy * sc)
        dye = jnp.concatenate([dy, jnp.where(i < nt - 1, dn_ref[...], 0.0)], axis=0) * sc
        dyy = _mm_nt(dye, w)
        z = dyy / _pool_cnt(g, i * tb, n)
        r2 = z + pltpu.roll(z, n - 1, 0)
        r4 = r2 + pltpu.roll(r2, n - 2, 0)
        r8 = r4 + pltpu.roll(r4, n - 4, 0)
        r16 = r8 + pltpu.roll(r8, n - 8, 0)
        dx_o[...] = (_pool_sel(g, r2, r4, r8, r16)[:tb] - dyy[:tb]).astype(dx_o.dtype)

    last = t // POOL_HALO - 1
    return pl.pallas_call(
        body, grid=(POOL_G, nt),
        in_specs=[pl.BlockSpec((tb, LANE), lambda g, i: (i, _PCOL + g)),
                  pl.BlockSpec((POOL_HALO, LANE), lambda g, i: (jnp.maximum(i * hpb - 1, 0), _PCOL + g)),
                  pl.BlockSpec((tb, LANE), lambda g, i: (i, _CPOOL + g)),
                  pl.BlockSpec((POOL_HALO, LANE), lambda g, i: (jnp.minimum((i + 1) * hpb, last), _CPOOL + g)),
                  pl.BlockSpec((None, LANE, LANE), lambda g, i: (g, 0, 0)),
                  pl.BlockSpec((1, LANE), lambda g, i: (0, g))],
        out_specs=[pl.BlockSpec((tb, LANE), lambda g, i: (i, g)),
                   pl.BlockSpec((None, LANE, LANE), lambda g, i: (g, 0, 0)),
                   pl.BlockSpec((1, LANE), lambda g, i: (0, g))],
        out_shape=(jax.ShapeDtypeStruct((t, POOL_G * LANE), BF16),
                   jax.ShapeDtypeStruct((POOL_G, LANE, LANE), F32),
                   jax.ShapeDtypeStruct((1, POOL_G * LANE), F32)),
        compiler_params=_cp("arbitrary", "arbitrary"), name="pool_bwd")(p, p, dc, dc, pool_w, pool_scale)


_QCOL, _KCOL, _VCOL = O_SQ // 768, O_SK // 256, O_SV // 256
_GQ = SWA_G * SWA_BLK


def _rope(x, c2, s2):
    return x * c2 + pltpu.roll(x, HD // 2, 1) * s2


def _rope_bwd(d, c2, s2):
    return d * c2 + pltpu.roll(d * s2, HD // 2, 1)


def _hs(x, h):
    return x[:, h * HD:(h + 1) * HD]


def _swa_group(q, kc, kp, vc, vp, c2c, s2c, c2p, s2p, sinks, h, blk):
    kcat = jnp.concatenate([_rope(_hs(kp, h), c2p, s2p), _rope(_hs(kc, h), c2c, s2c)], axis=0)
    vcat = jnp.concatenate([_hs(vp, h), _hs(vc, h)], axis=0)
    qs = jnp.concatenate([_rope(_hs(q, SWA_G * h + g), c2c, s2c) for g in range(SWA_G)], axis=0)
    s = _mm_nt(qs, kcat) * SCALE
    ii = lax.broadcasted_iota(jnp.int32, (_GQ, 2 * SWA_BLK), 0) & (SWA_BLK - 1)
    jj = lax.broadcasted_iota(jnp.int32, (_GQ, 2 * SWA_BLK), 1)
    lo = jnp.where(blk > 0, 0, SWA_BLK)
    s = jnp.where((jj > ii) & (jj <= ii + SWA_BLK) & (jj >= lo), s, NEG)
    sink = jnp.concatenate(
        [jnp.broadcast_to(sinks[:, SWA_G * h + g:SWA_G * h + g + 1], (SWA_BLK, 1)) for g in range(SWA_G)], axis=0)
    m = jnp.maximum(jnp.max(s, axis=1, keepdims=True), sink)
    p = jnp.exp(s - m)
    ps = jnp.exp(sink - m)
    l = jnp.sum(p, axis=1, keepdims=True) + ps
    return qs, kcat, vcat, p, ps, l


def _swa_specs(blk_of):
    cur = lambda w, c: pl.BlockSpec((SWA_BLK, w), lambda n: (blk_of(n), c))
    prev = lambda w, c: pl.BlockSpec((SWA_BLK, w), lambda n: (jnp.maximum(blk_of(n) - 1, 0), c))
    return [cur(768, _QCOL), cur(256, _KCOL), prev(256, _KCOL), cur(256, _VCOL), prev(256, _VCOL),
            cur(HD, 0), cur(HD, 0), prev(HD, 0), prev(HD, 0), pl.BlockSpec((1, LANE), lambda n: (0, 0))]


def swa_fwd(p, cos2, sin2, sinks):
    t = p.shape[0]

    def body(q_ref, kc, kp, vc, vp, c2c, s2c, c2p, s2p, sk_ref, o_ref):
        n = pl.program_id(0)
        for h in range(SWA_KV):
            _, _, vcat, pr, _, l = _swa_group(q_ref[...], kc[...], kp[...], vc[...], vp[...], c2c[...], s2c[...],
                                              c2p[...], s2p[...], sk_ref[...], h, n)
            o = _mm(pr, vcat) / l
            for g in range(SWA_G):
                hh = SWA_G * h + g
                o_ref[:, hh * HD:(hh + 1) * HD] = o[g * SWA_BLK:(g + 1) * SWA_BLK].astype(o_ref.dtype)

    return pl.pallas_call(
        body, grid=(t // SWA_BLK,), in_specs=_swa_specs(lambda n: n),
        out_specs=pl.BlockSpec((SWA_BLK, 768), lambda n: (n, 0)),
        out_shape=jax.ShapeDtypeStruct((t, 768), BF16),
        compiler_params=_cp("parallel"), name="swa_fwd")(p, p, p, p, p, cos2, sin2, cos2, sin2, sinks)


def swa_bwd(p, dc, cos2, sin2, sinks):
    t = p.shape[0]
    nb = t // SWA_BLK

    def body(q_ref, kc, kp, vc, vp, c2c, s2c, c2p, s2p, sk_ref, do_ref, dq_o, dk_o, dv_o, dsk_o, ck, cv):
        i = pl.program_id(0)
        r = nb - 1 - i

        @pl.when(i == 0)
        def _():
            ck[...] = jnp.zeros_like(ck)
            cv[...] = jnp.zeros_like(cv)
            dsk_o[...] = jnp.zeros_like(dsk_o)

        lane = lax.broadcasted_iota(jnp.int32, (1, LANE), 1)
        dsk = jnp.zeros((1, LANE), F32)
        do = do_ref[...]
        for h in range(SWA_KV):
            qs, kcat, vcat, pr, ps, l = _swa_group(q_ref[...], kc[...], kp[...], vc[...], vp[...], c2c[...],
                                                   s2c[...], c2p[...], s2p[...], sk_ref[...], h, r)
            pn = pr / l
            dos = jnp.concatenate([_hs(do, SWA_G * h + g) for g in range(SWA_G)], axis=0)
            dp = _mm_nt(dos, vcat)
            delta = jnp.sum(pn * dp, axis=1, keepdims=True)
            ds = pn * (dp - delta)
            dsr = -(ps / l) * delta
            for g in range(SWA_G):
                tot = jnp.sum(dsr[g * SWA_BLK:(g + 1) * SWA_BLK], axis=0, keepdims=True)
                dsk = dsk + jnp.where(lane == SWA_G * h + g, tot, 0.0)
            dqs = _mm(ds, kcat) * SCALE
            for g in range(SWA_G):
                hh = SWA_G * h + g
                dq_o[:, hh * HD:(hh + 1) * HD] = _rope_bwd(dqs[g * SWA_BLK:(g + 1) * SWA_BLK], c2c[...],
                                                          s2c[...]).astype(dq_o.dtype)
            dk = _mm_tn(ds, qs) * SCALE
            dv = _mm_tn(pn, dos)
            cs = slice(h * HD, (h + 1) * HD)
            dk_o[:, cs] = (_rope_bwd(dk[SWA_BLK:], c2c[...], s2c[...]) + ck[:, cs]).astype(dk_o.dtype)
            dv_o[:, cs] = (dv[SWA_BLK:] + cv[:, cs]).astype(dv_o.dtype)
            ck[:, cs] = _rope_bwd(dk[:SWA_BLK], c2p[...], s2p[...])
            cv[:, cs] = dv[:SWA_BLK]
        dsk_o[...] += dsk

    rev = lambda n: nb - 1 - n
    return pl.pallas_call(
        body, grid=(nb,),
        in_specs=_swa_specs(rev) + [pl.BlockSpec((SWA_BLK, 768), lambda n: (rev(n), 1))],
        out_specs=[pl.BlockSpec((SWA_BLK, 768), lambda n: (rev(n), 0)),
                   pl.BlockSpec((SWA_BLK, 256), lambda n: (rev(n), 0)),
                   pl.BlockSpec((SWA_BLK, 256), lambda n: (rev(n), 0)),
                   pl.BlockSpec((1, LANE), lambda n: (0, 0))],
        out_shape=(jax.ShapeDtypeStruct((t, 768), BF16), jax.ShapeDtypeStruct((t, 256), BF16),
                   jax.ShapeDtypeStruct((t, 256), BF16), jax.ShapeDtypeStruct((1, LANE), F32)),
        scratch_shapes=[pltpu.VMEM((SWA_BLK, 256), F32), pltpu.VMEM((SWA_BLK, 256), F32)],
        compiler_params=_cp("arbitrary"), name="swa_bwd")(p, p, p, p, p, cos2, sin2, cos2, sin2, sinks, dc)


_ZCOL, _GCOL = O_Z // DN_W, O_GATE // LANE
_QKV_W = 3 * DN_W
_INV_STEPS = int(math.log2(CH)) - 1


class _Bag(dict):
    __getattr__ = dict.__getitem__


def _dn_consts():
    ii = lax.broadcasted_iota(jnp.int32, (CH, CH), 0)
    jj = lax.broadcasted_iota(jnp.int32, (CH, CH), 1)
    return _Bag(lower=ii >= jj, strict=ii > jj, diag=ii == jj,
                eye=jnp.where(ii == jj, 1.0, 0.0).astype(F32),
                tril=jnp.where(ii >= jj, 1.0, 0.0).astype(F32),
                ones=jnp.ones((CH, CH), F32), ones_w=jnp.ones((CH, LANE), F32),
                rows=lax.broadcasted_iota(jnp.int32, (CH, 1), 0),
                lane=lax.broadcasted_iota(jnp.int32, (1, LANE), 1))


def _dn_conv(ext_ref, cw):
    return (cw[0:1, :] * ext_ref[pl.ds(5, CH), :] + cw[1:2, :] * ext_ref[pl.ds(6, CH), :]
            + cw[2:3, :] * ext_ref[pl.ds(7, CH), :] + cw[3:4, :] * ext_ref[pl.ds(8, CH), :])


def _dn_gates(gt, arow, drow, c):
    beta = _sigmoid(gt)
    ea = jnp.exp(arow)
    xa = gt + drow
    g = -ea * _softplus(xa)
    return beta, g, _mm_hi(c.tril, g), ea, _sigmoid(xa)


def _dn_head_fwd(qh, kh, vh, beta, gc, s0, c):
    rq = lax.rsqrt(jnp.sum(qh * qh, axis=1, keepdims=True) + EPS)
    rk = lax.rsqrt(jnp.sum(kh * kh, axis=1, keepdims=True) + EPS)
    qn = qh * rq * SCALE
    kn = kh * rk
    kb = kn * beta
    vb = vh * beta
    gcol = _mm_hi(c.ones, jnp.where(c.diag, gc, 0.0))
    gam = jnp.where(c.lower, jnp.exp(jnp.minimum(gc - gcol, 0.0)), 0.0)
    lmat = jnp.where(c.strict, _mm_nt(kb, kn) * gam, 0.0)
    nil = -lmat
    inv = c.eye + nil
    powk = nil
    for _ in range(_INV_STEPS):
        powk = _mm(powk, powk)
        inv = _mm(inv, c.eye + powk)
    eg = jnp.exp(gc)
    kbe = kb * eg
    u = _mm(inv, vb)
    w = _mm(inv, kbe)
    amat = _mm_nt(qn, kn) * gam
    gl = gc[CH - 1:CH, :]
    e2 = jnp.exp(gl - gc)
    cd = jnp.exp(gl)
    qd = qn * eg
    kd = kn * e2
    vnew = u - _mm(w, s0)
    o = _mm(qd, s0) + _mm(amat, vnew)
    s1 = s0 * cd + _mm_tn(kd, vnew)
    return _Bag(rq=rq, rk=rk, qn=qn, kn=kn, kb=kb, vb=vb, gam=gam, lmat=lmat, inv=inv, eg=eg, kbe=kbe, w=w,
                amat=amat, e2=e2, cd=cd, qd=qd, kd=kd, vnew=vnew, o=o, s1=s1)


def _dn_post(o, zh, nw):
    ro = lax.rsqrt(jnp.mean(o * o, axis=1, keepdims=True) + EPS)
    oh = o * ro
    sz = _sigmoid(zh)
    return ro, oh, sz, oh * nw * (zh * sz)


def _dn_head_bwd(f, dy, dsn, s0, qh, vh, zh, beta, nw, c):
    ro, oh, sz, _ = _dn_post(f.o, zh, nw)
    don = dy * (zh * sz)
    dz = dy * (oh * nw) * (sz * (1.0 + zh * (1.0 - sz)))
    dnw = jnp.sum(don * oh, axis=0, keepdims=True)
    doh = don * nw
    do = ro * (doh - oh * jnp.mean(doh * oh, axis=1, keepdims=True))

    dvnew = _mm_tn(f.amat, do) + _mm(f.kd, dsn)
    da = jnp.where(c.lower, _mm_nt(do, f.vnew), 0.0)
    dqd = _mm_nt(do, s0)
    ds0 = _mm_tn(f.qd, do) + f.cd * dsn - _mm_tn(f.w, dvnew)
    dcd = jnp.sum(jnp.sum(s0 * dsn, axis=1, keepdims=True), axis=0, keepdims=True)
    dkd = _mm_nt(f.vnew, dsn)
    dw = -_mm_nt(dvnew, s0)
    dt = _mm_nt(dvnew, f.vb) + _mm_nt(dw, f.kbe)
    dvb = _mm_tn(f.inv, dvnew)
    dkbe = _mm_tn(f.inv, dw)
    dl = -jnp.where(c.strict, _mm_tn(f.inv, _mm_nt(dt, f.inv)), 0.0)
    dm = dl * f.gam
    dn = da * f.gam
    dkb = _mm(dm, f.kn) + dkbe * f.eg
    dkn = _mm_tn(dm, f.kb) + _mm_tn(dn, f.qn) + dkd * f.e2 + beta * dkb
    dqn = _mm(dn, f.kn) + dqd * f.eg
    pm = dl * f.lmat + da * f.amat
    colsum = _mm_hi(pm, c.ones_w, _TN)[:, 0:1]
    tkd = jnp.sum(dkd * f.kn, axis=1, keepdims=True) * f.e2
    dgc = (jnp.sum(pm, axis=1, keepdims=True) - colsum - tkd
           + (jnp.sum(dqd * f.qn, axis=1, keepdims=True) + jnp.sum(dkbe * f.kb, axis=1, keepdims=True)) * f.eg)
    dgl = jnp.sum(tkd, axis=0, keepdims=True) + dcd * f.cd
    dgc = dgc + jnp.where(c.rows == CH - 1, dgl, 0.0)
    dbeta = jnp.sum(dkb * f.kn, axis=1, keepdims=True) + jnp.sum(dvb * vh, axis=1, keepdims=True)
    dvh = beta * dvb
    qhat = qh * f.rq
    dqs = dqn * SCALE
    dqh = f.rq * (dqs - qhat * jnp.sum(qhat * dqs, axis=1, keepdims=True))
    dkh = f.rk * (dkn - f.kn * jnp.sum(f.kn * dkn, axis=1, keepdims=True))
    return dqh, dkh, dvh, dz, dnw, dbeta, dgc, ds0


def _dn_in_specs(chunk_of):
    return [pl.BlockSpec((CH, _QKV_W), lambda n: (chunk_of(n), 0)),
            pl.BlockSpec((SUB, _QKV_W), lambda n: (jnp.maximum(chunk_of(n) * (CH // SUB) - 1, 0), 0)),
            pl.BlockSpec((CH, DN_W), lambda n: (chunk_of(n), _ZCOL)),
            pl.BlockSpec((CH, LANE), lambda n: (chunk_of(n), _GCOL)),
            pl.BlockSpec((DN_K, _QKV_W), lambda n: (0, 0)),
            pl.BlockSpec((SUB, LANE), lambda n: (0, 0))]


def dn_fwd(p, conv_w, par):
    t = p.shape[0]
    nc = t // CH

    def body(x_ref, h_ref, z_ref, g_ref, cw_ref, par_ref, y_o, s_o, ext, st):
        n = pl.program_id(0)
        c = _dn_consts()

        @pl.when(n == 0)
        def _():
            st[...] = jnp.zeros_like(st)

        ext[0:SUB, :] = jnp.where(n > 0, h_ref[...], 0.0)
        ext[SUB:, :] = x_ref[...]
        pre = _dn_conv(ext, cw_ref[...])
        qkv = pre * _sigmoid(pre)
        par = par_ref[...]
        beta_all, _, gc_all, _, _ = _dn_gates(g_ref[...], par[0:1, :], par[1:2, :], c)
        z = z_ref[...]
        for h in range(DN_H):
            s0 = st[h]
            s_o[h] = s0
            f = _dn_head_fwd(_hs(qkv, h), _hs(qkv, DN_H + h), _hs(qkv, 2 * DN_H + h),
                             beta_all[:, h:h + 1], gc_all[:, DN_H + h:DN_H + h + 1], s0, c)
            st[h] = f.s1
            y_o[:, h * HD:(h + 1) * HD] = _dn_post(f.o, _hs(z, h), par[2:3, :])[3].astype(y_o.dtype)

    return pl.pallas_call(
        body, grid=(nc,), in_specs=_dn_in_specs(lambda n: n),
        out_specs=[pl.BlockSpec((CH, DN_W), lambda n: (n, 0)),
                   pl.BlockSpec((None, DN_H, HD, HD), lambda n: (n, 0, 0, 0))],
        out_shape=(jax.ShapeDtypeStruct((t, DN_W), BF16), jax.ShapeDtypeStruct((nc, DN_H, HD, HD), F32)),
        scratch_shapes=[pltpu.VMEM((CH + SUB, _QKV_W), F32), pltpu.VMEM((DN_H, HD, HD), F32)],
        compiler_params=_cp("arbitrary"), name="dn_fwd")(p, p, p, p, conv_w, par)


def dn_bwd(p, dc, states, conv_w, par):
    t = p.shape[0]
    nc = t // CH

    def body(x_ref, h_ref, z_ref, g_ref, cw_ref, par_ref, s_ref, dy_ref,
             dx_o, dz_o, dg_o, dcw_o, dpar_o, ext, dst, dpost, x2):
        i = pl.program_id(0)
        r = nc - 1 - i
        c = _dn_consts()

        @pl.when(i == 0)
        def _():
            dst[...] = jnp.zeros_like(dst)
            dcw_o[...] = jnp.zeros_like(dcw_o)
            dpar_o[...] = jnp.zeros_like(dpar_o)
            x2[CH:, :] = jnp.zeros((SUB, _QKV_W), F32)

        ext[0:SUB, :] = jnp.where(r > 0, h_ref[...], 0.0)
        ext[SUB:, :] = x_ref[...]
        cw = cw_ref[...]
        pre = _dn_conv(ext, cw)
        sg = _sigmoid(pre)
        qkv = pre * sg
        par = par_ref[...]
        gt = g_ref[...]
        beta_all, g_all, gc_all, ea, sxa = _dn_gates(gt, par[0:1, :], par[1:2, :], c)
        z, dy = z_ref[...], dy_ref[...]
        nw = par[2:3, :]
        dnw = jnp.zeros((1, LANE), F32)
        dbeta_all = jnp.zeros((CH, LANE), F32)
        dgc_all = jnp.zeros((CH, LANE), F32)
        for h in range(DN_H):
            s0 = s_ref[h]
            qh, vh, zh = _hs(qkv, h), _hs(qkv, 2 * DN_H + h), _hs(z, h)
            beta = beta_all[:, h:h + 1]
            f = _dn_head_fwd(qh, _hs(qkv, DN_H + h), vh, beta, gc_all[:, DN_H + h:DN_H + h + 1], s0, c)
            dqh, dkh, dvh, dz, dnw_h, dbeta, dgc, ds0 = _dn_head_bwd(f, _hs(dy, h), dst[h], s0, qh, vh, zh,
                                                                     beta, nw, c)
            dst[h] = ds0
            dnw = dnw + dnw_h
            dbeta_all = dbeta_all + jnp.where(c.lane == h, dbeta, 0.0)
            dgc_all = dgc_all + jnp.where(c.lane == DN_H + h, dgc, 0.0)
            dpost[:, h * HD:(h + 1) * HD] = dqh
            dpost[:, (DN_H + h) * HD:(DN_H + h + 1) * HD] = dkh
            dpost[:, (2 * DN_H + h) * HD:(2 * DN_H + h + 1) * HD] = dvh
            dz_o[:, h * HD:(h + 1) * HD] = dz.astype(dz_o.dtype)
        dg_all = _mm_hi(c.tril, dgc_all, _TN)
        dpa = dg_all * (-ea) * sxa
        dpb = dbeta_all * beta_all * (1.0 - beta_all)
        is_b = c.lane < DN_H
        is_a = (c.lane >= DN_H) & (c.lane < 2 * DN_H)
        dg_o[...] = jnp.where(is_b, dpb, jnp.where(is_a, dpa, 0.0)).astype(dg_o.dtype)
        dpar_o[0:1, :] += jnp.where(is_a, jnp.sum(dg_all * g_all, axis=0, keepdims=True), 0.0)
        dpar_o[1:2, :] += jnp.where(is_a, jnp.sum(dpa, axis=0, keepdims=True), 0.0)
        dpar_o[2:3, :] += dnw
        dpre = dpost[...] * (sg * (1.0 + pre * (1.0 - sg)))
        for k in range(DN_K):
            dcw_o[k:k + 1, :] += jnp.sum(dpre * ext[pl.ds(5 + k, CH), :], axis=0, keepdims=True)
        x2[0:CH, :] = dpre
        dx_o[...] = (cw[3:4, :] * dpre + cw[2:3, :] * x2[pl.ds(1, CH), :] + cw[1:2, :] * x2[pl.ds(2, CH), :]
                     + cw[0:1, :] * x2[pl.ds(3, CH), :]).astype(dx_o.dtype)
        x2[CH:, :] = dpre[0:SUB, :]

    rev = lambda n: nc - 1 - n
    return pl.pallas_call(
        body, grid=(nc,),
        in_specs=_dn_in_specs(rev) + [pl.BlockSpec((None, DN_H, HD, HD), lambda n: (rev(n), 0, 0, 0)),
                                      pl.BlockSpec((CH, DN_W), lambda n: (rev(n), 0))],
        out_specs=[pl.BlockSpec((CH, _QKV_W), lambda n: (rev(n), 0)),
                   pl.BlockSpec((CH, DN_W), lambda n: (rev(n), 0)),
                   pl.BlockSpec((CH, LANE), lambda n: (rev(n), 0)),
                   pl.BlockSpec((SUB, _QKV_W), lambda n: (0, 0)),
                   pl.BlockSpec((SUB, LANE), lambda n: (0, 0))],
        out_shape=(jax.ShapeDtypeStruct((t, _QKV_W), BF16), jax.ShapeDtypeStruct((t, DN_W), BF16),
                   jax.ShapeDtypeStruct((t, LANE), BF16), jax.ShapeDtypeStruct((SUB, _QKV_W), F32),
                   jax.ShapeDtypeStruct((SUB, LANE), F32)),
        scratch_shapes=[pltpu.VMEM((CH + SUB, _QKV_W), F32), pltpu.VMEM((DN_H, HD, HD), F32),
                        pltpu.VMEM((CH, _QKV_W), F32), pltpu.VMEM((CH + SUB, _QKV_W), F32)],
        compiler_params=_cp("arbitrary"), name="dn_bwd")(p, p, p, p, conv_w, par, states, dc)


_ANY = pl.BlockSpec(memory_space=pl.ANY)
_MESH = pl.DeviceIdType.MESH


def _me():
    return lax.axis_index("x"), lax.axis_index("y"), lax.axis_index("c")


def all_gather(x, name):
    def body(x_ref, out_ref, send_sems, recv_sems, local_sem):
        mx, my, mc = _me()
        me, sibling = (mx, my, mc), (mx, my, 1 - mc)
        chips = [(1 - mx, my), (mx, 1 - my), (1 - mx, 1 - my)]

        def slot(px, py, pc):
            return out_ref.at[4 * px + 2 * py + pc]

        def copy(k, block, to, src=None):
            return pltpu.make_async_remote_copy(
                src_ref=slot(*block) if src is None else src, dst_ref=slot(*block),
                send_sem=send_sems.at[k], recv_sem=recv_sems.at[k], device_id=to, device_id_type=_MESH)

        mine = pltpu.make_async_copy(x_ref, slot(*me), local_sem)
        mine.start()
        first = [copy(0, me, sibling, src=x_ref)]
        first += [copy(1 + j, me, (*chip, mc), src=x_ref) for j, chip in enumerate(chips)]
        for cp in first:
            cp.start()
        passed = [copy(4 + j, (*chip, mc), sibling) for j, chip in enumerate(chips)]
        for j, chip in enumerate(chips):
            copy(1 + j, (*chip, mc), me).wait_recv()
            passed[j].start()
        copy(0, sibling, me).wait_recv()
        for j, chip in enumerate(chips):
            copy(4 + j, (*chip, 1 - mc), me).wait_recv()
        for cp in first + passed:
            cp.wait_send()
        mine.wait()

    return pl.pallas_call(
        body, out_shape=jax.ShapeDtypeStruct((N_DEV,) + x.shape, x.dtype), in_specs=[_ANY], out_specs=_ANY,
        scratch_shapes=[pltpu.SemaphoreType.DMA((7,)), pltpu.SemaphoreType.DMA((7,)), pltpu.SemaphoreType.DMA],
        name=name)(x)


_HBM = pl.BlockSpec(memory_space=pltpu.HBM)
_SEM = pl.BlockSpec(memory_space=pltpu.SEMAPHORE)
_EFFECT = pltpu.SideEffectType.DATAFLOW_SIDE_EFFECTING
_TOKEN = jax.ShapeDtypeStruct((SUB, LANE), F32)


def _peers(mx, my, mc):
    for rel in range(1, N_DEV):
        yield (1 - mx if rel & 4 else mx, 1 - my if rel & 2 else my, 1 - mc if rel & 1 else mc)


def _in_hbm(a):
    return pltpu.with_memory_space_constraint(a, pltpu.HBM)


def gather_start(bufs, after, name):
    n = len(bufs)

    def body(*refs):
        ins, sems, token = refs[:n], refs[n + 1:3 * n + 1], refs[4 * n + 1]
        mx, my, mc = _me()
        me = 4 * mx + 2 * my + mc
        for b in range(n):
            for peer in _peers(mx, my, mc):
                pltpu.make_async_remote_copy(
                    src_ref=ins[b].at[me], dst_ref=ins[b].at[me], send_sem=sems[2 * b], recv_sem=sems[2 * b + 1],
                    device_id=peer, device_id_type=_MESH).start()
        token[...] = jnp.zeros_like(token)

    outs = pl.pallas_call(
        body, name=name,
        out_shape=tuple([pltpu.SemaphoreType.DMA(())] * (2 * n) + [pltpu.HBM(b.shape, b.dtype) for b in bufs]
                        + [_TOKEN]),
        in_specs=[_HBM] * n + [_ANY],
        out_specs=tuple([_SEM] * (2 * n) + [_HBM] * n + [pl.BlockSpec(memory_space=pltpu.VMEM)]),
        input_output_aliases={b: 2 * n + b for b in range(n)},
        compiler_params=pltpu.CompilerParams(has_side_effects=_EFFECT))(*[_in_hbm(b) for b in bufs], after)
    return [(outs[2 * b], outs[2 * b + 1]) for b in range(n)], list(outs[2 * n:3 * n]), outs[3 * n]


def exchange_start(src, name):
    def body(src_ref, land_ref, send_sem, recv_sem, src_thru, land_thru, token):
        mx, my, mc = _me()
        me = 4 * mx + 2 * my + mc
        for px, py, pc in _peers(mx, my, mc):
            pltpu.make_async_remote_copy(
                src_ref=src_ref.at[4 * px + 2 * py + pc], dst_ref=land_ref.at[me], send_sem=send_sem,
                recv_sem=recv_sem, device_id=(px, py, pc), device_id_type=_MESH).start()
        token[...] = jnp.zeros_like(token)

    hbm = pltpu.HBM(src.shape, src.dtype)
    send_sem, recv_sem, src_thru, land_thru, token = pl.pallas_call(
        body, name=name,
        out_shape=(pltpu.SemaphoreType.DMA(()), pltpu.SemaphoreType.DMA(()), hbm, hbm, _TOKEN),
        in_specs=[_HBM, _HBM], out_specs=(_SEM, _SEM, _HBM, _HBM, pl.BlockSpec(memory_space=pltpu.VMEM)),
        input_output_aliases={0: 2, 1: 3},
        compiler_params=pltpu.CompilerParams(has_side_effects=_EFFECT))(
            _in_hbm(src), _in_hbm(lax.empty(src.shape, src.dtype)))
    return (send_sem, recv_sem), src_thru, land_thru, token


def transfer_wait(sems, bufs, after, name):
    n = len(bufs)

    def body(*refs):
        seven = refs[0].at[pl.ds(0, N_DEV - 1)]
        cp = pltpu.make_async_remote_copy(src_ref=seven, dst_ref=seven, send_sem=refs[n], recv_sem=refs[n + 1],
                                          device_id=_me(), device_id_type=_MESH)
        cp.wait_send()
        cp.wait_recv()

    outs = pl.pallas_call(
        body, name=name, out_shape=tuple(pltpu.HBM(b.shape, b.dtype) for b in bufs),
        in_specs=[_HBM] * n + [_SEM, _SEM, _ANY], out_specs=tuple([_HBM] * n),
        input_output_aliases={b: b for b in range(n)},
        compiler_params=pltpu.CompilerParams(has_side_effects=_EFFECT))(*bufs, sems[0], sems[1], after)
    return list(outs)


def sum_slabs(x, name, own=None, me=None):
    _, r, c = x.shape
    tr = _pick(r, max(SUB, (1 << 19) // c // SUB * SUB), SUB)
    out_shape = jax.ShapeDtypeStruct((r, c), F32)
    if own is None:
        def body(x_ref, o_ref):
            acc = x_ref[0].astype(F32)
            for s in range(1, N_DEV):
                acc = acc + x_ref[s].astype(F32)
            o_ref[...] = acc

        return pl.pallas_call(
            body, grid=(r // tr,), in_specs=[pl.BlockSpec((N_DEV, tr, c), lambda i: (0, i, 0))],
            out_specs=pl.BlockSpec((tr, c), lambda i: (i, 0)), out_shape=out_shape,
            compiler_params=_cp("parallel"), name=name)(x)

    def body_own(me_ref, x_ref, own_ref, o_ref):
        acc = None
        for s in range(N_DEV):
            val = jnp.where(me_ref[0] == s, own_ref[...], x_ref[s]).astype(F32)
            acc = val if acc is None else acc + val
        o_ref[...] = acc

    return pl.pallas_call(
        body_own, out_shape=out_shape, name=name, compiler_params=_cp("parallel"),
        grid_spec=pltpu.PrefetchScalarGridSpec(
            num_scalar_prefetch=1, grid=(r // tr,),
            in_specs=[pl.BlockSpec((N_DEV, tr, c), lambda i, me_ref: (0, i, 0)),
                      pl.BlockSpec((None, tr, c), lambda i, me_ref: (me_ref[0], i, 0))],
            out_specs=pl.BlockSpec((tr, c), lambda i, me_ref: (i, 0))))(me, x, own)


def adamw(w, g, m, v, name):
    r, c = w.shape
    tr = _pick(r, max(SUB, (1 << 18) // c // SUB * SUB), SUB)
    c1 = 1.0 / (1.0 - ADAM_B1 ** ADAM_STEP)
    c2 = 1.0 / (1.0 - ADAM_B2 ** ADAM_STEP)

    def body(w_ref, g_ref, m_ref, v_ref, d_o, m_o, v_o):
        gg = g_ref[...]
        mn = ADAM_B1 * m_ref[...] + (1.0 - ADAM_B1) * gg
        vn = ADAM_B2 * v_ref[...] + (1.0 - ADAM_B2) * (gg * gg)
        m_o[...] = mn
        v_o[...] = vn
        d_o[...] = -ADAM_LR * ((mn * c1) / (jnp.sqrt(vn * c2) + ADAM_EPS) + ADAM_WD * w_ref[...])

    spec = pl.BlockSpec((tr, c), lambda i: (i, 0))
    sds = jax.ShapeDtypeStruct((r, c), F32)
    return pl.pallas_call(body, grid=(r // tr,), in_specs=[spec] * 4, out_specs=[spec] * 3, out_shape=(sds,) * 3,
                          compiler_params=_cp("parallel"), name=name)(w, g, m, v)


def _adamw_nd(w, g, m, v, name):
    shp = w.shape
    f = lambda a: a.reshape(-1, shp[-1])
    return tuple(o.reshape(shp) for o in adamw(f(w), f(g), f(m), f(v), name))


def _pack(parts):
    flat = jnp.concatenate([a.reshape(-1).astype(F32) for a in parts])
    n = flat.shape[0]
    npad = -n % (PACK_ROWS * LANE)
    return jnp.pad(flat, (0, npad)).reshape(-1, LANE)


def _unpack(buf, shapes, lead=()):
    flat = buf.reshape(lead + (-1,))
    out, off = [], 0
    for s in shapes:
        n = math.prod(s)
        out.append(flat[..., off:off + n].reshape(lead + tuple(s)))
        off += n
    return out


ROPE_THETA = 10000.0

_SMALL = ("norm_mix_pre", "dn_conv_w", "dn_a_log", "dn_dt_bias", "dn_norm_w", "pool_w", "pool_scale", "swa_sinks",
          "norm_mix_post", "norm_ffn_pre", "ffn_conv_w", "ffn_conv_b", "norm_ffn_post")
_BIG = ("w_in", "w_out", "ffn_w_up", "ffn_w_down")
_ORDER = ("norm_mix_pre", "w_in", "dn_conv_w", "dn_a_log", "dn_dt_bias", "dn_norm_w", "pool_w", "pool_scale",
          "swa_sinks", "w_out", "norm_mix_post", "norm_ffn_pre", "ffn_w_up", "ffn_conv_w", "ffn_conv_b",
          "ffn_w_down", "norm_ffn_post")


def _step(x, positions, loss_target, w, m, v):
    nl = w["w_in"].shape[0]
    t, d = x.shape[1], x.shape[2]
    nb = w["ffn_w_up"].shape[2]
    f = nb * N_DEV // 2
    me = 4 * lax.axis_index("x") + 2 * lax.axis_index("y") + lax.axis_index("c")
    x_in, tgt = x[0], loss_target[0]

    inv_freq = 1.0 / (ROPE_THETA ** (jnp.arange(0, HD, 2, dtype=F32) / HD))
    ang = positions[0].astype(F32)[:, None] * inv_freq
    cos, sin = jnp.cos(ang), jnp.sin(ang)
    cos2 = jnp.concatenate([cos, cos], axis=1)
    sin2 = jnp.concatenate([-sin, sin], axis=1)

    conv_shapes = [w["dn_conv_w"].shape, w["ffn_conv_w"].shape]
    gathered_conv = all_gather(_pack([w["dn_conv_w"], w["ffn_conv_w"]]), "ag_conv")
    dn_cw_g, ffn_cw_g = _unpack(gathered_conv, conv_shapes, lead=(N_DEV,))
    dn_cw = jnp.moveaxis(dn_cw_g, 0, 2).reshape(nl, DN_K, _QKV_W)
    ffn_cw = jnp.moveaxis(ffn_cw_g, 0, 1).reshape(nl, 2, N_DEV // 2, 3, nb)
    ffn_cb = w["ffn_conv_b"].reshape(nl, 2, N_DEV // 2, 1, nb)

    def lane_row(vec, off):
        return jnp.zeros((LANE,), F32).at[off:off + vec.shape[0]].set(vec)

    dn_par = jnp.stack([
        jnp.zeros((SUB, LANE), F32).at[0].set(lane_row(w["dn_a_log"][l], DN_H))
        .at[1].set(lane_row(w["dn_dt_bias"][l], DN_H)).at[2].set(w["dn_norm_w"][l]) for l in range(nl)])
    sinks = jnp.stack([lane_row(w["swa_sinks"][l], 0)[None, :] for l in range(nl)])

    def place(shard):
        return lax.dynamic_update_slice(lax.empty((N_DEV,) + shard.shape, shard.dtype), shard[None], (me, 0, 0))

    kinds = ("w_in", "w_out", "ffn_w_up", "ffn_w_down")
    started = {}

    def start(i, after):
        if i >= 4 * nl:
            return 0.0
        l, k = divmod(i, 4)
        shard = _align_in(w[kinds[k]][l]) if k == 0 else w[kinds[k]][l]
        sems, bufs, token = gather_start([place(shard.astype(BF16))], after, f"ag_start_{kinds[k]}_{l}")
        started[i] = (sems[0], bufs[0])
        return token[0, 0]

    def gathered(l, k, after):
        i = 4 * l + k
        got = transfer_wait(started[i][0], [started[i][1]], after, f"ag_wait_{kinds[k]}_{l}")[0]
        return got, (start(1, got) if i == 0 else 0.0) + start(i + 2, got)

    win, wout, wup, wdown = [None] * nl, [None] * nl, [None] * nl, [None] * nl
    row = lambda a, l: a[l][None, :]
    g1, g2, g3, g4 = w["norm_mix_pre"], w["norm_mix_post"], w["norm_ffn_pre"], w["norm_ffn_post"]

    saved = []
    xl = x_in
    h1 = norm_first(xl, row(g1, 0) + start(0, gathered_conv))
    for l in range(nl):
        buf, tk = gathered(l, 0, h1)
        win[l] = buf.reshape(d, PW)
        p = mm_nn(h1, win[l], F32, "mm_in")
        y_dn, states = dn_fwd(p, dn_cw[l], dn_par[l] + tk)
        y_pool = pool_fwd(p, w["pool_w"][l], row(w["pool_scale"], l))
        y_swa = swa_fwd(p, cos2, sin2, sinks[l])
        c = jnp.concatenate([y_dn, y_swa, y_pool], axis=1)
        buf, tk = gathered(l, 1, c)
        wout[l] = _perm_mix_rows(buf.reshape(MIX_W, d))
        mix = mm_nn(c, wout[l], F32, "mm_out")
        x1, h2 = post_pre(xl, mix, row(g2, l) + tk, row(g3, l))
        wup[l], tk = gathered(l, 2, h2)
        u0 = mm_up(h2, wup[l], "mm_up")
        act = glu_fwd(u0, ffn_cw[l], ffn_cb[l] + tk)
        buf, tk = gathered(l, 3, act)
        wdown[l] = buf.reshape(f, d)
        fo = mm_nn(act, wdown[l], F32, "mm_down")
        saved.append(dict(x=xl, h1=h1, p=p, states=states, c=c, mix=mix, x1=x1, h2=h2, u0=u0, act=act, f=fo))
        if l < nl - 1:
            xl, h1 = post_pre(x1, fo, row(g4, l) + tk, row(g1, l + 1))
        else:
            dx, loss_part = post_loss(x1, fo, row(g4, l) + tk, tgt)

    small_g = [dict() for _ in range(nl)]
    pending = {name: [None] * nl for name in _BIG}

    def exchange(name, l, dw):
        sems, src, land, token = exchange_start(dw, f"xch_start_{name}_{l}")
        pending[name][l] = (sems, src, land)
        return token[0, 0]

    df, small_g[nl - 1]["norm_ffn_post"] = bwd_norms(dx, post=(saved[-1]["f"], row(g4, nl - 1)))
    for l in range(nl - 1, -1, -1):
        s, sg = saved[l], small_g[l]
        dact = mm_nt(df, wdown[l], F32, "mm_down_d")
        tk = exchange("ffn_w_down", l, mm_tn(s["act"], df, BF16, "mm_down_w").reshape(N_DEV, f // N_DEV, d))
        du0, dcw = glu_bwd(dact, s["u0"], ffn_cw[l], ffn_cb[l])
        sg["ffn_conv"] = dcw
        dh2 = mm_up_dgrad(du0, wup[l], "mm_up_d")
        tk = tk + exchange("ffn_w_up", l, mm_up_wgrad(s["h2"], du0, "mm_up_w"))
        dx1, sg["norm_ffn_pre"], dmix, sg["norm_mix_post"] = bwd_norms(
            dx, pre=(dh2, s["x1"], row(g3, l) + tk), post=(s["mix"], row(g2, l)))
        dc = mm_nt(dmix, wout[l], F32, "mm_out_d")
        tk = exchange("w_out", l, _unperm_mix_rows(mm_tn(s["c"], dmix, BF16, "mm_out_w"))
                      .reshape(N_DEV, MIX_W // N_DEV, d))
        dqkv, dz, dgate, sg["dn_conv_w"], sg["dn_par"] = dn_bwd(s["p"], dc, s["states"], dn_cw[l], dn_par[l])
        dpool, sg["pool_w"], sg["pool_scale"] = pool_bwd(s["p"], dc, w["pool_w"][l], row(w["pool_scale"], l))
        dsq, dsk, dsv, sg["swa_sinks"] = swa_bwd(s["p"], dc, cos2, sin2, sinks[l])
        dp = jnp.concatenate([dqkv, dz, dsq, dsk, dsv, dpool, dgate], axis=1)
        dh1 = mm_nt(dp, win[l], F32, "mm_in_d")
        tk = tk + exchange("w_in", l, mm_tn(s["h1"], dp, BF16, "mm_in_w").reshape(N_DEV, d // N_DEV, PW))
        if l > 0:
            dx, sg["norm_mix_pre"], df, small_g[l - 1]["norm_ffn_post"] = bwd_norms(
                dx1, pre=(dh1, s["x"], row(g1, l) + tk), post=(saved[l - 1]["f"], row(g4, l - 1)))
        else:
            grad_x, sg["norm_mix_pre"] = bwd_norms(dx1, pre=(dh1, s["x"], row(g1, 0) + tk))

    keys = ("norm_mix_pre", "norm_mix_post", "norm_ffn_pre", "norm_ffn_post", "dn_conv_w", "dn_par", "pool_w",
            "pool_scale", "swa_sinks", "ffn_conv")
    grads = {}
    parts = [small_g[l][k] for l in range(nl) for k in keys] + [loss_part]
    shapes = [a.shape for a in parts]
    summed = sum_slabs(all_gather(_pack(parts), "ag_small"), "sum_small")
    vals = _unpack(summed, shapes)
    loss = vals[-1][0, 0]
    sm = [dict(zip(keys, vals[l * len(keys):(l + 1) * len(keys)])) for l in range(nl)]
    st = lambda fn: jnp.stack([fn(sm[l]) for l in range(nl)])
    for k in ("norm_mix_pre", "norm_mix_post", "norm_ffn_pre", "norm_ffn_post"):
        grads[k] = st(lambda q: q[k][0])
    grads["dn_conv_w"] = lax.dynamic_slice_in_dim(st(lambda q: q["dn_conv_w"][0:DN_K]), me * (_QKV_W // N_DEV),
                                                  _QKV_W // N_DEV, axis=2)
    grads["dn_a_log"] = st(lambda q: q["dn_par"][0, DN_H:2 * DN_H])
    grads["dn_dt_bias"] = st(lambda q: q["dn_par"][1, DN_H:2 * DN_H])
    grads["dn_norm_w"] = st(lambda q: q["dn_par"][2])
    grads["pool_w"] = st(lambda q: q["pool_w"])
    grads["pool_scale"] = st(lambda q: q["pool_scale"][0])
    grads["swa_sinks"] = st(lambda q: q["swa_sinks"][0, 0:SWA_H])
    conv_all = st(lambda q: q["ffn_conv"].reshape(N_DEV, SUB, nb))
    grads["ffn_conv_w"] = lax.dynamic_index_in_dim(conv_all, me, axis=1, keepdims=False)[:, 0:3, :]
    grads["ffn_conv_b"] = conv_all[:, :, 3, :].reshape(nl, 2 * f)

    delta, new_m, new_v = {}, {}, {}
    shapes = [w[k].shape for k in _SMALL]
    pk = lambda tree: _pack([tree[k] for k in _SMALL])
    outs = adamw(pk(w), pk(grads), pk(m), pk(v), "adamw_small")
    for tree, buf in zip((delta, new_m, new_v), outs):
        for k, a in zip(_SMALL, _unpack(buf, shapes)):
            tree[k] = a
    me_arr = jnp.reshape(me, (1,)).astype(jnp.int32)
    after = outs[0]
    for name in ("ffn_w_down", "ffn_w_up", "w_out", "w_in"):
        per_layer = []
        for l in range(nl - 1, -1, -1):
            sems, src, land = pending[name][l]
            src, land = transfer_wait(sems, [src, land], after, f"xch_wait_{name}_{l}")
            per_layer.append(sum_slabs(land, "sum_" + name, own=src, me=me_arr))
        g = jnp.stack(per_layer[::-1])
        grads[name] = _unalign_in(g) if name == "w_in" else g
        delta[name], new_m[name], new_v[name] = _adamw_nd(w[name], grads[name], m[name], v[name], "adamw_" + name)
        after = delta[name]

    return (loss, grad_x[None], *[grads[k] for k in _ORDER], *[delta[k] for k in _ORDER],
            *[new_m[k] for k in _ORDER], *[new_v[k] for k in _ORDER])


def kernel(x, positions, norm_mix_pre, w_in, dn_conv_w, dn_a_log, dn_dt_bias, dn_norm_w, pool_w, pool_scale, swa_sinks, w_out, norm_mix_post, norm_ffn_pre, ffn_w_up, ffn_conv_w, ffn_conv_b, ffn_w_down, norm_ffn_post, loss_target, m_norm_mix_pre, m_w_in, m_dn_conv_w, m_dn_a_log, m_dn_dt_bias, m_dn_norm_w, m_pool_w, m_pool_scale, m_swa_sinks, m_w_out, m_norm_mix_post, m_norm_ffn_pre, m_ffn_w_up, m_ffn_conv_w, m_ffn_conv_b, m_ffn_w_down, m_norm_ffn_post, v_norm_mix_pre, v_w_in, v_dn_conv_w, v_dn_a_log, v_dn_dt_bias, v_dn_norm_w, v_pool_w, v_pool_scale, v_swa_sinks, v_w_out, v_norm_mix_post, v_norm_ffn_pre, v_ffn_w_up, v_ffn_conv_w, v_ffn_conv_b, v_ffn_w_down, v_norm_ffn_post):
    args = locals()
    w = {k: args[k] for k in _ORDER}
    m = {k: args["m_" + k] for k in _ORDER}
    v = {k: args["v_" + k] for k in _ORDER}
    return _step(x, positions, loss_target, w, m, v)
```

```python
import functools
import math

import jax
import jax.numpy as jnp
from jax import lax
from jax.experimental import pallas as pl
from jax.experimental.pallas import tpu as pltpu

F32 = jnp.float32
BF16 = jnp.bfloat16
MXU_DT = jnp.bfloat16
HI = lax.Precision.HIGHEST

N_DEV = 8
LANE = 128
SUB = 8
VMEM_LIMIT = 56 * 1024 * 1024
ROW_TILE = 512
NORM_TILE = 256
MM_TM, MM_TN, MM_TK = 512, 1664, 2816
MM_TN_NT = 2048
PACK_ROWS = 512

HD = 128
DN_H, DN_W, DN_K, CH = 6, 768, 4, 64
POOL_G = 4
SWA_H, SWA_KV, SWA_G, SWA_BLK = 6, 2, 3, 128
EPS = 1e-6
SCALE = HD ** -0.5
NEG = -1e30

O_QKV, O_Z, O_SQ, O_SK, O_SV, O_POOL, O_GATE, PW = 0, 2304, 3072, 3840, 4096, 4352, 4864, 4992
IN_W = 4876
MIX_W = 2048

ADAM_LR, ADAM_B1, ADAM_B2, ADAM_EPS, ADAM_WD, ADAM_STEP = 0.001, 0.9, 0.999, 1e-08, 0.01, 10


def _pick(n, cap, mult=LANE):
    best = None
    for d in range(mult, min(n, cap) + 1, mult):
        if n % d == 0:
            best = d
    return best if best is not None else n


def _cp(*sem):
    return pltpu.CompilerParams(dimension_semantics=sem, vmem_limit_bytes=VMEM_LIMIT)


def _dot(a, b, dims):
    return lax.dot_general(a.astype(MXU_DT), b.astype(MXU_DT), dims, preferred_element_type=F32)


_NN = (((1,), (0,)), ((), ()))
_NT = (((1,), (1,)), ((), ()))
_TN = (((0,), (0,)), ((), ()))


def _mm(a, b):
    return _dot(a, b, _NN)


def _mm_nt(a, b):
    return _dot(a, b, _NT)


def _mm_tn(a, b):
    return _dot(a, b, _TN)


def _mm_hi(a, b, dims=_NN):
    return lax.dot_general(a, b, dims, precision=HI, preferred_element_type=F32)


def _sigmoid(x):
    return jax.nn.sigmoid(x)


def _softplus(x):
    return jnp.maximum(x, 0.0) + jnp.log(1.0 + jnp.exp(-jnp.abs(x)))


def _align_in(w):
    pad = jnp.zeros(w.shape[:-1] + (PW - IN_W,), w.dtype)
    return jnp.concatenate([w[..., 0:3072], w[..., 3596:4364], w[..., 4364:4620], w[..., 4620:4876],
                            w[..., 3084:3596], w[..., 3072:3084], pad], axis=-1)


def _unalign_in(g):
    return jnp.concatenate([g[..., 0:3072], g[..., O_GATE:O_GATE + 12], g[..., O_POOL:O_POOL + 512],
                            g[..., O_SQ:O_SQ + 768], g[..., O_SK:O_SK + 256], g[..., O_SV:O_SV + 256]], axis=-1)


def _perm_mix_rows(w):
    return jnp.concatenate([w[0:768], w[1280:2048], w[768:1280]], axis=0)


def _unperm_mix_rows(w):
    return jnp.concatenate([w[0:768], w[1536:2048], w[768:1536]], axis=0)


def _mm_call(name, a, b, out_shape, grid, a_spec, b_spec, o_spec, dims, acc_shape):
    nk = grid[2]
    if nk == 1:
        def body_once(a_ref, b_ref, o_ref):
            o_ref[...] = _dot(a_ref[...], b_ref[...], dims).astype(o_ref.dtype)

        return pl.pallas_call(
            body_once, grid=grid, in_specs=[a_spec, b_spec], out_specs=o_spec, out_shape=out_shape,
            compiler_params=_cp("parallel", "parallel", "arbitrary"), name=name)(a, b)

    def body(a_ref, b_ref, o_ref, acc_ref):
        k = pl.program_id(2)

        @pl.when(k == 0)
        def _():
            acc_ref[...] = jnp.zeros_like(acc_ref)

        acc_ref[...] += _dot(a_ref[...], b_ref[...], dims)

        @pl.when(k == nk - 1)
        def _():
            o_ref[...] = acc_ref[...].astype(o_ref.dtype)

    return pl.pallas_call(
        body, grid=grid, in_specs=[a_spec, b_spec], out_specs=o_spec, out_shape=out_shape,
        scratch_shapes=[pltpu.VMEM(acc_shape, F32)],
        compiler_params=_cp("parallel", "parallel", "arbitrary"), name=name)(a, b)


def mm_nn(a, b, out_dtype, name):
    (m, k), n = a.shape, b.shape[1]
    tm, tn, tk = _pick(m, MM_TM, SUB), _pick(n, MM_TN), _pick(k, MM_TK)
    return _mm_call(name, a, b, jax.ShapeDtypeStruct((m, n), out_dtype), (m // tm, n // tn, k // tk),
                    pl.BlockSpec((tm, tk), lambda i, j, kk: (i, kk)),
                    pl.BlockSpec((tk, tn), lambda i, j, kk: (kk, j)),
                    pl.BlockSpec((tm, tn), lambda i, j, kk: (i, j)), _NN, (tm, tn))


def mm_nt(a, b, out_dtype, name):
    (m, k), n = a.shape, b.shape[0]
    tm, tn, tk = _pick(m, MM_TM, SUB), _pick(n, MM_TN_NT), _pick(k, MM_TK)
    return _mm_call(name, a, b, jax.ShapeDtypeStruct((m, n), out_dtype), (m // tm, n // tn, k // tk),
                    pl.BlockSpec((tm, tk), lambda i, j, kk: (i, kk)),
                    pl.BlockSpec((tn, tk), lambda i, j, kk: (j, kk)),
                    pl.BlockSpec((tm, tn), lambda i, j, kk: (i, j)), _NT, (tm, tn))


def mm_tn(a, b, out_dtype, name):
    (k, m), n = a.shape, b.shape[1]
    tm, tn, tk = _pick(m, MM_TM), _pick(n, MM_TN), _pick(k, MM_TK, SUB)
    return _mm_call(name, a, b, jax.ShapeDtypeStruct((m, n), out_dtype), (m // tm, n // tn, k // tk),
                    pl.BlockSpec((tk, tm), lambda i, j, kk: (kk, i)),
                    pl.BlockSpec((tk, tn), lambda i, j, kk: (kk, j)),
                    pl.BlockSpec((tm, tn), lambda i, j, kk: (i, j)), _TN, (tm, tn))


def mm_up(h, wblk, name):
    (t, d), (nblk, _, nb) = h.shape, wblk.shape
    tm, tk = _pick(t, MM_TM, SUB), _pick(d, MM_TK)
    hb = nblk // 2
    return _mm_call(name, h, wblk, jax.ShapeDtypeStruct((2, t, hb * nb), F32), (t // tm, nblk, d // tk),
                    pl.BlockSpec((tm, tk), lambda i, j, kk: (i, kk)),
                    pl.BlockSpec((None, tk, nb), lambda i, j, kk: (j, kk, 0)),
                    pl.BlockSpec((None, tm, nb), lambda i, j, kk: (j // hb, i, j % hb)), _NN, (tm, nb))


def mm_up_dgrad(du0, wblk, name):
    (_, t, _), (nblk, d, nb) = du0.shape, wblk.shape
    tm, tn = _pick(t, MM_TM, SUB), _pick(d, MM_TN_NT)
    hb = nblk // 2
    return _mm_call(name, du0, wblk, jax.ShapeDtypeStruct((t, d), F32), (t // tm, d // tn, nblk),
                    pl.BlockSpec((None, tm, nb), lambda i, j, kk: (kk // hb, i, kk % hb)),
                    pl.BlockSpec((None, tn, nb), lambda i, j, kk: (kk, j, 0)),
                    pl.BlockSpec((tm, tn), lambda i, j, kk: (i, j)), _NT, (tm, tn))


def mm_up_wgrad(h, du0, name):
    (t, d), (_, _, f) = h.shape, du0.shape
    nb = f // (N_DEV // 2)
    hb = N_DEV // 2
    tm, tk = _pick(d, MM_TM), _pick(t, MM_TK, SUB)
    return _mm_call(name, h, du0, jax.ShapeDtypeStruct((N_DEV, d, nb), BF16), (d // tm, N_DEV, t // tk),
                    pl.BlockSpec((tk, tm), lambda i, j, kk: (kk, i)),
                    pl.BlockSpec((None, tk, nb), lambda i, j, kk: (j // hb, kk, j % hb)),
                    pl.BlockSpec((None, tm, nb), lambda i, j, kk: (j, i, 0)), _TN, (tm, nb))


def _rms(x, w):
    r = lax.rsqrt(jnp.mean(x * x, axis=-1, keepdims=True) + EPS)
    return x * r * w


def _rms_bwd(dy, x, w):
    r = lax.rsqrt(jnp.mean(x * x, axis=-1, keepdims=True) + EPS)
    xh = x * r
    dxh = dy * w
    dx = r * (dxh - xh * jnp.mean(dxh * xh, axis=-1, keepdims=True))
    return dx, jnp.sum(dy * xh, axis=0, keepdims=True)


def _row_spec(tb, d):
    return pl.BlockSpec((tb, d), lambda i: (i, 0))


def _fix_spec(r, d):
    return pl.BlockSpec((r, d), lambda i: (0, 0))


def norm_first(x, w):
    t, d = x.shape
    tb = _pick(t, NORM_TILE, SUB)

    def body(x_ref, w_ref, h_ref):
        h_ref[...] = _rms(x_ref[...], w_ref[...]).astype(h_ref.dtype)

    return pl.pallas_call(body, grid=(t // tb,), in_specs=[_row_spec(tb, d), _fix_spec(1, d)],
                          out_specs=_row_spec(tb, d), out_shape=jax.ShapeDtypeStruct((t, d), BF16),
                          compiler_params=_cp("parallel"), name="norm_first")(x, w)


def post_pre(x, y, w_post, w_pre):
    t, d = x.shape
    tb = _pick(t, NORM_TILE, SUB)

    def body(x_ref, y_ref, wp_ref, wq_ref, xn_ref, h_ref):
        xn = x_ref[...] + _rms(y_ref[...], wp_ref[...])
        xn_ref[...] = xn
        h_ref[...] = _rms(xn, wq_ref[...]).astype(h_ref.dtype)

    return pl.pallas_call(
        body, grid=(t // tb,),
        in_specs=[_row_spec(tb, d), _row_spec(tb, d), _fix_spec(1, d), _fix_spec(1, d)],
        out_specs=[_row_spec(tb, d), _row_spec(tb, d)],
        out_shape=(jax.ShapeDtypeStruct((t, d), F32), jax.ShapeDtypeStruct((t, d), BF16)),
        compiler_params=_cp("parallel"), name="post_pre")(x, y, w_post, w_pre)


def post_loss(x, y, w_post, target):
    t, d = x.shape
    tb = _pick(t, NORM_TILE, SUB)

    def body(x_ref, y_ref, wp_ref, t_ref, g_ref, l_ref):
        err = x_ref[...] + _rms(y_ref[...], wp_ref[...]) - t_ref[...]
        g_ref[...] = err * (1.0 / d)

        @pl.when(pl.program_id(0) == 0)
        def _():
            l_ref[...] = jnp.zeros_like(l_ref)

        part = 0.5 * jnp.sum(jnp.mean(err * err, axis=-1, keepdims=True), axis=0, keepdims=True)
        l_ref[...] += jnp.broadcast_to(part, l_ref.shape)

    return pl.pallas_call(
        body, grid=(t // tb,),
        in_specs=[_row_spec(tb, d), _row_spec(tb, d), _fix_spec(1, d), _row_spec(tb, d)],
        out_specs=[_row_spec(tb, d), _fix_spec(1, LANE)],
        out_shape=(jax.ShapeDtypeStruct((t, d), F32), jax.ShapeDtypeStruct((1, LANE), F32)),
        compiler_params=_cp("arbitrary"), name="post_loss")(x, y, w_post, target)


def bwd_norms(dx_in, *, pre=None, post=None):
    t, d = dx_in.shape
    tb = _pick(t, NORM_TILE, SUB)
    has_pre, has_post = pre is not None, post is not None

    def body(*refs):
        refs = list(refs)
        dxi = refs.pop(0)
        if has_pre:
            dh, x, wq = refs.pop(0), refs.pop(0), refs.pop(0)
        if has_post:
            y, wp = refs.pop(0), refs.pop(0)
        first = pl.program_id(0) == 0
        dx = dxi[...]
        if has_pre:
            dxo, dwq = refs.pop(0), refs.pop(0)
            g, dw = _rms_bwd(dh[...], x[...], wq[...])
            dx = dx + g
            dxo[...] = dx

            @pl.when(first)
            def _():
                dwq[...] = jnp.zeros_like(dwq)

            dwq[...] += dw
        if has_post:
            dyo, dwp = refs.pop(0), refs.pop(0)
            g, dw = _rms_bwd(dx, y[...], wp[...])
            dyo[...] = g.astype(dyo.dtype)

            @pl.when(first)
            def _():
                dwp[...] = jnp.zeros_like(dwp)

            dwp[...] += dw

    ins, in_specs, outs, out_specs = [dx_in], [_row_spec(tb, d)], [], []
    if has_pre:
        ins += list(pre)
        in_specs += [_row_spec(tb, d), _row_spec(tb, d), _fix_spec(1, d)]
        outs += [jax.ShapeDtypeStruct((t, d), F32), jax.ShapeDtypeStruct((1, d), F32)]
        out_specs += [_row_spec(tb, d), _fix_spec(1, d)]
    if has_post:
        ins += list(post)
        in_specs += [_row_spec(tb, d), _fix_spec(1, d)]
        outs += [jax.ShapeDtypeStruct((t, d), BF16), jax.ShapeDtypeStruct((1, d), F32)]
        out_specs += [_row_spec(tb, d), _fix_spec(1, d)]
    name = "bwd_norms" + ("_pre" if has_pre else "") + ("_post" if has_post else "")
    return pl.pallas_call(body, grid=(t // tb,), in_specs=in_specs, out_specs=out_specs, out_shape=tuple(outs),
                          compiler_params=_cp("arbitrary"), name=name)(*ins)


GLU_ROWS = 32


def _ffn_conv_blk(blk, cw, cb):
    r = blk.shape[1] - SUB
    x0, x1, x2 = blk[:, 6:6 + r], blk[:, 7:7 + r], blk[:, 8:8 + r]
    return x0, x1, x2, cw[:, 0:1, :] * x0 + cw[:, 1:2, :] * x1 + cw[:, 2:3, :] * x2 + cb


def _glu_specs(tb, nb, hpb, row_of):
    tile = pl.BlockSpec((2, tb, nb), lambda j, i: (0, row_of(i), j))
    halo = pl.BlockSpec((2, SUB, nb), lambda j, i: (0, jnp.maximum(row_of(i) * hpb - 1, 0), j))
    cw = pl.BlockSpec((2, None, 3, nb), lambda j, i: (0, j, 0, 0))
    cb = pl.BlockSpec((2, None, 1, nb), lambda j, i: (0, j, 0, 0))
    return tile, halo, cw, cb


def glu_fwd(u0, cw, cb):
    _, t, f = u0.shape
    nb = cw.shape[-1]
    tb = _pick(t, ROW_TILE, SUB)
    nt, hpb = t // tb, tb // SUB

    def body(u, h, cwr, cbr, o_ref, e):
        i = pl.program_id(1)
        e[:, 0:SUB, :] = jnp.where(i > 0, h[...], 0.0)
        e[:, SUB:, :] = u[...]
        w, bias = cwr[...], cbr[...]

        def rows(g, carry):
            s = pl.multiple_of(g * GLU_ROWS, GLU_ROWS)
            for lg in range(nb // LANE):
                ls = slice(lg * LANE, (lg + 1) * LANE)
                ab = _ffn_conv_blk(e[:, pl.ds(s, GLU_ROWS + SUB), ls], w[:, :, ls], bias[:, :, ls])[3]
                a, b = ab[0], ab[1]
                o_ref[pl.ds(s, GLU_ROWS), ls] = (a * _sigmoid(a) * b).astype(o_ref.dtype)
            return carry

        lax.fori_loop(0, tb // GLU_ROWS, rows, 0)

    return pl.pallas_call(
        body, grid=(f // nb, nt), in_specs=list(_glu_specs(tb, nb, hpb, lambda i: i)),
        out_specs=pl.BlockSpec((tb, nb), lambda j, i: (i, j)),
        out_shape=jax.ShapeDtypeStruct((t, f), BF16),
        scratch_shapes=[pltpu.VMEM((2, tb + SUB, nb), F32)],
        compiler_params=_cp("parallel", "arbitrary"), name="glu_fwd")(u0, u0, cw, cb)


def glu_bwd(dact, u0, cw, cb):
    _, t, f = u0.shape
    nb = cw.shape[-1]
    tb = _pick(t, ROW_TILE, SUB)
    nt, hpb = t // tb, tb // SUB

    def body(d_ref, u, h, cwr, cbr, du_o, dc_o, e, x2):
        i = pl.program_id(1)
        r = nt - 1 - i
        e[:, 0:SUB, :] = jnp.where(r > 0, h[...], 0.0)
        e[:, SUB:, :] = u[...]
        w, bias = cwr[...], cbr[...]

        @pl.when(i == 0)
        def _():
            dc_o[...] = jnp.zeros_like(dc_o)
            x2[:, tb:, :] = jnp.zeros((2, SUB, nb), F32)

        fold = lambda v: jnp.sum(v.reshape(2, GLU_ROWS // SUB, SUB, LANE), axis=1)
        for lg in range(nb // LANE):
            ls = slice(lg * LANE, (lg + 1) * LANE)
            wl, bl = w[:, :, ls], bias[:, :, ls]

            def grads(g, acc):
                s = pl.multiple_of(g * GLU_ROWS, GLU_ROWS)
                x0, x1, xc, ab = _ffn_conv_blk(e[:, pl.ds(s, GLU_ROWS + SUB), ls], wl, bl)
                a, b = ab[0], ab[1]
                sa = _sigmoid(a)
                d = d_ref[pl.ds(s, GLU_ROWS), ls]
                x2[0, pl.ds(s, GLU_ROWS), ls] = d * b * (sa * (1.0 + a * (1.0 - sa)))
                x2[1, pl.ds(s, GLU_ROWS), ls] = d * (a * sa)
                du = x2[:, pl.ds(s, GLU_ROWS), ls]
                return (acc[0] + fold(du * x0), acc[1] + fold(du * x1), acc[2] + fold(du * xc), acc[3] + fold(du))

            zero = jnp.zeros((2, SUB, LANE), F32)
            acc = lax.fori_loop(0, tb // GLU_ROWS, grads, (zero, zero, zero, zero))
            for k in range(4):
                dc_o[:, k:k + 1, ls] += jnp.sum(acc[k], axis=1, keepdims=True)

            def transposed_conv(g, carry):
                s = pl.multiple_of(g * GLU_ROWS, GLU_ROWS)
                blk = x2[:, pl.ds(s, GLU_ROWS + SUB), ls]
                du_o[:, pl.ds(s, GLU_ROWS), ls] = (
                    wl[:, 2:3, :] * blk[:, 0:GLU_ROWS] + wl[:, 1:2, :] * blk[:, 1:1 + GLU_ROWS]
                    + wl[:, 0:1, :] * blk[:, 2:2 + GLU_ROWS]).astype(du_o.dtype)
                return carry

            lax.fori_loop(0, tb // GLU_ROWS, transposed_conv, 0)
        x2[:, tb:, :] = x2[:, 0:SUB, :]

    rev = lambda i: nt - 1 - i
    return pl.pallas_call(
        body, grid=(f // nb, nt),
        in_specs=[pl.BlockSpec((tb, nb), lambda j, i: (rev(i), j))] + list(_glu_specs(tb, nb, hpb, rev)),
        out_specs=[pl.BlockSpec((2, tb, nb), lambda j, i: (0, rev(i), j)),
                   pl.BlockSpec((2, None, SUB, nb), lambda j, i: (0, j, 0, 0))],
        out_shape=(jax.ShapeDtypeStruct((2, t, f), BF16), jax.ShapeDtypeStruct((2, f // nb, SUB, nb), F32)),
        scratch_shapes=[pltpu.VMEM((2, tb + SUB, nb), F32), pltpu.VMEM((2, tb + SUB, nb), F32)],
        compiler_params=_cp("arbitrary", "arbitrary"), name="glu_bwd")(dact, u0, u0, cw, cb)


POOL_HALO = 16
_PCOL = O_POOL // LANE
_CPOOL = 1536 // LANE


def _pool_sel(g, v2, v4, v8, v16):
    return jnp.where(g == 0, v2, jnp.where(g == 1, v4, jnp.where(g == 2, v8, v16)))


def _pool_cnt(g, t0, n):
    win = _pool_sel(g, 2, 4, 8, 16)
    tpos = t0 + lax.broadcasted_iota(jnp.int32, (n, 1), 0)
    return jnp.minimum(tpos + 1, win).astype(F32)


def _pool_core(e, g, t0, tb):
    s2 = e + pltpu.roll(e, 1, 0)
    s4 = s2 + pltpu.roll(s2, 2, 0)
    s8 = s4 + pltpu.roll(s4, 4, 0)
    s16 = s8 + pltpu.roll(s8, 8, 0)
    sw = _pool_sel(g, s2, s4, s8, s16)[POOL_HALO:]
    return sw / _pool_cnt(g, t0, tb) - e[POOL_HALO:]


def pool_fwd(p, pool_w, pool_scale):
    t = p.shape[0]
    tb = _pick(t, ROW_TILE, POOL_HALO)
    nt, hpb = t // tb, tb // POOL_HALO

    def body(x_ref, h_ref, w_ref, s_ref, o_ref):
        i, g = pl.program_id(0), pl.program_id(1)
        e = jnp.concatenate([jnp.where(i > 0, h_ref[...], 0.0), x_ref[...]], axis=0)
        yy = _pool_core(e, g, i * tb, tb)
        o_ref[...] = (_mm(yy, w_ref[...]) * s_ref[...]).astype(o_ref.dtype)

    return pl.pallas_call(
        body, grid=(nt, POOL_G),
        in_specs=[pl.BlockSpec((tb, LANE), lambda i, g: (i, _PCOL + g)),
                  pl.BlockSpec((POOL_HALO, LANE), lambda i, g: (jnp.maximum(i * hpb - 1, 0), _PCOL + g)),
                  pl.BlockSpec((None, LANE, LANE), lambda i, g: (g, 0, 0)),
                  pl.BlockSpec((1, LANE), lambda i, g: (0, g))],
        out_specs=pl.BlockSpec((tb, LANE), lambda i, g: (i, g)),
        out_shape=jax.ShapeDtypeStruct((t, POOL_G * LANE), BF16),
        compiler_params=_cp("parallel", "parallel"), name="pool_fwd")(p, p, pool_w, pool_scale)


def pool_bwd(p, dc, pool_w, pool_scale):
    t = p.shape[0]
    tb = _pick(t, ROW_TILE, POOL_HALO)
    nt, hpb = t // tb, tb // POOL_HALO
    n = tb + POOL_HALO

    def body(x_ref, h_ref, dy_ref, dn_ref, w_ref, s_ref, dx_o, dw_o, ds_o):
        g, i = pl.program_id(0), pl.program_id(1)
        e = jnp.concatenate([jnp.where(i > 0, h_ref[...], 0.0), x_ref[...]], axis=0)
        yy = _pool_core(e, g, i * tb, tb)
        w, sc, dy = w_ref[...], s_ref[...], dy_ref[...]

        @pl.when(i == 0)
        def _():
            dw_o[...] = jnp.zeros_like(dw_o)
            ds_o[...] = jnp.zeros_like(ds_o)

        ds_o[...] += jnp.sum(dy * _mm(yy, w), axis=0, keepdims=True)
        dw_o[...] += _mm_tn(yy, dy * sc)
        dye = jnp.concatenate([dy, jnp.where(i < nt - 1, dn_ref[...], 0.0)], axis=0) * sc
        dyy = _mm_nt(dye, w)
        z = dyy / _pool_cnt(g, i * tb, n)
        r2 = z + pltpu.roll(z, n - 1, 0)
        r4 = r2 + pltpu.roll(r2, n - 2, 0)
        r8 = r4 + pltpu.roll(r4, n - 4, 0)
        r16 = r8 + pltpu.roll(r8, n - 8, 0)
        dx_o[...] = (_pool_sel(g, r2, r4, r8, r16)[:tb] - dyy[:tb]).astype(dx_o.dtype)

    last = t // POOL_HALO - 1
    return pl.pallas_call(
        body, grid=(POOL_G, nt),
        in_specs=[pl.BlockSpec((tb, LANE), lambda g, i: (i, _PCOL + g)),
                  pl.BlockSpec((POOL_HALO, LANE), lambda g, i: (jnp.maximum(i * hpb - 1, 0), _PCOL + g)),
                  pl.BlockSpec((tb, LANE), lambda g, i: (i, _CPOOL + g)),
                  pl.BlockSpec((POOL_HALO, LANE), lambda g, i: (jnp.minimum((i + 1) * hpb, last), _CPOOL + g)),
                  pl.BlockSpec((None, LANE, LANE), lambda g, i: (g, 0, 0)),
                  pl.BlockSpec((1, LANE), lambda g, i: (0, g))],
        out_specs=[pl.BlockSpec((tb, LANE), lambda g, i: (i, g)),
                   pl.BlockSpec((None, LANE, LANE), lambda g, i: (g, 0, 0)),
                   pl.BlockSpec((1, LANE), lambda g, i: (0, g))],
        out_shape=(jax.ShapeDtypeStruct((t, POOL_G * LANE), BF16),
                   jax.ShapeDtypeStruct((POOL_G, LANE, LANE), F32),
                   jax.ShapeDtypeStruct((1, POOL_G * LANE), F32)),
        compiler_params=_cp("arbitrary", "arbitrary"), name="pool_bwd")(p, p, dc, dc, pool_w, pool_scale)


_QCOL, _KCOL, _VCOL = O_SQ // 768, O_SK // 256, O_SV // 256
_GQ = SWA_G * SWA_BLK


def _rope(x, c2, s2):
    return x * c2 + pltpu.roll(x, HD // 2, 1) * s2


def _rope_bwd(d, c2, s2):
    return d * c2 + pltpu.roll(d * s2, HD // 2, 1)


def _hs(x, h):
    return x[:, h * HD:(h + 1) * HD]


def _swa_group(q, kc, kp, vc, vp, c2c, s2c, c2p, s2p, sinks, h, blk):
    kcat = jnp.concatenate([_rope(_hs(kp, h), c2p, s2p), _rope(_hs(kc, h), c2c, s2c)], axis=0)
    vcat = jnp.concatenate([_hs(vp, h), _hs(vc, h)], axis=0)
    qs = jnp.concatenate([_rope(_hs(q, SWA_G * h + g), c2c, s2c) for g in range(SWA_G)], axis=0)
    s = _mm_nt(qs, kcat) * SCALE
    ii = lax.broadcasted_iota(jnp.int32, (_GQ, 2 * SWA_BLK), 0) & (SWA_BLK - 1)
    jj = lax.broadcasted_iota(jnp.int32, (_GQ, 2 * SWA_BLK), 1)
    lo = jnp.where(blk > 0, 0, SWA_BLK)
    s = jnp.where((jj > ii) & (jj <= ii + SWA_BLK) & (jj >= lo), s, NEG)
    sink = jnp.concatenate(
        [jnp.broadcast_to(sinks[:, SWA_G * h + g:SWA_G * h + g + 1], (SWA_BLK, 1)) for g in range(SWA_G)], axis=0)
    m = jnp.maximum(jnp.max(s, axis=1, keepdims=True), sink)
    p = jnp.exp(s - m)
    ps = jnp.exp(sink - m)
    l = jnp.sum(p, axis=1, keepdims=True) + ps
    return qs, kcat, vcat, p, ps, l


def _swa_specs(blk_of):
    cur = lambda w, c: pl.BlockSpec((SWA_BLK, w), lambda n: (blk_of(n), c))
    prev = lambda w, c: pl.BlockSpec((SWA_BLK, w), lambda n: (jnp.maximum(blk_of(n) - 1, 0), c))
    return [cur(768, _QCOL), cur(256, _KCOL), prev(256, _KCOL), cur(256, _VCOL), prev(256, _VCOL),
            cur(HD, 0), cur(HD, 0), prev(HD, 0), prev(HD, 0), pl.BlockSpec((1, LANE), lambda n: (0, 0))]


def swa_fwd(p, cos2, sin2, sinks):
    t = p.shape[0]

    def body(q_ref, kc, kp, vc, vp, c2c, s2c, c2p, s2p, sk_ref, o_ref):
        n = pl.program_id(0)
        for h in range(SWA_KV):
            _, _, vcat, pr, _, l = _swa_group(q_ref[...], kc[...], kp[...], vc[...], vp[...], c2c[...], s2c[...],
                                              c2p[...], s2p[...], sk_ref[...], h, n)
            o = _mm(pr, vcat) / l
            for g in range(SWA_G):
                hh = SWA_G * h + g
                o_ref[:, hh * HD:(hh + 1) * HD] = o[g * SWA_BLK:(g + 1) * SWA_BLK].astype(o_ref.dtype)

    return pl.pallas_call(
        body, grid=(t // SWA_BLK,), in_specs=_swa_specs(lambda n: n),
        out_specs=pl.BlockSpec((SWA_BLK, 768), lambda n: (n, 0)),
        out_shape=jax.ShapeDtypeStruct((t, 768), BF16),
        compiler_params=_cp("parallel"), name="swa_fwd")(p, p, p, p, p, cos2, sin2, cos2, sin2, sinks)


def swa_bwd(p, dc, cos2, sin2, sinks):
    t = p.shape[0]
    nb = t // SWA_BLK

    def body(q_ref, kc, kp, vc, vp, c2c, s2c, c2p, s2p, sk_ref, do_ref, dq_o, dk_o, dv_o, dsk_o, ck, cv):
        i = pl.program_id(0)
        r = nb - 1 - i

        @pl.when(i == 0)
        def _():
            ck[...] = jnp.zeros_like(ck)
            cv[...] = jnp.zeros_like(cv)
            dsk_o[...] = jnp.zeros_like(dsk_o)

        lane = lax.broadcasted_iota(jnp.int32, (1, LANE), 1)
        dsk = jnp.zeros((1, LANE), F32)
        do = do_ref[...]
        for h in range(SWA_KV):
            qs, kcat, vcat, pr, ps, l = _swa_group(q_ref[...], kc[...], kp[...], vc[...], vp[...], c2c[...],
                                                   s2c[...], c2p[...], s2p[...], sk_ref[...], h, r)
            pn = pr / l
            dos = jnp.concatenate([_hs(do, SWA_G * h + g) for g in range(SWA_G)], axis=0)
            dp = _mm_nt(dos, vcat)
            delta = jnp.sum(pn * dp, axis=1, keepdims=True)
            ds = pn * (dp - delta)
            dsr = -(ps / l) * delta
            for g in range(SWA_G):
                tot = jnp.sum(dsr[g * SWA_BLK:(g + 1) * SWA_BLK], axis=0, keepdims=True)
                dsk = dsk + jnp.where(lane == SWA_G * h + g, tot, 0.0)
            dqs = _mm(ds, kcat) * SCALE
            for g in range(SWA_G):
                hh = SWA_G * h + g
                dq_o[:, hh * HD:(hh + 1) * HD] = _rope_bwd(dqs[g * SWA_BLK:(g + 1) * SWA_BLK], c2c[...],
                                                          s2c[...]).astype(dq_o.dtype)
            dk = _mm_tn(ds, qs) * SCALE
            dv = _mm_tn(pn, dos)
            cs = slice(h * HD, (h + 1) * HD)
            dk_o[:, cs] = (_rope_bwd(dk[SWA_BLK:], c2c[...], s2c[...]) + ck[:, cs]).astype(dk_o.dtype)
            dv_o[:, cs] = (dv[SWA_BLK:] + cv[:, cs]).astype(dv_o.dtype)
            ck[:, cs] = _rope_bwd(dk[:SWA_BLK], c2p[...], s2p[...])
            cv[:, cs] = dv[:SWA_BLK]
        dsk_o[...] += dsk

    rev = lambda n: nb - 1 - n
    return pl.pallas_call(
        body, grid=(nb,),
        in_specs=_swa_specs(rev) + [pl.BlockSpec((SWA_BLK, 768), lambda n: (rev(n), 1))],
        out_specs=[pl.BlockSpec((SWA_BLK, 768), lambda n: (rev(n), 0)),
                   pl.BlockSpec((SWA_BLK, 256), lambda n: (rev(n), 0)),
                   pl.BlockSpec((SWA_BLK, 256), lambda n: (rev(n), 0)),
                   pl.BlockSpec((1, LANE), lambda n: (0, 0))],
        out_shape=(jax.ShapeDtypeStruct((t, 768), BF16), jax.ShapeDtypeStruct((t, 256), BF16),
                   jax.ShapeDtypeStruct((t, 256), BF16), jax.ShapeDtypeStruct((1, LANE), F32)),
        scratch_shapes=[pltpu.VMEM((SWA_BLK, 256), F32), pltpu.VMEM((SWA_BLK, 256), F32)],
        compiler_params=_cp("arbitrary"), name="swa_bwd")(p, p, p, p, p, cos2, sin2, cos2, sin2, sinks, dc)


_ZCOL, _GCOL = O_Z // DN_W, O_GATE // LANE
_QKV_W = 3 * DN_W
_INV_STEPS = int(math.log2(CH)) - 1


class _Bag(dict):
    __getattr__ = dict.__getitem__


DN_CB = 4
_DN_ROWS = DN_CB * CH
_CH_SHIFT = CH.bit_length() - 1


def _dn_consts():
    ii = lax.broadcasted_iota(jnp.int32, (CH, CH), 0)
    jj = lax.broadcasted_iota(jnp.int32, (CH, CH), 1)
    bi = lax.broadcasted_iota(jnp.int32, (_DN_ROWS, _DN_ROWS), 0)
    bj = lax.broadcasted_iota(jnp.int32, (_DN_ROWS, _DN_ROWS), 1)
    same_chunk = jnp.right_shift(bi, _CH_SHIFT) == jnp.right_shift(bj, _CH_SHIFT)
    return _Bag(lower=ii >= jj, strict=ii > jj, diag=ii == jj,
                eye=jnp.where(ii == jj, 1.0, 0.0).astype(F32),
                tril_blk=jnp.where(same_chunk & (bi >= bj), 1.0, 0.0).astype(F32),
                ones=jnp.ones((CH, CH), F32), ones_w=jnp.ones((CH, LANE), F32),
                rows=lax.broadcasted_iota(jnp.int32, (CH, 1), 0),
                lane=lax.broadcasted_iota(jnp.int32, (1, LANE), 1))


def _dn_conv(ext_ref, cw):
    return (cw[0:1, :] * ext_ref[pl.ds(5, _DN_ROWS), :] + cw[1:2, :] * ext_ref[pl.ds(6, _DN_ROWS), :]
            + cw[2:3, :] * ext_ref[pl.ds(7, _DN_ROWS), :] + cw[3:4, :] * ext_ref[pl.ds(8, _DN_ROWS), :])


def _dn_gates(gt, arow, drow, c):
    beta = _sigmoid(gt)
    ea = jnp.exp(arow)
    xa = gt + drow
    g = -ea * _softplus(xa)
    return beta, g, _mm_hi(c.tril_blk, g), ea, _sigmoid(xa)


def _blk(a, ci, j):
    return a[ci * CH:(ci + 1) * CH, j * HD:(j + 1) * HD]


def _lockstep(gens):
    out, live = [None] * len(gens), list(range(len(gens)))
    while live:
        still = []
        for i in live:
            try:
                next(gens[i])
                still.append(i)
            except StopIteration as stop:
                out[i] = stop.value
        live = still
    return out


def _dn_head_a(qh, kh, vh, beta, gc, c):
    rq = lax.rsqrt(jnp.sum(qh * qh, axis=1, keepdims=True) + EPS)
    rk = lax.rsqrt(jnp.sum(kh * kh, axis=1, keepdims=True) + EPS)
    qn = qh * rq * SCALE
    kn = kh * rk
    kb = kn * beta
    vb = vh * beta
    gcol = _mm_hi(c.ones, jnp.where(c.diag, gc, 0.0))
    kk = _mm_nt(kb, kn)
    qk = _mm_nt(qn, kn)
    yield
    gam = jnp.where(c.lower, jnp.exp(jnp.minimum(gc - gcol, 0.0)), 0.0)
    lmat = jnp.where(c.strict, kk * gam, 0.0)
    amat = qk * gam
    nil = -lmat
    inv = c.eye + nil
    powk = nil
    for _ in range(_INV_STEPS):
        powk = _mm(powk, powk)
        yield
        inv = _mm(inv, c.eye + powk)
    eg = jnp.exp(gc)
    kbe = kb * eg
    yield
    u = _mm(inv, vb)
    w = _mm(inv, kbe)
    gl = gc[CH - 1:CH, :]
    e2 = jnp.exp(gl - gc)
    cd = jnp.exp(gl)
    qd = qn * eg
    kd = kn * e2
    return _Bag(rq=rq, rk=rk, qn=qn, kn=kn, kb=kb, vb=vb, gam=gam, lmat=lmat, inv=inv, eg=eg, kbe=kbe, u=u, w=w,
                amat=amat, e2=e2, cd=cd, qd=qd, kd=kd)


def _dn_head_b(f, s0):
    ws = _mm(f.w, s0)
    qs = _mm(f.qd, s0)
    yield
    vnew = f.u - ws
    return vnew, qs + _mm(f.amat, vnew), s0 * f.cd + _mm_tn(f.kd, vnew)


def _dn_post(o, zh, nw):
    ro = lax.rsqrt(jnp.mean(o * o, axis=1, keepdims=True) + EPS)
    oh = o * ro
    sz = _sigmoid(zh)
    return ro, oh, sz, oh * nw * (zh * sz)


def _dn_post_bwd(o, zh, nw, dy):
    ro, oh, sz, _ = _dn_post(o, zh, nw)
    don = dy * (zh * sz)
    dz = dy * (oh * nw) * (sz * (1.0 + zh * (1.0 - sz)))
    doh = don * nw
    return (ro * (doh - oh * jnp.mean(doh * oh, axis=1, keepdims=True)), dz,
            jnp.sum(don * oh, axis=0, keepdims=True))


def _dn_head_bwd_b(f, vnew, do, dsn, s0):
    a_do = _mm_tn(f.amat, do)
    kd_ds = _mm(f.kd, dsn)
    qd_do = _mm_tn(f.qd, do)
    dkd = _mm_nt(vnew, dsn)
    yield
    dvnew = a_do + kd_ds
    ds0 = qd_do + f.cd * dsn - _mm_tn(f.w, dvnew)
    dcd = jnp.sum(jnp.sum(s0 * dsn, axis=1, keepdims=True), axis=0, keepdims=True)
    return dvnew, ds0, dkd, dcd


def _dn_head_bwd_c(f, vnew, do, dvnew, dkd, dcd, s0, qh, vh, beta, c):
    da = jnp.where(c.lower, _mm_nt(do, vnew), 0.0)
    dqd = _mm_nt(do, s0)
    dw = -_mm_nt(dvnew, s0)
    dt_u = _mm_nt(dvnew, f.vb)
    dvb = _mm_tn(f.inv, dvnew)
    yield
    dt = dt_u + _mm_nt(dw, f.kbe)
    dkbe = _mm_tn(f.inv, dw)
    yield
    dt_inv = _mm_nt(dt, f.inv)
    yield
    dl = -jnp.where(c.strict, _mm_tn(f.inv, dt_inv), 0.0)
    yield
    dm = dl * f.gam
    dn = da * f.gam
    dkb = _mm(dm, f.kn) + dkbe * f.eg
    dkn = _mm_tn(dm, f.kb) + _mm_tn(dn, f.qn) + dkd * f.e2 + beta * dkb
    dqn = _mm(dn, f.kn) + dqd * f.eg
    pm = dl * f.lmat + da * f.amat
    colsum = _mm_hi(pm, c.ones_w, _TN)[:, 0:1]
    yield
    tkd = jnp.sum(dkd * f.kn, axis=1, keepdims=True) * f.e2
    dgc = (jnp.sum(pm, axis=1, keepdims=True) - colsum - tkd
           + (jnp.sum(dqd * f.qn, axis=1, keepdims=True) + jnp.sum(dkbe * f.kb, axis=1, keepdims=True)) * f.eg)
    dgl = jnp.sum(tkd, axis=0, keepdims=True) + dcd * f.cd
    dgc = dgc + jnp.where(c.rows == CH - 1, dgl, 0.0)
    dbeta = jnp.sum(dkb * f.kn, axis=1, keepdims=True) + jnp.sum(dvb * vh, axis=1, keepdims=True)
    dvh = beta * dvb
    qhat = qh * f.rq
    dqs = dqn * SCALE
    dqh = f.rq * (dqs - qhat * jnp.sum(qhat * dqs, axis=1, keepdims=True))
    dkh = f.rk * (dkn - f.kn * jnp.sum(f.kn * dkn, axis=1, keepdims=True))
    return dqh, dkh, dvh, dbeta, dgc


def _dn_in_specs(step_of):
    return [pl.BlockSpec((_DN_ROWS, _QKV_W), lambda n: (step_of(n), 0)),
            pl.BlockSpec((SUB, _QKV_W), lambda n: (jnp.maximum(step_of(n) * (_DN_ROWS // SUB) - 1, 0), 0)),
            pl.BlockSpec((_DN_ROWS, DN_W), lambda n: (step_of(n), _ZCOL)),
            pl.BlockSpec((_DN_ROWS, LANE), lambda n: (step_of(n), _GCOL)),
            pl.BlockSpec((DN_K, _QKV_W), lambda n: (0, 0)),
            pl.BlockSpec((SUB, LANE), lambda n: (0, 0))]


def _dn_heads_a(qkv, beta_all, gc_all, c):
    rows = lambda a, ci: a[ci * CH:(ci + 1) * CH]
    flat = _lockstep([_dn_head_a(_blk(qkv, ci, h), _blk(qkv, ci, DN_H + h), _blk(qkv, ci, 2 * DN_H + h),
                                 rows(beta_all, ci)[:, h:h + 1], rows(gc_all, ci)[:, DN_H + h:DN_H + h + 1], c)
                      for ci in range(DN_CB) for h in range(DN_H)])
    return [flat[ci * DN_H:(ci + 1) * DN_H] for ci in range(DN_CB)]


def dn_fwd(p, conv_w, par):
    t = p.shape[0]
    nc = t // CH
    assert t % _DN_ROWS == 0

    def body(x_ref, h_ref, z_ref, g_ref, cw_ref, par_ref, y_o, s_o, ext, st):
        n = pl.program_id(0)
        c = _dn_consts()

        @pl.when(n == 0)
        def _():
            st[...] = jnp.zeros_like(st)

        ext[0:SUB, :] = jnp.where(n > 0, h_ref[...], 0.0)
        ext[SUB:, :] = x_ref[...]
        pre = _dn_conv(ext, cw_ref[...])
        qkv = pre * _sigmoid(pre)
        par = par_ref[...]
        beta_all, _, gc_all, _, _ = _dn_gates(g_ref[...], par[0:1, :], par[1:2, :], c)
        z = z_ref[...]
        fa = _dn_heads_a(qkv, beta_all, gc_all, c)
        s = [st[h] for h in range(DN_H)]
        for ci in range(DN_CB):
            for h in range(DN_H):
                s_o[ci, h] = s[h]
            res = _lockstep([_dn_head_b(fa[ci][h], s[h]) for h in range(DN_H)])
            for h in range(DN_H):
                _, o, s[h] = res[h]
                y_o[ci * CH:(ci + 1) * CH, h * HD:(h + 1) * HD] = _dn_post(o, _blk(z, ci, h),
                                                                           par[2:3, :])[3].astype(y_o.dtype)
        for h in range(DN_H):
            st[h] = s[h]

    return pl.pallas_call(
        body, grid=(t // _DN_ROWS,), in_specs=_dn_in_specs(lambda n: n),
        out_specs=[pl.BlockSpec((_DN_ROWS, DN_W), lambda n: (n, 0)),
                   pl.BlockSpec((DN_CB, DN_H, HD, HD), lambda n: (n, 0, 0, 0))],
        out_shape=(jax.ShapeDtypeStruct((t, DN_W), BF16), jax.ShapeDtypeStruct((nc, DN_H, HD, HD), F32)),
        scratch_shapes=[pltpu.VMEM((_DN_ROWS + SUB, _QKV_W), F32), pltpu.VMEM((DN_H, HD, HD), F32)],
        compiler_params=_cp("arbitrary"), name="dn_fwd")(p, p, p, p, conv_w, par)


def dn_bwd(p, dc, states, conv_w, par):
    t = p.shape[0]
    ns = t // _DN_ROWS

    def body(x_ref, h_ref, z_ref, g_ref, cw_ref, par_ref, s_ref, dy_ref,
             dx_o, dz_o, dg_o, dcw_o, dpar_o, ext, dst, dpost, x2):
        i = pl.program_id(0)
        r = ns - 1 - i
        c = _dn_consts()

        @pl.when(i == 0)
        def _():
            dst[...] = jnp.zeros_like(dst)
            dcw_o[...] = jnp.zeros_like(dcw_o)
            dpar_o[...] = jnp.zeros_like(dpar_o)
            x2[_DN_ROWS:, :] = jnp.zeros((SUB, _QKV_W), F32)

        ext[0:SUB, :] = jnp.where(r > 0, h_ref[...], 0.0)
        ext[SUB:, :] = x_ref[...]
        cw = cw_ref[...]
        pre = _dn_conv(ext, cw)
        sg = _sigmoid(pre)
        qkv = pre * sg
        par = par_ref[...]
        gt = g_ref[...]
        beta_all, g_all, gc_all, ea, sxa = _dn_gates(gt, par[0:1, :], par[1:2, :], c)
        z, dy = z_ref[...], dy_ref[...]
        nw = par[2:3, :]
        dnw = jnp.zeros((1, LANE), F32)
        pairs = [(ci, h) for ci in range(DN_CB) for h in range(DN_H)]
        fa = _dn_heads_a(qkv, beta_all, gc_all, c)
        vnew, do = {}, {}
        fwd = _lockstep([_dn_head_b(fa[ci][h], s_ref[ci, h]) for ci, h in pairs])
        for (ci, h), (vn, o, _) in zip(pairs, fwd):
            vnew[ci, h] = vn
            do[ci, h], dz, dnw_h = _dn_post_bwd(o, _blk(z, ci, h), nw, _blk(dy, ci, h))
            dnw = dnw + dnw_h
            dz_o[ci * CH:(ci + 1) * CH, h * HD:(h + 1) * HD] = dz.astype(dz_o.dtype)
        ds = [dst[h] for h in range(DN_H)]
        seq = {}
        for ci in range(DN_CB - 1, -1, -1):
            res = _lockstep([_dn_head_bwd_b(fa[ci][h], vnew[ci, h], do[ci, h], ds[h], s_ref[ci, h])
                             for h in range(DN_H)])
            for h in range(DN_H):
                dvnew, ds[h], dkd, dcd = res[h]
                seq[ci, h] = (dvnew, dkd, dcd)
        for h in range(DN_H):
            dst[h] = ds[h]
        rest = _lockstep([_dn_head_bwd_c(
            fa[ci][h], vnew[ci, h], do[ci, h], *seq[ci, h], s_ref[ci, h], _blk(qkv, ci, h),
            _blk(qkv, ci, 2 * DN_H + h), beta_all[ci * CH:(ci + 1) * CH, h:h + 1], c) for ci, h in pairs])
        dbeta_rows, dgc_rows = [], []
        for ci in range(DN_CB):
            dbeta_c = jnp.zeros((CH, LANE), F32)
            dgc_c = jnp.zeros((CH, LANE), F32)
            for h in range(DN_H):
                dqh, dkh, dvh, dbeta, dgc = rest[ci * DN_H + h]
                dbeta_c = dbeta_c + jnp.where(c.lane == h, dbeta, 0.0)
                dgc_c = dgc_c + jnp.where(c.lane == DN_H + h, dgc, 0.0)
                rs = slice(ci * CH, (ci + 1) * CH)
                dpost[rs, h * HD:(h + 1) * HD] = dqh
                dpost[rs, (DN_H + h) * HD:(DN_H + h + 1) * HD] = dkh
                dpost[rs, (2 * DN_H + h) * HD:(2 * DN_H + h + 1) * HD] = dvh
            dbeta_rows.append(dbeta_c)
            dgc_rows.append(dgc_c)
        dbeta_all = jnp.concatenate(dbeta_rows, axis=0)
        dgc_all = jnp.concatenate(dgc_rows, axis=0)
        dg_all = _mm_hi(c.tril_blk, dgc_all, _TN)
        dpa = dg_all * (-ea) * sxa
        dpb = dbeta_all * beta_all * (1.0 - beta_all)
        is_b = c.lane < DN_H
        is_a = (c.lane >= DN_H) & (c.lane < 2 * DN_H)
        dg_o[...] = jnp.where(is_b, dpb, jnp.where(is_a, dpa, 0.0)).astype(dg_o.dtype)
        dpar_o[0:1, :] += jnp.where(is_a, jnp.sum(dg_all * g_all, axis=0, keepdims=True), 0.0)
        dpar_o[1:2, :] += jnp.where(is_a, jnp.sum(dpa, axis=0, keepdims=True), 0.0)
        dpar_o[2:3, :] += dnw
        dpre = dpost[...] * (sg * (1.0 + pre * (1.0 - sg)))
        for k in range(DN_K):
            dcw_o[k:k + 1, :] += jnp.sum(dpre * ext[pl.ds(5 + k, _DN_ROWS), :], axis=0, keepdims=True)
        x2[0:_DN_ROWS, :] = dpre
        dx_o[...] = (cw[3:4, :] * dpre + cw[2:3, :] * x2[pl.ds(1, _DN_ROWS), :]
                     + cw[1:2, :] * x2[pl.ds(2, _DN_ROWS), :]
                     + cw[0:1, :] * x2[pl.ds(3, _DN_ROWS), :]).astype(dx_o.dtype)
        x2[_DN_ROWS:, :] = dpre[0:SUB, :]

    rev = lambda n: ns - 1 - n
    return pl.pallas_call(
        body, grid=(ns,),
        in_specs=_dn_in_specs(rev) + [pl.BlockSpec((DN_CB, DN_H, HD, HD), lambda n: (rev(n), 0, 0, 0)),
                                      pl.BlockSpec((_DN_ROWS, DN_W), lambda n: (rev(n), 0))],
        out_specs=[pl.BlockSpec((_DN_ROWS, _QKV_W), lambda n: (rev(n), 0)),
                   pl.BlockSpec((_DN_ROWS, DN_W), lambda n: (rev(n), 0)),
                   pl.BlockSpec((_DN_ROWS, LANE), lambda n: (rev(n), 0)),
                   pl.BlockSpec((SUB, _QKV_W), lambda n: (0, 0)),
                   pl.BlockSpec((SUB, LANE), lambda n: (0, 0))],
        out_shape=(jax.ShapeDtypeStruct((t, _QKV_W), BF16), jax.ShapeDtypeStruct((t, DN_W), BF16),
                   jax.ShapeDtypeStruct((t, LANE), BF16), jax.ShapeDtypeStruct((SUB, _QKV_W), F32),
                   jax.ShapeDtypeStruct((SUB, LANE), F32)),
        scratch_shapes=[pltpu.VMEM((_DN_ROWS + SUB, _QKV_W), F32), pltpu.VMEM((DN_H, HD, HD), F32),
                        pltpu.VMEM((_DN_ROWS, _QKV_W), F32), pltpu.VMEM((_DN_ROWS + SUB, _QKV_W), F32)],
        compiler_params=_cp("arbitrary"), name="dn_bwd")(p, p, p, p, conv_w, par, states, dc)


_ANY = pl.BlockSpec(memory_space=pl.ANY)
_MESH = pl.DeviceIdType.MESH


def _me():
    return lax.axis_index("x"), lax.axis_index("y"), lax.axis_index("c")


def all_gather(x, name):
    def body(x_ref, out_ref, send_sems, recv_sems, local_sem):
        mx, my, mc = _me()
        me, sibling = (mx, my, mc), (mx, my, 1 - mc)
        chips = [(1 - mx, my), (mx, 1 - my), (1 - mx, 1 - my)]

        def slot(px, py, pc):
            return out_ref.at[4 * px + 2 * py + pc]

        def copy(k, block, to, src=None):
            return pltpu.make_async_remote_copy(
                src_ref=slot(*block) if src is None else src, dst_ref=slot(*block),
                send_sem=send_sems.at[k], recv_sem=recv_sems.at[k], device_id=to, device_id_type=_MESH)

        mine = pltpu.make_async_copy(x_ref, slot(*me), local_sem)
        mine.start()
        first = [copy(0, me, sibling, src=x_ref)]
        first += [copy(1 + j, me, (*chip, mc), src=x_ref) for j, chip in enumerate(chips)]
        for cp in first:
            cp.start()
        passed = [copy(4 + j, (*chip, mc), sibling) for j, chip in enumerate(chips)]
        for j, chip in enumerate(chips):
            copy(1 + j, (*chip, mc), me).wait_recv()
            passed[j].start()
        copy(0, sibling, me).wait_recv()
        for j, chip in enumerate(chips):
            copy(4 + j, (*chip, 1 - mc), me).wait_recv()
        for cp in first + passed:
            cp.wait_send()
        mine.wait()

    return pl.pallas_call(
        body, out_shape=jax.ShapeDtypeStruct((N_DEV,) + x.shape, x.dtype), in_specs=[_ANY], out_specs=_ANY,
        scratch_shapes=[pltpu.SemaphoreType.DMA((7,)), pltpu.SemaphoreType.DMA((7,)), pltpu.SemaphoreType.DMA],
        name=name)(x)


_HBM = pl.BlockSpec(memory_space=pltpu.HBM)
_SEM = pl.BlockSpec(memory_space=pltpu.SEMAPHORE)
_EFFECT = pltpu.SideEffectType.DATAFLOW_SIDE_EFFECTING
_TOKEN = jax.ShapeDtypeStruct((SUB, LANE), F32)


def _peers(mx, my, mc):
    for rel in range(1, N_DEV):
        yield (1 - mx if rel & 4 else mx, 1 - my if rel & 2 else my, 1 - mc if rel & 1 else mc)


def _in_hbm(a):
    return pltpu.with_memory_space_constraint(a, pltpu.HBM)


def gather_start(bufs, after, name):
    n = len(bufs)

    def body(*refs):
        ins, sems, token = refs[:n], refs[n + 1:3 * n + 1], refs[4 * n + 1]
        mx, my, mc = _me()
        me = 4 * mx + 2 * my + mc
        for b in range(n):
            for peer in _peers(mx, my, mc):
                pltpu.make_async_remote_copy(
                    src_ref=ins[b].at[me], dst_ref=ins[b].at[me], send_sem=sems[2 * b], recv_sem=sems[2 * b + 1],
                    device_id=peer, device_id_type=_MESH).start()
        token[...] = jnp.zeros_like(token)

    outs = pl.pallas_call(
        body, name=name,
        out_shape=tuple([pltpu.SemaphoreType.DMA(())] * (2 * n) + [pltpu.HBM(b.shape, b.dtype) for b in bufs]
                        + [_TOKEN]),
        in_specs=[_HBM] * n + [_ANY],
        out_specs=tuple([_SEM] * (2 * n) + [_HBM] * n + [pl.BlockSpec(memory_space=pltpu.VMEM)]),
        input_output_aliases={b: 2 * n + b for b in range(n)},
        compiler_params=pltpu.CompilerParams(has_side_effects=_EFFECT))(*[_in_hbm(b) for b in bufs], after)
    return [(outs[2 * b], outs[2 * b + 1]) for b in range(n)], list(outs[2 * n:3 * n]), outs[3 * n]


def exchange_start(src, name):
    def body(src_ref, land_ref, send_sem, recv_sem, src_thru, land_thru, token):
        mx, my, mc = _me()
        me = 4 * mx + 2 * my + mc
        for px, py, pc in _peers(mx, my, mc):
            pltpu.make_async_remote_copy(
                src_ref=src_ref.at[4 * px + 2 * py + pc], dst_ref=land_ref.at[me], send_sem=send_sem,
                recv_sem=recv_sem, device_id=(px, py, pc), device_id_type=_MESH).start()
        token[...] = jnp.zeros_like(token)

    hbm = pltpu.HBM(src.shape, src.dtype)
    send_sem, recv_sem, src_thru, land_thru, token = pl.pallas_call(
        body, name=name,
        out_shape=(pltpu.SemaphoreType.DMA(()), pltpu.SemaphoreType.DMA(()), hbm, hbm, _TOKEN),
        in_specs=[_HBM, _HBM], out_specs=(_SEM, _SEM, _HBM, _HBM, pl.BlockSpec(memory_space=pltpu.VMEM)),
        input_output_aliases={0: 2, 1: 3},
        compiler_params=pltpu.CompilerParams(has_side_effects=_EFFECT))(
            _in_hbm(src), _in_hbm(lax.empty(src.shape, src.dtype)))
    return (send_sem, recv_sem), src_thru, land_thru, token


def transfer_wait(sems, bufs, after, name):
    n = len(bufs)

    def body(*refs):
        seven = refs[0].at[pl.ds(0, N_DEV - 1)]
        cp = pltpu.make_async_remote_copy(src_ref=seven, dst_ref=seven, send_sem=refs[n], recv_sem=refs[n + 1],
                                          device_id=_me(), device_id_type=_MESH)
        cp.wait_send()
        cp.wait_recv()

    outs = pl.pallas_call(
        body, name=name, out_shape=tuple(pltpu.HBM(b.shape, b.dtype) for b in bufs),
        in_specs=[_HBM] * n + [_SEM, _SEM, _ANY], out_specs=tuple([_HBM] * n),
        input_output_aliases={b: b for b in range(n)},
        compiler_params=pltpu.CompilerParams(has_side_effects=_EFFECT))(*bufs, sems[0], sems[1], after)
    return list(outs)


def sum_slabs(x, name, own=None, me=None):
    _, r, c = x.shape
    tr = _pick(r, max(SUB, (1 << 19) // c // SUB * SUB), SUB)
    out_shape = jax.ShapeDtypeStruct((r, c), F32)
    if own is None:
        def body(x_ref, o_ref):
            acc = x_ref[0].astype(F32)
            for s in range(1, N_DEV):
                acc = acc + x_ref[s].astype(F32)
            o_ref[...] = acc

        return pl.pallas_call(
            body, grid=(r // tr,), in_specs=[pl.BlockSpec((N_DEV, tr, c), lambda i: (0, i, 0))],
            out_specs=pl.BlockSpec((tr, c), lambda i: (i, 0)), out_shape=out_shape,
            compiler_params=_cp("parallel"), name=name)(x)

    def body_own(me_ref, x_ref, own_ref, o_ref):
        acc = None
        for s in range(N_DEV):
            val = jnp.where(me_ref[0] == s, own_ref[...], x_ref[s]).astype(F32)
            acc = val if acc is None else acc + val
        o_ref[...] = acc

    return pl.pallas_call(
        body_own, out_shape=out_shape, name=name, compiler_params=_cp("parallel"),
        grid_spec=pltpu.PrefetchScalarGridSpec(
            num_scalar_prefetch=1, grid=(r // tr,),
            in_specs=[pl.BlockSpec((N_DEV, tr, c), lambda i, me_ref: (0, i, 0)),
                      pl.BlockSpec((None, tr, c), lambda i, me_ref: (me_ref[0], i, 0))],
            out_specs=pl.BlockSpec((tr, c), lambda i, me_ref: (i, 0))))(me, x, own)


def adamw(w, g, m, v, name):
    r, c = w.shape
    tr = _pick(r, max(SUB, (1 << 18) // c // SUB * SUB), SUB)
    c1 = 1.0 / (1.0 - ADAM_B1 ** ADAM_STEP)
    c2 = 1.0 / (1.0 - ADAM_B2 ** ADAM_STEP)

    def body(w_ref, g_ref, m_ref, v_ref, d_o, m_o, v_o):
        gg = g_ref[...]
        mn = ADAM_B1 * m_ref[...] + (1.0 - ADAM_B1) * gg
        vn = ADAM_B2 * v_ref[...] + (1.0 - ADAM_B2) * (gg * gg)
        m_o[...] = mn
        v_o[...] = vn
        d_o[...] = -ADAM_LR * ((mn * c1) / (jnp.sqrt(vn * c2) + ADAM_EPS) + ADAM_WD * w_ref[...])

    spec = pl.BlockSpec((tr, c), lambda i: (i, 0))
    sds = jax.ShapeDtypeStruct((r, c), F32)
    return pl.pallas_call(body, grid=(r // tr,), in_specs=[spec] * 4, out_specs=[spec] * 3, out_shape=(sds,) * 3,
                          compiler_params=_cp("parallel"), name=name)(w, g, m, v)


def _adamw_nd(w, g, m, v, name):
    shp = w.shape
    f = lambda a: a.reshape(-1, shp[-1])
    return tuple(o.reshape(shp) for o in adamw(f(w), f(g), f(m), f(v), name))


def _pack(parts):
    flat = jnp.concatenate([a.reshape(-1).astype(F32) for a in parts])
    n = flat.shape[0]
    npad = -n % (PACK_ROWS * LANE)
    return jnp.pad(flat, (0, npad)).reshape(-1, LANE)


def _unpack(buf, shapes, lead=()):
    flat = buf.reshape(lead + (-1,))
    out, off = [], 0
    for s in shapes:
        n = math.prod(s)
        out.append(flat[..., off:off + n].reshape(lead + tuple(s)))
        off += n
    return out


ROPE_THETA = 10000.0

_SMALL = ("norm_mix_pre", "dn_conv_w", "dn_a_log", "dn_dt_bias", "dn_norm_w", "pool_w", "pool_scale", "swa_sinks",
          "norm_mix_post", "norm_ffn_pre", "ffn_conv_w", "ffn_conv_b", "norm_ffn_post")
_BIG = ("w_in", "w_out", "ffn_w_up", "ffn_w_down")
_ORDER = ("norm_mix_pre", "w_in", "dn_conv_w", "dn_a_log", "dn_dt_bias", "dn_norm_w", "pool_w", "pool_scale",
          "swa_sinks", "w_out", "norm_mix_post", "norm_ffn_pre", "ffn_w_up", "ffn_conv_w", "ffn_conv_b",
          "ffn_w_down", "norm_ffn_post")


def _step(x, positions, loss_target, w, m, v):
    nl = w["w_in"].shape[0]
    t, d = x.shape[1], x.shape[2]
    nb = w["ffn_w_up"].shape[2]
    f = nb * N_DEV // 2
    me = 4 * lax.axis_index("x") + 2 * lax.axis_index("y") + lax.axis_index("c")
    x_in, tgt = x[0], loss_target[0]

    inv_freq = 1.0 / (ROPE_THETA ** (jnp.arange(0, HD, 2, dtype=F32) / HD))
    ang = positions[0].astype(F32)[:, None] * inv_freq
    cos, sin = jnp.cos(ang), jnp.sin(ang)
    cos2 = jnp.concatenate([cos, cos], axis=1)
    sin2 = jnp.concatenate([-sin, sin], axis=1)

    conv_shapes = [w["dn_conv_w"].shape, w["ffn_conv_w"].shape]
    gathered_conv = all_gather(_pack([w["dn_conv_w"], w["ffn_conv_w"]]), "ag_conv")
    dn_cw_g, ffn_cw_g = _unpack(gathered_conv, conv_shapes, lead=(N_DEV,))
    dn_cw = jnp.moveaxis(dn_cw_g, 0, 2).reshape(nl, DN_K, _QKV_W)
    ffn_cw = jnp.moveaxis(ffn_cw_g, 0, 1).reshape(nl, 2, N_DEV // 2, 3, nb)
    ffn_cb = w["ffn_conv_b"].reshape(nl, 2, N_DEV // 2, 1, nb)

    def lane_row(vec, off):
        return jnp.zeros((LANE,), F32).at[off:off + vec.shape[0]].set(vec)

    dn_par = jnp.stack([
        jnp.zeros((SUB, LANE), F32).at[0].set(lane_row(w["dn_a_log"][l], DN_H))
        .at[1].set(lane_row(w["dn_dt_bias"][l], DN_H)).at[2].set(w["dn_norm_w"][l]) for l in range(nl)])
    sinks = jnp.stack([lane_row(w["swa_sinks"][l], 0)[None, :] for l in range(nl)])

    def place(shard):
        return lax.dynamic_update_slice(lax.empty((N_DEV,) + shard.shape, shard.dtype), shard[None], (me, 0, 0))

    kinds = ("w_in", "w_out", "ffn_w_up", "ffn_w_down")
    started = {}

    def start(i, after):
        if i >= 4 * nl:
            return 0.0
        l, k = divmod(i, 4)
        shard = _align_in(w[kinds[k]][l]) if k == 0 else w[kinds[k]][l]
        sems, bufs, token = gather_start([place(shard.astype(BF16))], after, f"ag_start_{kinds[k]}_{l}")
        started[i] = (sems[0], bufs[0])
        return token[0, 0]

    def gathered(l, k, after):
        i = 4 * l + k
        got = transfer_wait(started[i][0], [started[i][1]], after, f"ag_wait_{kinds[k]}_{l}")[0]
        return got, (start(1, got) if i == 0 else 0.0) + start(i + 2, got)

    win, wout, wup, wdown = [None] * nl, [None] * nl, [None] * nl, [None] * nl
    row = lambda a, l: a[l][None, :]
    g1, g2, g3, g4 = w["norm_mix_pre"], w["norm_mix_post"], w["norm_ffn_pre"], w["norm_ffn_post"]

    saved = []
    xl = x_in
    h1 = norm_first(xl, row(g1, 0) + start(0, gathered_conv))
    for l in range(nl):
        buf, tk = gathered(l, 0, h1)
        win[l] = buf.reshape(d, PW)
        p = mm_nn(h1, win[l], F32, "mm_in")
        y_dn, states = dn_fwd(p, dn_cw[l], dn_par[l] + tk)
        y_pool = pool_fwd(p, w["pool_w"][l], row(w["pool_scale"], l))
        y_swa = swa_fwd(p, cos2, sin2, sinks[l])
        c = jnp.concatenate([y_dn, y_swa, y_pool], axis=1)
        buf, tk = gathered(l, 1, c)
        wout[l] = _perm_mix_rows(buf.reshape(MIX_W, d))
        mix = mm_nn(c, wout[l], F32, "mm_out")
        x1, h2 = post_pre(xl, mix, row(g2, l) + tk, row(g3, l))
        wup[l], tk = gathered(l, 2, h2)
        u0 = mm_up(h2, wup[l], "mm_up")
        act = glu_fwd(u0, ffn_cw[l], ffn_cb[l] + tk)
        buf, tk = gathered(l, 3, act)
        wdown[l] = buf.reshape(f, d)
        fo = mm_nn(act, wdown[l], F32, "mm_down")
        saved.append(dict(x=xl, h1=h1, p=p, states=states, c=c, mix=mix, x1=x1, h2=h2, u0=u0, act=act, f=fo))
        if l < nl - 1:
            xl, h1 = post_pre(x1, fo, row(g4, l) + tk, row(g1, l + 1))
        else:
            dx, loss_part = post_loss(x1, fo, row(g4, l) + tk, tgt)

    small_g = [dict() for _ in range(nl)]
    pending = {name: [None] * nl for name in _BIG}

    def exchange(name, l, dw):
        sems, src, land, token = exchange_start(dw, f"xch_start_{name}_{l}")
        pending[name][l] = (sems, src, land)
        return token[0, 0]

    df, small_g[nl - 1]["norm_ffn_post"] = bwd_norms(dx, post=(saved[-1]["f"], row(g4, nl - 1)))
    for l in range(nl - 1, -1, -1):
        s, sg = saved[l], small_g[l]
        dact = mm_nt(df, wdown[l], F32, "mm_down_d")
        tk = exchange("ffn_w_down", l, mm_tn(s["act"], df, BF16, "mm_down_w").reshape(N_DEV, f // N_DEV, d))
        du0, dcw = glu_bwd(dact, s["u0"], ffn_cw[l], ffn_cb[l])
        sg["ffn_conv"] = dcw
        dh2 = mm_up_dgrad(du0, wup[l], "mm_up_d")
        tk = tk + exchange("ffn_w_up", l, mm_up_wgrad(s["h2"], du0, "mm_up_w"))
        dx1, sg["norm_ffn_pre"], dmix, sg["norm_mix_post"] = bwd_norms(
            dx, pre=(dh2, s["x1"], row(g3, l) + tk), post=(s["mix"], row(g2, l)))
        dc = mm_nt(dmix, wout[l], F32, "mm_out_d")
        tk = exchange("w_out", l, _unperm_mix_rows(mm_tn(s["c"], dmix, BF16, "mm_out_w"))
                      .reshape(N_DEV, MIX_W // N_DEV, d))
        dqkv, dz, dgate, sg["dn_conv_w"], sg["dn_par"] = dn_bwd(s["p"], dc, s["states"], dn_cw[l], dn_par[l])
        dpool, sg["pool_w"], sg["pool_scale"] = pool_bwd(s["p"], dc, w["pool_w"][l], row(w["pool_scale"], l))
        dsq, dsk, dsv, sg["swa_sinks"] = swa_bwd(s["p"], dc, cos2, sin2, sinks[l])
        dp = jnp.concatenate([dqkv, dz, dsq, dsk, dsv, dpool, dgate], axis=1)
        dh1 = mm_nt(dp, win[l], F32, "mm_in_d")
        tk = tk + exchange("w_in", l, mm_tn(s["h1"], dp, BF16, "mm_in_w").reshape(N_DEV, d // N_DEV, PW))
        if l > 0:
            dx, sg["norm_mix_pre"], df, small_g[l - 1]["norm_ffn_post"] = bwd_norms(
                dx1, pre=(dh1, s["x"], row(g1, l) + tk), post=(saved[l - 1]["f"], row(g4, l - 1)))
        else:
            grad_x, sg["norm_mix_pre"] = bwd_norms(dx1, pre=(dh1, s["x"], row(g1, 0) + tk))

    keys = ("norm_mix_pre", "norm_mix_post", "norm_ffn_pre", "norm_ffn_post", "dn_conv_w", "dn_par", "pool_w",
            "pool_scale", "swa_sinks", "ffn_conv")
    grads = {}
    parts = [small_g[l][k] for l in range(nl) for k in keys] + [loss_part]
    shapes = [a.shape for a in parts]
    summed = sum_slabs(all_gather(_pack(parts), "ag_small"), "sum_small")
    vals = _unpack(summed, shapes)
    loss = vals[-1][0, 0]
    sm = [dict(zip(keys, vals[l * len(keys):(l + 1) * len(keys)])) for l in range(nl)]
    st = lambda fn: jnp.stack([fn(sm[l]) for l in range(nl)])
    for k in ("norm_mix_pre", "norm_mix_post", "norm_ffn_pre", "norm_ffn_post"):
        grads[k] = st(lambda q: q[k][0])
    grads["dn_conv_w"] = lax.dynamic_slice_in_dim(st(lambda q: q["dn_conv_w"][0:DN_K]), me * (_QKV_W // N_DEV),
                                                  _QKV_W // N_DEV, axis=2)
    grads["dn_a_log"] = st(lambda q: q["dn_par"][0, DN_H:2 * DN_H])
    grads["dn_dt_bias"] = st(lambda q: q["dn_par"][1, DN_H:2 * DN_H])
    grads["dn_norm_w"] = st(lambda q: q["dn_par"][2])
    grads["pool_w"] = st(lambda q: q["pool_w"])
    grads["pool_scale"] = st(lambda q: q["pool_scale"][0])
    grads["swa_sinks"] = st(lambda q: q["swa_sinks"][0, 0:SWA_H])
    conv_all = st(lambda q: q["ffn_conv"].reshape(N_DEV, SUB, nb))
    grads["ffn_conv_w"] = lax.dynamic_index_in_dim(conv_all, me, axis=1, keepdims=False)[:, 0:3, :]
    grads["ffn_conv_b"] = conv_all[:, :, 3, :].reshape(nl, 2 * f)

    delta, new_m, new_v = {}, {}, {}
    shapes = [w[k].shape for k in _SMALL]
    pk = lambda tree: _pack([tree[k] for k in _SMALL])
    outs = adamw(pk(w), pk(grads), pk(m), pk(v), "adamw_small")
    for tree, buf in zip((delta, new_m, new_v), outs):
        for k, a in zip(_SMALL, _unpack(buf, shapes)):
            tree[k] = a
    me_arr = jnp.reshape(me, (1,)).astype(jnp.int32)
    after = outs[0]
    for name in ("ffn_w_down", "ffn_w_up", "w_out", "w_in"):
        per_layer = []
        for l in range(nl - 1, -1, -1):
            sems, src, land = pending[name][l]
            src, land = transfer_wait(sems, [src, land], after, f"xch_wait_{name}_{l}")
            per_layer.append(sum_slabs(land, "sum_" + name, own=src, me=me_arr))
        g = jnp.stack(per_layer[::-1])
        grads[name] = _unalign_in(g) if name == "w_in" else g
        delta[name], new_m[name], new_v[name] = _adamw_nd(w[name], grads[name], m[name], v[name], "adamw_" + name)
        after = delta[name]

    return (loss, grad_x[None], *[grads[k] for k in _ORDER], *[delta[k] for k in _ORDER],
            *[new_m[k] for k in _ORDER], *[new_v[k] for k in _ORDER])


def kernel(x, positions, norm_mix_pre, w_in, dn_conv_w, dn_a_log, dn_dt_bias, dn_norm_w, pool_w, pool_scale, swa_sinks, w_out, norm_mix_post, norm_ffn_pre, ffn_w_up, ffn_conv_w, ffn_conv_b, ffn_w_down, norm_ffn_post, loss_target, m_norm_mix_pre, m_w_in, m_dn_conv_w, m_dn_a_log, m_dn_dt_bias, m_dn_norm_w, m_pool_w, m_pool_scale, m_swa_sinks, m_w_out, m_norm_mix_post, m_norm_ffn_pre, m_ffn_w_up, m_ffn_conv_w, m_ffn_conv_b, m_ffn_w_down, m_norm_ffn_post, v_norm_mix_pre, v_w_in, v_dn_conv_w, v_dn_a_log, v_dn_dt_bias, v_dn_norm_w, v_pool_w, v_pool_scale, v_swa_sinks, v_w_out, v_norm_mix_post, v_norm_ffn_pre, v_ffn_w_up, v_ffn_conv_w, v_ffn_conv_b, v_ffn_w_down, v_norm_ffn_post):
    args = locals()
    w = {k: args[k] for k in _ORDER}
    m = {k: args["m_" + k] for k in _ORDER}
    v = {k: args["v_" + k] for k in _ORDER}
    return _step(x, positions, loss_target, w, m, v)
```

```python
import functools
import math

import jax
import jax.numpy as jnp
from jax import lax
from jax.experimental import pallas as pl
from jax.experimental.pallas import tpu as pltpu

F32 = jnp.float32
BF16 = jnp.bfloat16
MXU_DT = jnp.bfloat16
HI = lax.Precision.HIGHEST

N_DEV = 8
LANE = 128
SUB = 8
VMEM_LIMIT = 56 * 1024 * 1024
ROW_TILE = 512
NORM_TILE = 256
MM_TM, MM_TN, MM_TK = 512, 1664, 2816
MM_TN_NT = 2048
PACK_ROWS = 512

HD = 128
DN_H, DN_W, DN_K, CH = 6, 768, 4, 64
POOL_G = 4
SWA_H, SWA_KV, SWA_G, SWA_BLK = 6, 2, 3, 128
EPS = 1e-6
SCALE = HD ** -0.5
NEG = -1e30

O_QKV, O_Z, O_SQ, O_SK, O_SV, O_POOL, O_GATE, PW = 0, 2304, 3072, 3840, 4096, 4352, 4864, 4992
IN_W = 4876
MIX_W = 2048

ADAM_LR, ADAM_B1, ADAM_B2, ADAM_EPS, ADAM_WD, ADAM_STEP = 0.001, 0.9, 0.999, 1e-08, 0.01, 10


def _pick(n, cap, mult=LANE):
    best = None
    for d in range(mult, min(n, cap) + 1, mult):
        if n % d == 0:
            best = d
    return best if best is not None else n


def _cp(*sem):
    return pltpu.CompilerParams(dimension_semantics=sem, vmem_limit_bytes=VMEM_LIMIT)


def _dot(a, b, dims):
    return lax.dot_general(a.astype(MXU_DT), b.astype(MXU_DT), dims, preferred_element_type=F32)


_NN = (((1,), (0,)), ((), ()))
_NT = (((1,), (1,)), ((), ()))
_TN = (((0,), (0,)), ((), ()))


def _mm(a, b):
    return _dot(a, b, _NN)


def _mm_nt(a, b):
    return _dot(a, b, _NT)


def _mm_tn(a, b):
    return _dot(a, b, _TN)


def _mm_hi(a, b, dims=_NN):
    return lax.dot_general(a, b, dims, precision=HI, preferred_element_type=F32)


def _sigmoid(x):
    return jax.nn.sigmoid(x)


def _softplus(x):
    return jnp.maximum(x, 0.0) + jnp.log(1.0 + jnp.exp(-jnp.abs(x)))


def _align_in(w):
    pad = jnp.zeros(w.shape[:-1] + (PW - IN_W,), w.dtype)
    return jnp.concatenate([w[..., 0:3072], w[..., 3596:4364], w[..., 4364:4620], w[..., 4620:4876],
                            w[..., 3084:3596], w[..., 3072:3084], pad], axis=-1)


def _unalign_in(g):
    return jnp.concatenate([g[..., 0:3072], g[..., O_GATE:O_GATE + 12], g[..., O_POOL:O_POOL + 512],
                            g[..., O_SQ:O_SQ + 768], g[..., O_SK:O_SK + 256], g[..., O_SV:O_SV + 256]], axis=-1)


def _perm_mix_rows(w):
    return jnp.concatenate([w[0:768], w[1280:2048], w[768:1280]], axis=0)


def _unperm_mix_rows(w):
    return jnp.concatenate([w[0:768], w[1536:2048], w[768:1536]], axis=0)


def _mm_call(name, a, b, out_shape, grid, a_spec, b_spec, o_spec, dims, acc_shape):
    nk = grid[2]
    if nk == 1:
        def body_once(a_ref, b_ref, o_ref):
            o_ref[...] = _dot(a_ref[...], b_ref[...], dims).astype(o_ref.dtype)

        return pl.pallas_call(
            body_once, grid=grid, in_specs=[a_spec, b_spec], out_specs=o_spec, out_shape=out_shape,
            compiler_params=_cp("parallel", "parallel", "arbitrary"), name=name)(a, b)

    def body(a_ref, b_ref, o_ref, acc_ref):
        k = pl.program_id(2)

        @pl.when(k == 0)
        def _():
            acc_ref[...] = jnp.zeros_like(acc_ref)

        acc_ref[...] += _dot(a_ref[...], b_ref[...], dims)

        @pl.when(k == nk - 1)
        def _():
            o_ref[...] = acc_ref[...].astype(o_ref.dtype)

    return pl.pallas_call(
        body, grid=grid, in_specs=[a_spec, b_spec], out_specs=o_spec, out_shape=out_shape,
        scratch_shapes=[pltpu.VMEM(acc_shape, F32)],
        compiler_params=_cp("parallel", "parallel", "arbitrary"), name=name)(a, b)


def mm_nn(a, b, out_dtype, name):
    (m, k), n = a.shape, b.shape[1]
    tm, tn, tk = _pick(m, MM_TM, SUB), _pick(n, MM_TN), _pick(k, MM_TK)
    return _mm_call(name, a, b, jax.ShapeDtypeStruct((m, n), out_dtype), (m // tm, n // tn, k // tk),
                    pl.BlockSpec((tm, tk), lambda i, j, kk: (i, kk)),
                    pl.BlockSpec((tk, tn), lambda i, j, kk: (kk, j)),
                    pl.BlockSpec((tm, tn), lambda i, j, kk: (i, j)), _NN, (tm, tn))


def mm_nt(a, b, out_dtype, name):
    (m, k), n = a.shape, b.shape[0]
    tm, tn, tk = _pick(m, MM_TM, SUB), _pick(n, MM_TN_NT), _pick(k, MM_TK)
    return _mm_call(name, a, b, jax.ShapeDtypeStruct((m, n), out_dtype), (m // tm, n // tn, k // tk),
                    pl.BlockSpec((tm, tk), lambda i, j, kk: (i, kk)),
                    pl.BlockSpec((tn, tk), lambda i, j, kk: (j, kk)),
                    pl.BlockSpec((tm, tn), lambda i, j, kk: (i, j)), _NT, (tm, tn))


def mm_tn(a, b, out_dtype, name):
    (k, m), n = a.shape, b.shape[1]
    tm, tn, tk = _pick(m, MM_TM), _pick(n, MM_TN), _pick(k, MM_TK, SUB)
    return _mm_call(name, a, b, jax.ShapeDtypeStruct((m, n), out_dtype), (m // tm, n // tn, k // tk),
                    pl.BlockSpec((tk, tm), lambda i, j, kk: (kk, i)),
                    pl.BlockSpec((tk, tn), lambda i, j, kk: (kk, j)),
                    pl.BlockSpec((tm, tn), lambda i, j, kk: (i, j)), _TN, (tm, tn))


def mm_up(h, wblk, name):
    (t, d), (nblk, _, nb) = h.shape, wblk.shape
    tm, tk = _pick(t, MM_TM, SUB), _pick(d, MM_TK)
    hb = nblk // 2
    return _mm_call(name, h, wblk, jax.ShapeDtypeStruct((2, t, hb * nb), F32), (t // tm, nblk, d // tk),
                    pl.BlockSpec((tm, tk), lambda i, j, kk: (i, kk)),
                    pl.BlockSpec((None, tk, nb), lambda i, j, kk: (j, kk, 0)),
                    pl.BlockSpec((None, tm, nb), lambda i, j, kk: (j // hb, i, j % hb)), _NN, (tm, nb))


def mm_up_dgrad(du0, wblk, name):
    (_, t, _), (nblk, d, nb) = du0.shape, wblk.shape
    tm, tn = _pick(t, MM_TM, SUB), _pick(d, MM_TN_NT)
    hb = nblk // 2
    return _mm_call(name, du0, wblk, jax.ShapeDtypeStruct((t, d), F32), (t // tm, d // tn, nblk),
                    pl.BlockSpec((None, tm, nb), lambda i, j, kk: (kk // hb, i, kk % hb)),
                    pl.BlockSpec((None, tn, nb), lambda i, j, kk: (kk, j, 0)),
                    pl.BlockSpec((tm, tn), lambda i, j, kk: (i, j)), _NT, (tm, tn))


def mm_up_wgrad(h, du0, name):
    (t, d), (_, _, f) = h.shape, du0.shape
    nb = f // (N_DEV // 2)
    hb = N_DEV // 2
    tm, tk = _pick(d, MM_TM), _pick(t, MM_TK, SUB)
    return _mm_call(name, h, du0, jax.ShapeDtypeStruct((N_DEV, d, nb), BF16), (d // tm, N_DEV, t // tk),
                    pl.BlockSpec((tk, tm), lambda i, j, kk: (kk, i)),
                    pl.BlockSpec((None, tk, nb), lambda i, j, kk: (j // hb, kk, j % hb)),
                    pl.BlockSpec((None, tm, nb), lambda i, j, kk: (j, i, 0)), _TN, (tm, nb))


def _rms(x, w):
    r = lax.rsqrt(jnp.mean(x * x, axis=-1, keepdims=True) + EPS)
    return x * r * w


def _rms_bwd(dy, x, w):
    r = lax.rsqrt(jnp.mean(x * x, axis=-1, keepdims=True) + EPS)
    xh = x * r
    dxh = dy * w
    dx = r * (dxh - xh * jnp.mean(dxh * xh, axis=-1, keepdims=True))
    return dx, jnp.sum(dy * xh, axis=0, keepdims=True)


def _row_spec(tb, d):
    return pl.BlockSpec((tb, d), lambda i: (i, 0))


def _fix_spec(r, d):
    return pl.BlockSpec((r, d), lambda i: (0, 0))


def norm_first(x, w):
    t, d = x.shape
    tb = _pick(t, NORM_TILE, SUB)

    def body(x_ref, w_ref, h_ref):
        h_ref[...] = _rms(x_ref[...], w_ref[...]).astype(h_ref.dtype)

    return pl.pallas_call(body, grid=(t // tb,), in_specs=[_row_spec(tb, d), _fix_spec(1, d)],
                          out_specs=_row_spec(tb, d), out_shape=jax.ShapeDtypeStruct((t, d), BF16),
                          compiler_params=_cp("parallel"), name="norm_first")(x, w)


def post_pre(x, y, w_post, w_pre):
    t, d = x.shape
    tb = _pick(t, NORM_TILE, SUB)

    def body(x_ref, y_ref, wp_ref, wq_ref, xn_ref, h_ref):
        xn = x_ref[...] + _rms(y_ref[...], wp_ref[...])
        xn_ref[...] = xn
        h_ref[...] = _rms(xn, wq_ref[...]).astype(h_ref.dtype)

    return pl.pallas_call(
        body, grid=(t // tb,),
        in_specs=[_row_spec(tb, d), _row_spec(tb, d), _fix_spec(1, d), _fix_spec(1, d)],
        out_specs=[_row_spec(tb, d), _row_spec(tb, d)],
        out_shape=(jax.ShapeDtypeStruct((t, d), F32), jax.ShapeDtypeStruct((t, d), BF16)),
        compiler_params=_cp("parallel"), name="post_pre")(x, y, w_post, w_pre)


def post_loss(x, y, w_post, target):
    t, d = x.shape
    tb = _pick(t, NORM_TILE, SUB)

    def body(x_ref, y_ref, wp_ref, t_ref, g_ref, l_ref):
        err = x_ref[...] + _rms(y_ref[...], wp_ref[...]) - t_ref[...]
        g_ref[...] = err * (1.0 / d)

        @pl.when(pl.program_id(0) == 0)
        def _():
            l_ref[...] = jnp.zeros_like(l_ref)

        part = 0.5 * jnp.sum(jnp.mean(err * err, axis=-1, keepdims=True), axis=0, keepdims=True)
        l_ref[...] += jnp.broadcast_to(part, l_ref.shape)

    return pl.pallas_call(
        body, grid=(t // tb,),
        in_specs=[_row_spec(tb, d), _row_spec(tb, d), _fix_spec(1, d), _row_spec(tb, d)],
        out_specs=[_row_spec(tb, d), _fix_spec(1, LANE)],
        out_shape=(jax.ShapeDtypeStruct((t, d), F32), jax.ShapeDtypeStruct((1, LANE), F32)),
        compiler_params=_cp("arbitrary"), name="post_loss")(x, y, w_post, target)


def bwd_norms(dx_in, *, pre=None, post=None):
    t, d = dx_in.shape
    tb = _pick(t, NORM_TILE, SUB)
    has_pre, has_post = pre is not None, post is not None

    def body(*refs):
        refs = list(refs)
        dxi = refs.pop(0)
        if has_pre:
            dh, x, wq = refs.pop(0), refs.pop(0), refs.pop(0)
        if has_post:
            y, wp = refs.pop(0), refs.pop(0)
        first = pl.program_id(0) == 0
        dx = dxi[...]
        if has_pre:
            dxo, dwq = refs.pop(0), refs.pop(0)
            g, dw = _rms_bwd(dh[...], x[...], wq[...])
            dx = dx + g
            dxo[...] = dx

            @pl.when(first)
            def _():
                dwq[...] = jnp.zeros_like(dwq)

            dwq[...] += dw
        if has_post:
            dyo, dwp = refs.pop(0), refs.pop(0)
            g, dw = _rms_bwd(dx, y[...], wp[...])
            dyo[...] = g.astype(dyo.dtype)

            @pl.when(first)
            def _():
                dwp[...] = jnp.zeros_like(dwp)

            dwp[...] += dw

    ins, in_specs, outs, out_specs = [dx_in], [_row_spec(tb, d)], [], []
    if has_pre:
        ins += list(pre)
        in_specs += [_row_spec(tb, d), _row_spec(tb, d), _fix_spec(1, d)]
        outs += [jax.ShapeDtypeStruct((t, d), F32), jax.ShapeDtypeStruct((1, d), F32)]
        out_specs += [_row_spec(tb, d), _fix_spec(1, d)]
    if has_post:
        ins += list(post)
        in_specs += [_row_spec(tb, d), _fix_spec(1, d)]
        outs += [jax.ShapeDtypeStruct((t, d), BF16), jax.ShapeDtypeStruct((1, d), F32)]
        out_specs += [_row_spec(tb, d), _fix_spec(1, d)]
    name = "bwd_norms" + ("_pre" if has_pre else "") + ("_post" if has_post else "")
    return pl.pallas_call(body, grid=(t // tb,), in_specs=in_specs, out_specs=out_specs, out_shape=tuple(outs),
                          compiler_params=_cp("arbitrary"), name=name)(*ins)


GLU_ROWS = 32


def _ffn_conv_blk(blk, cw, cb):
    r = blk.shape[1] - SUB
    x0, x1, x2 = blk[:, 6:6 + r], blk[:, 7:7 + r], blk[:, 8:8 + r]
    return x0, x1, x2, cw[:, 0:1, :] * x0 + cw[:, 1:2, :] * x1 + cw[:, 2:3, :] * x2 + cb


def _glu_specs(tb, nb, hpb, row_of):
    tile = pl.BlockSpec((2, tb, nb), lambda j, i: (0, row_of(i), j))
    halo = pl.BlockSpec((2, SUB, nb), lambda j, i: (0, jnp.maximum(row_of(i) * hpb - 1, 0), j))
    cw = pl.BlockSpec((2, None, 3, nb), lambda j, i: (0, j, 0, 0))
    cb = pl.BlockSpec((2, None, 1, nb), lambda j, i: (0, j, 0, 0))
    return tile, halo, cw, cb


def glu_fwd(u0, cw, cb):
    _, t, f = u0.shape
    nb = cw.shape[-1]
    tb = _pick(t, ROW_TILE, SUB)
    nt, hpb = t // tb, tb // SUB

    def body(u, h, cwr, cbr, o_ref, e):
        i = pl.program_id(1)
        e[:, 0:SUB, :] = jnp.where(i > 0, h[...], 0.0)
        e[:, SUB:, :] = u[...]
        w, bias = cwr[...], cbr[...]

        def rows(g, carry):
            s = pl.multiple_of(g * GLU_ROWS, GLU_ROWS)
            for lg in range(nb // LANE):
                ls = slice(lg * LANE, (lg + 1) * LANE)
                ab = _ffn_conv_blk(e[:, pl.ds(s, GLU_ROWS + SUB), ls], w[:, :, ls], bias[:, :, ls])[3]
                a, b = ab[0], ab[1]
                o_ref[pl.ds(s, GLU_ROWS), ls] = (a * _sigmoid(a) * b).astype(o_ref.dtype)
            return carry

        lax.fori_loop(0, tb // GLU_ROWS, rows, 0)

    return pl.pallas_call(
        body, grid=(f // nb, nt), in_specs=list(_glu_specs(tb, nb, hpb, lambda i: i)),
        out_specs=pl.BlockSpec((tb, nb), lambda j, i: (i, j)),
        out_shape=jax.ShapeDtypeStruct((t, f), BF16),
        scratch_shapes=[pltpu.VMEM((2, tb + SUB, nb), F32)],
        compiler_params=_cp("parallel", "arbitrary"), name="glu_fwd")(u0, u0, cw, cb)


def glu_bwd(dact, u0, cw, cb):
    _, t, f = u0.shape
    nb = cw.shape[-1]
    tb = _pick(t, ROW_TILE, SUB)
    nt, hpb = t // tb, tb // SUB

    def body(d_ref, u, h, cwr, cbr, du_o, dc_o, e, x2):
        i = pl.program_id(1)
        r = nt - 1 - i
        e[:, 0:SUB, :] = jnp.where(r > 0, h[...], 0.0)
        e[:, SUB:, :] = u[...]
        w, bias = cwr[...], cbr[...]

        @pl.when(i == 0)
        def _():
            dc_o[...] = jnp.zeros_like(dc_o)
            x2[:, tb:, :] = jnp.zeros((2, SUB, nb), F32)

        fold = lambda v: jnp.sum(v.reshape(2, GLU_ROWS // SUB, SUB, LANE), axis=1)
        for lg in range(nb // LANE):
            ls = slice(lg * LANE, (lg + 1) * LANE)
            wl, bl = w[:, :, ls], bias[:, :, ls]

            def grads(g, acc):
                s = pl.multiple_of(g * GLU_ROWS, GLU_ROWS)
                x0, x1, xc, ab = _ffn_conv_blk(e[:, pl.ds(s, GLU_ROWS + SUB), ls], wl, bl)
                a, b = ab[0], ab[1]
                sa = _sigmoid(a)
                d = d_ref[pl.ds(s, GLU_ROWS), ls]
                x2[0, pl.ds(s, GLU_ROWS), ls] = d * b * (sa * (1.0 + a * (1.0 - sa)))
                x2[1, pl.ds(s, GLU_ROWS), ls] = d * (a * sa)
                du = x2[:, pl.ds(s, GLU_ROWS), ls]
                return (acc[0] + fold(du * x0), acc[1] + fold(du * x1), acc[2] + fold(du * xc), acc[3] + fold(du))

            zero = jnp.zeros((2, SUB, LANE), F32)
            acc = lax.fori_loop(0, tb // GLU_ROWS, grads, (zero, zero, zero, zero))
            for k in range(4):
                dc_o[:, k:k + 1, ls] += jnp.sum(acc[k], axis=1, keepdims=True)

            def transposed_conv(g, carry):
                s = pl.multiple_of(g * GLU_ROWS, GLU_ROWS)
                blk = x2[:, pl.ds(s, GLU_ROWS + SUB), ls]
                du_o[:, pl.ds(s, GLU_ROWS), ls] = (
                    wl[:, 2:3, :] * blk[:, 0:GLU_ROWS] + wl[:, 1:2, :] * blk[:, 1:1 + GLU_ROWS]
                    + wl[:, 0:1, :] * blk[:, 2:2 + GLU_ROWS]).astype(du_o.dtype)
                return carry

            lax.fori_loop(0, tb // GLU_ROWS, transposed_conv, 0)
        x2[:, tb:, :] = x2[:, 0:SUB, :]

    rev = lambda i: nt - 1 - i
    return pl.pallas_call(
        body, grid=(f // nb, nt),
        in_specs=[pl.BlockSpec((tb, nb), lambda j, i: (rev(i), j))] + list(_glu_specs(tb, nb, hpb, rev)),
        out_specs=[pl.BlockSpec((2, tb, nb), lambda j, i: (0, rev(i), j)),
                   pl.BlockSpec((2, None, SUB, nb), lambda j, i: (0, j, 0, 0))],
        out_shape=(jax.ShapeDtypeStruct((2, t, f), BF16), jax.ShapeDtypeStruct((2, f // nb, SUB, nb), F32)),
        scratch_shapes=[pltpu.VMEM((2, tb + SUB, nb), F32), pltpu.VMEM((2, tb + SUB, nb), F32)],
        compiler_params=_cp("arbitrary", "arbitrary"), name="glu_bwd")(dact, u0, u0, cw, cb)


POOL_HALO = 16
_PCOL = O_POOL // LANE
_CPOOL = 1536 // LANE


def _pool_sel(g, v2, v4, v8, v16):
    return jnp.where(g == 0, v2, jnp.where(g == 1, v4, jnp.where(g == 2, v8, v16)))


def _pool_cnt(g, t0, n):
    win = _pool_sel(g, 2, 4, 8, 16)
    tpos = t0 + lax.broadcasted_iota(jnp.int32, (n, 1), 0)
    return jnp.minimum(tpos + 1, win).astype(F32)


def _pool_core(e, g, t0, tb):
    s2 = e + pltpu.roll(e, 1, 0)
    s4 = s2 + pltpu.roll(s2, 2, 0)
    s8 = s4 + pltpu.roll(s4, 4, 0)
    s16 = s8 + pltpu.roll(s8, 8, 0)
    sw = _pool_sel(g, s2, s4, s8, s16)[POOL_HALO:]
    return sw / _pool_cnt(g, t0, tb) - e[POOL_HALO:]


def pool_fwd(p, pool_w, pool_scale):
    t = p.shape[0]
    tb = _pick(t, ROW_TILE, POOL_HALO)
    nt, hpb = t // tb, tb // POOL_HALO

    def body(x_ref, h_ref, w_ref, s_ref, o_ref):
        i, g = pl.program_id(0), pl.program_id(1)
        e = jnp.concatenate([jnp.where(i > 0, h_ref[...], 0.0), x_ref[...]], axis=0)
        yy = _pool_core(e, g, i * tb, tb)
        o_ref[...] = (_mm(yy, w_ref[...]) * s_ref[...]).astype(o_ref.dtype)

    return pl.pallas_call(
        body, grid=(nt, POOL_G),
        in_specs=[pl.BlockSpec((tb, LANE), lambda i, g: (i, _PCOL + g)),
                  pl.BlockSpec((POOL_HALO, LANE), lambda i, g: (jnp.maximum(i * hpb - 1, 0), _PCOL + g)),
                  pl.BlockSpec((None, LANE, LANE), lambda i, g: (g, 0, 0)),
                  pl.BlockSpec((1, LANE), lambda i, g: (0, g))],
        out_specs=pl.BlockSpec((tb, LANE), lambda i, g: (i, g)),
        out_shape=jax.ShapeDtypeStruct((t, POOL_G * LANE), BF16),
        compiler_params=_cp("parallel", "parallel"), name="pool_fwd")(p, p, pool_w, pool_scale)


def pool_bwd(p, dc, pool_w, pool_scale):
    t = p.shape[0]
    tb = _pick(t, ROW_TILE, POOL_HALO)
    nt, hpb = t // tb, tb // POOL_HALO
    n = tb + POOL_HALO

    def body(x_ref, h_ref, dy_ref, dn_ref, w_ref, s_ref, dx_o, dw_o, ds_o):
        g, i = pl.program_id(0), pl.program_id(1)
        e = jnp.concatenate([jnp.where(i > 0, h_ref[...], 0.0), x_ref[...]], axis=0)
        yy = _pool_core(e, g, i * tb, tb)
        w, sc, dy = w_ref[...], s_ref[...], dy_ref[...]

        @pl.when(i == 0)
        def _():
            dw_o[...] = jnp.zeros_like(dw_o)
            ds_o[...] = jnp.zeros_like(ds_o)

        ds_o[...] += jnp.sum(dy * _mm(yy, w), axis=0, keepdims=True)
        dw_o[...] += _mm_tn(yy, dy * sc)
        dye = jnp.concatenate([dy, jnp.where(i < nt - 1, dn_ref[...], 0.0)], axis=0) * sc
        dyy = _mm_nt(dye, w)
        z = dyy / _pool_cnt(g, i * tb, n)
        r2 = z + pltpu.roll(z, n - 1, 0)
        r4 = r2 + pltpu.roll(r2, n - 2, 0)
        r8 = r4 + pltpu.roll(r4, n - 4, 0)
        r16 = r8 + pltpu.roll(r8, n - 8, 0)
        dx_o[...] = (_pool_sel(g, r2, r4, r8, r16)[:tb] - dyy[:tb]).astype(dx_o.dtype)

    last = t // POOL_HALO - 1
    return pl.pallas_call(
        body, grid=(POOL_G, nt),
        in_specs=[pl.BlockSpec((tb, LANE), lambda g, i: (i, _PCOL + g)),
                  pl.BlockSpec((POOL_HALO, LANE), lambda g, i: (jnp.maximum(i * hpb - 1, 0), _PCOL + g)),
                  pl.BlockSpec((tb, LANE), lambda g, i: (i, _CPOOL + g)),
                  pl.BlockSpec((POOL_HALO, LANE), lambda g, i: (jnp.minimum((i + 1) * hpb, last), _CPOOL + g)),
                  pl.BlockSpec((None, LANE, LANE), lambda g, i: (g, 0, 0)),
                  pl.BlockSpec((1, LANE), lambda g, i: (0, g))],
        out_specs=[pl.BlockSpec((tb, LANE), lambda g, i: (i, g)),
                   pl.BlockSpec((None, LANE, LANE), lambda g, i: (g, 0, 0)),
                   pl.BlockSpec((1, LANE), lambda g, i: (0, g))],
        out_shape=(jax.ShapeDtypeStruct((t, POOL_G * LANE), BF16),
                   jax.ShapeDtypeStruct((POOL_G, LANE, LANE), F32),
                   jax.ShapeDtypeStruct((1, POOL_G * LANE), F32)),
        compiler_params=_cp("arbitrary", "arbitrary"), name="pool_bwd")(p, p, dc, dc, pool_w, pool_scale)


_QCOL, _KCOL, _VCOL = O_SQ // 768, O_SK // 256, O_SV // 256
_GQ = SWA_G * SWA_BLK


def _rope(x, c2, s2):
    return x * c2 + pltpu.roll(x, HD // 2, 1) * s2


def _rope_bwd(d, c2, s2):
    return d * c2 + pltpu.roll(d * s2, HD // 2, 1)


def _hs(x, h):
    return x[:, h * HD:(h + 1) * HD]


def _swa_group(q, kc, kp, vc, vp, c2c, s2c, c2p, s2p, sinks, h, blk):
    kcat = jnp.concatenate([_rope(_hs(kp, h), c2p, s2p), _rope(_hs(kc, h), c2c, s2c)], axis=0)
    vcat = jnp.concatenate([_hs(vp, h), _hs(vc, h)], axis=0)
    qs = jnp.concatenate([_rope(_hs(q, SWA_G * h + g), c2c, s2c) for g in range(SWA_G)], axis=0)
    s = _mm_nt(qs, kcat) * SCALE
    ii = lax.broadcasted_iota(jnp.int32, (_GQ, 2 * SWA_BLK), 0) & (SWA_BLK - 1)
    jj = lax.broadcasted_iota(jnp.int32, (_GQ, 2 * SWA_BLK), 1)
    lo = jnp.where(blk > 0, 0, SWA_BLK)
    s = jnp.where((jj > ii) & (jj <= ii + SWA_BLK) & (jj >= lo), s, NEG)
    sink = jnp.concatenate(
        [jnp.broadcast_to(sinks[:, SWA_G * h + g:SWA_G * h + g + 1], (SWA_BLK, 1)) for g in range(SWA_G)], axis=0)
    m = jnp.maximum(jnp.max(s, axis=1, keepdims=True), sink)
    p = jnp.exp(s - m)
    ps = jnp.exp(sink - m)
    l = jnp.sum(p, axis=1, keepdims=True) + ps
    return qs, kcat, vcat, p, ps, l


def _swa_specs(blk_of):
    cur = lambda w, c: pl.BlockSpec((SWA_BLK, w), lambda n: (blk_of(n), c))
    prev = lambda w, c: pl.BlockSpec((SWA_BLK, w), lambda n: (jnp.maximum(blk_of(n) - 1, 0), c))
    return [cur(768, _QCOL), cur(256, _KCOL), prev(256, _KCOL), cur(256, _VCOL), prev(256, _VCOL),
            cur(HD, 0), cur(HD, 0), prev(HD, 0), prev(HD, 0), pl.BlockSpec((1, LANE), lambda n: (0, 0))]


def swa_fwd(p, cos2, sin2, sinks):
    t = p.shape[0]

    def body(q_ref, kc, kp, vc, vp, c2c, s2c, c2p, s2p, sk_ref, o_ref):
        n = pl.program_id(0)
        for h in range(SWA_KV):
            _, _, vcat, pr, _, l = _swa_group(q_ref[...], kc[...], kp[...], vc[...], vp[...], c2c[...], s2c[...],
                                              c2p[...], s2p[...], sk_ref[...], h, n)
            o = _mm(pr, vcat) / l
            for g in range(SWA_G):
                hh = SWA_G * h + g
                o_ref[:, hh * HD:(hh + 1) * HD] = o[g * SWA_BLK:(g + 1) * SWA_BLK].astype(o_ref.dtype)

    return pl.pallas_call(
        body, grid=(t // SWA_BLK,), in_specs=_swa_specs(lambda n: n),
        out_specs=pl.BlockSpec((SWA_BLK, 768), lambda n: (n, 0)),
        out_shape=jax.ShapeDtypeStruct((t, 768), BF16),
        compiler_params=_cp("parallel"), name="swa_fwd")(p, p, p, p, p, cos2, sin2, cos2, sin2, sinks)


def swa_bwd(p, dc, cos2, sin2, sinks):
    t = p.shape[0]
    nb = t // SWA_BLK

    def body(q_ref, kc, kp, vc, vp, c2c, s2c, c2p, s2p, sk_ref, do_ref, dq_o, dk_o, dv_o, dsk_o, ck, cv):
        i = pl.program_id(0)
        r = nb - 1 - i

        @pl.when(i == 0)
        def _():
            ck[...] = jnp.zeros_like(ck)
            cv[...] = jnp.zeros_like(cv)
            dsk_o[...] = jnp.zeros_like(dsk_o)

        lane = lax.broadcasted_iota(jnp.int32, (1, LANE), 1)
        dsk = jnp.zeros((1, LANE), F32)
        do = do_ref[...]
        for h in range(SWA_KV):
            qs, kcat, vcat, pr, ps, l = _swa_group(q_ref[...], kc[...], kp[...], vc[...], vp[...], c2c[...],
                                                   s2c[...], c2p[...], s2p[...], sk_ref[...], h, r)
            pn = pr / l
            dos = jnp.concatenate([_hs(do, SWA_G * h + g) for g in range(SWA_G)], axis=0)
            dp = _mm_nt(dos, vcat)
            delta = jnp.sum(pn * dp, axis=1, keepdims=True)
            ds = pn * (dp - delta)
            dsr = -(ps / l) * delta
            for g in range(SWA_G):
                tot = jnp.sum(dsr[g * SWA_BLK:(g + 1) * SWA_BLK], axis=0, keepdims=True)
                dsk = dsk + jnp.where(lane == SWA_G * h + g, tot, 0.0)
            dqs = _mm(ds, kcat) * SCALE
            for g in range(SWA_G):
                hh = SWA_G * h + g
                dq_o[:, hh * HD:(hh + 1) * HD] = _rope_bwd(dqs[g * SWA_BLK:(g + 1) * SWA_BLK], c2c[...],
                                                          s2c[...]).astype(dq_o.dtype)
            dk = _mm_tn(ds, qs) * SCALE
            dv = _mm_tn(pn, dos)
            cs = slice(h * HD, (h + 1) * HD)
            dk_o[:, cs] = (_rope_bwd(dk[SWA_BLK:], c2c[...], s2c[...]) + ck[:, cs]).astype(dk_o.dtype)
            dv_o[:, cs] = (dv[SWA_BLK:] + cv[:, cs]).astype(dv_o.dtype)
            ck[:, cs] = _rope_bwd(dk[:SWA_BLK], c2p[...], s2p[...])
            cv[:, cs] = dv[:SWA_BLK]
        dsk_o[...] += dsk

    rev = lambda n: nb - 1 - n
    return pl.pallas_call(
        body, grid=(nb,),
        in_specs=_swa_specs(rev) + [pl.BlockSpec((SWA_BLK, 768), lambda n: (rev(n), 1))],
        out_specs=[pl.BlockSpec((SWA_BLK, 768), lambda n: (rev(n), 0)),
                   pl.BlockSpec((SWA_BLK, 256), lambda n: (rev(n), 0)),
                   pl.BlockSpec((SWA_BLK, 256), lambda n: (rev(n), 0)),
                   pl.BlockSpec((1, LANE), lambda n: (0, 0))],
        out_shape=(jax.ShapeDtypeStruct((t, 768), BF16), jax.ShapeDtypeStruct((t, 256), BF16),
                   jax.ShapeDtypeStruct((t, 256), BF16), jax.ShapeDtypeStruct((1, LANE), F32)),
        scratch_shapes=[pltpu.VMEM((SWA_BLK, 256), F32), pltpu.VMEM((SWA_BLK, 256), F32)],
        compiler_params=_cp("arbitrary"), name="swa_bwd")(p, p, p, p, p, cos2, sin2, cos2, sin2, sinks, dc)


_ZCOL, _GCOL = O_Z // DN_W, O_GATE // LANE
_QKV_W = 3 * DN_W
_INV_STEPS = int(math.log2(CH)) - 1


class _Bag(dict):
    __getattr__ = dict.__getitem__


DN_CB = 4
_DN_ROWS = DN_CB * CH
_CH_SHIFT = CH.bit_length() - 1


def _dn_consts():
    ii = lax.broadcasted_iota(jnp.int32, (CH, CH), 0)
    jj = lax.broadcasted_iota(jnp.int32, (CH, CH), 1)
    bi = lax.broadcasted_iota(jnp.int32, (_DN_ROWS, _DN_ROWS), 0)
    bj = lax.broadcasted_iota(jnp.int32, (_DN_ROWS, _DN_ROWS), 1)
    same_chunk = jnp.right_shift(bi, _CH_SHIFT) == jnp.right_shift(bj, _CH_SHIFT)
    return _Bag(lower=ii >= jj, strict=ii > jj, diag=ii == jj,
                eye=jnp.where(ii == jj, 1.0, 0.0).astype(F32),
                tril_blk=jnp.where(same_chunk & (bi >= bj), 1.0, 0.0).astype(F32),
                ones=jnp.ones((CH, CH), F32), ones_w=jnp.ones((CH, LANE), F32),
                rows=lax.broadcasted_iota(jnp.int32, (CH, 1), 0),
                lane=lax.broadcasted_iota(jnp.int32, (1, LANE), 1))


def _dn_conv(ext_ref, cw):
    return (cw[0:1, :] * ext_ref[pl.ds(5, _DN_ROWS), :] + cw[1:2, :] * ext_ref[pl.ds(6, _DN_ROWS), :]
            + cw[2:3, :] * ext_ref[pl.ds(7, _DN_ROWS), :] + cw[3:4, :] * ext_ref[pl.ds(8, _DN_ROWS), :])


def _dn_gates(gt, arow, drow, c):
    beta = _sigmoid(gt)
    ea = jnp.exp(arow)
    xa = gt + drow
    g = -ea * _softplus(xa)
    return beta, g, _mm_hi(c.tril_blk, g), ea, _sigmoid(xa)


def _blk(a, ci, j):
    return a[ci * CH:(ci + 1) * CH, j * HD:(j + 1) * HD]


def _lockstep(gens):
    out, live = [None] * len(gens), list(range(len(gens)))
    while live:
        still = []
        for i in live:
            try:
                next(gens[i])
                still.append(i)
            except StopIteration as stop:
                out[i] = stop.value
        live = still
    return out


def _dn_head_a(qh, kh, vh, beta, gc, c):
    rq = lax.rsqrt(jnp.sum(qh * qh, axis=1, keepdims=True) + EPS)
    rk = lax.rsqrt(jnp.sum(kh * kh, axis=1, keepdims=True) + EPS)
    qn = qh * rq * SCALE
    kn = kh * rk
    kb = kn * beta
    vb = vh * beta
    gcol = _mm_hi(c.ones, jnp.where(c.diag, gc, 0.0))
    kk = _mm_nt(kb, kn)
    qk = _mm_nt(qn, kn)
    yield
    gam = jnp.where(c.lower, jnp.exp(jnp.minimum(gc - gcol, 0.0)), 0.0)
    lmat = jnp.where(c.strict, kk * gam, 0.0)
    amat = qk * gam
    nil = -lmat
    inv = c.eye + nil
    powk = nil
    for _ in range(_INV_STEPS):
        powk = _mm(powk, powk)
        yield
        inv = _mm(inv, c.eye + powk)
    eg = jnp.exp(gc)
    kbe = kb * eg
    yield
    u = _mm(inv, vb)
    w = _mm(inv, kbe)
    gl = gc[CH - 1:CH, :]
    e2 = jnp.exp(gl - gc)
    cd = jnp.exp(gl)
    qd = qn * eg
    kd = kn * e2
    return _Bag(rq=rq, rk=rk, qn=qn, kn=kn, kb=kb, vb=vb, gam=gam, lmat=lmat, inv=inv, eg=eg, kbe=kbe, u=u, w=w,
                amat=amat, e2=e2, cd=cd, qd=qd, kd=kd)


def _dn_head_b(f, s0):
    ws = _mm(f.w, s0)
    qs = _mm(f.qd, s0)
    yield
    vnew = f.u - ws
    return vnew, qs + _mm(f.amat, vnew), s0 * f.cd + _mm_tn(f.kd, vnew)


def _dn_post(o, zh, nw):
    ro = lax.rsqrt(jnp.mean(o * o, axis=1, keepdims=True) + EPS)
    oh = o * ro
    sz = _sigmoid(zh)
    return ro, oh, sz, oh * nw * (zh * sz)


def _dn_post_bwd(o, zh, nw, dy):
    ro, oh, sz, _ = _dn_post(o, zh, nw)
    don = dy * (zh * sz)
    dz = dy * (oh * nw) * (sz * (1.0 + zh * (1.0 - sz)))
    doh = don * nw
    return (ro * (doh - oh * jnp.mean(doh * oh, axis=1, keepdims=True)), dz,
            jnp.sum(don * oh, axis=0, keepdims=True))


def _dn_head_bwd_b(f, vnew, do, dsn, s0):
    a_do = _mm_tn(f.amat, do)
    kd_ds = _mm(f.kd, dsn)
    qd_do = _mm_tn(f.qd, do)
    dkd = _mm_nt(vnew, dsn)
    yield
    dvnew = a_do + kd_ds
    ds0 = qd_do + f.cd * dsn - _mm_tn(f.w, dvnew)
    dcd = jnp.sum(jnp.sum(s0 * dsn, axis=1, keepdims=True), axis=0, keepdims=True)
    return dvnew, ds0, dkd, dcd


def _dn_head_bwd_c(f, vnew, do, dvnew, dkd, dcd, s0, qh, vh, beta, c):
    da = jnp.where(c.lower, _mm_nt(do, vnew), 0.0)
    dqd = _mm_nt(do, s0)
    dw = -_mm_nt(dvnew, s0)
    dt_u = _mm_nt(dvnew, f.vb)
    dvb = _mm_tn(f.inv, dvnew)
    yield
    dt = dt_u + _mm_nt(dw, f.kbe)
    dkbe = _mm_tn(f.inv, dw)
    yield
    dt_inv = _mm_nt(dt, f.inv)
    yield
    dl = -jnp.where(c.strict, _mm_tn(f.inv, dt_inv), 0.0)
    yield
    dm = dl * f.gam
    dn = da * f.gam
    dkb = _mm(dm, f.kn) + dkbe * f.eg
    dkn = _mm_tn(dm, f.kb) + _mm_tn(dn, f.qn) + dkd * f.e2 + beta * dkb
    dqn = _mm(dn, f.kn) + dqd * f.eg
    pm = dl * f.lmat + da * f.amat
    colsum = _mm_hi(pm, c.ones_w, _TN)[:, 0:1]
    yield
    tkd = jnp.sum(dkd * f.kn, axis=1, keepdims=True) * f.e2
    dgc = (jnp.sum(pm, axis=1, keepdims=True) - colsum - tkd
           + (jnp.sum(dqd * f.qn, axis=1, keepdims=True) + jnp.sum(dkbe * f.kb, axis=1, keepdims=True)) * f.eg)
    dgl = jnp.sum(tkd, axis=0, keepdims=True) + dcd * f.cd
    dgc = dgc + jnp.where(c.rows == CH - 1, dgl, 0.0)
    dbeta = jnp.sum(dkb * f.kn, axis=1, keepdims=True) + jnp.sum(dvb * vh, axis=1, keepdims=True)
    dvh = beta * dvb
    qhat = qh * f.rq
    dqs = dqn * SCALE
    dqh = f.rq * (dqs - qhat * jnp.sum(qhat * dqs, axis=1, keepdims=True))
    dkh = f.rk * (dkn - f.kn * jnp.sum(f.kn * dkn, axis=1, keepdims=True))
    return dqh, dkh, dvh, dbeta, dgc


def _dn_in_specs(step_of):
    return [pl.BlockSpec((_DN_ROWS, _QKV_W), lambda n: (step_of(n), 0)),
            pl.BlockSpec((SUB, _QKV_W), lambda n: (jnp.maximum(step_of(n) * (_DN_ROWS // SUB) - 1, 0), 0)),
            pl.BlockSpec((_DN_ROWS, DN_W), lambda n: (step_of(n), _ZCOL)),
            pl.BlockSpec((_DN_ROWS, LANE), lambda n: (step_of(n), _GCOL)),
            pl.BlockSpec((DN_K, _QKV_W), lambda n: (0, 0)),
            pl.BlockSpec((SUB, LANE), lambda n: (0, 0))]


def _dn_heads_a(qkv, beta_all, gc_all, c):
    rows = lambda a, ci: a[ci * CH:(ci + 1) * CH]
    flat = _lockstep([_dn_head_a(_blk(qkv, ci, h), _blk(qkv, ci, DN_H + h), _blk(qkv, ci, 2 * DN_H + h),
                                 rows(beta_all, ci)[:, h:h + 1], rows(gc_all, ci)[:, DN_H + h:DN_H + h + 1], c)
                      for ci in range(DN_CB) for h in range(DN_H)])
    return [flat[ci * DN_H:(ci + 1) * DN_H] for ci in range(DN_CB)]


def dn_fwd(p, conv_w, par):
    t = p.shape[0]
    nc = t // CH
    assert t % _DN_ROWS == 0

    def body(x_ref, h_ref, z_ref, g_ref, cw_ref, par_ref, y_o, s_o, ext, st):
        n = pl.program_id(0)
        c = _dn_consts()

        @pl.when(n == 0)
        def _():
            st[...] = jnp.zeros_like(st)

        ext[0:SUB, :] = jnp.where(n > 0, h_ref[...], 0.0)
        ext[SUB:, :] = x_ref[...]
        pre = _dn_conv(ext, cw_ref[...])
        qkv = pre * _sigmoid(pre)
        par = par_ref[...]
        beta_all, _, gc_all, _, _ = _dn_gates(g_ref[...], par[0:1, :], par[1:2, :], c)
        z = z_ref[...]
        fa = _dn_heads_a(qkv, beta_all, gc_all, c)
        s = [st[h] for h in range(DN_H)]
        for ci in range(DN_CB):
            for h in range(DN_H):
                s_o[ci, h] = s[h]
            res = _lockstep([_dn_head_b(fa[ci][h], s[h]) for h in range(DN_H)])
            for h in range(DN_H):
                _, o, s[h] = res[h]
                y_o[ci * CH:(ci + 1) * CH, h * HD:(h + 1) * HD] = _dn_post(o, _blk(z, ci, h),
                                                                           par[2:3, :])[3].astype(y_o.dtype)
        for h in range(DN_H):
            st[h] = s[h]

    return pl.pallas_call(
        body, grid=(t // _DN_ROWS,), in_specs=_dn_in_specs(lambda n: n),
        out_specs=[pl.BlockSpec((_DN_ROWS, DN_W), lambda n: (n, 0)),
                   pl.BlockSpec((DN_CB, DN_H, HD, HD), lambda n: (n, 0, 0, 0))],
        out_shape=(jax.ShapeDtypeStruct((t, DN_W), BF16), jax.ShapeDtypeStruct((nc, DN_H, HD, HD), F32)),
        scratch_shapes=[pltpu.VMEM((_DN_ROWS + SUB, _QKV_W), F32), pltpu.VMEM((DN_H, HD, HD), F32)],
        compiler_params=_cp("arbitrary"), name="dn_fwd")(p, p, p, p, conv_w, par)


def dn_bwd(p, dc, states, conv_w, par):
    t = p.shape[0]
    ns = t // _DN_ROWS

    def body(x_ref, h_ref, z_ref, g_ref, cw_ref, par_ref, s_ref, dy_ref,
             dx_o, dz_o, dg_o, dcw_o, dpar_o, ext, dst, dpost, x2):
        i = pl.program_id(0)
        r = ns - 1 - i
        c = _dn_consts()

        @pl.when(i == 0)
        def _():
            dst[...] = jnp.zeros_like(dst)
            dcw_o[...] = jnp.zeros_like(dcw_o)
            dpar_o[...] = jnp.zeros_like(dpar_o)
            x2[_DN_ROWS:, :] = jnp.zeros((SUB, _QKV_W), F32)

        ext[0:SUB, :] = jnp.where(r > 0, h_ref[...], 0.0)
        ext[SUB:, :] = x_ref[...]
        cw = cw_ref[...]
        pre = _dn_conv(ext, cw)
        sg = _sigmoid(pre)
        qkv = pre * sg
        par = par_ref[...]
        gt = g_ref[...]
        beta_all, g_all, gc_all, ea, sxa = _dn_gates(gt, par[0:1, :], par[1:2, :], c)
        z, dy = z_ref[...], dy_ref[...]
        nw = par[2:3, :]
        dnw = jnp.zeros((1, LANE), F32)
        pairs = [(ci, h) for ci in range(DN_CB) for h in range(DN_H)]
        fa = _dn_heads_a(qkv, beta_all, gc_all, c)
        vnew, do = {}, {}
        fwd = _lockstep([_dn_head_b(fa[ci][h], s_ref[ci, h]) for ci, h in pairs])
        for (ci, h), (vn, o, _) in zip(pairs, fwd):
            vnew[ci, h] = vn
            do[ci, h], dz, dnw_h = _dn_post_bwd(o, _blk(z, ci, h), nw, _blk(dy, ci, h))
            dnw = dnw + dnw_h
            dz_o[ci * CH:(ci + 1) * CH, h * HD:(h + 1) * HD] = dz.astype(dz_o.dtype)
        ds = [dst[h] for h in range(DN_H)]
        seq = {}
        for ci in range(DN_CB - 1, -1, -1):
            res = _lockstep([_dn_head_bwd_b(fa[ci][h], vnew[ci, h], do[ci, h], ds[h], s_ref[ci, h])
                             for h in range(DN_H)])
            for h in range(DN_H):
                dvnew, ds[h], dkd, dcd = res[h]
                seq[ci, h] = (dvnew, dkd, dcd)
        for h in range(DN_H):
            dst[h] = ds[h]
        rest = _lockstep([_dn_head_bwd_c(
            fa[ci][h], vnew[ci, h], do[ci, h], *seq[ci, h], s_ref[ci, h], _blk(qkv, ci, h),
            _blk(qkv, ci, 2 * DN_H + h), beta_all[ci * CH:(ci + 1) * CH, h:h + 1], c) for ci, h in pairs])
        dbeta_rows, dgc_rows = [], []
        for ci in range(DN_CB):
            dbeta_c = jnp.zeros((CH, LANE), F32)
            dgc_c = jnp.zeros((CH, LANE), F32)
            for h in range(DN_H):
                dqh, dkh, dvh, dbeta, dgc = rest[ci * DN_H + h]
                dbeta_c = dbeta_c + jnp.where(c.lane == h, dbeta, 0.0)
                dgc_c = dgc_c + jnp.where(c.lane == DN_H + h, dgc, 0.0)
                rs = slice(ci * CH, (ci + 1) * CH)
                dpost[rs, h * HD:(h + 1) * HD] = dqh
                dpost[rs, (DN_H + h) * HD:(DN_H + h + 1) * HD] = dkh
                dpost[rs, (2 * DN_H + h) * HD:(2 * DN_H + h + 1) * HD] = dvh
            dbeta_rows.append(dbeta_c)
            dgc_rows.append(dgc_c)
        dbeta_all = jnp.concatenate(dbeta_rows, axis=0)
        dgc_all = jnp.concatenate(dgc_rows, axis=0)
        dg_all = _mm_hi(c.tril_blk, dgc_all, _TN)
        dpa = dg_all * (-ea) * sxa
        dpb = dbeta_all * beta_all * (1.0 - beta_all)
        is_b = c.lane < DN_H
        is_a = (c.lane >= DN_H) & (c.lane < 2 * DN_H)
        dg_o[...] = jnp.where(is_b, dpb, jnp.where(is_a, dpa, 0.0)).astype(dg_o.dtype)
        dpar_o[0:1, :] += jnp.where(is_a, jnp.sum(dg_all * g_all, axis=0, keepdims=True), 0.0)
        dpar_o[1:2, :] += jnp.where(is_a, jnp.sum(dpa, axis=0, keepdims=True), 0.0)
        dpar_o[2:3, :] += dnw
        dpre = dpost[...] * (sg * (1.0 + pre * (1.0 - sg)))
        for k in range(DN_K):
            dcw_o[k:k + 1, :] += jnp.sum(dpre * ext[pl.ds(5 + k, _DN_ROWS), :], axis=0, keepdims=True)
        x2[0:_DN_ROWS, :] = dpre
        dx_o[...] = (cw[3:4, :] * dpre + cw[2:3, :] * x2[pl.ds(1, _DN_ROWS), :]
                     + cw[1:2, :] * x2[pl.ds(2, _DN_ROWS), :]
                     + cw[0:1, :] * x2[pl.ds(3, _DN_ROWS), :]).astype(dx_o.dtype)
        x2[_DN_ROWS:, :] = dpre[0:SUB, :]

    rev = lambda n: ns - 1 - n
    return pl.pallas_call(
        body, grid=(ns,),
        in_specs=_dn_in_specs(rev) + [pl.BlockSpec((DN_CB, DN_H, HD, HD), lambda n: (rev(n), 0, 0, 0)),
                                      pl.BlockSpec((_DN_ROWS, DN_W), lambda n: (rev(n), 0))],
        out_specs=[pl.BlockSpec((_DN_ROWS, _QKV_W), lambda n: (rev(n), 0)),
                   pl.BlockSpec((_DN_ROWS, DN_W), lambda n: (rev(n), 0)),
                   pl.BlockSpec((_DN_ROWS, LANE), lambda n: (rev(n), 0)),
                   pl.BlockSpec((SUB, _QKV_W), lambda n: (0, 0)),
                   pl.BlockSpec((SUB, LANE), lambda n: (0, 0))],
        out_shape=(jax.ShapeDtypeStruct((t, _QKV_W), BF16), jax.ShapeDtypeStruct((t, DN_W), BF16),
                   jax.ShapeDtypeStruct((t, LANE), BF16), jax.ShapeDtypeStruct((SUB, _QKV_W), F32),
                   jax.ShapeDtypeStruct((SUB, LANE), F32)),
        scratch_shapes=[pltpu.VMEM((_DN_ROWS + SUB, _QKV_W), F32), pltpu.VMEM((DN_H, HD, HD), F32),
                        pltpu.VMEM((_DN_ROWS, _QKV_W), F32), pltpu.VMEM((_DN_ROWS + SUB, _QKV_W), F32)],
        compiler_params=_cp("arbitrary"), name="dn_bwd")(p, p, p, p, conv_w, par, states, dc)


_ANY = pl.BlockSpec(memory_space=pl.ANY)
_MESH = pl.DeviceIdType.MESH


def _me():
    return lax.axis_index("x"), lax.axis_index("y"), lax.axis_index("c")


def all_gather(x, name):
    def body(x_ref, out_ref, send_sems, recv_sems, local_sem):
        mx, my, mc = _me()
        me, sibling = (mx, my, mc), (mx, my, 1 - mc)
        chips = [(1 - mx, my), (mx, 1 - my), (1 - mx, 1 - my)]

        def slot(px, py, pc):
            return out_ref.at[4 * px + 2 * py + pc]

        def copy(k, block, to, src=None):
            return pltpu.make_async_remote_copy(
                src_ref=slot(*block) if src is None else src, dst_ref=slot(*block),
                send_sem=send_sems.at[k], recv_sem=recv_sems.at[k], device_id=to, device_id_type=_MESH)

        mine = pltpu.make_async_copy(x_ref, slot(*me), local_sem)
        mine.start()
        first = [copy(0, me, sibling, src=x_ref)]
        first += [copy(1 + j, me, (*chip, mc), src=x_ref) for j, chip in enumerate(chips)]
        for cp in first:
            cp.start()
        passed = [copy(4 + j, (*chip, mc), sibling) for j, chip in enumerate(chips)]
        for j, chip in enumerate(chips):
            copy(1 + j, (*chip, mc), me).wait_recv()
            passed[j].start()
        copy(0, sibling, me).wait_recv()
        for j, chip in enumerate(chips):
            copy(4 + j, (*chip, 1 - mc), me).wait_recv()
        for cp in first + passed:
            cp.wait_send()
        mine.wait()

    return pl.pallas_call(
        body, out_shape=jax.ShapeDtypeStruct((N_DEV,) + x.shape, x.dtype), in_specs=[_ANY], out_specs=_ANY,
        scratch_shapes=[pltpu.SemaphoreType.DMA((7,)), pltpu.SemaphoreType.DMA((7,)), pltpu.SemaphoreType.DMA],
        name=name)(x)


_HBM = pl.BlockSpec(memory_space=pltpu.HBM)
_SEM = pl.BlockSpec(memory_space=pltpu.SEMAPHORE)
_EFFECT = pltpu.SideEffectType.DATAFLOW_SIDE_EFFECTING
_TOKEN = jax.ShapeDtypeStruct((SUB, LANE), F32)


def _peers(mx, my, mc):
    for rel in range(1, N_DEV):
        yield (1 - mx if rel & 4 else mx, 1 - my if rel & 2 else my, 1 - mc if rel & 1 else mc)


def _in_hbm(a):
    return pltpu.with_memory_space_constraint(a, pltpu.HBM)


GATHER_SLOTS = (4, 3)


def gather_start(buf, phase, after, name):
    def body(buf_ref, after_ref, send_sem, recv_sem, thru, token):
        mx, my, mc = _me()
        sibling = (mx, my, 1 - mc)
        chips = [(1 - mx, my), (mx, 1 - my), (1 - mx, 1 - my)]
        if phase == 0:
            slot = buf_ref.at[4 * mx + 2 * my + mc]
            copies = [(slot, sibling)] + [(slot, (px, py, mc)) for px, py in chips]
        else:
            copies = [(buf_ref.at[4 * px + 2 * py + mc], sibling) for px, py in chips]
        for slot, peer in copies:
            pltpu.make_async_remote_copy(src_ref=slot, dst_ref=slot, send_sem=send_sem, recv_sem=recv_sem,
                                         device_id=peer, device_id_type=_MESH).start()
        token[...] = jnp.zeros_like(token)

    send_sem, recv_sem, thru, token = pl.pallas_call(
        body, name=name,
        out_shape=(pltpu.SemaphoreType.DMA(()), pltpu.SemaphoreType.DMA(()), pltpu.HBM(buf.shape, buf.dtype), _TOKEN),
        in_specs=[_HBM, _ANY], out_specs=(_SEM, _SEM, _HBM, pl.BlockSpec(memory_space=pltpu.VMEM)),
        input_output_aliases={0: 2},
        compiler_params=pltpu.CompilerParams(has_side_effects=_EFFECT))(_in_hbm(buf), after)
    return (send_sem, recv_sem), thru, token


def exchange_start(src, name):
    def body(src_ref, land_ref, send_sem, recv_sem, src_thru, land_thru, token):
        mx, my, mc = _me()
        me = 4 * mx + 2 * my + mc
        for px, py, pc in _peers(mx, my, mc):
            pltpu.make_async_remote_copy(
                src_ref=src_ref.at[4 * px + 2 * py + pc], dst_ref=land_ref.at[me], send_sem=send_sem,
                recv_sem=recv_sem, device_id=(px, py, pc), device_id_type=_MESH).start()
        token[...] = jnp.zeros_like(token)

    hbm = pltpu.HBM(src.shape, src.dtype)
    send_sem, recv_sem, src_thru, land_thru, token = pl.pallas_call(
        body, name=name,
        out_shape=(pltpu.SemaphoreType.DMA(()), pltpu.SemaphoreType.DMA(()), hbm, hbm, _TOKEN),
        in_specs=[_HBM, _HBM], out_specs=(_SEM, _SEM, _HBM, _HBM, pl.BlockSpec(memory_space=pltpu.VMEM)),
        input_output_aliases={0: 2, 1: 3},
        compiler_params=pltpu.CompilerParams(has_side_effects=_EFFECT))(
            _in_hbm(src), _in_hbm(lax.empty(src.shape, src.dtype)))
    return (send_sem, recv_sem), src_thru, land_thru, token


def transfer_wait(sems, bufs, after, name, slots=N_DEV - 1):
    n = len(bufs)

    def body(*refs):
        seven = refs[0].at[pl.ds(0, slots)]
        cp = pltpu.make_async_remote_copy(src_ref=seven, dst_ref=seven, send_sem=refs[n], recv_sem=refs[n + 1],
                                          device_id=_me(), device_id_type=_MESH)
        cp.wait_send()
        cp.wait_recv()

    outs = pl.pallas_call(
        body, name=name, out_shape=tuple(pltpu.HBM(b.shape, b.dtype) for b in bufs),
        in_specs=[_HBM] * n + [_SEM, _SEM, _ANY], out_specs=tuple([_HBM] * n),
        input_output_aliases={b: b for b in range(n)},
        compiler_params=pltpu.CompilerParams(has_side_effects=_EFFECT))(*bufs, sems[0], sems[1], after)
    return list(outs)


def sum_slabs(x, name, own=None, me=None):
    _, r, c = x.shape
    tr = _pick(r, max(SUB, (1 << 19) // c // SUB * SUB), SUB)
    out_shape = jax.ShapeDtypeStruct((r, c), F32)
    if own is None:
        def body(x_ref, o_ref):
            acc = x_ref[0].astype(F32)
            for s in range(1, N_DEV):
                acc = acc + x_ref[s].astype(F32)
            o_ref[...] = acc

        return pl.pallas_call(
            body, grid=(r // tr,), in_specs=[pl.BlockSpec((N_DEV, tr, c), lambda i: (0, i, 0))],
            out_specs=pl.BlockSpec((tr, c), lambda i: (i, 0)), out_shape=out_shape,
            compiler_params=_cp("parallel"), name=name)(x)

    def body_own(me_ref, x_ref, own_ref, o_ref):
        acc = None
        for s in range(N_DEV):
            val = jnp.where(me_ref[0] == s, own_ref[...], x_ref[s]).astype(F32)
            acc = val if acc is None else acc + val
        o_ref[...] = acc

    return pl.pallas_call(
        body_own, out_shape=out_shape, name=name, compiler_params=_cp("parallel"),
        grid_spec=pltpu.PrefetchScalarGridSpec(
            num_scalar_prefetch=1, grid=(r // tr,),
            in_specs=[pl.BlockSpec((N_DEV, tr, c), lambda i, me_ref: (0, i, 0)),
                      pl.BlockSpec((None, tr, c), lambda i, me_ref: (me_ref[0], i, 0))],
            out_specs=pl.BlockSpec((tr, c), lambda i, me_ref: (i, 0))))(me, x, own)


def adamw(w, g, m, v, name):
    r, c = w.shape
    tr = _pick(r, max(SUB, (1 << 18) // c // SUB * SUB), SUB)
    c1 = 1.0 / (1.0 - ADAM_B1 ** ADAM_STEP)
    c2 = 1.0 / (1.0 - ADAM_B2 ** ADAM_STEP)

    def body(w_ref, g_ref, m_ref, v_ref, d_o, m_o, v_o):
        gg = g_ref[...]
        mn = ADAM_B1 * m_ref[...] + (1.0 - ADAM_B1) * gg
        vn = ADAM_B2 * v_ref[...] + (1.0 - ADAM_B2) * (gg * gg)
        m_o[...] = mn
        v_o[...] = vn
        d_o[...] = -ADAM_LR * ((mn * c1) / (jnp.sqrt(vn * c2) + ADAM_EPS) + ADAM_WD * w_ref[...])

    spec = pl.BlockSpec((tr, c), lambda i: (i, 0))
    sds = jax.ShapeDtypeStruct((r, c), F32)
    return pl.pallas_call(body, grid=(r // tr,), in_specs=[spec] * 4, out_specs=[spec] * 3, out_shape=(sds,) * 3,
                          compiler_params=_cp("parallel"), name=name)(w, g, m, v)


def _adamw_nd(w, g, m, v, name):
    shp = w.shape
    f = lambda a: a.reshape(-1, shp[-1])
    return tuple(o.reshape(shp) for o in adamw(f(w), f(g), f(m), f(v), name))


def _pack(parts):
    flat = jnp.concatenate([a.reshape(-1).astype(F32) for a in parts])
    n = flat.shape[0]
    npad = -n % (PACK_ROWS * LANE)
    return jnp.pad(flat, (0, npad)).reshape(-1, LANE)


def _unpack(buf, shapes, lead=()):
    flat = buf.reshape(lead + (-1,))
    out, off = [], 0
    for s in shapes:
        n = math.prod(s)
        out.append(flat[..., off:off + n].reshape(lead + tuple(s)))
        off += n
    return out


ROPE_THETA = 10000.0

_SMALL = ("norm_mix_pre", "dn_conv_w", "dn_a_log", "dn_dt_bias", "dn_norm_w", "pool_w", "pool_scale", "swa_sinks",
          "norm_mix_post", "norm_ffn_pre", "ffn_conv_w", "ffn_conv_b", "norm_ffn_post")
_BIG = ("w_in", "w_out", "ffn_w_up", "ffn_w_down")
_ORDER = ("norm_mix_pre", "w_in", "dn_conv_w", "dn_a_log", "dn_dt_bias", "dn_norm_w", "pool_w", "pool_scale",
          "swa_sinks", "w_out", "norm_mix_post", "norm_ffn_pre", "ffn_w_up", "ffn_conv_w", "ffn_conv_b",
          "ffn_w_down", "norm_ffn_post")


def _step(x, positions, loss_target, w, m, v):
    nl = w["w_in"].shape[0]
    t, d = x.shape[1], x.shape[2]
    nb = w["ffn_w_up"].shape[2]
    f = nb * N_DEV // 2
    me = 4 * lax.axis_index("x") + 2 * lax.axis_index("y") + lax.axis_index("c")
    x_in, tgt = x[0], loss_target[0]

    inv_freq = 1.0 / (ROPE_THETA ** (jnp.arange(0, HD, 2, dtype=F32) / HD))
    ang = positions[0].astype(F32)[:, None] * inv_freq
    cos, sin = jnp.cos(ang), jnp.sin(ang)
    cos2 = jnp.concatenate([cos, cos], axis=1)
    sin2 = jnp.concatenate([-sin, sin], axis=1)

    conv_shapes = [w["dn_conv_w"].shape, w["ffn_conv_w"].shape]
    gathered_conv = all_gather(_pack([w["dn_conv_w"], w["ffn_conv_w"]]), "ag_conv")
    dn_cw_g, ffn_cw_g = _unpack(gathered_conv, conv_shapes, lead=(N_DEV,))
    dn_cw = jnp.moveaxis(dn_cw_g, 0, 2).reshape(nl, DN_K, _QKV_W)
    ffn_cw = jnp.moveaxis(ffn_cw_g, 0, 1).reshape(nl, 2, N_DEV // 2, 3, nb)
    ffn_cb = w["ffn_conv_b"].reshape(nl, 2, N_DEV // 2, 1, nb)

    def lane_row(vec, off):
        return jnp.zeros((LANE,), F32).at[off:off + vec.shape[0]].set(vec)

    dn_par = jnp.stack([
        jnp.zeros((SUB, LANE), F32).at[0].set(lane_row(w["dn_a_log"][l], DN_H))
        .at[1].set(lane_row(w["dn_dt_bias"][l], DN_H)).at[2].set(w["dn_norm_w"][l]) for l in range(nl)])
    sinks = jnp.stack([lane_row(w["swa_sinks"][l], 0)[None, :] for l in range(nl)])

    def place(shard):
        return lax.dynamic_update_slice(lax.empty((N_DEV,) + shard.shape, shard.dtype), shard[None], (me, 0, 0))

    kinds = ("w_in", "w_out", "ffn_w_up", "ffn_w_down")
    flight = {}
    tag = lambda i: f"{kinds[i % 4]}_{i // 4}"

    def start_first(i, after):
        if i >= 4 * nl:
            return jnp.zeros(_TOKEN.shape, F32)
        l, k = divmod(i, 4)
        shard = _align_in(w[kinds[k]][l]) if k == 0 else w[kinds[k]][l]
        sems, buf, token = gather_start(place(shard.astype(BF16)), 0, after, f"ag_start_{tag(i)}")
        flight[i] = (sems, buf)
        return token

    def start_second(i, after):
        if i >= 4 * nl:
            return jnp.zeros(_TOKEN.shape, F32)
        arrived = transfer_wait(flight[i][0], [flight[i][1]], after, f"ag_wait_{tag(i)}", GATHER_SLOTS[0])[0]
        sems, buf, token = gather_start(arrived, 1, after, f"ag_pass_{tag(i)}")
        flight[i] = (sems, buf)
        return token

    def gathered(l, k, after):
        i = 4 * l + k
        late = start_second(1, after) if i == 1 else None
        got = transfer_wait(flight[i][0], [flight[i][1]], after if late is None else late, f"ag_done_{tag(i)}",
                            GATHER_SLOTS[1])[0]
        first = start_first(i + 3, got)
        if i == 0:
            return got, first[0, 0]
        return got, (start_second(i + 1, first) + first)[0, 0]

    win, wout, wup, wdown = [None] * nl, [None] * nl, [None] * nl, [None] * nl
    row = lambda a, l: a[l][None, :]
    g1, g2, g3, g4 = w["norm_mix_pre"], w["norm_mix_post"], w["norm_ffn_pre"], w["norm_ffn_post"]

    saved = []
    xl = x_in
    passed = start_second(0, start_first(0, gathered_conv))
    h1 = norm_first(xl, row(g1, 0) + (passed + start_first(1, passed) + start_first(2, passed))[0, 0])
    for l in range(nl):
        buf, tk = gathered(l, 0, h1)
        win[l] = buf.reshape(d, PW)
        p = mm_nn(h1, win[l], F32, "mm_in")
        y_dn, states = dn_fwd(p, dn_cw[l], dn_par[l] + tk)
        y_pool = pool_fwd(p, w["pool_w"][l], row(w["pool_scale"], l))
        y_swa = swa_fwd(p, cos2, sin2, sinks[l])
        c = jnp.concatenate([y_dn, y_swa, y_pool], axis=1)
        buf, tk = gathered(l, 1, c)
        wout[l] = _perm_mix_rows(buf.reshape(MIX_W, d))
        mix = mm_nn(c, wout[l], F32, "mm_out")
        x1, h2 = post_pre(xl, mix, row(g2, l) + tk, row(g3, l))
        wup[l], tk = gathered(l, 2, h2)
        u0 = mm_up(h2, wup[l], "mm_up")
        act = glu_fwd(u0, ffn_cw[l], ffn_cb[l] + tk)
        buf, tk = gathered(l, 3, act)
        wdown[l] = buf.reshape(f, d)
        fo = mm_nn(act, wdown[l], F32, "mm_down")
        saved.append(dict(x=xl, h1=h1, p=p, states=states, c=c, mix=mix, x1=x1, h2=h2, u0=u0, act=act, f=fo))
        if l < nl - 1:
            xl, h1 = post_pre(x1, fo, row(g4, l) + tk, row(g1, l + 1))
        else:
            dx, loss_part = post_loss(x1, fo, row(g4, l) + tk, tgt)

    small_g = [dict() for _ in range(nl)]
    pending = {name: [None] * nl for name in _BIG}

    def exchange(name, l, dw):
        sems, src, land, token = exchange_start(dw, f"xch_start_{name}_{l}")
        pending[name][l] = (sems, src, land)
        return token[0, 0]

    df, small_g[nl - 1]["norm_ffn_post"] = bwd_norms(dx, post=(saved[-1]["f"], row(g4, nl - 1)))
    for l in range(nl - 1, -1, -1):
        s, sg = saved[l], small_g[l]
        dact = mm_nt(df, wdown[l], F32, "mm_down_d")
        tk = exchange("ffn_w_down", l, mm_tn(s["act"], df, BF16, "mm_down_w").reshape(N_DEV, f // N_DEV, d))
        du0, dcw = glu_bwd(dact, s["u0"], ffn_cw[l], ffn_cb[l])
        sg["ffn_conv"] = dcw
        dh2 = mm_up_dgrad(du0, wup[l], "mm_up_d")
        tk = tk + exchange("ffn_w_up", l, mm_up_wgrad(s["h2"], du0, "mm_up_w"))
        dx1, sg["norm_ffn_pre"], dmix, sg["norm_mix_post"] = bwd_norms(
            dx, pre=(dh2, s["x1"], row(g3, l) + tk), post=(s["mix"], row(g2, l)))
        dc = mm_nt(dmix, wout[l], F32, "mm_out_d")
        tk = exchange("w_out", l, _unperm_mix_rows(mm_tn(s["c"], dmix, BF16, "mm_out_w"))
                      .reshape(N_DEV, MIX_W // N_DEV, d))
        dqkv, dz, dgate, sg["dn_conv_w"], sg["dn_par"] = dn_bwd(s["p"], dc, s["states"], dn_cw[l], dn_par[l])
        dpool, sg["pool_w"], sg["pool_scale"] = pool_bwd(s["p"], dc, w["pool_w"][l], row(w["pool_scale"], l))
        dsq, dsk, dsv, sg["swa_sinks"] = swa_bwd(s["p"], dc, cos2, sin2, sinks[l])
        dp = jnp.concatenate([dqkv, dz, dsq, dsk, dsv, dpool, dgate], axis=1)
        dh1 = mm_nt(dp, win[l], F32, "mm_in_d")
        tk = tk + exchange("w_in", l, mm_tn(s["h1"], dp, BF16, "mm_in_w").reshape(N_DEV, d // N_DEV, PW))
        if l > 0:
            dx, sg["norm_mix_pre"], df, small_g[l - 1]["norm_ffn_post"] = bwd_norms(
                dx1, pre=(dh1, s["x"], row(g1, l) + tk), post=(saved[l - 1]["f"], row(g4, l - 1)))
        else:
            grad_x, sg["norm_mix_pre"] = bwd_norms(dx1, pre=(dh1, s["x"], row(g1, 0) + tk))

    keys = ("norm_mix_pre", "norm_mix_post", "norm_ffn_pre", "norm_ffn_post", "dn_conv_w", "dn_par", "pool_w",
            "pool_scale", "swa_sinks", "ffn_conv")
    grads = {}
    parts = [small_g[l][k] for l in range(nl) for k in keys] + [loss_part]
    shapes = [a.shape for a in parts]
    summed = sum_slabs(all_gather(_pack(parts), "ag_small"), "sum_small")
    vals = _unpack(summed, shapes)
    loss = vals[-1][0, 0]
    sm = [dict(zip(keys, vals[l * len(keys):(l + 1) * len(keys)])) for l in range(nl)]
    st = lambda fn: jnp.stack([fn(sm[l]) for l in range(nl)])
    for k in ("norm_mix_pre", "norm_mix_post", "norm_ffn_pre", "norm_ffn_post"):
        grads[k] = st(lambda q: q[k][0])
    grads["dn_conv_w"] = lax.dynamic_slice_in_dim(st(lambda q: q["dn_conv_w"][0:DN_K]), me * (_QKV_W // N_DEV),
                                                  _QKV_W // N_DEV, axis=2)
    grads["dn_a_log"] = st(lambda q: q["dn_par"][0, DN_H:2 * DN_H])
    grads["dn_dt_bias"] = st(lambda q: q["dn_par"][1, DN_H:2 * DN_H])
    grads["dn_norm_w"] = st(lambda q: q["dn_par"][2])
    grads["pool_w"] = st(lambda q: q["pool_w"])
    grads["pool_scale"] = st(lambda q: q["pool_scale"][0])
    grads["swa_sinks"] = st(lambda q: q["swa_sinks"][0, 0:SWA_H])
    conv_all = st(lambda q: q["ffn_conv"].reshape(N_DEV, SUB, nb))
    grads["ffn_conv_w"] = lax.dynamic_index_in_dim(conv_all, me, axis=1, keepdims=False)[:, 0:3, :]
    grads["ffn_conv_b"] = conv_all[:, :, 3, :].reshape(nl, 2 * f)

    delta, new_m, new_v = {}, {}, {}
    shapes = [w[k].shape for k in _SMALL]
    pk = lambda tree: _pack([tree[k] for k in _SMALL])
    outs = adamw(pk(w), pk(grads), pk(m), pk(v), "adamw_small")
    for tree, buf in zip((delta, new_m, new_v), outs):
        for k, a in zip(_SMALL, _unpack(buf, shapes)):
            tree[k] = a
    me_arr = jnp.reshape(me, (1,)).astype(jnp.int32)
    after = outs[0]
    for name in ("ffn_w_down", "ffn_w_up", "w_out", "w_in"):
        per_layer = []
        for l in range(nl - 1, -1, -1):
            sems, src, land = pending[name][l]
            src, land = transfer_wait(sems, [src, land], after, f"xch_wait_{name}_{l}")
            per_layer.append(sum_slabs(land, "sum_" + name, own=src, me=me_arr))
        g = jnp.stack(per_layer[::-1])
        grads[name] = _unalign_in(g) if name == "w_in" else g
        delta[name], new_m[name], new_v[name] = _adamw_nd(w[name], grads[name], m[name], v[name], "adamw_" + name)
        after = delta[name]

    return (loss, grad_x[None], *[grads[k] for k in _ORDER], *[delta[k] for k in _ORDER],
            *[new_m[k] for k in _ORDER], *[new_v[k] for k in _ORDER])


def kernel(x, positions, norm_mix_pre, w_in, dn_conv_w, dn_a_log, dn_dt_bias, dn_norm_w, pool_w, pool_scale, swa_sinks, w_out, norm_mix_post, norm_ffn_pre, ffn_w_up, ffn_conv_w, ffn_conv_b, ffn_w_down, norm_ffn_post, loss_target, m_norm_mix_pre, m_w_in, m_dn_conv_w, m_dn_a_log, m_dn_dt_bias, m_dn_norm_w, m_pool_w, m_pool_scale, m_swa_sinks, m_w_out, m_norm_mix_post, m_norm_ffn_pre, m_ffn_w_up, m_ffn_conv_w, m_ffn_conv_b, m_ffn_w_down, m_norm_ffn_post, v_norm_mix_pre, v_w_in, v_dn_conv_w, v_dn_a_log, v_dn_dt_bias, v_dn_norm_w, v_pool_w, v_pool_scale, v_swa_sinks, v_w_out, v_norm_mix_post, v_norm_ffn_pre, v_ffn_w_up, v_ffn_conv_w, v_ffn_conv_b, v_ffn_w_down, v_norm_ffn_post):
    args = locals()
    w = {k: args[k] for k in _ORDER}
    m = {k: args["m_" + k] for k in _ORDER}
    v = {k: args["v_" + k] for k in _ORDER}
    return _step(x, positions, loss_target, w, m, v)
```

```python
import functools
import math

import jax
import jax.numpy as jnp
from jax import lax
from jax.experimental import pallas as pl
from jax.experimental.pallas import tpu as pltpu

F32 = jnp.float32
BF16 = jnp.bfloat16
MXU_DT = jnp.bfloat16
HI = lax.Precision.HIGHEST

N_DEV = 8
LANE = 128
SUB = 8
VMEM_LIMIT = 56 * 1024 * 1024
ROW_TILE = 512
NORM_TILE = 256
MM_TM, MM_TN, MM_TK = 512, 1664, 2816
MM_TN_NT = 2048
PACK_ROWS = 512

HD = 128
DN_H, DN_W, DN_K, CH = 6, 768, 4, 64
POOL_G = 4
SWA_H, SWA_KV, SWA_G, SWA_BLK = 6, 2, 3, 128
EPS = 1e-6
SCALE = HD ** -0.5
NEG = -1e30

O_QKV, O_Z, O_SQ, O_SK, O_SV, O_POOL, O_GATE, PW = 0, 2304, 3072, 3840, 4096, 4352, 4864, 4992
IN_W = 4876
MIX_W = 2048

ADAM_LR, ADAM_B1, ADAM_B2, ADAM_EPS, ADAM_WD, ADAM_STEP = 0.001, 0.9, 0.999, 1e-08, 0.01, 10


def _pick(n, cap, mult=LANE):
    best = None
    for d in range(mult, min(n, cap) + 1, mult):
        if n % d == 0:
            best = d
    return best if best is not None else n


def _cp(*sem):
    return pltpu.CompilerParams(dimension_semantics=sem, vmem_limit_bytes=VMEM_LIMIT)


def _dot(a, b, dims):
    return lax.dot_general(a.astype(MXU_DT), b.astype(MXU_DT), dims, preferred_element_type=F32)


_NN = (((1,), (0,)), ((), ()))
_NT = (((1,), (1,)), ((), ()))
_TN = (((0,), (0,)), ((), ()))


def _mm(a, b):
    return _dot(a, b, _NN)


def _mm_nt(a, b):
    return _dot(a, b, _NT)


def _mm_tn(a, b):
    return _dot(a, b, _TN)


def _mm_hi(a, b, dims=_NN):
    return lax.dot_general(a, b, dims, precision=HI, preferred_element_type=F32)


def _sigmoid(x):
    return jax.nn.sigmoid(x)


def _softplus(x):
    return jnp.maximum(x, 0.0) + jnp.log(1.0 + jnp.exp(-jnp.abs(x)))


def _align_in(w):
    pad = jnp.zeros(w.shape[:-1] + (PW - IN_W,), w.dtype)
    return jnp.concatenate([w[..., 0:3072], w[..., 3596:4364], w[..., 4364:4620], w[..., 4620:4876],
                            w[..., 3084:3596], w[..., 3072:3084], pad], axis=-1)


def _unalign_in(g):
    return jnp.concatenate([g[..., 0:3072], g[..., O_GATE:O_GATE + 12], g[..., O_POOL:O_POOL + 512],
                            g[..., O_SQ:O_SQ + 768], g[..., O_SK:O_SK + 256], g[..., O_SV:O_SV + 256]], axis=-1)


def _perm_mix_rows(w):
    return jnp.concatenate([w[0:768], w[1280:2048], w[768:1280]], axis=0)


def _unperm_mix_rows(w):
    return jnp.concatenate([w[0:768], w[1536:2048], w[768:1536]], axis=0)


def _mm_call(name, a, b, out_shape, grid, a_spec, b_spec, o_spec, dims, acc_shape):
    nk = grid[2]
    if nk == 1:
        def body_once(a_ref, b_ref, o_ref):
            o_ref[...] = _dot(a_ref[...], b_ref[...], dims).astype(o_ref.dtype)

        return pl.pallas_call(
            body_once, grid=grid, in_specs=[a_spec, b_spec], out_specs=o_spec, out_shape=out_shape,
            compiler_params=_cp("parallel", "parallel", "arbitrary"), name=name)(a, b)

    def body(a_ref, b_ref, o_ref, acc_ref):
        k = pl.program_id(2)

        @pl.when(k == 0)
        def _():
            acc_ref[...] = jnp.zeros_like(acc_ref)

        acc_ref[...] += _dot(a_ref[...], b_ref[...], dims)

        @pl.when(k == nk - 1)
        def _():
            o_ref[...] = acc_ref[...].astype(o_ref.dtype)

    return pl.pallas_call(
        body, grid=grid, in_specs=[a_spec, b_spec], out_specs=o_spec, out_shape=out_shape,
        scratch_shapes=[pltpu.VMEM(acc_shape, F32)],
        compiler_params=_cp("parallel", "parallel", "arbitrary"), name=name)(a, b)


def mm_nn(a, b, out_dtype, name):
    (m, k), n = a.shape, b.shape[1]
    tm, tn, tk = _pick(m, MM_TM, SUB), _pick(n, MM_TN), _pick(k, MM_TK)
    return _mm_call(name, a, b, jax.ShapeDtypeStruct((m, n), out_dtype), (m // tm, n // tn, k // tk),
                    pl.BlockSpec((tm, tk), lambda i, j, kk: (i, kk)),
                    pl.BlockSpec((tk, tn), lambda i, j, kk: (kk, j)),
                    pl.BlockSpec((tm, tn), lambda i, j, kk: (i, j)), _NN, (tm, tn))


def mm_nt(a, b, out_dtype, name):
    (m, k), n = a.shape, b.shape[0]
    tm, tn, tk = _pick(m, MM_TM, SUB), _pick(n, MM_TN_NT), _pick(k, MM_TK)
    return _mm_call(name, a, b, jax.ShapeDtypeStruct((m, n), out_dtype), (m // tm, n // tn, k // tk),
                    pl.BlockSpec((tm, tk), lambda i, j, kk: (i, kk)),
                    pl.BlockSpec((tn, tk), lambda i, j, kk: (j, kk)),
                    pl.BlockSpec((tm, tn), lambda i, j, kk: (i, j)), _NT, (tm, tn))


def mm_tn(a, b, out_dtype, name):
    (k, m), n = a.shape, b.shape[1]
    tm, tn, tk = _pick(m, MM_TM), _pick(n, MM_TN), _pick(k, MM_TK, SUB)
    return _mm_call(name, a, b, jax.ShapeDtypeStruct((m, n), out_dtype), (m // tm, n // tn, k // tk),
                    pl.BlockSpec((tk, tm), lambda i, j, kk: (kk, i)),
                    pl.BlockSpec((tk, tn), lambda i, j, kk: (kk, j)),
                    pl.BlockSpec((tm, tn), lambda i, j, kk: (i, j)), _TN, (tm, tn))


def mm_up(h, wblk, name):
    (t, d), (nblk, _, nb) = h.shape, wblk.shape
    tm, tk = _pick(t, MM_TM, SUB), _pick(d, MM_TK)
    hb = nblk // 2
    return _mm_call(name, h, wblk, jax.ShapeDtypeStruct((2, t, hb * nb), F32), (t // tm, nblk, d // tk),
                    pl.BlockSpec((tm, tk), lambda i, j, kk: (i, kk)),
                    pl.BlockSpec((None, tk, nb), lambda i, j, kk: (j, kk, 0)),
                    pl.BlockSpec((None, tm, nb), lambda i, j, kk: (j // hb, i, j % hb)), _NN, (tm, nb))


def mm_up_dgrad(du0, wblk, name):
    (_, t, _), (nblk, d, nb) = du0.shape, wblk.shape
    tm, tn = _pick(t, MM_TM, SUB), _pick(d, MM_TN_NT)
    hb = nblk // 2
    return _mm_call(name, du0, wblk, jax.ShapeDtypeStruct((t, d), F32), (t // tm, d // tn, nblk),
                    pl.BlockSpec((None, tm, nb), lambda i, j, kk: (kk // hb, i, kk % hb)),
                    pl.BlockSpec((None, tn, nb), lambda i, j, kk: (kk, j, 0)),
                    pl.BlockSpec((tm, tn), lambda i, j, kk: (i, j)), _NT, (tm, tn))


def mm_up_wgrad(h, du0, name):
    (t, d), (_, _, f) = h.shape, du0.shape
    nb = f // (N_DEV // 2)
    hb = N_DEV // 2
    tm, tk = _pick(d, MM_TM), _pick(t, MM_TK, SUB)
    return _mm_call(name, h, du0, jax.ShapeDtypeStruct((N_DEV, d, nb), BF16), (d // tm, N_DEV, t // tk),
                    pl.BlockSpec((tk, tm), lambda i, j, kk: (kk, i)),
                    pl.BlockSpec((None, tk, nb), lambda i, j, kk: (j // hb, kk, j % hb)),
                    pl.BlockSpec((None, tm, nb), lambda i, j, kk: (j, i, 0)), _TN, (tm, nb))


def _rms(x, w):
    r = lax.rsqrt(jnp.mean(x * x, axis=-1, keepdims=True) + EPS)
    return x * r * w


def _rms_bwd(dy, x, w):
    r = lax.rsqrt(jnp.mean(x * x, axis=-1, keepdims=True) + EPS)
    xh = x * r
    dxh = dy * w
    dx = r * (dxh - xh * jnp.mean(dxh * xh, axis=-1, keepdims=True))
    return dx, jnp.sum(dy * xh, axis=0, keepdims=True)


def _row_spec(tb, d):
    return pl.BlockSpec((tb, d), lambda i: (i, 0))


def _fix_spec(r, d):
    return pl.BlockSpec((r, d), lambda i: (0, 0))


def norm_first(x, w):
    t, d = x.shape
    tb = _pick(t, NORM_TILE, SUB)

    def body(x_ref, w_ref, h_ref):
        h_ref[...] = _rms(x_ref[...], w_ref[...]).astype(h_ref.dtype)

    return pl.pallas_call(body, grid=(t // tb,), in_specs=[_row_spec(tb, d), _fix_spec(1, d)],
                          out_specs=_row_spec(tb, d), out_shape=jax.ShapeDtypeStruct((t, d), BF16),
                          compiler_params=_cp("parallel"), name="norm_first")(x, w)


def post_pre(x, y, w_post, w_pre):
    t, d = x.shape
    tb = _pick(t, NORM_TILE, SUB)

    def body(x_ref, y_ref, wp_ref, wq_ref, xn_ref, h_ref):
        xn = x_ref[...] + _rms(y_ref[...], wp_ref[...])
        xn_ref[...] = xn
        h_ref[...] = _rms(xn, wq_ref[...]).astype(h_ref.dtype)

    return pl.pallas_call(
        body, grid=(t // tb,),
        in_specs=[_row_spec(tb, d), _row_spec(tb, d), _fix_spec(1, d), _fix_spec(1, d)],
        out_specs=[_row_spec(tb, d), _row_spec(tb, d)],
        out_shape=(jax.ShapeDtypeStruct((t, d), F32), jax.ShapeDtypeStruct((t, d), BF16)),
        compiler_params=_cp("parallel"), name="post_pre")(x, y, w_post, w_pre)


def post_loss(x, y, w_post, target):
    t, d = x.shape
    tb = _pick(t, NORM_TILE, SUB)

    def body(x_ref, y_ref, wp_ref, t_ref, g_ref, l_ref):
        err = x_ref[...] + _rms(y_ref[...], wp_ref[...]) - t_ref[...]
        g_ref[...] = err * (1.0 / d)

        @pl.when(pl.program_id(0) == 0)
        def _():
            l_ref[...] = jnp.zeros_like(l_ref)

        part = 0.5 * jnp.sum(jnp.mean(err * err, axis=-1, keepdims=True), axis=0, keepdims=True)
        l_ref[...] += jnp.broadcast_to(part, l_ref.shape)

    return pl.pallas_call(
        body, grid=(t // tb,),
        in_specs=[_row_spec(tb, d), _row_spec(tb, d), _fix_spec(1, d), _row_spec(tb, d)],
        out_specs=[_row_spec(tb, d), _fix_spec(1, LANE)],
        out_shape=(jax.ShapeDtypeStruct((t, d), F32), jax.ShapeDtypeStruct((1, LANE), F32)),
        compiler_params=_cp("arbitrary"), name="post_loss")(x, y, w_post, target)


def bwd_norms(dx_in, *, pre=None, post=None):
    t, d = dx_in.shape
    tb = _pick(t, NORM_TILE, SUB)
    has_pre, has_post = pre is not None, post is not None

    def body(*refs):
        refs = list(refs)
        dxi = refs.pop(0)
        if has_pre:
            dh, x, wq = refs.pop(0), refs.pop(0), refs.pop(0)
        if has_post:
            y, wp = refs.pop(0), refs.pop(0)
        first = pl.program_id(0) == 0
        dx = dxi[...]
        if has_pre:
            dxo, dwq = refs.pop(0), refs.pop(0)
            g, dw = _rms_bwd(dh[...], x[...], wq[...])
            dx = dx + g
            dxo[...] = dx

            @pl.when(first)
            def _():
                dwq[...] = jnp.zeros_like(dwq)

            dwq[...] += dw
        if has_post:
            dyo, dwp = refs.pop(0), refs.pop(0)
            g, dw = _rms_bwd(dx, y[...], wp[...])
            dyo[...] = g.astype(dyo.dtype)

            @pl.when(first)
            def _():
                dwp[...] = jnp.zeros_like(dwp)

            dwp[...] += dw

    ins, in_specs, outs, out_specs = [dx_in], [_row_spec(tb, d)], [], []
    if has_pre:
        ins += list(pre)
        in_specs += [_row_spec(tb, d), _row_spec(tb, d), _fix_spec(1, d)]
        outs += [jax.ShapeDtypeStruct((t, d), F32), jax.ShapeDtypeStruct((1, d), F32)]
        out_specs += [_row_spec(tb, d), _fix_spec(1, d)]
    if has_post:
        ins += list(post)
        in_specs += [_row_spec(tb, d), _fix_spec(1, d)]
        outs += [jax.ShapeDtypeStruct((t, d), BF16), jax.ShapeDtypeStruct((1, d), F32)]
        out_specs += [_row_spec(tb, d), _fix_spec(1, d)]
    name = "bwd_norms" + ("_pre" if has_pre else "") + ("_post" if has_post else "")
    return pl.pallas_call(body, grid=(t // tb,), in_specs=in_specs, out_specs=out_specs, out_shape=tuple(outs),
                          compiler_params=_cp("arbitrary"), name=name)(*ins)


GLU_ROWS = 32


def _ffn_conv_blk(blk, cw, cb):
    r = blk.shape[1] - SUB
    x0, x1, x2 = blk[:, 6:6 + r], blk[:, 7:7 + r], blk[:, 8:8 + r]
    return x0, x1, x2, cw[:, 0:1, :] * x0 + cw[:, 1:2, :] * x1 + cw[:, 2:3, :] * x2 + cb


def _glu_specs(tb, nb, hpb, row_of):
    tile = pl.BlockSpec((2, tb, nb), lambda j, i: (0, row_of(i), j))
    halo = pl.BlockSpec((2, SUB, nb), lambda j, i: (0, jnp.maximum(row_of(i) * hpb - 1, 0), j))
    cw = pl.BlockSpec((2, None, 3, nb), lambda j, i: (0, j, 0, 0))
    cb = pl.BlockSpec((2, None, 1, nb), lambda j, i: (0, j, 0, 0))
    return tile, halo, cw, cb


def glu_fwd(u0, cw, cb):
    _, t, f = u0.shape
    nb = cw.shape[-1]
    tb = _pick(t, ROW_TILE, SUB)
    nt, hpb = t // tb, tb // SUB

    def body(u, h, cwr, cbr, o_ref, e):
        i = pl.program_id(1)
        e[:, 0:SUB, :] = jnp.where(i > 0, h[...], 0.0)
        e[:, SUB:, :] = u[...]
        w, bias = cwr[...], cbr[...]

        def rows(g, carry):
            s = pl.multiple_of(g * GLU_ROWS, GLU_ROWS)
            for lg in range(nb // LANE):
                ls = slice(lg * LANE, (lg + 1) * LANE)
                ab = _ffn_conv_blk(e[:, pl.ds(s, GLU_ROWS + SUB), ls], w[:, :, ls], bias[:, :, ls])[3]
                a, b = ab[0], ab[1]
                o_ref[pl.ds(s, GLU_ROWS), ls] = (a * _sigmoid(a) * b).astype(o_ref.dtype)
            return carry

        lax.fori_loop(0, tb // GLU_ROWS, rows, 0)

    return pl.pallas_call(
        body, grid=(f // nb, nt), in_specs=list(_glu_specs(tb, nb, hpb, lambda i: i)),
        out_specs=pl.BlockSpec((tb, nb), lambda j, i: (i, j)),
        out_shape=jax.ShapeDtypeStruct((t, f), BF16),
        scratch_shapes=[pltpu.VMEM((2, tb + SUB, nb), F32)],
        compiler_params=_cp("parallel", "arbitrary"), name="glu_fwd")(u0, u0, cw, cb)


def glu_bwd(dact, u0, cw, cb):
    _, t, f = u0.shape
    nb = cw.shape[-1]
    tb = _pick(t, ROW_TILE, SUB)
    nt, hpb = t // tb, tb // SUB

    def body(d_ref, u, h, cwr, cbr, du_o, dc_o, e, x2):
        i = pl.program_id(1)
        r = nt - 1 - i
        e[:, 0:SUB, :] = jnp.where(r > 0, h[...], 0.0)
        e[:, SUB:, :] = u[...]
        w, bias = cwr[...], cbr[...]

        @pl.when(i == 0)
        def _():
            dc_o[...] = jnp.zeros_like(dc_o)
            x2[:, tb:, :] = jnp.zeros((2, SUB, nb), F32)

        fold = lambda v: jnp.sum(v.reshape(2, GLU_ROWS // SUB, SUB, LANE), axis=1)
        for lg in range(nb // LANE):
            ls = slice(lg * LANE, (lg + 1) * LANE)
            wl, bl = w[:, :, ls], bias[:, :, ls]

            def grads(g, acc):
                s = pl.multiple_of(g * GLU_ROWS, GLU_ROWS)
                x0, x1, xc, ab = _ffn_conv_blk(e[:, pl.ds(s, GLU_ROWS + SUB), ls], wl, bl)
                a, b = ab[0], ab[1]
                sa = _sigmoid(a)
                d = d_ref[pl.ds(s, GLU_ROWS), ls]
                x2[0, pl.ds(s, GLU_ROWS), ls] = d * b * (sa * (1.0 + a * (1.0 - sa)))
                x2[1, pl.ds(s, GLU_ROWS), ls] = d * (a * sa)
                du = x2[:, pl.ds(s, GLU_ROWS), ls]
                return (acc[0] + fold(du * x0), acc[1] + fold(du * x1), acc[2] + fold(du * xc), acc[3] + fold(du))

            zero = jnp.zeros((2, SUB, LANE), F32)
            acc = lax.fori_loop(0, tb // GLU_ROWS, grads, (zero, zero, zero, zero))
            for k in range(4):
                dc_o[:, k:k + 1, ls] += jnp.sum(acc[k], axis=1, keepdims=True)

            def transposed_conv(g, carry):
                s = pl.multiple_of(g * GLU_ROWS, GLU_ROWS)
                blk = x2[:, pl.ds(s, GLU_ROWS + SUB), ls]
                du_o[:, pl.ds(s, GLU_ROWS), ls] = (
                    wl[:, 2:3, :] * blk[:, 0:GLU_ROWS] + wl[:, 1:2, :] * blk[:, 1:1 + GLU_ROWS]
                    + wl[:, 0:1, :] * blk[:, 2:2 + GLU_ROWS]).astype(du_o.dtype)
                return carry

            lax.fori_loop(0, tb // GLU_ROWS, transposed_conv, 0)
        x2[:, tb:, :] = x2[:, 0:SUB, :]

    rev = lambda i: nt - 1 - i
    return pl.pallas_call(
        body, grid=(f // nb, nt),
        in_specs=[pl.BlockSpec((tb, nb), lambda j, i: (rev(i), j))] + list(_glu_specs(tb, nb, hpb, rev)),
        out_specs=[pl.BlockSpec((2, tb, nb), lambda j, i: (0, rev(i), j)),
                   pl.BlockSpec((2, None, SUB, nb), lambda j, i: (0, j, 0, 0))],
        out_shape=(jax.ShapeDtypeStruct((2, t, f), BF16), jax.ShapeDtypeStruct((2, f // nb, SUB, nb), F32)),
        scratch_shapes=[pltpu.VMEM((2, tb + SUB, nb), F32), pltpu.VMEM((2, tb + SUB, nb), F32)],
        compiler_params=_cp("arbitrary", "arbitrary"), name="glu_bwd")(dact, u0, u0, cw, cb)


POOL_HALO = 16
_PCOL = O_POOL // LANE
_CPOOL = 1536 // LANE


def _pool_sel(g, v2, v4, v8, v16):
    return jnp.where(g == 0, v2, jnp.where(g == 1, v4, jnp.where(g == 2, v8, v16)))


def _pool_cnt(g, t0, n):
    win = _pool_sel(g, 2, 4, 8, 16)
    tpos = t0 + lax.broadcasted_iota(jnp.int32, (n, 1), 0)
    return jnp.minimum(tpos + 1, win).astype(F32)


def _pool_core(e, g, t0, tb):
    s2 = e + pltpu.roll(e, 1, 0)
    s4 = s2 + pltpu.roll(s2, 2, 0)
    s8 = s4 + pltpu.roll(s4, 4, 0)
    s16 = s8 + pltpu.roll(s8, 8, 0)
    sw = _pool_sel(g, s2, s4, s8, s16)[POOL_HALO:]
    return sw / _pool_cnt(g, t0, tb) - e[POOL_HALO:]


def pool_fwd(p, pool_w, pool_scale):
    t = p.shape[0]
    tb = _pick(t, ROW_TILE, POOL_HALO)
    nt, hpb = t // tb, tb // POOL_HALO

    def body(x_ref, h_ref, w_ref, s_ref, o_ref):
        i, g = pl.program_id(0), pl.program_id(1)
        e = jnp.concatenate([jnp.where(i > 0, h_ref[...], 0.0), x_ref[...]], axis=0)
        yy = _pool_core(e, g, i * tb, tb)
        o_ref[...] = (_mm(yy, w_ref[...]) * s_ref[...]).astype(o_ref.dtype)

    return pl.pallas_call(
        body, grid=(nt, POOL_G),
        in_specs=[pl.BlockSpec((tb, LANE), lambda i, g: (i, _PCOL + g)),
                  pl.BlockSpec((POOL_HALO, LANE), lambda i, g: (jnp.maximum(i * hpb - 1, 0), _PCOL + g)),
                  pl.BlockSpec((None, LANE, LANE), lambda i, g: (g, 0, 0)),
                  pl.BlockSpec((1, LANE), lambda i, g: (0, g))],
        out_specs=pl.BlockSpec((tb, LANE), lambda i, g: (i, g)),
        out_shape=jax.ShapeDtypeStruct((t, POOL_G * LANE), BF16),
        compiler_params=_cp("parallel", "parallel"), name="pool_fwd")(p, p, pool_w, pool_scale)


def pool_bwd(p, dc, pool_w, pool_scale):
    t = p.shape[0]
    tb = _pick(t, ROW_TILE, POOL_HALO)
    nt, hpb = t // tb, tb // POOL_HALO
    n = tb + POOL_HALO

    def body(x_ref, h_ref, dy_ref, dn_ref, w_ref, s_ref, dx_o, dw_o, ds_o):
        g, i = pl.program_id(0), pl.program_id(1)
        e = jnp.concatenate([jnp.where(i > 0, h_ref[...], 0.0), x_ref[...]], axis=0)
        yy = _pool_core(e, g, i * tb, tb)
        w, sc, dy = w_ref[...], s_ref[...], dy_ref[...]

        @pl.when(i == 0)
        def _():
            dw_o[...] = jnp.zeros_like(dw_o)
            ds_o[...] = jnp.zeros_like(ds_o)

        ds_o[...] += jnp.sum(dy * _mm(yy, w), axis=0, keepdims=True)
        dw_o[...] += _mm_tn(yy, dy * sc)
        dye = jnp.concatenate([dy, jnp.where(i < nt - 1, dn_ref[...], 0.0)], axis=0) * sc
        dyy = _mm_nt(dye, w)
        z = dyy / _pool_cnt(g, i * tb, n)
        r2 = z + pltpu.roll(z, n - 1, 0)
        r4 = r2 + pltpu.roll(r2, n - 2, 0)
        r8 = r4 + pltpu.roll(r4, n - 4, 0)
        r16 = r8 + pltpu.roll(r8, n - 8, 0)
        dx_o[...] = (_pool_sel(g, r2, r4, r8, r16)[:tb] - dyy[:tb]).astype(dx_o.dtype)

    last = t // POOL_HALO - 1
    return pl.pallas_call(
        body, grid=(POOL_G, nt),
        in_specs=[pl.BlockSpec((tb, LANE), lambda g, i: (i, _PCOL + g)),
                  pl.BlockSpec((POOL_HALO, LANE), lambda g, i: (jnp.maximum(i * hpb - 1, 0), _PCOL + g)),
                  pl.BlockSpec((tb, LANE), lambda g, i: (i, _CPOOL + g)),
                  pl.BlockSpec((POOL_HALO, LANE), lambda g, i: (jnp.minimum((i + 1) * hpb, last), _CPOOL + g)),
                  pl.BlockSpec((None, LANE, LANE), lambda g, i: (g, 0, 0)),
                  pl.BlockSpec((1, LANE), lambda g, i: (0, g))],
        out_specs=[pl.BlockSpec((tb, LANE), lambda g, i: (i, g)),
                   pl.BlockSpec((None, LANE, LANE), lambda g, i: (g, 0, 0)),
                   pl.BlockSpec((1, LANE), lambda g, i: (0, g))],
        out_shape=(jax.ShapeDtypeStruct((t, POOL_G * LANE), BF16),
                   jax.ShapeDtypeStruct((POOL_G, LANE, LANE), F32),
                   jax.ShapeDtypeStruct((1, POOL_G * LANE), F32)),
        compiler_params=_cp("arbitrary", "arbitrary"), name="pool_bwd")(p, p, dc, dc, pool_w, pool_scale)


_QCOL, _KCOL, _VCOL = O_SQ // 768, O_SK // 256, O_SV // 256
_GQ = SWA_G * SWA_BLK


def _rope(x, c2, s2):
    return x * c2 + pltpu.roll(x, HD // 2, 1) * s2


def _rope_bwd(d, c2, s2):
    return d * c2 + pltpu.roll(d * s2, HD // 2, 1)


def _hs(x, h):
    return x[:, h * HD:(h + 1) * HD]


def _swa_group(q, kc, kp, vc, vp, c2c, s2c, c2p, s2p, sinks, h, blk):
    kcat = jnp.concatenate([_rope(_hs(kp, h), c2p, s2p), _rope(_hs(kc, h), c2c, s2c)], axis=0)
    vcat = jnp.concatenate([_hs(vp, h), _hs(vc, h)], axis=0)
    qs = jnp.concatenate([_rope(_hs(q, SWA_G * h + g), c2c, s2c) for g in range(SWA_G)], axis=0)
    s = _mm_nt(qs, kcat) * SCALE
    yield
    ii = lax.broadcasted_iota(jnp.int32, (_GQ, 2 * SWA_BLK), 0) & (SWA_BLK - 1)
    jj = lax.broadcasted_iota(jnp.int32, (_GQ, 2 * SWA_BLK), 1)
    lo = jnp.where(blk > 0, 0, SWA_BLK)
    s = jnp.where((jj > ii) & (jj <= ii + SWA_BLK) & (jj >= lo), s, NEG)
    sink = jnp.concatenate(
        [jnp.broadcast_to(sinks[:, SWA_G * h + g:SWA_G * h + g + 1], (SWA_BLK, 1)) for g in range(SWA_G)], axis=0)
    m = jnp.maximum(jnp.max(s, axis=1, keepdims=True), sink)
    p = jnp.exp(s - m)
    ps = jnp.exp(sink - m)
    l = jnp.sum(p, axis=1, keepdims=True) + ps
    return qs, kcat, vcat, p, ps, l


def _swa_specs(blk_of):
    cur = lambda w, c: pl.BlockSpec((SWA_BLK, w), lambda n: (blk_of(n), c))
    prev = lambda w, c: pl.BlockSpec((SWA_BLK, w), lambda n: (jnp.maximum(blk_of(n) - 1, 0), c))
    return [cur(768, _QCOL), cur(256, _KCOL), prev(256, _KCOL), cur(256, _VCOL), prev(256, _VCOL),
            cur(HD, 0), cur(HD, 0), prev(HD, 0), prev(HD, 0), pl.BlockSpec((1, LANE), lambda n: (0, 0))]


def swa_fwd(p, cos2, sin2, sinks):
    t = p.shape[0]

    def body(q_ref, kc, kp, vc, vp, c2c, s2c, c2p, s2p, sk_ref, o_ref):
        n = pl.program_id(0)

        def head(h):
            _, _, vcat, pr, _, l = yield from _swa_group(q_ref[...], kc[...], kp[...], vc[...], vp[...], c2c[...],
                                                         s2c[...], c2p[...], s2p[...], sk_ref[...], h, n)
            o = _mm(pr, vcat)
            yield
            return o / l

        for h, o in enumerate(_lockstep([head(h) for h in range(SWA_KV)])):
            for g in range(SWA_G):
                hh = SWA_G * h + g
                o_ref[:, hh * HD:(hh + 1) * HD] = o[g * SWA_BLK:(g + 1) * SWA_BLK].astype(o_ref.dtype)

    return pl.pallas_call(
        body, grid=(t // SWA_BLK,), in_specs=_swa_specs(lambda n: n),
        out_specs=pl.BlockSpec((SWA_BLK, 768), lambda n: (n, 0)),
        out_shape=jax.ShapeDtypeStruct((t, 768), BF16),
        compiler_params=_cp("parallel"), name="swa_fwd")(p, p, p, p, p, cos2, sin2, cos2, sin2, sinks)


def swa_bwd(p, dc, cos2, sin2, sinks):
    t = p.shape[0]
    nb = t // SWA_BLK

    def body(q_ref, kc, kp, vc, vp, c2c, s2c, c2p, s2p, sk_ref, do_ref, dq_o, dk_o, dv_o, dsk_o, ck, cv):
        i = pl.program_id(0)
        r = nb - 1 - i

        @pl.when(i == 0)
        def _():
            ck[...] = jnp.zeros_like(ck)
            cv[...] = jnp.zeros_like(cv)
            dsk_o[...] = jnp.zeros_like(dsk_o)

        lane = lax.broadcasted_iota(jnp.int32, (1, LANE), 1)
        dsk = jnp.zeros((1, LANE), F32)
        do = do_ref[...]
        def head(h):
            qs, kcat, vcat, pr, ps, l = yield from _swa_group(q_ref[...], kc[...], kp[...], vc[...], vp[...],
                                                              c2c[...], s2c[...], c2p[...], s2p[...], sk_ref[...], h, r)
            pn = pr / l
            dos = jnp.concatenate([_hs(do, SWA_G * h + g) for g in range(SWA_G)], axis=0)
            dp = _mm_nt(dos, vcat)
            dv = _mm_tn(pn, dos)
            yield
            delta = jnp.sum(pn * dp, axis=1, keepdims=True)
            ds = pn * (dp - delta)
            dqs = _mm(ds, kcat) * SCALE
            dk = _mm_tn(ds, qs) * SCALE
            yield
            return dqs, dk, dv, -(ps / l) * delta

        for h, (dqs, dk, dv, dsr) in enumerate(_lockstep([head(h) for h in range(SWA_KV)])):
            for g in range(SWA_G):
                tot = jnp.sum(dsr[g * SWA_BLK:(g + 1) * SWA_BLK], axis=0, keepdims=True)
                dsk = dsk + jnp.where(lane == SWA_G * h + g, tot, 0.0)
            for g in range(SWA_G):
                hh = SWA_G * h + g
                dq_o[:, hh * HD:(hh + 1) * HD] = _rope_bwd(dqs[g * SWA_BLK:(g + 1) * SWA_BLK], c2c[...],
                                                          s2c[...]).astype(dq_o.dtype)
            cs = slice(h * HD, (h + 1) * HD)
            dk_o[:, cs] = (_rope_bwd(dk[SWA_BLK:], c2c[...], s2c[...]) + ck[:, cs]).astype(dk_o.dtype)
            dv_o[:, cs] = (dv[SWA_BLK:] + cv[:, cs]).astype(dv_o.dtype)
            ck[:, cs] = _rope_bwd(dk[:SWA_BLK], c2p[...], s2p[...])
            cv[:, cs] = dv[:SWA_BLK]
        dsk_o[...] += dsk

    rev = lambda n: nb - 1 - n
    return pl.pallas_call(
        body, grid=(nb,),
        in_specs=_swa_specs(rev) + [pl.BlockSpec((SWA_BLK, 768), lambda n: (rev(n), 1))],
        out_specs=[pl.BlockSpec((SWA_BLK, 768), lambda n: (rev(n), 0)),
                   pl.BlockSpec((SWA_BLK, 256), lambda n: (rev(n), 0)),
                   pl.BlockSpec((SWA_BLK, 256), lambda n: (rev(n), 0)),
                   pl.BlockSpec((1, LANE), lambda n: (0, 0))],
        out_shape=(jax.ShapeDtypeStruct((t, 768), BF16), jax.ShapeDtypeStruct((t, 256), BF16),
                   jax.ShapeDtypeStruct((t, 256), BF16), jax.ShapeDtypeStruct((1, LANE), F32)),
        scratch_shapes=[pltpu.VMEM((SWA_BLK, 256), F32), pltpu.VMEM((SWA_BLK, 256), F32)],
        compiler_params=_cp("arbitrary"), name="swa_bwd")(p, p, p, p, p, cos2, sin2, cos2, sin2, sinks, dc)


_ZCOL, _GCOL = O_Z // DN_W, O_GATE // LANE
_QKV_W = 3 * DN_W
_INV_STEPS = int(math.log2(CH)) - 1


class _Bag(dict):
    __getattr__ = dict.__getitem__


DN_CB = 4
_DN_ROWS = DN_CB * CH
_CH_SHIFT = CH.bit_length() - 1


def _dn_consts():
    ii = lax.broadcasted_iota(jnp.int32, (CH, CH), 0)
    jj = lax.broadcasted_iota(jnp.int32, (CH, CH), 1)
    bi = lax.broadcasted_iota(jnp.int32, (_DN_ROWS, _DN_ROWS), 0)
    bj = lax.broadcasted_iota(jnp.int32, (_DN_ROWS, _DN_ROWS), 1)
    same_chunk = jnp.right_shift(bi, _CH_SHIFT) == jnp.right_shift(bj, _CH_SHIFT)
    return _Bag(lower=ii >= jj, strict=ii > jj, diag=ii == jj,
                eye=jnp.where(ii == jj, 1.0, 0.0).astype(F32),
                tril_blk=jnp.where(same_chunk & (bi >= bj), 1.0, 0.0).astype(F32),
                ones=jnp.ones((CH, CH), F32), ones_w=jnp.ones((CH, LANE), F32),
                rows=lax.broadcasted_iota(jnp.int32, (CH, 1), 0),
                lane=lax.broadcasted_iota(jnp.int32, (1, LANE), 1))


def _dn_conv(ext_ref, cw):
    return (cw[0:1, :] * ext_ref[pl.ds(5, _DN_ROWS), :] + cw[1:2, :] * ext_ref[pl.ds(6, _DN_ROWS), :]
            + cw[2:3, :] * ext_ref[pl.ds(7, _DN_ROWS), :] + cw[3:4, :] * ext_ref[pl.ds(8, _DN_ROWS), :])


def _dn_gates(gt, arow, drow, c):
    beta = _sigmoid(gt)
    ea = jnp.exp(arow)
    xa = gt + drow
    g = -ea * _softplus(xa)
    return beta, g, _mm_hi(c.tril_blk, g), ea, _sigmoid(xa)


def _blk(a, ci, j):
    return a[ci * CH:(ci + 1) * CH, j * HD:(j + 1) * HD]


def _lockstep(gens):
    out, live = [None] * len(gens), list(range(len(gens)))
    while live:
        still = []
        for i in live:
            try:
                next(gens[i])
                still.append(i)
            except StopIteration as stop:
                out[i] = stop.value
        live = still
    return out


def _dn_head_a(qh, kh, vh, beta, gc, c):
    rq = lax.rsqrt(jnp.sum(qh * qh, axis=1, keepdims=True) + EPS)
    rk = lax.rsqrt(jnp.sum(kh * kh, axis=1, keepdims=True) + EPS)
    qn = qh * rq * SCALE
    kn = kh * rk
    kb = kn * beta
    vb = vh * beta
    gcol = _mm_hi(c.ones, jnp.where(c.diag, gc, 0.0))
    kk = _mm_nt(kb, kn)
    qk = _mm_nt(qn, kn)
    yield
    gam = jnp.where(c.lower, jnp.exp(jnp.minimum(gc - gcol, 0.0)), 0.0)
    lmat = jnp.where(c.strict, kk * gam, 0.0)
    amat = qk * gam
    nil = -lmat
    inv = c.eye + nil
    powk = nil
    for _ in range(_INV_STEPS):
        powk = _mm(powk, powk)
        yield
        inv = _mm(inv, c.eye + powk)
    eg = jnp.exp(gc)
    kbe = kb * eg
    yield
    u = _mm(inv, vb)
    w = _mm(inv, kbe)
    gl = gc[CH - 1:CH, :]
    e2 = jnp.exp(gl - gc)
    cd = jnp.exp(gl)
    qd = qn * eg
    kd = kn * e2
    return _Bag(rq=rq, rk=rk, qn=qn, kn=kn, kb=kb, vb=vb, gam=gam, lmat=lmat, inv=inv, eg=eg, kbe=kbe, u=u, w=w,
                amat=amat, e2=e2, cd=cd, qd=qd, kd=kd)


def _dn_head_b(f, s0):
    ws = _mm(f.w, s0)
    qs = _mm(f.qd, s0)
    yield
    vnew = f.u - ws
    return vnew, qs + _mm(f.amat, vnew), s0 * f.cd + _mm_tn(f.kd, vnew)


def _dn_post(o, zh, nw):
    ro = lax.rsqrt(jnp.mean(o * o, axis=1, keepdims=True) + EPS)
    oh = o * ro
    sz = _sigmoid(zh)
    return ro, oh, sz, oh * nw * (zh * sz)


def _dn_post_bwd(o, zh, nw, dy):
    ro, oh, sz, _ = _dn_post(o, zh, nw)
    don = dy * (zh * sz)
    dz = dy * (oh * nw) * (sz * (1.0 + zh * (1.0 - sz)))
    doh = don * nw
    return (ro * (doh - oh * jnp.mean(doh * oh, axis=1, keepdims=True)), dz,
            jnp.sum(don * oh, axis=0, keepdims=True))


def _dn_head_bwd_b(f, vnew, do, dsn, s0):
    a_do = _mm_tn(f.amat, do)
    kd_ds = _mm(f.kd, dsn)
    qd_do = _mm_tn(f.qd, do)
    dkd = _mm_nt(vnew, dsn)
    yield
    dvnew = a_do + kd_ds
    ds0 = qd_do + f.cd * dsn - _mm_tn(f.w, dvnew)
    dcd = jnp.sum(jnp.sum(s0 * dsn, axis=1, keepdims=True), axis=0, keepdims=True)
    return dvnew, ds0, dkd, dcd


def _dn_head_bwd_c(f, vnew, do, dvnew, dkd, dcd, s0, qh, vh, beta, c):
    da = jnp.where(c.lower, _mm_nt(do, vnew), 0.0)
    dqd = _mm_nt(do, s0)
    dw = -_mm_nt(dvnew, s0)
    dt_u = _mm_nt(dvnew, f.vb)
    dvb = _mm_tn(f.inv, dvnew)
    yield
    dt = dt_u + _mm_nt(dw, f.kbe)
    dkbe = _mm_tn(f.inv, dw)
    yield
    dt_inv = _mm_nt(dt, f.inv)
    yield
    dl = -jnp.where(c.strict, _mm_tn(f.inv, dt_inv), 0.0)
    yield
    dm = dl * f.gam
    dn = da * f.gam
    dkb = _mm(dm, f.kn) + dkbe * f.eg
    dkn = _mm_tn(dm, f.kb) + _mm_tn(dn, f.qn) + dkd * f.e2 + beta * dkb
    dqn = _mm(dn, f.kn) + dqd * f.eg
    pm = dl * f.lmat + da * f.amat
    colsum = _mm_hi(pm, c.ones_w, _TN)[:, 0:1]
    yield
    tkd = jnp.sum(dkd * f.kn, axis=1, keepdims=True) * f.e2
    dgc = (jnp.sum(pm, axis=1, keepdims=True) - colsum - tkd
           + (jnp.sum(dqd * f.qn, axis=1, keepdims=True) + jnp.sum(dkbe * f.kb, axis=1, keepdims=True)) * f.eg)
    dgl = jnp.sum(tkd, axis=0, keepdims=True) + dcd * f.cd
    dgc = dgc + jnp.where(c.rows == CH - 1, dgl, 0.0)
    dbeta = jnp.sum(dkb * f.kn, axis=1, keepdims=True) + jnp.sum(dvb * vh, axis=1, keepdims=True)
    dvh = beta * dvb
    qhat = qh * f.rq
    dqs = dqn * SCALE
    dqh = f.rq * (dqs - qhat * jnp.sum(qhat * dqs, axis=1, keepdims=True))
    dkh = f.rk * (dkn - f.kn * jnp.sum(f.kn * dkn, axis=1, keepdims=True))
    return dqh, dkh, dvh, dbeta, dgc


def _dn_in_specs(step_of):
    return [pl.BlockSpec((_DN_ROWS, _QKV_W), lambda n: (step_of(n), 0)),
            pl.BlockSpec((SUB, _QKV_W), lambda n: (jnp.maximum(step_of(n) * (_DN_ROWS // SUB) - 1, 0), 0)),
            pl.BlockSpec((_DN_ROWS, DN_W), lambda n: (step_of(n), _ZCOL)),
            pl.BlockSpec((_DN_ROWS, LANE), lambda n: (step_of(n), _GCOL)),
            pl.BlockSpec((DN_K, _QKV_W), lambda n: (0, 0)),
            pl.BlockSpec((SUB, LANE), lambda n: (0, 0))]


def _dn_heads_a(qkv, beta_all, gc_all, c):
    rows = lambda a, ci: a[ci * CH:(ci + 1) * CH]
    flat = _lockstep([_dn_head_a(_blk(qkv, ci, h), _blk(qkv, ci, DN_H + h), _blk(qkv, ci, 2 * DN_H + h),
                                 rows(beta_all, ci)[:, h:h + 1], rows(gc_all, ci)[:, DN_H + h:DN_H + h + 1], c)
                      for ci in range(DN_CB) for h in range(DN_H)])
    return [flat[ci * DN_H:(ci + 1) * DN_H] for ci in range(DN_CB)]


def dn_fwd(p, conv_w, par):
    t = p.shape[0]
    nc = t // CH
    assert t % _DN_ROWS == 0

    def body(x_ref, h_ref, z_ref, g_ref, cw_ref, par_ref, y_o, s_o, ext, st):
        n = pl.program_id(0)
        c = _dn_consts()

        @pl.when(n == 0)
        def _():
            st[...] = jnp.zeros_like(st)

        ext[0:SUB, :] = jnp.where(n > 0, h_ref[...], 0.0)
        ext[SUB:, :] = x_ref[...]
        pre = _dn_conv(ext, cw_ref[...])
        qkv = pre * _sigmoid(pre)
        par = par_ref[...]
        beta_all, _, gc_all, _, _ = _dn_gates(g_ref[...], par[0:1, :], par[1:2, :], c)
        z = z_ref[...]
        fa = _dn_heads_a(qkv, beta_all, gc_all, c)
        s = [st[h] for h in range(DN_H)]
        for ci in range(DN_CB):
            for h in range(DN_H):
                s_o[ci, h] = s[h]
            res = _lockstep([_dn_head_b(fa[ci][h], s[h]) for h in range(DN_H)])
            for h in range(DN_H):
                _, o, s[h] = res[h]
                y_o[ci * CH:(ci + 1) * CH, h * HD:(h + 1) * HD] = _dn_post(o, _blk(z, ci, h),
                                                                           par[2:3, :])[3].astype(y_o.dtype)
        for h in range(DN_H):
            st[h] = s[h]

    return pl.pallas_call(
        body, grid=(t // _DN_ROWS,), in_specs=_dn_in_specs(lambda n: n),
        out_specs=[pl.BlockSpec((_DN_ROWS, DN_W), lambda n: (n, 0)),
                   pl.BlockSpec((DN_CB, DN_H, HD, HD), lambda n: (n, 0, 0, 0))],
        out_shape=(jax.ShapeDtypeStruct((t, DN_W), BF16), jax.ShapeDtypeStruct((nc, DN_H, HD, HD), F32)),
        scratch_shapes=[pltpu.VMEM((_DN_ROWS + SUB, _QKV_W), F32), pltpu.VMEM((DN_H, HD, HD), F32)],
        compiler_params=_cp("arbitrary"), name="dn_fwd")(p, p, p, p, conv_w, par)


def dn_bwd(p, dc, states, conv_w, par):
    t = p.shape[0]
    ns = t // _DN_ROWS

    def body(x_ref, h_ref, z_ref, g_ref, cw_ref, par_ref, s_ref, dy_ref,
             dx_o, dz_o, dg_o, dcw_o, dpar_o, ext, dst, dpost, x2):
        i = pl.program_id(0)
        r = ns - 1 - i
        c = _dn_consts()

        @pl.when(i == 0)
        def _():
            dst[...] = jnp.zeros_like(dst)
            dcw_o[...] = jnp.zeros_like(dcw_o)
            dpar_o[...] = jnp.zeros_like(dpar_o)
            x2[_DN_ROWS:, :] = jnp.zeros((SUB, _QKV_W), F32)

        ext[0:SUB, :] = jnp.where(r > 0, h_ref[...], 0.0)
        ext[SUB:, :] = x_ref[...]
        cw = cw_ref[...]
        pre = _dn_conv(ext, cw)
        sg = _sigmoid(pre)
        qkv = pre * sg
        par = par_ref[...]
        gt = g_ref[...]
        beta_all, g_all, gc_all, ea, sxa = _dn_gates(gt, par[0:1, :], par[1:2, :], c)
        z, dy = z_ref[...], dy_ref[...]
        nw = par[2:3, :]
        dnw = jnp.zeros((1, LANE), F32)
        pairs = [(ci, h) for ci in range(DN_CB) for h in range(DN_H)]
        fa = _dn_heads_a(qkv, beta_all, gc_all, c)
        vnew, do = {}, {}
        fwd = _lockstep([_dn_head_b(fa[ci][h], s_ref[ci, h]) for ci, h in pairs])
        for (ci, h), (vn, o, _) in zip(pairs, fwd):
            vnew[ci, h] = vn
            do[ci, h], dz, dnw_h = _dn_post_bwd(o, _blk(z, ci, h), nw, _blk(dy, ci, h))
            dnw = dnw + dnw_h
            dz_o[ci * CH:(ci + 1) * CH, h * HD:(h + 1) * HD] = dz.astype(dz_o.dtype)
        ds = [dst[h] for h in range(DN_H)]
        seq = {}
        for ci in range(DN_CB - 1, -1, -1):
            res = _lockstep([_dn_head_bwd_b(fa[ci][h], vnew[ci, h], do[ci, h], ds[h], s_ref[ci, h])
                             for h in range(DN_H)])
            for h in range(DN_H):
                dvnew, ds[h], dkd, dcd = res[h]
                seq[ci, h] = (dvnew, dkd, dcd)
        for h in range(DN_H):
            dst[h] = ds[h]
        rest = _lockstep([_dn_head_bwd_c(
            fa[ci][h], vnew[ci, h], do[ci, h], *seq[ci, h], s_ref[ci, h], _blk(qkv, ci, h),
            _blk(qkv, ci, 2 * DN_H + h), beta_all[ci * CH:(ci + 1) * CH, h:h + 1], c) for ci, h in pairs])
        dbeta_rows, dgc_rows = [], []
        for ci in range(DN_CB):
            dbeta_c = jnp.zeros((CH, LANE), F32)
            dgc_c = jnp.zeros((CH, LANE), F32)
            for h in range(DN_H):
                dqh, dkh, dvh, dbeta, dgc = rest[ci * DN_H + h]
                dbeta_c = dbeta_c + jnp.where(c.lane == h, dbeta, 0.0)
                dgc_c = dgc_c + jnp.where(c.lane == DN_H + h, dgc, 0.0)
                rs = slice(ci * CH, (ci + 1) * CH)
                dpost[rs, h * HD:(h + 1) * HD] = dqh
                dpost[rs, (DN_H + h) * HD:(DN_H + h + 1) * HD] = dkh
                dpost[rs, (2 * DN_H + h) * HD:(2 * DN_H + h + 1) * HD] = dvh
            dbeta_rows.append(dbeta_c)
            dgc_rows.append(dgc_c)
        dbeta_all = jnp.concatenate(dbeta_rows, axis=0)
        dgc_all = jnp.concatenate(dgc_rows, axis=0)
        dg_all = _mm_hi(c.tril_blk, dgc_all, _TN)
        dpa = dg_all * (-ea) * sxa
        dpb = dbeta_all * beta_all * (1.0 - beta_all)
        is_b = c.lane < DN_H
        is_a = (c.lane >= DN_H) & (c.lane < 2 * DN_H)
        dg_o[...] = jnp.where(is_b, dpb, jnp.where(is_a, dpa, 0.0)).astype(dg_o.dtype)
        dpar_o[0:1, :] += jnp.where(is_a, jnp.sum(dg_all * g_all, axis=0, keepdims=True), 0.0)
        dpar_o[1:2, :] += jnp.where(is_a, jnp.sum(dpa, axis=0, keepdims=True), 0.0)
        dpar_o[2:3, :] += dnw
        dpre = dpost[...] * (sg * (1.0 + pre * (1.0 - sg)))
        for k in range(DN_K):
            dcw_o[k:k + 1, :] += jnp.sum(dpre * ext[pl.ds(5 + k, _DN_ROWS), :], axis=0, keepdims=True)
        x2[0:_DN_ROWS, :] = dpre
        dx_o[...] = (cw[3:4, :] * dpre + cw[2:3, :] * x2[pl.ds(1, _DN_ROWS), :]
                     + cw[1:2, :] * x2[pl.ds(2, _DN_ROWS), :]
                     + cw[0:1, :] * x2[pl.ds(3, _DN_ROWS), :]).astype(dx_o.dtype)
        x2[_DN_ROWS:, :] = dpre[0:SUB, :]

    rev = lambda n: ns - 1 - n
    return pl.pallas_call(
        body, grid=(ns,),
        in_specs=_dn_in_specs(rev) + [pl.BlockSpec((DN_CB, DN_H, HD, HD), lambda n: (rev(n), 0, 0, 0)),
                                      pl.BlockSpec((_DN_ROWS, DN_W), lambda n: (rev(n), 0))],
        out_specs=[pl.BlockSpec((_DN_ROWS, _QKV_W), lambda n: (rev(n), 0)),
                   pl.BlockSpec((_DN_ROWS, DN_W), lambda n: (rev(n), 0)),
                   pl.BlockSpec((_DN_ROWS, LANE), lambda n: (rev(n), 0)),
                   pl.BlockSpec((SUB, _QKV_W), lambda n: (0, 0)),
                   pl.BlockSpec((SUB, LANE), lambda n: (0, 0))],
        out_shape=(jax.ShapeDtypeStruct((t, _QKV_W), BF16), jax.ShapeDtypeStruct((t, DN_W), BF16),
                   jax.ShapeDtypeStruct((t, LANE), BF16), jax.ShapeDtypeStruct((SUB, _QKV_W), F32),
                   jax.ShapeDtypeStruct((SUB, LANE), F32)),
        scratch_shapes=[pltpu.VMEM((_DN_ROWS + SUB, _QKV_W), F32), pltpu.VMEM((DN_H, HD, HD), F32),
                        pltpu.VMEM((_DN_ROWS, _QKV_W), F32), pltpu.VMEM((_DN_ROWS + SUB, _QKV_W), F32)],
        compiler_params=_cp("arbitrary"), name="dn_bwd")(p, p, p, p, conv_w, par, states, dc)


_ANY = pl.BlockSpec(memory_space=pl.ANY)
_MESH = pl.DeviceIdType.MESH


def _me():
    return lax.axis_index("x"), lax.axis_index("y"), lax.axis_index("c")


def all_gather(x, name):
    def body(x_ref, out_ref, send_sems, recv_sems, local_sem):
        mx, my, mc = _me()
        me, sibling = (mx, my, mc), (mx, my, 1 - mc)
        chips = [(1 - mx, my), (mx, 1 - my), (1 - mx, 1 - my)]

        def slot(px, py, pc):
            return out_ref.at[4 * px + 2 * py + pc]

        def copy(k, block, to, src=None):
            return pltpu.make_async_remote_copy(
                src_ref=slot(*block) if src is None else src, dst_ref=slot(*block),
                send_sem=send_sems.at[k], recv_sem=recv_sems.at[k], device_id=to, device_id_type=_MESH)

        mine = pltpu.make_async_copy(x_ref, slot(*me), local_sem)
        mine.start()
        first = [copy(0, me, sibling, src=x_ref)]
        first += [copy(1 + j, me, (*chip, mc), src=x_ref) for j, chip in enumerate(chips)]
        for cp in first:
            cp.start()
        passed = [copy(4 + j, (*chip, mc), sibling) for j, chip in enumerate(chips)]
        for j, chip in enumerate(chips):
            copy(1 + j, (*chip, mc), me).wait_recv()
            passed[j].start()
        copy(0, sibling, me).wait_recv()
        for j, chip in enumerate(chips):
            copy(4 + j, (*chip, 1 - mc), me).wait_recv()
        for cp in first + passed:
            cp.wait_send()
        mine.wait()

    return pl.pallas_call(
        body, out_shape=jax.ShapeDtypeStruct((N_DEV,) + x.shape, x.dtype), in_specs=[_ANY], out_specs=_ANY,
        scratch_shapes=[pltpu.SemaphoreType.DMA((7,)), pltpu.SemaphoreType.DMA((7,)), pltpu.SemaphoreType.DMA],
        name=name)(x)


_HBM = pl.BlockSpec(memory_space=pltpu.HBM)
_SEM = pl.BlockSpec(memory_space=pltpu.SEMAPHORE)
_EFFECT = pltpu.SideEffectType.DATAFLOW_SIDE_EFFECTING
_TOKEN = jax.ShapeDtypeStruct((SUB, LANE), F32)


def _peers(mx, my, mc):
    for rel in range(1, N_DEV):
        yield (1 - mx if rel & 4 else mx, 1 - my if rel & 2 else my, 1 - mc if rel & 1 else mc)


def _in_hbm(a):
    return pltpu.with_memory_space_constraint(a, pltpu.HBM)


GATHER_SLOTS = (4, 3)


def gather_start(buf, phase, after, name):
    def body(buf_ref, after_ref, send_sem, recv_sem, thru, token):
        mx, my, mc = _me()
        sibling = (mx, my, 1 - mc)
        chips = [(1 - mx, my), (mx, 1 - my), (1 - mx, 1 - my)]
        if phase == 0:
            slot = buf_ref.at[4 * mx + 2 * my + mc]
            copies = [(slot, sibling)] + [(slot, (px, py, mc)) for px, py in chips]
        else:
            copies = [(buf_ref.at[4 * px + 2 * py + mc], sibling) for px, py in chips]
        for slot, peer in copies:
            pltpu.make_async_remote_copy(src_ref=slot, dst_ref=slot, send_sem=send_sem, recv_sem=recv_sem,
                                         device_id=peer, device_id_type=_MESH).start()
        token[...] = jnp.zeros_like(token)

    send_sem, recv_sem, thru, token = pl.pallas_call(
        body, name=name,
        out_shape=(pltpu.SemaphoreType.DMA(()), pltpu.SemaphoreType.DMA(()), pltpu.HBM(buf.shape, buf.dtype), _TOKEN),
        in_specs=[_HBM, _ANY], out_specs=(_SEM, _SEM, _HBM, pl.BlockSpec(memory_space=pltpu.VMEM)),
        input_output_aliases={0: 2},
        compiler_params=pltpu.CompilerParams(has_side_effects=_EFFECT))(_in_hbm(buf), after)
    return (send_sem, recv_sem), thru, token


def exchange_start(src, name):
    def body(src_ref, land_ref, send_sem, recv_sem, src_thru, land_thru, token):
        mx, my, mc = _me()
        me = 4 * mx + 2 * my + mc
        for px, py, pc in _peers(mx, my, mc):
            pltpu.make_async_remote_copy(
                src_ref=src_ref.at[4 * px + 2 * py + pc], dst_ref=land_ref.at[me], send_sem=send_sem,
                recv_sem=recv_sem, device_id=(px, py, pc), device_id_type=_MESH).start()
        token[...] = jnp.zeros_like(token)

    hbm = pltpu.HBM(src.shape, src.dtype)
    send_sem, recv_sem, src_thru, land_thru, token = pl.pallas_call(
        body, name=name,
        out_shape=(pltpu.SemaphoreType.DMA(()), pltpu.SemaphoreType.DMA(()), hbm, hbm, _TOKEN),
        in_specs=[_HBM, _HBM], out_specs=(_SEM, _SEM, _HBM, _HBM, pl.BlockSpec(memory_space=pltpu.VMEM)),
        input_output_aliases={0: 2, 1: 3},
        compiler_params=pltpu.CompilerParams(has_side_effects=_EFFECT))(
            _in_hbm(src), _in_hbm(lax.empty(src.shape, src.dtype)))
    return (send_sem, recv_sem), src_thru, land_thru, token


def transfer_wait(sems, bufs, after, name, slots=N_DEV - 1):
    n = len(bufs)

    def body(*refs):
        seven = refs[0].at[pl.ds(0, slots)]
        cp = pltpu.make_async_remote_copy(src_ref=seven, dst_ref=seven, send_sem=refs[n], recv_sem=refs[n + 1],
                                          device_id=_me(), device_id_type=_MESH)
        cp.wait_send()
        cp.wait_recv()

    outs = pl.pallas_call(
        body, name=name, out_shape=tuple(pltpu.HBM(b.shape, b.dtype) for b in bufs),
        in_specs=[_HBM] * n + [_SEM, _SEM, _ANY], out_specs=tuple([_HBM] * n),
        input_output_aliases={b: b for b in range(n)},
        compiler_params=pltpu.CompilerParams(has_side_effects=_EFFECT))(*bufs, sems[0], sems[1], after)
    return list(outs)


def sum_slabs(x, name, own=None, me=None):
    _, r, c = x.shape
    tr = _pick(r, max(SUB, (1 << 19) // c // SUB * SUB), SUB)
    out_shape = jax.ShapeDtypeStruct((r, c), F32)
    if own is None:
        def body(x_ref, o_ref):
            acc = x_ref[0].astype(F32)
            for s in range(1, N_DEV):
                acc = acc + x_ref[s].astype(F32)
            o_ref[...] = acc

        return pl.pallas_call(
            body, grid=(r // tr,), in_specs=[pl.BlockSpec((N_DEV, tr, c), lambda i: (0, i, 0))],
            out_specs=pl.BlockSpec((tr, c), lambda i: (i, 0)), out_shape=out_shape,
            compiler_params=_cp("parallel"), name=name)(x)

    def body_own(me_ref, x_ref, own_ref, o_ref):
        acc = None
        for s in range(N_DEV):
            val = jnp.where(me_ref[0] == s, own_ref[...], x_ref[s]).astype(F32)
            acc = val if acc is None else acc + val
        o_ref[...] = acc

    return pl.pallas_call(
        body_own, out_shape=out_shape, name=name, compiler_params=_cp("parallel"),
        grid_spec=pltpu.PrefetchScalarGridSpec(
            num_scalar_prefetch=1, grid=(r // tr,),
            in_specs=[pl.BlockSpec((N_DEV, tr, c), lambda i, me_ref: (0, i, 0)),
                      pl.BlockSpec((None, tr, c), lambda i, me_ref: (me_ref[0], i, 0))],
            out_specs=pl.BlockSpec((tr, c), lambda i, me_ref: (i, 0))))(me, x, own)


def adamw(w, g, m, v, name):
    r, c = w.shape
    tr = _pick(r, max(SUB, (1 << 18) // c // SUB * SUB), SUB)
    c1 = 1.0 / (1.0 - ADAM_B1 ** ADAM_STEP)
    c2 = 1.0 / (1.0 - ADAM_B2 ** ADAM_STEP)

    def body(w_ref, g_ref, m_ref, v_ref, d_o, m_o, v_o):
        gg = g_ref[...]
        mn = ADAM_B1 * m_ref[...] + (1.0 - ADAM_B1) * gg
        vn = ADAM_B2 * v_ref[...] + (1.0 - ADAM_B2) * (gg * gg)
        m_o[...] = mn
        v_o[...] = vn
        d_o[...] = -ADAM_LR * ((mn * c1) / (jnp.sqrt(vn * c2) + ADAM_EPS) + ADAM_WD * w_ref[...])

    spec = pl.BlockSpec((tr, c), lambda i: (i, 0))
    sds = jax.ShapeDtypeStruct((r, c), F32)
    return pl.pallas_call(body, grid=(r // tr,), in_specs=[spec] * 4, out_specs=[spec] * 3, out_shape=(sds,) * 3,
                          compiler_params=_cp("parallel"), name=name)(w, g, m, v)


def _adam_update(w, gg, m, v):
    mn = ADAM_B1 * m + (1.0 - ADAM_B1) * gg
    vn = ADAM_B2 * v + (1.0 - ADAM_B2) * (gg * gg)
    c1 = 1.0 / (1.0 - ADAM_B1 ** ADAM_STEP)
    c2 = 1.0 / (1.0 - ADAM_B2 ** ADAM_STEP)
    return -ADAM_LR * ((mn * c1) / (jnp.sqrt(vn * c2) + ADAM_EPS) + ADAM_WD * w), mn, vn


def adamw_layer(layer, w, m, v, prev, name, g=None, land=None, own=None, me=None):
    nl, r, c = w.shape
    tr = _pick(r, max(SUB, (1 << 17) // c // SUB * SUB), SUB)
    from_slabs = g is None
    if prev is None:
        prev = tuple(lax.empty((nl, r, c), F32) for _ in range(4))

    def body(*refs):
        if from_slabs:
            me_ref, land_ref, own_ref, w_ref, m_ref, v_ref = refs[:6]
            gg = None
            for s in range(N_DEV):
                val = jnp.where(me_ref[0] == s, own_ref[...], land_ref[s]).astype(F32)
                gg = val if gg is None else gg + val
        else:
            me_ref, g_ref, w_ref, m_ref, v_ref = refs[:5]
            gg = g_ref[...]
        g_o, d_o, m_o, v_o = refs[-4:]
        g_o[...] = gg
        d_o[...], m_o[...], v_o[...] = _adam_update(w_ref[...], gg, m_ref[...], v_ref[...])

    lay = pl.BlockSpec((None, tr, c), lambda i, me_ref: (layer, i, 0))
    if from_slabs:
        grad_specs = [pl.BlockSpec((N_DEV, tr, c), lambda i, me_ref: (0, i, 0)),
                      pl.BlockSpec((None, tr, c), lambda i, me_ref: (me_ref[0], i, 0))]
        grad_args = [land, own]
    else:
        grad_specs = [pl.BlockSpec((tr, c), lambda i, me_ref: (i, 0))]
        grad_args = [g]
        me = jnp.zeros((1,), jnp.int32)
    n_in = 1 + len(grad_args) + 3
    return pl.pallas_call(
        body, out_shape=tuple(jax.ShapeDtypeStruct((nl, r, c), F32) for _ in range(4)), name=name,
        input_output_aliases={n_in + k: k for k in range(4)}, compiler_params=_cp("parallel"),
        grid_spec=pltpu.PrefetchScalarGridSpec(
            num_scalar_prefetch=1, grid=(r // tr,), in_specs=grad_specs + [lay] * 3 + [_ANY] * 4,
            out_specs=[lay] * 4))(me, *grad_args, w, m, v, *prev)


def _pack(parts):
    flat = jnp.concatenate([a.reshape(-1).astype(F32) for a in parts])
    n = flat.shape[0]
    npad = -n % (PACK_ROWS * LANE)
    return jnp.pad(flat, (0, npad)).reshape(-1, LANE)


def _unpack(buf, shapes, lead=()):
    flat = buf.reshape(lead + (-1,))
    out, off = [], 0
    for s in shapes:
        n = math.prod(s)
        out.append(flat[..., off:off + n].reshape(lead + tuple(s)))
        off += n
    return out


ROPE_THETA = 10000.0

_SMALL = ("norm_mix_pre", "dn_conv_w", "dn_a_log", "dn_dt_bias", "dn_norm_w", "pool_w", "pool_scale", "swa_sinks",
          "norm_mix_post", "norm_ffn_pre", "ffn_conv_w", "ffn_conv_b", "norm_ffn_post")
_BIG = ("w_in", "w_out", "ffn_w_up", "ffn_w_down")
_ORDER = ("norm_mix_pre", "w_in", "dn_conv_w", "dn_a_log", "dn_dt_bias", "dn_norm_w", "pool_w", "pool_scale",
          "swa_sinks", "w_out", "norm_mix_post", "norm_ffn_pre", "ffn_w_up", "ffn_conv_w", "ffn_conv_b",
          "ffn_w_down", "norm_ffn_post")


def _step(x, positions, loss_target, w, m, v):
    nl = w["w_in"].shape[0]
    t, d = x.shape[1], x.shape[2]
    nb = w["ffn_w_up"].shape[2]
    f = nb * N_DEV // 2
    me = 4 * lax.axis_index("x") + 2 * lax.axis_index("y") + lax.axis_index("c")
    x_in, tgt = x[0], loss_target[0]

    inv_freq = 1.0 / (ROPE_THETA ** (jnp.arange(0, HD, 2, dtype=F32) / HD))
    ang = positions[0].astype(F32)[:, None] * inv_freq
    cos, sin = jnp.cos(ang), jnp.sin(ang)
    cos2 = jnp.concatenate([cos, cos], axis=1)
    sin2 = jnp.concatenate([-sin, sin], axis=1)

    conv_shapes = [w["dn_conv_w"].shape, w["ffn_conv_w"].shape]
    gathered_conv = all_gather(_pack([w["dn_conv_w"], w["ffn_conv_w"]]), "ag_conv")
    dn_cw_g, ffn_cw_g = _unpack(gathered_conv, conv_shapes, lead=(N_DEV,))
    dn_cw = jnp.moveaxis(dn_cw_g, 0, 2).reshape(nl, DN_K, _QKV_W)
    ffn_cw = jnp.moveaxis(ffn_cw_g, 0, 1).reshape(nl, 2, N_DEV // 2, 3, nb)
    ffn_cb = w["ffn_conv_b"].reshape(nl, 2, N_DEV // 2, 1, nb)

    def lane_row(vec, off):
        return jnp.zeros((LANE,), F32).at[off:off + vec.shape[0]].set(vec)

    dn_par = jnp.stack([
        jnp.zeros((SUB, LANE), F32).at[0].set(lane_row(w["dn_a_log"][l], DN_H))
        .at[1].set(lane_row(w["dn_dt_bias"][l], DN_H)).at[2].set(w["dn_norm_w"][l]) for l in range(nl)])
    sinks = jnp.stack([lane_row(w["swa_sinks"][l], 0)[None, :] for l in range(nl)])

    def place(shard):
        return lax.dynamic_update_slice(lax.empty((N_DEV,) + shard.shape, shard.dtype), shard[None], (me, 0, 0))

    kinds = ("w_in", "w_out", "ffn_w_up", "ffn_w_down")
    flight = {}
    tag = lambda i: f"{kinds[i % 4]}_{i // 4}"

    def start_first(i, after):
        if i >= 4 * nl:
            return jnp.zeros(_TOKEN.shape, F32)
        l, k = divmod(i, 4)
        shard = _align_in(w[kinds[k]][l]) if k == 0 else w[kinds[k]][l]
        sems, buf, token = gather_start(place(shard.astype(BF16)), 0, after, f"ag_start_{tag(i)}")
        flight[i] = (sems, buf)
        return token

    def start_second(i, after):
        if i >= 4 * nl:
            return jnp.zeros(_TOKEN.shape, F32)
        arrived = transfer_wait(flight[i][0], [flight[i][1]], after, f"ag_wait_{tag(i)}", GATHER_SLOTS[0])[0]
        sems, buf, token = gather_start(arrived, 1, after, f"ag_pass_{tag(i)}")
        flight[i] = (sems, buf)
        return token

    def gathered(l, k, after):
        i = 4 * l + k
        late = start_second(1, after) if i == 1 else None
        got = transfer_wait(flight[i][0], [flight[i][1]], after if late is None else late, f"ag_done_{tag(i)}",
                            GATHER_SLOTS[1])[0]
        first = start_first(i + 3, got)
        if i == 0:
            return got, first[0, 0]
        return got, (start_second(i + 1, first) + first)[0, 0]

    win, wout, wup, wdown = [None] * nl, [None] * nl, [None] * nl, [None] * nl
    row = lambda a, l: a[l][None, :]
    g1, g2, g3, g4 = w["norm_mix_pre"], w["norm_mix_post"], w["norm_ffn_pre"], w["norm_ffn_post"]

    saved = []
    xl = x_in
    passed = start_second(0, start_first(0, gathered_conv))
    h1 = norm_first(xl, row(g1, 0) + (passed + start_first(1, passed) + start_first(2, passed))[0, 0])
    for l in range(nl):
        buf, tk = gathered(l, 0, h1)
        win[l] = buf.reshape(d, PW)
        p = mm_nn(h1, win[l], F32, "mm_in")
        y_dn, states = dn_fwd(p, dn_cw[l], dn_par[l] + tk)
        y_pool = pool_fwd(p, w["pool_w"][l], row(w["pool_scale"], l))
        y_swa = swa_fwd(p, cos2, sin2, sinks[l])
        c = jnp.concatenate([y_dn, y_swa, y_pool], axis=1)
        buf, tk = gathered(l, 1, c)
        wout[l] = _perm_mix_rows(buf.reshape(MIX_W, d))
        mix = mm_nn(c, wout[l], F32, "mm_out")
        x1, h2 = post_pre(xl, mix, row(g2, l) + tk, row(g3, l))
        wup[l], tk = gathered(l, 2, h2)
        u0 = mm_up(h2, wup[l], "mm_up")
        act = glu_fwd(u0, ffn_cw[l], ffn_cb[l] + tk)
        buf, tk = gathered(l, 3, act)
        wdown[l] = buf.reshape(f, d)
        fo = mm_nn(act, wdown[l], F32, "mm_down")
        saved.append(dict(x=xl, h1=h1, p=p, states=states, c=c, mix=mix, x1=x1, h2=h2, u0=u0, act=act, f=fo))
        if l < nl - 1:
            xl, h1 = post_pre(x1, fo, row(g4, l) + tk, row(g1, l + 1))
        else:
            dx, loss_part = post_loss(x1, fo, row(g4, l) + tk, tgt)

    small_g = [dict() for _ in range(nl)]
    pending = {name: [None] * nl for name in _BIG}

    def exchange(name, l, dw):
        sems, src, land, token = exchange_start(dw, f"xch_start_{name}_{l}")
        pending[name][l] = (sems, src, land)
        return token[0, 0]

    df, small_g[nl - 1]["norm_ffn_post"] = bwd_norms(dx, post=(saved[-1]["f"], row(g4, nl - 1)))
    for l in range(nl - 1, -1, -1):
        s, sg = saved[l], small_g[l]
        dact = mm_nt(df, wdown[l], F32, "mm_down_d")
        tk = exchange("ffn_w_down", l, mm_tn(s["act"], df, BF16, "mm_down_w").reshape(N_DEV, f // N_DEV, d))
        du0, dcw = glu_bwd(dact, s["u0"], ffn_cw[l], ffn_cb[l])
        sg["ffn_conv"] = dcw
        dh2 = mm_up_dgrad(du0, wup[l], "mm_up_d")
        tk = tk + exchange("ffn_w_up", l, mm_up_wgrad(s["h2"], du0, "mm_up_w"))
        dx1, sg["norm_ffn_pre"], dmix, sg["norm_mix_post"] = bwd_norms(
            dx, pre=(dh2, s["x1"], row(g3, l) + tk), post=(s["mix"], row(g2, l)))
        dc = mm_nt(dmix, wout[l], F32, "mm_out_d")
        tk = exchange("w_out", l, _unperm_mix_rows(mm_tn(s["c"], dmix, BF16, "mm_out_w"))
                      .reshape(N_DEV, MIX_W // N_DEV, d))
        dqkv, dz, dgate, sg["dn_conv_w"], sg["dn_par"] = dn_bwd(s["p"], dc, s["states"], dn_cw[l], dn_par[l])
        dpool, sg["pool_w"], sg["pool_scale"] = pool_bwd(s["p"], dc, w["pool_w"][l], row(w["pool_scale"], l))
        dsq, dsk, dsv, sg["swa_sinks"] = swa_bwd(s["p"], dc, cos2, sin2, sinks[l])
        dp = jnp.concatenate([dqkv, dz, dsq, dsk, dsv, dpool, dgate], axis=1)
        dh1 = mm_nt(dp, win[l], F32, "mm_in_d")
        tk = tk + exchange("w_in", l, mm_tn(s["h1"], dp, BF16, "mm_in_w").reshape(N_DEV, d // N_DEV, PW))
        if l > 0:
            dx, sg["norm_mix_pre"], df, small_g[l - 1]["norm_ffn_post"] = bwd_norms(
                dx1, pre=(dh1, s["x"], row(g1, l) + tk), post=(saved[l - 1]["f"], row(g4, l - 1)))
        else:
            grad_x, sg["norm_mix_pre"] = bwd_norms(dx1, pre=(dh1, s["x"], row(g1, 0) + tk))

    me_arr = jnp.reshape(me, (1,)).astype(jnp.int32)
    big = {name: None for name in _BIG}

    def finish(name, l, after):
        sems, src, land = pending[name][l]
        src, land = transfer_wait(sems, [src, land], after, f"xch_wait_{name}_{l}")
        if name == "w_in":
            g = _unalign_in(sum_slabs(land, "sum_w_in", own=src, me=me_arr))
            big[name] = adamw_layer(l, w[name], m[name], v[name], big[name], "adamw_w_in", g=g)
            return g
        big[name] = adamw_layer(l, w[name], m[name], v[name], big[name], "adamw_" + name, land=land, own=src,
                                me=me_arr)
        return big[name][1]

    after = grad_x
    for l in range(nl - 1, 0, -1):
        for name in ("ffn_w_down", "ffn_w_up", "w_out", "w_in"):
            after = finish(name, l, after)

    keys = ("norm_mix_pre", "norm_mix_post", "norm_ffn_pre", "norm_ffn_post", "dn_conv_w", "dn_par", "pool_w",
            "pool_scale", "swa_sinks", "ffn_conv")
    grads = {}
    parts = [small_g[l][k] for l in range(nl) for k in keys] + [loss_part]
    shapes = [a.shape for a in parts]
    ordered = parts + [after[0:SUB, 0:LANE]]
    summed = sum_slabs(all_gather(_pack(ordered), "ag_small"), "sum_small")
    vals = _unpack(summed, shapes)
    loss = vals[-1][0, 0]
    sm = [dict(zip(keys, vals[l * len(keys):(l + 1) * len(keys)])) for l in range(nl)]
    st = lambda fn: jnp.stack([fn(sm[l]) for l in range(nl)])
    for k in ("norm_mix_pre", "norm_mix_post", "norm_ffn_pre", "norm_ffn_post"):
        grads[k] = st(lambda q: q[k][0])
    grads["dn_conv_w"] = lax.dynamic_slice_in_dim(st(lambda q: q["dn_conv_w"][0:DN_K]), me * (_QKV_W // N_DEV),
                                                  _QKV_W // N_DEV, axis=2)
    grads["dn_a_log"] = st(lambda q: q["dn_par"][0, DN_H:2 * DN_H])
    grads["dn_dt_bias"] = st(lambda q: q["dn_par"][1, DN_H:2 * DN_H])
    grads["dn_norm_w"] = st(lambda q: q["dn_par"][2])
    grads["pool_w"] = st(lambda q: q["pool_w"])
    grads["pool_scale"] = st(lambda q: q["pool_scale"][0])
    grads["swa_sinks"] = st(lambda q: q["swa_sinks"][0, 0:SWA_H])
    conv_all = st(lambda q: q["ffn_conv"].reshape(N_DEV, SUB, nb))
    grads["ffn_conv_w"] = lax.dynamic_index_in_dim(conv_all, me, axis=1, keepdims=False)[:, 0:3, :]
    grads["ffn_conv_b"] = conv_all[:, :, 3, :].reshape(nl, 2 * f)

    delta, new_m, new_v = {}, {}, {}
    shapes = [w[k].shape for k in _SMALL]
    pk = lambda tree: _pack([tree[k] for k in _SMALL])
    outs = adamw(pk(w), pk(grads), pk(m), pk(v), "adamw_small")
    for tree, buf in zip((delta, new_m, new_v), outs):
        for k, a in zip(_SMALL, _unpack(buf, shapes)):
            tree[k] = a
    after = outs[0]
    for name in ("ffn_w_down", "ffn_w_up", "w_out", "w_in"):
        after = finish(name, 0, after)
        grads[name], delta[name], new_m[name], new_v[name] = big[name]

    return (loss, grad_x[None], *[grads[k] for k in _ORDER], *[delta[k] for k in _ORDER],
            *[new_m[k] for k in _ORDER], *[new_v[k] for k in _ORDER])


def kernel(x, positions, norm_mix_pre, w_in, dn_conv_w, dn_a_log, dn_dt_bias, dn_norm_w, pool_w, pool_scale, swa_sinks, w_out, norm_mix_post, norm_ffn_pre, ffn_w_up, ffn_conv_w, ffn_conv_b, ffn_w_down, norm_ffn_post, loss_target, m_norm_mix_pre, m_w_in, m_dn_conv_w, m_dn_a_log, m_dn_dt_bias, m_dn_norm_w, m_pool_w, m_pool_scale, m_swa_sinks, m_w_out, m_norm_mix_post, m_norm_ffn_pre, m_ffn_w_up, m_ffn_conv_w, m_ffn_conv_b, m_ffn_w_down, m_norm_ffn_post, v_norm_mix_pre, v_w_in, v_dn_conv_w, v_dn_a_log, v_dn_dt_bias, v_dn_norm_w, v_pool_w, v_pool_scale, v_swa_sinks, v_w_out, v_norm_mix_post, v_norm_ffn_pre, v_ffn_w_up, v_ffn_conv_w, v_ffn_conv_b, v_ffn_w_down, v_norm_ffn_post):
    args = locals()
    w = {k: args[k] for k in _ORDER}
    m = {k: args["m_" + k] for k in _ORDER}
    v = {k: args["v_" + k] for k in _ORDER}
    return _step(x, positions, loss_target, w, m, v)
```

```python
import functools
import math

import jax
import jax.numpy as jnp
from jax import lax
from jax.experimental import pallas as pl
from jax.experimental.pallas import tpu as pltpu

F32 = jnp.float32
BF16 = jnp.bfloat16
MXU_DT = jnp.bfloat16
HI = lax.Precision.HIGHEST

N_DEV = 8
LANE = 128
SUB = 8
VMEM_LIMIT = 56 * 1024 * 1024
ROW_TILE = 512
NORM_TILE = 256
MM_TM, MM_TN, MM_TK = 512, 1664, 2816
MM_TN_NT = 2048
PACK_ROWS = 512

HD = 128
DN_H, DN_W, DN_K, CH = 6, 768, 4, 64
POOL_G = 4
SWA_H, SWA_KV, SWA_G, SWA_BLK = 6, 2, 3, 128
EPS = 1e-6
SCALE = HD ** -0.5
NEG = -1e30

O_QKV, O_Z, O_SQ, O_SK, O_SV, O_POOL, O_GATE, PW = 0, 2304, 3072, 3840, 4096, 4352, 4864, 4992
IN_W = 4876
MIX_W = 2048

ADAM_LR, ADAM_B1, ADAM_B2, ADAM_EPS, ADAM_WD, ADAM_STEP = 0.001, 0.9, 0.999, 1e-08, 0.01, 10


def _pick(n, cap, mult=LANE):
    best = None
    for d in range(mult, min(n, cap) + 1, mult):
        if n % d == 0:
            best = d
    return best if best is not None else n


def _cp(*sem):
    return pltpu.CompilerParams(dimension_semantics=sem, vmem_limit_bytes=VMEM_LIMIT)


def _dot(a, b, dims):
    return lax.dot_general(a.astype(MXU_DT), b.astype(MXU_DT), dims, preferred_element_type=F32)


_NN = (((1,), (0,)), ((), ()))
_NT = (((1,), (1,)), ((), ()))
_TN = (((0,), (0,)), ((), ()))


def _mm(a, b):
    return _dot(a, b, _NN)


def _mm_nt(a, b):
    return _dot(a, b, _NT)


def _mm_tn(a, b):
    return _dot(a, b, _TN)


def _mm_hi(a, b, dims=_NN):
    return lax.dot_general(a, b, dims, precision=HI, preferred_element_type=F32)


def _sigmoid(x):
    return jax.nn.sigmoid(x)


def _softplus(x):
    return jnp.maximum(x, 0.0) + jnp.log(1.0 + jnp.exp(-jnp.abs(x)))


def _align_in(w):
    pad = jnp.zeros(w.shape[:-1] + (PW - IN_W,), w.dtype)
    return jnp.concatenate([w[..., 0:3072], w[..., 3596:4364], w[..., 4364:4620], w[..., 4620:4876],
                            w[..., 3084:3596], w[..., 3072:3084], pad], axis=-1)


def _unalign_in(g):
    return jnp.concatenate([g[..., 0:3072], g[..., O_GATE:O_GATE + 12], g[..., O_POOL:O_POOL + 512],
                            g[..., O_SQ:O_SQ + 768], g[..., O_SK:O_SK + 256], g[..., O_SV:O_SV + 256]], axis=-1)


def _perm_mix_rows(w):
    return jnp.concatenate([w[0:768], w[1280:2048], w[768:1280]], axis=0)


def _unperm_mix_rows(w):
    return jnp.concatenate([w[0:768], w[1536:2048], w[768:1536]], axis=0)


def _mm_call(name, a, b, out_shape, grid, a_spec, b_spec, o_spec, dims, acc_shape):
    nk = grid[2]
    if nk == 1:
        def body_once(a_ref, b_ref, o_ref):
            o_ref[...] = _dot(a_ref[...], b_ref[...], dims).astype(o_ref.dtype)

        return pl.pallas_call(
            body_once, grid=grid, in_specs=[a_spec, b_spec], out_specs=o_spec, out_shape=out_shape,
            compiler_params=_cp("parallel", "parallel", "arbitrary"), name=name)(a, b)

    def body(a_ref, b_ref, o_ref, acc_ref):
        k = pl.program_id(2)

        @pl.when(k == 0)
        def _():
            acc_ref[...] = jnp.zeros_like(acc_ref)

        acc_ref[...] += _dot(a_ref[...], b_ref[...], dims)

        @pl.when(k == nk - 1)
        def _():
            o_ref[...] = acc_ref[...].astype(o_ref.dtype)

    return pl.pallas_call(
        body, grid=grid, in_specs=[a_spec, b_spec], out_specs=o_spec, out_shape=out_shape,
        scratch_shapes=[pltpu.VMEM(acc_shape, F32)],
        compiler_params=_cp("parallel", "parallel", "arbitrary"), name=name)(a, b)


def mm_nn(a, b, out_dtype, name):
    (m, k), n = a.shape, b.shape[1]
    tm, tn, tk = _pick(m, MM_TM, SUB), _pick(n, MM_TN), _pick(k, MM_TK)
    return _mm_call(name, a, b, jax.ShapeDtypeStruct((m, n), out_dtype), (m // tm, n // tn, k // tk),
                    pl.BlockSpec((tm, tk), lambda i, j, kk: (i, kk)),
                    pl.BlockSpec((tk, tn), lambda i, j, kk: (kk, j)),
                    pl.BlockSpec((tm, tn), lambda i, j, kk: (i, j)), _NN, (tm, tn))


def mm_nt(a, b, out_dtype, name):
    (m, k), n = a.shape, b.shape[0]
    tm, tn, tk = _pick(m, MM_TM, SUB), _pick(n, MM_TN_NT), _pick(k, MM_TK)
    return _mm_call(name, a, b, jax.ShapeDtypeStruct((m, n), out_dtype), (m // tm, n // tn, k // tk),
                    pl.BlockSpec((tm, tk), lambda i, j, kk: (i, kk)),
                    pl.BlockSpec((tn, tk), lambda i, j, kk: (j, kk)),
                    pl.BlockSpec((tm, tn), lambda i, j, kk: (i, j)), _NT, (tm, tn))


def mm_tn(a, b, out_dtype, name):
    (k, m), n = a.shape, b.shape[1]
    tm, tn, tk = _pick(m, MM_TM), _pick(n, MM_TN), _pick(k, MM_TK, SUB)
    return _mm_call(name, a, b, jax.ShapeDtypeStruct((m, n), out_dtype), (m // tm, n // tn, k // tk),
                    pl.BlockSpec((tk, tm), lambda i, j, kk: (kk, i)),
                    pl.BlockSpec((tk, tn), lambda i, j, kk: (kk, j)),
                    pl.BlockSpec((tm, tn), lambda i, j, kk: (i, j)), _TN, (tm, tn))


def mm_up(h, wblk, name):
    (t, d), (nblk, _, nb) = h.shape, wblk.shape
    tm, tk = _pick(t, MM_TM, SUB), _pick(d, MM_TK)
    hb = nblk // 2
    return _mm_call(name, h, wblk, jax.ShapeDtypeStruct((2, t, hb * nb), F32), (t // tm, nblk, d // tk),
                    pl.BlockSpec((tm, tk), lambda i, j, kk: (i, kk)),
                    pl.BlockSpec((None, tk, nb), lambda i, j, kk: (j, kk, 0)),
                    pl.BlockSpec((None, tm, nb), lambda i, j, kk: (j // hb, i, j % hb)), _NN, (tm, nb))


def mm_up_dgrad(du0, wblk, name):
    (_, t, _), (nblk, d, nb) = du0.shape, wblk.shape
    tm, tn = _pick(t, MM_TM, SUB), _pick(d, MM_TN_NT)
    hb = nblk // 2
    return _mm_call(name, du0, wblk, jax.ShapeDtypeStruct((t, d), F32), (t // tm, d // tn, nblk),
                    pl.BlockSpec((None, tm, nb), lambda i, j, kk: (kk // hb, i, kk % hb)),
                    pl.BlockSpec((None, tn, nb), lambda i, j, kk: (kk, j, 0)),
                    pl.BlockSpec((tm, tn), lambda i, j, kk: (i, j)), _NT, (tm, tn))


def mm_up_wgrad(h, du0, name):
    (t, d), (_, _, f) = h.shape, du0.shape
    nb = f // (N_DEV // 2)
    hb = N_DEV // 2
    tm, tk = _pick(d, MM_TM), _pick(t, MM_TK, SUB)
    return _mm_call(name, h, du0, jax.ShapeDtypeStruct((N_DEV, d, nb), BF16), (d // tm, N_DEV, t // tk),
                    pl.BlockSpec((tk, tm), lambda i, j, kk: (kk, i)),
                    pl.BlockSpec((None, tk, nb), lambda i, j, kk: (j // hb, kk, j % hb)),
                    pl.BlockSpec((None, tm, nb), lambda i, j, kk: (j, i, 0)), _TN, (tm, nb))


def _rms(x, w):
    r = lax.rsqrt(jnp.mean(x * x, axis=-1, keepdims=True) + EPS)
    return x * r * w


def _rms_bwd(dy, x, w):
    r = lax.rsqrt(jnp.mean(x * x, axis=-1, keepdims=True) + EPS)
    xh = x * r
    dxh = dy * w
    dx = r * (dxh - xh * jnp.mean(dxh * xh, axis=-1, keepdims=True))
    return dx, jnp.sum(dy * xh, axis=0, keepdims=True)


def _row_spec(tb, d):
    return pl.BlockSpec((tb, d), lambda i: (i, 0))


def _fix_spec(r, d):
    return pl.BlockSpec((r, d), lambda i: (0, 0))


def norm_first(x, w):
    t, d = x.shape
    tb = _pick(t, NORM_TILE, SUB)

    def body(x_ref, w_ref, h_ref):
        h_ref[...] = _rms(x_ref[...], w_ref[...]).astype(h_ref.dtype)

    return pl.pallas_call(body, grid=(t // tb,), in_specs=[_row_spec(tb, d), _fix_spec(1, d)],
                          out_specs=_row_spec(tb, d), out_shape=jax.ShapeDtypeStruct((t, d), BF16),
                          compiler_params=_cp("parallel"), name="norm_first")(x, w)


def post_pre(x, y, w_post, w_pre):
    t, d = x.shape
    tb = _pick(t, NORM_TILE, SUB)

    def body(x_ref, y_ref, wp_ref, wq_ref, xn_ref, h_ref):
        xn = x_ref[...] + _rms(y_ref[...], wp_ref[...])
        xn_ref[...] = xn
        h_ref[...] = _rms(xn, wq_ref[...]).astype(h_ref.dtype)

    return pl.pallas_call(
        body, grid=(t // tb,),
        in_specs=[_row_spec(tb, d), _row_spec(tb, d), _fix_spec(1, d), _fix_spec(1, d)],
        out_specs=[_row_spec(tb, d), _row_spec(tb, d)],
        out_shape=(jax.ShapeDtypeStruct((t, d), F32), jax.ShapeDtypeStruct((t, d), BF16)),
        compiler_params=_cp("parallel"), name="post_pre")(x, y, w_post, w_pre)


def post_loss(x, y, w_post, target):
    t, d = x.shape
    tb = _pick(t, NORM_TILE, SUB)

    def body(x_ref, y_ref, wp_ref, t_ref, g_ref, l_ref):
        err = x_ref[...] + _rms(y_ref[...], wp_ref[...]) - t_ref[...]
        g_ref[...] = err * (1.0 / d)

        @pl.when(pl.program_id(0) == 0)
        def _():
            l_ref[...] = jnp.zeros_like(l_ref)

        part = 0.5 * jnp.sum(jnp.mean(err * err, axis=-1, keepdims=True), axis=0, keepdims=True)
        l_ref[...] += jnp.broadcast_to(part, l_ref.shape)

    return pl.pallas_call(
        body, grid=(t // tb,),
        in_specs=[_row_spec(tb, d), _row_spec(tb, d), _fix_spec(1, d), _row_spec(tb, d)],
        out_specs=[_row_spec(tb, d), _fix_spec(1, LANE)],
        out_shape=(jax.ShapeDtypeStruct((t, d), F32), jax.ShapeDtypeStruct((1, LANE), F32)),
        compiler_params=_cp("arbitrary"), name="post_loss")(x, y, w_post, target)


def bwd_norms(dx_in, *, pre=None, post=None):
    t, d = dx_in.shape
    tb = _pick(t, NORM_TILE, SUB)
    has_pre, has_post = pre is not None, post is not None

    def body(*refs):
        refs = list(refs)
        dxi = refs.pop(0)
        if has_pre:
            dh, x, wq = refs.pop(0), refs.pop(0), refs.pop(0)
        if has_post:
            y, wp = refs.pop(0), refs.pop(0)
        first = pl.program_id(0) == 0
        dx = dxi[...]
        if has_pre:
            dxo, dwq = refs.pop(0), refs.pop(0)
            g, dw = _rms_bwd(dh[...], x[...], wq[...])
            dx = dx + g
            dxo[...] = dx

            @pl.when(first)
            def _():
                dwq[...] = jnp.zeros_like(dwq)

            dwq[...] += dw
        if has_post:
            dyo, dwp = refs.pop(0), refs.pop(0)
            g, dw = _rms_bwd(dx, y[...], wp[...])
            dyo[...] = g.astype(dyo.dtype)

            @pl.when(first)
            def _():
                dwp[...] = jnp.zeros_like(dwp)

            dwp[...] += dw

    ins, in_specs, outs, out_specs = [dx_in], [_row_spec(tb, d)], [], []
    if has_pre:
        ins += list(pre)
        in_specs += [_row_spec(tb, d), _row_spec(tb, d), _fix_spec(1, d)]
        outs += [jax.ShapeDtypeStruct((t, d), F32), jax.ShapeDtypeStruct((1, d), F32)]
        out_specs += [_row_spec(tb, d), _fix_spec(1, d)]
    if has_post:
        ins += list(post)
        in_specs += [_row_spec(tb, d), _fix_spec(1, d)]
        outs += [jax.ShapeDtypeStruct((t, d), BF16), jax.ShapeDtypeStruct((1, d), F32)]
        out_specs += [_row_spec(tb, d), _fix_spec(1, d)]
    name = "bwd_norms" + ("_pre" if has_pre else "") + ("_post" if has_post else "")
    return pl.pallas_call(body, grid=(t // tb,), in_specs=in_specs, out_specs=out_specs, out_shape=tuple(outs),
                          compiler_params=_cp("arbitrary"), name=name)(*ins)


GLU_ROWS = 32


def _ffn_conv_blk(blk, cw, cb):
    r = blk.shape[1] - SUB
    x0, x1, x2 = blk[:, 6:6 + r], blk[:, 7:7 + r], blk[:, 8:8 + r]
    return x0, x1, x2, cw[:, 0:1, :] * x0 + cw[:, 1:2, :] * x1 + cw[:, 2:3, :] * x2 + cb


def _glu_specs(tb, nb, hpb, row_of):
    tile = pl.BlockSpec((2, tb, nb), lambda j, i: (0, row_of(i), j))
    halo = pl.BlockSpec((2, SUB, nb), lambda j, i: (0, jnp.maximum(row_of(i) * hpb - 1, 0), j))
    cw = pl.BlockSpec((2, None, 3, nb), lambda j, i: (0, j, 0, 0))
    cb = pl.BlockSpec((2, None, 1, nb), lambda j, i: (0, j, 0, 0))
    return tile, halo, cw, cb


def glu_fwd(u0, cw, cb):
    _, t, f = u0.shape
    nb = cw.shape[-1]
    tb = _pick(t, ROW_TILE, SUB)
    nt, hpb = t // tb, tb // SUB

    def body(u, h, cwr, cbr, o_ref, e):
        i = pl.program_id(1)
        e[:, 0:SUB, :] = jnp.where(i > 0, h[...], 0.0)
        e[:, SUB:, :] = u[...]
        w, bias = cwr[...], cbr[...]

        def rows(g, carry):
            s = pl.multiple_of(g * GLU_ROWS, GLU_ROWS)
            for lg in range(nb // LANE):
                ls = slice(lg * LANE, (lg + 1) * LANE)
                ab = _ffn_conv_blk(e[:, pl.ds(s, GLU_ROWS + SUB), ls], w[:, :, ls], bias[:, :, ls])[3]
                a, b = ab[0], ab[1]
                o_ref[pl.ds(s, GLU_ROWS), ls] = (a * _sigmoid(a) * b).astype(o_ref.dtype)
            return carry

        lax.fori_loop(0, tb // GLU_ROWS, rows, 0)

    return pl.pallas_call(
        body, grid=(f // nb, nt), in_specs=list(_glu_specs(tb, nb, hpb, lambda i: i)),
        out_specs=pl.BlockSpec((tb, nb), lambda j, i: (i, j)),
        out_shape=jax.ShapeDtypeStruct((t, f), BF16),
        scratch_shapes=[pltpu.VMEM((2, tb + SUB, nb), F32)],
        compiler_params=_cp("parallel", "arbitrary"), name="glu_fwd")(u0, u0, cw, cb)


def glu_bwd(dact, u0, cw, cb):
    _, t, f = u0.shape
    nb = cw.shape[-1]
    tb = _pick(t, ROW_TILE, SUB)
    nt, hpb = t // tb, tb // SUB

    def body(d_ref, u, h, cwr, cbr, du_o, dc_o, e, x2):
        i = pl.program_id(1)
        r = nt - 1 - i
        e[:, 0:SUB, :] = jnp.where(r > 0, h[...], 0.0)
        e[:, SUB:, :] = u[...]
        w, bias = cwr[...], cbr[...]

        @pl.when(i == 0)
        def _():
            dc_o[...] = jnp.zeros_like(dc_o)
            x2[:, tb:, :] = jnp.zeros((2, SUB, nb), F32)

        fold = lambda v: jnp.sum(v.reshape(2, GLU_ROWS // SUB, SUB, LANE), axis=1)
        for lg in range(nb // LANE):
            ls = slice(lg * LANE, (lg + 1) * LANE)
            wl, bl = w[:, :, ls], bias[:, :, ls]

            def grads(g, acc):
                s = pl.multiple_of(g * GLU_ROWS, GLU_ROWS)
                x0, x1, xc, ab = _ffn_conv_blk(e[:, pl.ds(s, GLU_ROWS + SUB), ls], wl, bl)
                a, b = ab[0], ab[1]
                sa = _sigmoid(a)
                d = d_ref[pl.ds(s, GLU_ROWS), ls]
                x2[0, pl.ds(s, GLU_ROWS), ls] = d * b * (sa * (1.0 + a * (1.0 - sa)))
                x2[1, pl.ds(s, GLU_ROWS), ls] = d * (a * sa)
                du = x2[:, pl.ds(s, GLU_ROWS), ls]
                return (acc[0] + fold(du * x0), acc[1] + fold(du * x1), acc[2] + fold(du * xc), acc[3] + fold(du))

            zero = jnp.zeros((2, SUB, LANE), F32)
            acc = lax.fori_loop(0, tb // GLU_ROWS, grads, (zero, zero, zero, zero))
            for k in range(4):
                dc_o[:, k:k + 1, ls] += jnp.sum(acc[k], axis=1, keepdims=True)

            def transposed_conv(g, carry):
                s = pl.multiple_of(g * GLU_ROWS, GLU_ROWS)
                blk = x2[:, pl.ds(s, GLU_ROWS + SUB), ls]
                du_o[:, pl.ds(s, GLU_ROWS), ls] = (
                    wl[:, 2:3, :] * blk[:, 0:GLU_ROWS] + wl[:, 1:2, :] * blk[:, 1:1 + GLU_ROWS]
                    + wl[:, 0:1, :] * blk[:, 2:2 + GLU_ROWS]).astype(du_o.dtype)
                return carry

            lax.fori_loop(0, tb // GLU_ROWS, transposed_conv, 0)
        x2[:, tb:, :] = x2[:, 0:SUB, :]

    rev = lambda i: nt - 1 - i
    return pl.pallas_call(
        body, grid=(f // nb, nt),
        in_specs=[pl.BlockSpec((tb, nb), lambda j, i: (rev(i), j))] + list(_glu_specs(tb, nb, hpb, rev)),
        out_specs=[pl.BlockSpec((2, tb, nb), lambda j, i: (0, rev(i), j)),
                   pl.BlockSpec((2, None, SUB, nb), lambda j, i: (0, j, 0, 0))],
        out_shape=(jax.ShapeDtypeStruct((2, t, f), BF16), jax.ShapeDtypeStruct((2, f // nb, SUB, nb), F32)),
        scratch_shapes=[pltpu.VMEM((2, tb + SUB, nb), F32), pltpu.VMEM((2, tb + SUB, nb), F32)],
        compiler_params=_cp("arbitrary", "arbitrary"), name="glu_bwd")(dact, u0, u0, cw, cb)


POOL_HALO = 16
_PCOL = O_POOL // LANE
_CPOOL = 1536 // LANE


def _pool_sel(g, v2, v4, v8, v16):
    return jnp.where(g == 0, v2, jnp.where(g == 1, v4, jnp.where(g == 2, v8, v16)))


def _pool_cnt(g, t0, n):
    win = _pool_sel(g, 2, 4, 8, 16)
    tpos = t0 + lax.broadcasted_iota(jnp.int32, (n, 1), 0)
    return jnp.minimum(tpos + 1, win).astype(F32)


def _pool_core(e, g, t0, tb):
    s2 = e + pltpu.roll(e, 1, 0)
    s4 = s2 + pltpu.roll(s2, 2, 0)
    s8 = s4 + pltpu.roll(s4, 4, 0)
    s16 = s8 + pltpu.roll(s8, 8, 0)
    sw = _pool_sel(g, s2, s4, s8, s16)[POOL_HALO:]
    return sw / _pool_cnt(g, t0, tb) - e[POOL_HALO:]


def pool_fwd(p, pool_w, pool_scale):
    t = p.shape[0]
    tb = _pick(t, ROW_TILE, POOL_HALO)
    nt, hpb = t // tb, tb // POOL_HALO

    def body(x_ref, h_ref, w_ref, s_ref, o_ref):
        i, g = pl.program_id(0), pl.program_id(1)
        e = jnp.concatenate([jnp.where(i > 0, h_ref[...], 0.0), x_ref[...]], axis=0)
        yy = _pool_core(e, g, i * tb, tb)
        o_ref[...] = (_mm(yy, w_ref[...]) * s_ref[...]).astype(o_ref.dtype)

    return pl.pallas_call(
        body, grid=(nt, POOL_G),
        in_specs=[pl.BlockSpec((tb, LANE), lambda i, g: (i, _PCOL + g)),
                  pl.BlockSpec((POOL_HALO, LANE), lambda i, g: (jnp.maximum(i * hpb - 1, 0), _PCOL + g)),
                  pl.BlockSpec((None, LANE, LANE), lambda i, g: (g, 0, 0)),
                  pl.BlockSpec((1, LANE), lambda i, g: (0, g))],
        out_specs=pl.BlockSpec((tb, LANE), lambda i, g: (i, g)),
        out_shape=jax.ShapeDtypeStruct((t, POOL_G * LANE), BF16),
        compiler_params=_cp("parallel", "parallel"), name="pool_fwd")(p, p, pool_w, pool_scale)


def pool_bwd(p, dc, pool_w, pool_scale):
    t = p.shape[0]
    tb = _pick(t, ROW_TILE, POOL_HALO)
    nt, hpb = t // tb, tb // POOL_HALO
    n = tb + POOL_HALO

    def body(x_ref, h_ref, dy_ref, dn_ref, w_ref, s_ref, dx_o, dw_o, ds_o):
        g, i = pl.program_id(0), pl.program_id(1)
        e = jnp.concatenate([jnp.where(i > 0, h_ref[...], 0.0), x_ref[...]], axis=0)
        yy = _pool_core(e, g, i * tb, tb)
        w, sc, dy = w_ref[...], s_ref[...], dy_ref[...]

        @pl.when(i == 0)
        def _():
            dw_o[...] = jnp.zeros_like(dw_o)
            ds_o[...] = jnp.zeros_like(ds_o)

        ds_o[...] += jnp.sum(dy * _mm(yy, w), axis=0, keepdims=True)
        dw_o[...] += _mm_tn(yy, dy * sc)
        dye = jnp.concatenate([dy, jnp.where(i < nt - 1, dn_ref[...], 0.0)], axis=0) * sc
        dyy = _mm_nt(dye, w)
        z = dyy / _pool_cnt(g, i * tb, n)
        r2 = z + pltpu.roll(z, n - 1, 0)
        r4 = r2 + pltpu.roll(r2, n - 2, 0)
        r8 = r4 + pltpu.roll(r4, n - 4, 0)
        r16 = r8 + pltpu.roll(r8, n - 8, 0)
        dx_o[...] = (_pool_sel(g, r2, r4, r8, r16)[:tb] - dyy[:tb]).astype(dx_o.dtype)

    last = t // POOL_HALO - 1
    return pl.pallas_call(
        body, grid=(POOL_G, nt),
        in_specs=[pl.BlockSpec((tb, LANE), lambda g, i: (i, _PCOL + g)),
                  pl.BlockSpec((POOL_HALO, LANE), lambda g, i: (jnp.maximum(i * hpb - 1, 0), _PCOL + g)),
                  pl.BlockSpec((tb, LANE), lambda g, i: (i, _CPOOL + g)),
                  pl.BlockSpec((POOL_HALO, LANE), lambda g, i: (jnp.minimum((i + 1) * hpb, last), _CPOOL + g)),
                  pl.BlockSpec((None, LANE, LANE), lambda g, i: (g, 0, 0)),
                  pl.BlockSpec((1, LANE), lambda g, i: (0, g))],
        out_specs=[pl.BlockSpec((tb, LANE), lambda g, i: (i, g)),
                   pl.BlockSpec((None, LANE, LANE), lambda g, i: (g, 0, 0)),
                   pl.BlockSpec((1, LANE), lambda g, i: (0, g))],
        out_shape=(jax.ShapeDtypeStruct((t, POOL_G * LANE), BF16),
                   jax.ShapeDtypeStruct((POOL_G, LANE, LANE), F32),
                   jax.ShapeDtypeStruct((1, POOL_G * LANE), F32)),
        compiler_params=_cp("arbitrary", "arbitrary"), name="pool_bwd")(p, p, dc, dc, pool_w, pool_scale)


_QCOL, _KCOL, _VCOL = O_SQ // 768, O_SK // 256, O_SV // 256
_GQ = SWA_G * SWA_BLK


def _rope(x, c2, s2):
    return x * c2 + pltpu.roll(x, HD // 2, 1) * s2


def _rope_bwd(d, c2, s2):
    return d * c2 + pltpu.roll(d * s2, HD // 2, 1)


def _hs(x, h):
    return x[:, h * HD:(h + 1) * HD]


def _swa_group(q, kc, kp, vc, vp, c2c, s2c, c2p, s2p, sinks, h, blk):
    kcat = jnp.concatenate([_rope(_hs(kp, h), c2p, s2p), _rope(_hs(kc, h), c2c, s2c)], axis=0)
    vcat = jnp.concatenate([_hs(vp, h), _hs(vc, h)], axis=0)
    qs = jnp.concatenate([_rope(_hs(q, SWA_G * h + g), c2c, s2c) for g in range(SWA_G)], axis=0)
    s = _mm_nt(qs, kcat) * SCALE
    yield
    ii = lax.broadcasted_iota(jnp.int32, (_GQ, 2 * SWA_BLK), 0) & (SWA_BLK - 1)
    jj = lax.broadcasted_iota(jnp.int32, (_GQ, 2 * SWA_BLK), 1)
    lo = jnp.where(blk > 0, 0, SWA_BLK)
    s = jnp.where((jj > ii) & (jj <= ii + SWA_BLK) & (jj >= lo), s, NEG)
    sink = jnp.concatenate(
        [jnp.broadcast_to(sinks[:, SWA_G * h + g:SWA_G * h + g + 1], (SWA_BLK, 1)) for g in range(SWA_G)], axis=0)
    m = jnp.maximum(jnp.max(s, axis=1, keepdims=True), sink)
    p = jnp.exp(s - m)
    ps = jnp.exp(sink - m)
    l = jnp.sum(p, axis=1, keepdims=True) + ps
    return qs, kcat, vcat, p, ps, l


def _swa_specs(blk_of):
    cur = lambda w, c: pl.BlockSpec((SWA_BLK, w), lambda n: (blk_of(n), c))
    prev = lambda w, c: pl.BlockSpec((SWA_BLK, w), lambda n: (jnp.maximum(blk_of(n) - 1, 0), c))
    return [cur(768, _QCOL), cur(256, _KCOL), prev(256, _KCOL), cur(256, _VCOL), prev(256, _VCOL),
            cur(HD, 0), cur(HD, 0), prev(HD, 0), prev(HD, 0), pl.BlockSpec((1, LANE), lambda n: (0, 0))]


def swa_fwd(p, cos2, sin2, sinks):
    t = p.shape[0]

    def body(q_ref, kc, kp, vc, vp, c2c, s2c, c2p, s2p, sk_ref, o_ref):
        n = pl.program_id(0)

        def head(h):
            _, _, vcat, pr, _, l = yield from _swa_group(q_ref[...], kc[...], kp[...], vc[...], vp[...], c2c[...],
                                                         s2c[...], c2p[...], s2p[...], sk_ref[...], h, n)
            o = _mm(pr, vcat)
            yield
            return o / l

        for h, o in enumerate(_lockstep([head(h) for h in range(SWA_KV)])):
            for g in range(SWA_G):
                hh = SWA_G * h + g
                o_ref[:, hh * HD:(hh + 1) * HD] = o[g * SWA_BLK:(g + 1) * SWA_BLK].astype(o_ref.dtype)

    return pl.pallas_call(
        body, grid=(t // SWA_BLK,), in_specs=_swa_specs(lambda n: n),
        out_specs=pl.BlockSpec((SWA_BLK, 768), lambda n: (n, 0)),
        out_shape=jax.ShapeDtypeStruct((t, 768), BF16),
        compiler_params=_cp("parallel"), name="swa_fwd")(p, p, p, p, p, cos2, sin2, cos2, sin2, sinks)


def swa_bwd(p, dc, cos2, sin2, sinks):
    t = p.shape[0]
    nb = t // SWA_BLK

    def body(q_ref, kc, kp, vc, vp, c2c, s2c, c2p, s2p, sk_ref, do_ref, dq_o, dk_o, dv_o, dsk_o, ck, cv):
        i = pl.program_id(0)
        r = nb - 1 - i

        @pl.when(i == 0)
        def _():
            ck[...] = jnp.zeros_like(ck)
            cv[...] = jnp.zeros_like(cv)
            dsk_o[...] = jnp.zeros_like(dsk_o)

        lane = lax.broadcasted_iota(jnp.int32, (1, LANE), 1)
        dsk = jnp.zeros((1, LANE), F32)
        do = do_ref[...]
        def head(h):
            qs, kcat, vcat, pr, ps, l = yield from _swa_group(q_ref[...], kc[...], kp[...], vc[...], vp[...],
                                                              c2c[...], s2c[...], c2p[...], s2p[...], sk_ref[...], h, r)
            pn = pr / l
            dos = jnp.concatenate([_hs(do, SWA_G * h + g) for g in range(SWA_G)], axis=0)
            dp = _mm_nt(dos, vcat)
            dv = _mm_tn(pn, dos)
            yield
            delta = jnp.sum(pn * dp, axis=1, keepdims=True)
            ds = pn * (dp - delta)
            dqs = _mm(ds, kcat) * SCALE
            dk = _mm_tn(ds, qs) * SCALE
            yield
            return dqs, dk, dv, -(ps / l) * delta

        for h, (dqs, dk, dv, dsr) in enumerate(_lockstep([head(h) for h in range(SWA_KV)])):
            for g in range(SWA_G):
                tot = jnp.sum(dsr[g * SWA_BLK:(g + 1) * SWA_BLK], axis=0, keepdims=True)
                dsk = dsk + jnp.where(lane == SWA_G * h + g, tot, 0.0)
            for g in range(SWA_G):
                hh = SWA_G * h + g
                dq_o[:, hh * HD:(hh + 1) * HD] = _rope_bwd(dqs[g * SWA_BLK:(g + 1) * SWA_BLK], c2c[...],
                                                          s2c[...]).astype(dq_o.dtype)
            cs = slice(h * HD, (h + 1) * HD)
            dk_o[:, cs] = (_rope_bwd(dk[SWA_BLK:], c2c[...], s2c[...]) + ck[:, cs]).astype(dk_o.dtype)
            dv_o[:, cs] = (dv[SWA_BLK:] + cv[:, cs]).astype(dv_o.dtype)
            ck[:, cs] = _rope_bwd(dk[:SWA_BLK], c2p[...], s2p[...])
            cv[:, cs] = dv[:SWA_BLK]
        dsk_o[...] += dsk

    rev = lambda n: nb - 1 - n
    return pl.pallas_call(
        body, grid=(nb,),
        in_specs=_swa_specs(rev) + [pl.BlockSpec((SWA_BLK, 768), lambda n: (rev(n), 1))],
        out_specs=[pl.BlockSpec((SWA_BLK, 768), lambda n: (rev(n), 0)),
                   pl.BlockSpec((SWA_BLK, 256), lambda n: (rev(n), 0)),
                   pl.BlockSpec((SWA_BLK, 256), lambda n: (rev(n), 0)),
                   pl.BlockSpec((1, LANE), lambda n: (0, 0))],
        out_shape=(jax.ShapeDtypeStruct((t, 768), BF16), jax.ShapeDtypeStruct((t, 256), BF16),
                   jax.ShapeDtypeStruct((t, 256), BF16), jax.ShapeDtypeStruct((1, LANE), F32)),
        scratch_shapes=[pltpu.VMEM((SWA_BLK, 256), F32), pltpu.VMEM((SWA_BLK, 256), F32)],
        compiler_params=_cp("arbitrary"), name="swa_bwd")(p, p, p, p, p, cos2, sin2, cos2, sin2, sinks, dc)


_ZCOL, _GCOL = O_Z // DN_W, O_GATE // LANE
_QKV_W = 3 * DN_W
_INV_STEPS = int(math.log2(CH)) - 1


class _Bag(dict):
    __getattr__ = dict.__getitem__


DN_CB = 4
_DN_ROWS = DN_CB * CH
_CH_SHIFT = CH.bit_length() - 1


def _dn_consts():
    ii = lax.broadcasted_iota(jnp.int32, (CH, CH), 0)
    jj = lax.broadcasted_iota(jnp.int32, (CH, CH), 1)
    bi = lax.broadcasted_iota(jnp.int32, (_DN_ROWS, _DN_ROWS), 0)
    bj = lax.broadcasted_iota(jnp.int32, (_DN_ROWS, _DN_ROWS), 1)
    same_chunk = jnp.right_shift(bi, _CH_SHIFT) == jnp.right_shift(bj, _CH_SHIFT)
    return _Bag(lower=ii >= jj, strict=ii > jj, diag=ii == jj,
                eye=jnp.where(ii == jj, 1.0, 0.0).astype(F32),
                tril_blk=jnp.where(same_chunk & (bi >= bj), 1.0, 0.0).astype(F32),
                ones=jnp.ones((CH, CH), F32), ones_w=jnp.ones((CH, LANE), F32),
                rows=lax.broadcasted_iota(jnp.int32, (CH, 1), 0),
                lane=lax.broadcasted_iota(jnp.int32, (1, LANE), 1))


def _dn_conv(ext_ref, cw):
    return (cw[0:1, :] * ext_ref[pl.ds(5, _DN_ROWS), :] + cw[1:2, :] * ext_ref[pl.ds(6, _DN_ROWS), :]
            + cw[2:3, :] * ext_ref[pl.ds(7, _DN_ROWS), :] + cw[3:4, :] * ext_ref[pl.ds(8, _DN_ROWS), :])


def _dn_gates(gt, arow, drow, c):
    beta = _sigmoid(gt)
    ea = jnp.exp(arow)
    xa = gt + drow
    g = -ea * _softplus(xa)
    return beta, g, _mm_hi(c.tril_blk, g), ea, _sigmoid(xa)


def _blk(a, ci, j):
    return a[ci * CH:(ci + 1) * CH, j * HD:(j + 1) * HD]


def _lockstep(gens):
    out, live = [None] * len(gens), list(range(len(gens)))
    while live:
        still = []
        for i in live:
            try:
                next(gens[i])
                still.append(i)
            except StopIteration as stop:
                out[i] = stop.value
        live = still
    return out


def _dn_head_a(qh, kh, vh, beta, gc, c):
    rq = lax.rsqrt(jnp.sum(qh * qh, axis=1, keepdims=True) + EPS)
    rk = lax.rsqrt(jnp.sum(kh * kh, axis=1, keepdims=True) + EPS)
    qn = qh * rq * SCALE
    kn = kh * rk
    kb = kn * beta
    vb = vh * beta
    gcol = _mm_hi(c.ones, jnp.where(c.diag, gc, 0.0))
    kk = _mm_nt(kb, kn)
    qk = _mm_nt(qn, kn)
    yield
    gam = jnp.where(c.lower, jnp.exp(jnp.minimum(gc - gcol, 0.0)), 0.0)
    lmat = jnp.where(c.strict, kk * gam, 0.0)
    amat = qk * gam
    nil = -lmat
    inv = c.eye + nil
    powk = nil
    for _ in range(_INV_STEPS):
        powk = _mm(powk, powk)
        yield
        inv = _mm(inv, c.eye + powk)
    eg = jnp.exp(gc)
    kbe = kb * eg
    yield
    u = _mm(inv, vb)
    w = _mm(inv, kbe)
    gl = gc[CH - 1:CH, :]
    e2 = jnp.exp(gl - gc)
    cd = jnp.exp(gl)
    qd = qn * eg
    kd = kn * e2
    return _Bag(rq=rq, rk=rk, qn=qn, kn=kn, kb=kb, vb=vb, gam=gam, lmat=lmat, inv=inv, eg=eg, kbe=kbe, u=u, w=w,
                amat=amat, e2=e2, cd=cd, qd=qd, kd=kd)


def _dn_head_b(f, s0):
    ws = _mm(f.w, s0)
    qs = _mm(f.qd, s0)
    yield
    vnew = f.u - ws
    return vnew, qs + _mm(f.amat, vnew), s0 * f.cd + _mm_tn(f.kd, vnew)


def _dn_post(o, zh, nw):
    ro = lax.rsqrt(jnp.mean(o * o, axis=1, keepdims=True) + EPS)
    oh = o * ro
    sz = _sigmoid(zh)
    return ro, oh, sz, oh * nw * (zh * sz)


def _dn_post_bwd(o, zh, nw, dy):
    ro, oh, sz, _ = _dn_post(o, zh, nw)
    don = dy * (zh * sz)
    dz = dy * (oh * nw) * (sz * (1.0 + zh * (1.0 - sz)))
    doh = don * nw
    return (ro * (doh - oh * jnp.mean(doh * oh, axis=1, keepdims=True)), dz,
            jnp.sum(don * oh, axis=0, keepdims=True))


def _dn_head_bwd_b(f, vnew, do, dsn, s0):
    a_do = _mm_tn(f.amat, do)
    kd_ds = _mm(f.kd, dsn)
    qd_do = _mm_tn(f.qd, do)
    dkd = _mm_nt(vnew, dsn)
    yield
    dvnew = a_do + kd_ds
    ds0 = qd_do + f.cd * dsn - _mm_tn(f.w, dvnew)
    dcd = jnp.sum(jnp.sum(s0 * dsn, axis=1, keepdims=True), axis=0, keepdims=True)
    return dvnew, ds0, dkd, dcd


def _dn_head_bwd_c(f, vnew, do, dvnew, dkd, dcd, s0, qh, vh, beta, c):
    da = jnp.where(c.lower, _mm_nt(do, vnew), 0.0)
    dqd = _mm_nt(do, s0)
    dw = -_mm_nt(dvnew, s0)
    dt_u = _mm_nt(dvnew, f.vb)
    dvb = _mm_tn(f.inv, dvnew)
    yield
    dt = dt_u + _mm_nt(dw, f.kbe)
    dkbe = _mm_tn(f.inv, dw)
    yield
    dt_inv = _mm_nt(dt, f.inv)
    yield
    dl = -jnp.where(c.strict, _mm_tn(f.inv, dt_inv), 0.0)
    yield
    dm = dl * f.gam
    dn = da * f.gam
    dkb = _mm(dm, f.kn) + dkbe * f.eg
    dkn = _mm_tn(dm, f.kb) + _mm_tn(dn, f.qn) + dkd * f.e2 + beta * dkb
    dqn = _mm(dn, f.kn) + dqd * f.eg
    pm = dl * f.lmat + da * f.amat
    colsum = _mm_hi(pm, c.ones_w, _TN)[:, 0:1]
    yield
    tkd = jnp.sum(dkd * f.kn, axis=1, keepdims=True) * f.e2
    dgc = (jnp.sum(pm, axis=1, keepdims=True) - colsum - tkd
           + (jnp.sum(dqd * f.qn, axis=1, keepdims=True) + jnp.sum(dkbe * f.kb, axis=1, keepdims=True)) * f.eg)
    dgl = jnp.sum(tkd, axis=0, keepdims=True) + dcd * f.cd
    dgc = dgc + jnp.where(c.rows == CH - 1, dgl, 0.0)
    dbeta = jnp.sum(dkb * f.kn, axis=1, keepdims=True) + jnp.sum(dvb * vh, axis=1, keepdims=True)
    dvh = beta * dvb
    qhat = qh * f.rq
    dqs = dqn * SCALE
    dqh = f.rq * (dqs - qhat * jnp.sum(qhat * dqs, axis=1, keepdims=True))
    dkh = f.rk * (dkn - f.kn * jnp.sum(f.kn * dkn, axis=1, keepdims=True))
    return dqh, dkh, dvh, dbeta, dgc


def _dn_in_specs(step_of):
    return [pl.BlockSpec((_DN_ROWS, _QKV_W), lambda n: (step_of(n), 0)),
            pl.BlockSpec((SUB, _QKV_W), lambda n: (jnp.maximum(step_of(n) * (_DN_ROWS // SUB) - 1, 0), 0)),
            pl.BlockSpec((_DN_ROWS, DN_W), lambda n: (step_of(n), _ZCOL)),
            pl.BlockSpec((_DN_ROWS, LANE), lambda n: (step_of(n), _GCOL)),
            pl.BlockSpec((DN_K, _QKV_W), lambda n: (0, 0)),
            pl.BlockSpec((SUB, LANE), lambda n: (0, 0))]


def _dn_heads_a(qkv, beta_all, gc_all, c):
    rows = lambda a, ci: a[ci * CH:(ci + 1) * CH]
    flat = _lockstep([_dn_head_a(_blk(qkv, ci, h), _blk(qkv, ci, DN_H + h), _blk(qkv, ci, 2 * DN_H + h),
                                 rows(beta_all, ci)[:, h:h + 1], rows(gc_all, ci)[:, DN_H + h:DN_H + h + 1], c)
                      for ci in range(DN_CB) for h in range(DN_H)])
    return [flat[ci * DN_H:(ci + 1) * DN_H] for ci in range(DN_CB)]


def dn_fwd(p, conv_w, par):
    t = p.shape[0]
    nc = t // CH
    assert t % _DN_ROWS == 0

    def body(x_ref, h_ref, z_ref, g_ref, cw_ref, par_ref, y_o, s_o, ext, st):
        n = pl.program_id(0)
        c = _dn_consts()

        @pl.when(n == 0)
        def _():
            st[...] = jnp.zeros_like(st)

        ext[0:SUB, :] = jnp.where(n > 0, h_ref[...], 0.0)
        ext[SUB:, :] = x_ref[...]
        pre = _dn_conv(ext, cw_ref[...])
        qkv = pre * _sigmoid(pre)
        par = par_ref[...]
        beta_all, _, gc_all, _, _ = _dn_gates(g_ref[...], par[0:1, :], par[1:2, :], c)
        z = z_ref[...]
        fa = _dn_heads_a(qkv, beta_all, gc_all, c)
        s = [st[h] for h in range(DN_H)]
        for ci in range(DN_CB):
            for h in range(DN_H):
                s_o[ci, h] = s[h]
            res = _lockstep([_dn_head_b(fa[ci][h], s[h]) for h in range(DN_H)])
            for h in range(DN_H):
                _, o, s[h] = res[h]
                y_o[ci * CH:(ci + 1) * CH, h * HD:(h + 1) * HD] = _dn_post(o, _blk(z, ci, h),
                                                                           par[2:3, :])[3].astype(y_o.dtype)
        for h in range(DN_H):
            st[h] = s[h]

    return pl.pallas_call(
        body, grid=(t // _DN_ROWS,), in_specs=_dn_in_specs(lambda n: n),
        out_specs=[pl.BlockSpec((_DN_ROWS, DN_W), lambda n: (n, 0)),
                   pl.BlockSpec((DN_CB, DN_H, HD, HD), lambda n: (n, 0, 0, 0))],
        out_shape=(jax.ShapeDtypeStruct((t, DN_W), BF16), jax.ShapeDtypeStruct((nc, DN_H, HD, HD), F32)),
        scratch_shapes=[pltpu.VMEM((_DN_ROWS + SUB, _QKV_W), F32), pltpu.VMEM((DN_H, HD, HD), F32)],
        compiler_params=_cp("arbitrary"), name="dn_fwd")(p, p, p, p, conv_w, par)


def dn_bwd(p, dc, states, conv_w, par):
    t = p.shape[0]
    ns = t // _DN_ROWS

    def body(x_ref, h_ref, z_ref, g_ref, cw_ref, par_ref, s_ref, dy_ref,
             dx_o, dz_o, dg_o, dcw_o, dpar_o, ext, dst, dpost, x2):
        i = pl.program_id(0)
        r = ns - 1 - i
        c = _dn_consts()

        @pl.when(i == 0)
        def _():
            dst[...] = jnp.zeros_like(dst)
            dcw_o[...] = jnp.zeros_like(dcw_o)
            dpar_o[...] = jnp.zeros_like(dpar_o)
            x2[_DN_ROWS:, :] = jnp.zeros((SUB, _QKV_W), F32)

        ext[0:SUB, :] = jnp.where(r > 0, h_ref[...], 0.0)
        ext[SUB:, :] = x_ref[...]
        cw = cw_ref[...]
        pre = _dn_conv(ext, cw)
        sg = _sigmoid(pre)
        qkv = pre * sg
        par = par_ref[...]
        gt = g_ref[...]
        beta_all, g_all, gc_all, ea, sxa = _dn_gates(gt, par[0:1, :], par[1:2, :], c)
        z, dy = z_ref[...], dy_ref[...]
        nw = par[2:3, :]
        dnw = jnp.zeros((1, LANE), F32)
        pairs = [(ci, h) for ci in range(DN_CB) for h in range(DN_H)]
        fa = _dn_heads_a(qkv, beta_all, gc_all, c)
        vnew, do = {}, {}
        fwd = _lockstep([_dn_head_b(fa[ci][h], s_ref[ci, h]) for ci, h in pairs])
        for (ci, h), (vn, o, _) in zip(pairs, fwd):
            vnew[ci, h] = vn
            do[ci, h], dz, dnw_h = _dn_post_bwd(o, _blk(z, ci, h), nw, _blk(dy, ci, h))
            dnw = dnw + dnw_h
            dz_o[ci * CH:(ci + 1) * CH, h * HD:(h + 1) * HD] = dz.astype(dz_o.dtype)
        ds = [dst[h] for h in range(DN_H)]
        seq = {}
        for ci in range(DN_CB - 1, -1, -1):
            res = _lockstep([_dn_head_bwd_b(fa[ci][h], vnew[ci, h], do[ci, h], ds[h], s_ref[ci, h])
                             for h in range(DN_H)])
            for h in range(DN_H):
                dvnew, ds[h], dkd, dcd = res[h]
                seq[ci, h] = (dvnew, dkd, dcd)
        for h in range(DN_H):
            dst[h] = ds[h]
        rest = _lockstep([_dn_head_bwd_c(
            fa[ci][h], vnew[ci, h], do[ci, h], *seq[ci, h], s_ref[ci, h], _blk(qkv, ci, h),
            _blk(qkv, ci, 2 * DN_H + h), beta_all[ci * CH:(ci + 1) * CH, h:h + 1], c) for ci, h in pairs])
        dbeta_rows, dgc_rows = [], []
        for ci in range(DN_CB):
            dbeta_c = jnp.zeros((CH, LANE), F32)
            dgc_c = jnp.zeros((CH, LANE), F32)
            for h in range(DN_H):
                dqh, dkh, dvh, dbeta, dgc = rest[ci * DN_H + h]
                dbeta_c = dbeta_c + jnp.where(c.lane == h, dbeta, 0.0)
                dgc_c = dgc_c + jnp.where(c.lane == DN_H + h, dgc, 0.0)
                rs = slice(ci * CH, (ci + 1) * CH)
                dpost[rs, h * HD:(h + 1) * HD] = dqh
                dpost[rs, (DN_H + h) * HD:(DN_H + h + 1) * HD] = dkh
                dpost[rs, (2 * DN_H + h) * HD:(2 * DN_H + h + 1) * HD] = dvh
            dbeta_rows.append(dbeta_c)
            dgc_rows.append(dgc_c)
        dbeta_all = jnp.concatenate(dbeta_rows, axis=0)
        dgc_all = jnp.concatenate(dgc_rows, axis=0)
        dg_all = _mm_hi(c.tril_blk, dgc_all, _TN)
        dpa = dg_all * (-ea) * sxa
        dpb = dbeta_all * beta_all * (1.0 - beta_all)
        is_b = c.lane < DN_H
        is_a = (c.lane >= DN_H) & (c.lane < 2 * DN_H)
        dg_o[...] = jnp.where(is_b, dpb, jnp.where(is_a, dpa, 0.0)).astype(dg_o.dtype)
        dpar_o[0:1, :] += jnp.where(is_a, jnp.sum(dg_all * g_all, axis=0, keepdims=True), 0.0)
        dpar_o[1:2, :] += jnp.where(is_a, jnp.sum(dpa, axis=0, keepdims=True), 0.0)
        dpar_o[2:3, :] += dnw
        dpre = dpost[...] * (sg * (1.0 + pre * (1.0 - sg)))
        for k in range(DN_K):
            dcw_o[k:k + 1, :] += jnp.sum(dpre * ext[pl.ds(5 + k, _DN_ROWS), :], axis=0, keepdims=True)
        x2[0:_DN_ROWS, :] = dpre
        dx_o[...] = (cw[3:4, :] * dpre + cw[2:3, :] * x2[pl.ds(1, _DN_ROWS), :]
                     + cw[1:2, :] * x2[pl.ds(2, _DN_ROWS), :]
                     + cw[0:1, :] * x2[pl.ds(3, _DN_ROWS), :]).astype(dx_o.dtype)
        x2[_DN_ROWS:, :] = dpre[0:SUB, :]

    rev = lambda n: ns - 1 - n
    return pl.pallas_call(
        body, grid=(ns,),
        in_specs=_dn_in_specs(rev) + [pl.BlockSpec((DN_CB, DN_H, HD, HD), lambda n: (rev(n), 0, 0, 0)),
                                      pl.BlockSpec((_DN_ROWS, DN_W), lambda n: (rev(n), 0))],
        out_specs=[pl.BlockSpec((_DN_ROWS, _QKV_W), lambda n: (rev(n), 0)),
                   pl.BlockSpec((_DN_ROWS, DN_W), lambda n: (rev(n), 0)),
                   pl.BlockSpec((_DN_ROWS, LANE), lambda n: (rev(n), 0)),
                   pl.BlockSpec((SUB, _QKV_W), lambda n: (0, 0)),
                   pl.BlockSpec((SUB, LANE), lambda n: (0, 0))],
        out_shape=(jax.ShapeDtypeStruct((t, _QKV_W), BF16), jax.ShapeDtypeStruct((t, DN_W), BF16),
                   jax.ShapeDtypeStruct((t, LANE), BF16), jax.ShapeDtypeStruct((SUB, _QKV_W), F32),
                   jax.ShapeDtypeStruct((SUB, LANE), F32)),
        scratch_shapes=[pltpu.VMEM((_DN_ROWS + SUB, _QKV_W), F32), pltpu.VMEM((DN_H, HD, HD), F32),
                        pltpu.VMEM((_DN_ROWS, _QKV_W), F32), pltpu.VMEM((_DN_ROWS + SUB, _QKV_W), F32)],
        compiler_params=_cp("arbitrary"), name="dn_bwd")(p, p, p, p, conv_w, par, states, dc)


_ANY = pl.BlockSpec(memory_space=pl.ANY)
_MESH = pl.DeviceIdType.MESH


def _me():
    return lax.axis_index("x"), lax.axis_index("y"), lax.axis_index("c")


def all_gather(x, name):
    def body(x_ref, out_ref, send_sems, recv_sems, local_sem):
        mx, my, mc = _me()
        me, sibling = (mx, my, mc), (mx, my, 1 - mc)
        chips = [(1 - mx, my), (mx, 1 - my), (1 - mx, 1 - my)]

        def slot(px, py, pc):
            return out_ref.at[4 * px + 2 * py + pc]

        def copy(k, block, to, src=None):
            return pltpu.make_async_remote_copy(
                src_ref=slot(*block) if src is None else src, dst_ref=slot(*block),
                send_sem=send_sems.at[k], recv_sem=recv_sems.at[k], device_id=to, device_id_type=_MESH)

        mine = pltpu.make_async_copy(x_ref, slot(*me), local_sem)
        mine.start()
        first = [copy(0, me, sibling, src=x_ref)]
        first += [copy(1 + j, me, (*chip, mc), src=x_ref) for j, chip in enumerate(chips)]
        for cp in first:
            cp.start()
        passed = [copy(4 + j, (*chip, mc), sibling) for j, chip in enumerate(chips)]
        for j, chip in enumerate(chips):
            copy(1 + j, (*chip, mc), me).wait_recv()
            passed[j].start()
        copy(0, sibling, me).wait_recv()
        for j, chip in enumerate(chips):
            copy(4 + j, (*chip, 1 - mc), me).wait_recv()
        for cp in first + passed:
            cp.wait_send()
        mine.wait()

    return pl.pallas_call(
        body, out_shape=jax.ShapeDtypeStruct((N_DEV,) + x.shape, x.dtype), in_specs=[_ANY], out_specs=_ANY,
        scratch_shapes=[pltpu.SemaphoreType.DMA((7,)), pltpu.SemaphoreType.DMA((7,)), pltpu.SemaphoreType.DMA],
        name=name)(x)


_HBM = pl.BlockSpec(memory_space=pltpu.HBM)
_SEM = pl.BlockSpec(memory_space=pltpu.SEMAPHORE)
_EFFECT = pltpu.SideEffectType.DATAFLOW_SIDE_EFFECTING
_TOKEN = jax.ShapeDtypeStruct((SUB, LANE), F32)


def _peers(mx, my, mc):
    for rel in range(1, N_DEV):
        yield (1 - mx if rel & 4 else mx, 1 - my if rel & 2 else my, 1 - mc if rel & 1 else mc)


def _in_hbm(a):
    return pltpu.with_memory_space_constraint(a, pltpu.HBM)


GATHER_SLOTS = (4, 3)


def gather_start(buf, phase, after, name):
    def body(buf_ref, after_ref, send_sem, recv_sem, thru, token):
        mx, my, mc = _me()
        sibling = (mx, my, 1 - mc)
        chips = [(1 - mx, my), (mx, 1 - my), (1 - mx, 1 - my)]
        if phase == 0:
            slot = buf_ref.at[4 * mx + 2 * my + mc]
            copies = [(slot, sibling)] + [(slot, (px, py, mc)) for px, py in chips]
        else:
            copies = [(buf_ref.at[4 * px + 2 * py + mc], sibling) for px, py in chips]
        for slot, peer in copies:
            pltpu.make_async_remote_copy(src_ref=slot, dst_ref=slot, send_sem=send_sem, recv_sem=recv_sem,
                                         device_id=peer, device_id_type=_MESH).start()
        token[...] = jnp.zeros_like(token)

    send_sem, recv_sem, thru, token = pl.pallas_call(
        body, name=name,
        out_shape=(pltpu.SemaphoreType.DMA(()), pltpu.SemaphoreType.DMA(()), pltpu.HBM(buf.shape, buf.dtype), _TOKEN),
        in_specs=[_HBM, _ANY], out_specs=(_SEM, _SEM, _HBM, pl.BlockSpec(memory_space=pltpu.VMEM)),
        input_output_aliases={0: 2},
        compiler_params=pltpu.CompilerParams(has_side_effects=_EFFECT))(_in_hbm(buf), after)
    return (send_sem, recv_sem), thru, token


def exchange_start(src, name):
    def body(src_ref, land_ref, send_sem, recv_sem, src_thru, land_thru, token):
        mx, my, mc = _me()
        me = 4 * mx + 2 * my + mc
        for px, py, pc in _peers(mx, my, mc):
            pltpu.make_async_remote_copy(
                src_ref=src_ref.at[4 * px + 2 * py + pc], dst_ref=land_ref.at[me], send_sem=send_sem,
                recv_sem=recv_sem, device_id=(px, py, pc), device_id_type=_MESH).start()
        token[...] = jnp.zeros_like(token)

    hbm = pltpu.HBM(src.shape, src.dtype)
    send_sem, recv_sem, src_thru, land_thru, token = pl.pallas_call(
        body, name=name,
        out_shape=(pltpu.SemaphoreType.DMA(()), pltpu.SemaphoreType.DMA(()), hbm, hbm, _TOKEN),
        in_specs=[_HBM, _HBM], out_specs=(_SEM, _SEM, _HBM, _HBM, pl.BlockSpec(memory_space=pltpu.VMEM)),
        input_output_aliases={0: 2, 1: 3},
        compiler_params=pltpu.CompilerParams(has_side_effects=_EFFECT))(
            _in_hbm(src), _in_hbm(lax.empty(src.shape, src.dtype)))
    return (send_sem, recv_sem), src_thru, land_thru, token


def transfer_wait(sems, bufs, after, name, slots=N_DEV - 1):
    n = len(bufs)

    def body(*refs):
        seven = refs[0].at[pl.ds(0, slots)]
        cp = pltpu.make_async_remote_copy(src_ref=seven, dst_ref=seven, send_sem=refs[n], recv_sem=refs[n + 1],
                                          device_id=_me(), device_id_type=_MESH)
        cp.wait_send()
        cp.wait_recv()

    outs = pl.pallas_call(
        body, name=name, out_shape=tuple(pltpu.HBM(b.shape, b.dtype) for b in bufs),
        in_specs=[_HBM] * n + [_SEM, _SEM, _ANY], out_specs=tuple([_HBM] * n),
        input_output_aliases={b: b for b in range(n)},
        compiler_params=pltpu.CompilerParams(has_side_effects=_EFFECT))(*bufs, sems[0], sems[1], after)
    return list(outs)


def sum_slabs(x, name, own=None, me=None):
    _, r, c = x.shape
    tr = _pick(r, max(SUB, (1 << 19) // c // SUB * SUB), SUB)
    out_shape = jax.ShapeDtypeStruct((r, c), F32)
    if own is None:
        def body(x_ref, o_ref):
            acc = x_ref[0].astype(F32)
            for s in range(1, N_DEV):
                acc = acc + x_ref[s].astype(F32)
            o_ref[...] = acc

        return pl.pallas_call(
            body, grid=(r // tr,), in_specs=[pl.BlockSpec((N_DEV, tr, c), lambda i: (0, i, 0))],
            out_specs=pl.BlockSpec((tr, c), lambda i: (i, 0)), out_shape=out_shape,
            compiler_params=_cp("parallel"), name=name)(x)

    def body_own(me_ref, x_ref, own_ref, o_ref):
        acc = None
        for s in range(N_DEV):
            val = jnp.where(me_ref[0] == s, own_ref[...], x_ref[s]).astype(F32)
            acc = val if acc is None else acc + val
        o_ref[...] = acc

    return pl.pallas_call(
        body_own, out_shape=out_shape, name=name, compiler_params=_cp("parallel"),
        grid_spec=pltpu.PrefetchScalarGridSpec(
            num_scalar_prefetch=1, grid=(r // tr,),
            in_specs=[pl.BlockSpec((N_DEV, tr, c), lambda i, me_ref: (0, i, 0)),
                      pl.BlockSpec((None, tr, c), lambda i, me_ref: (me_ref[0], i, 0))],
            out_specs=pl.BlockSpec((tr, c), lambda i, me_ref: (i, 0))))(me, x, own)


def adamw(w, g, m, v, name):
    r, c = w.shape
    tr = _pick(r, max(SUB, (1 << 18) // c // SUB * SUB), SUB)
    c1 = 1.0 / (1.0 - ADAM_B1 ** ADAM_STEP)
    c2 = 1.0 / (1.0 - ADAM_B2 ** ADAM_STEP)

    def body(w_ref, g_ref, m_ref, v_ref, d_o, m_o, v_o):
        gg = g_ref[...]
        mn = ADAM_B1 * m_ref[...] + (1.0 - ADAM_B1) * gg
        vn = ADAM_B2 * v_ref[...] + (1.0 - ADAM_B2) * (gg * gg)
        m_o[...] = mn
        v_o[...] = vn
        d_o[...] = -ADAM_LR * ((mn * c1) / (jnp.sqrt(vn * c2) + ADAM_EPS) + ADAM_WD * w_ref[...])

    spec = pl.BlockSpec((tr, c), lambda i: (i, 0))
    sds = jax.ShapeDtypeStruct((r, c), F32)
    return pl.pallas_call(body, grid=(r // tr,), in_specs=[spec] * 4, out_specs=[spec] * 3, out_shape=(sds,) * 3,
                          compiler_params=_cp("parallel"), name=name)(w, g, m, v)


def _adam_update(w, gg, m, v):
    mn = ADAM_B1 * m + (1.0 - ADAM_B1) * gg
    vn = ADAM_B2 * v + (1.0 - ADAM_B2) * (gg * gg)
    c1 = 1.0 / (1.0 - ADAM_B1 ** ADAM_STEP)
    c2 = 1.0 / (1.0 - ADAM_B2 ** ADAM_STEP)
    return -ADAM_LR * ((mn * c1) / (jnp.sqrt(vn * c2) + ADAM_EPS) + ADAM_WD * w), mn, vn


def adamw_layer(layer, w, m, v, prev, name, g=None, land=None, own=None, me=None):
    nl, r, c = w.shape
    tr = _pick(r, max(SUB, (1 << 17) // c // SUB * SUB), SUB)
    from_slabs = g is None
    if prev is None:
        prev = tuple(lax.empty((nl, r, c), F32) for _ in range(4))

    def body(*refs):
        if from_slabs:
            me_ref, land_ref, own_ref, w_ref, m_ref, v_ref = refs[:6]
            gg = None
            for s in range(N_DEV):
                val = jnp.where(me_ref[0] == s, own_ref[...], land_ref[s]).astype(F32)
                gg = val if gg is None else gg + val
        else:
            me_ref, g_ref, w_ref, m_ref, v_ref = refs[:5]
            gg = g_ref[...]
        g_o, d_o, m_o, v_o = refs[-4:]
        g_o[...] = gg
        d_o[...], m_o[...], v_o[...] = _adam_update(w_ref[...], gg, m_ref[...], v_ref[...])

    lay = pl.BlockSpec((None, tr, c), lambda i, me_ref: (layer, i, 0))
    if from_slabs:
        grad_specs = [pl.BlockSpec((N_DEV, tr, c), lambda i, me_ref: (0, i, 0)),
                      pl.BlockSpec((None, tr, c), lambda i, me_ref: (me_ref[0], i, 0))]
        grad_args = [land, own]
    else:
        grad_specs = [pl.BlockSpec((tr, c), lambda i, me_ref: (i, 0))]
        grad_args = [g]
        me = jnp.zeros((1,), jnp.int32)
    n_in = 1 + len(grad_args) + 3
    return pl.pallas_call(
        body, out_shape=tuple(jax.ShapeDtypeStruct((nl, r, c), F32) for _ in range(4)), name=name,
        input_output_aliases={n_in + k: k for k in range(4)}, compiler_params=_cp("parallel"),
        grid_spec=pltpu.PrefetchScalarGridSpec(
            num_scalar_prefetch=1, grid=(r // tr,), in_specs=grad_specs + [lay] * 3 + [_ANY] * 4,
            out_specs=[lay] * 4))(me, *grad_args, w, m, v, *prev)


def _pack(parts):
    flat = jnp.concatenate([a.reshape(-1).astype(F32) for a in parts])
    n = flat.shape[0]
    npad = -n % (PACK_ROWS * LANE)
    return jnp.pad(flat, (0, npad)).reshape(-1, LANE)


def _unpack(buf, shapes, lead=()):
    flat = buf.reshape(lead + (-1,))
    out, off = [], 0
    for s in shapes:
        n = math.prod(s)
        out.append(flat[..., off:off + n].reshape(lead + tuple(s)))
        off += n
    return out


ROPE_THETA = 10000.0

_SMALL = ("norm_mix_pre", "dn_conv_w", "dn_a_log", "dn_dt_bias", "dn_norm_w", "pool_w", "pool_scale", "swa_sinks",
          "norm_mix_post", "norm_ffn_pre", "ffn_conv_w", "ffn_conv_b", "norm_ffn_post")
_BIG = ("w_in", "w_out", "ffn_w_up", "ffn_w_down")
_ORDER = ("norm_mix_pre", "w_in", "dn_conv_w", "dn_a_log", "dn_dt_bias", "dn_norm_w", "pool_w", "pool_scale",
          "swa_sinks", "w_out", "norm_mix_post", "norm_ffn_pre", "ffn_w_up", "ffn_conv_w", "ffn_conv_b",
          "ffn_w_down", "norm_ffn_post")


def _step(x, positions, loss_target, w, m, v):
    nl = w["w_in"].shape[0]
    t, d = x.shape[1], x.shape[2]
    nb = w["ffn_w_up"].shape[2]
    f = nb * N_DEV // 2
    me = 4 * lax.axis_index("x") + 2 * lax.axis_index("y") + lax.axis_index("c")
    x_in, tgt = x[0], loss_target[0]

    inv_freq = 1.0 / (ROPE_THETA ** (jnp.arange(0, HD, 2, dtype=F32) / HD))
    ang = positions[0].astype(F32)[:, None] * inv_freq
    cos, sin = jnp.cos(ang), jnp.sin(ang)
    cos2 = jnp.concatenate([cos, cos], axis=1)
    sin2 = jnp.concatenate([-sin, sin], axis=1)

    conv_shapes = [w["dn_conv_w"].shape, w["ffn_conv_w"].shape]
    gathered_conv = all_gather(_pack([w["dn_conv_w"], w["ffn_conv_w"]]), "ag_conv")
    dn_cw_g, ffn_cw_g = _unpack(gathered_conv, conv_shapes, lead=(N_DEV,))
    dn_cw = jnp.moveaxis(dn_cw_g, 0, 2).reshape(nl, DN_K, _QKV_W)
    ffn_cw = jnp.moveaxis(ffn_cw_g, 0, 1).reshape(nl, 2, N_DEV // 2, 3, nb)
    ffn_cb = w["ffn_conv_b"].reshape(nl, 2, N_DEV // 2, 1, nb)

    def lane_row(vec, off):
        return jnp.zeros((LANE,), F32).at[off:off + vec.shape[0]].set(vec)

    dn_par = jnp.stack([
        jnp.zeros((SUB, LANE), F32).at[0].set(lane_row(w["dn_a_log"][l], DN_H))
        .at[1].set(lane_row(w["dn_dt_bias"][l], DN_H)).at[2].set(w["dn_norm_w"][l]) for l in range(nl)])
    sinks = jnp.stack([lane_row(w["swa_sinks"][l], 0)[None, :] for l in range(nl)])

    def place(shard):
        return lax.dynamic_update_slice(lax.empty((N_DEV,) + shard.shape, shard.dtype), shard[None], (me, 0, 0))

    kinds = ("w_in", "w_out", "ffn_w_up", "ffn_w_down")
    flight = {}
    tag = lambda i: f"{kinds[i % 4]}_{i // 4}"

    def start_first(i, after):
        if i >= 4 * nl:
            return jnp.zeros(_TOKEN.shape, F32)
        l, k = divmod(i, 4)
        shard = _align_in(w[kinds[k]][l]) if k == 0 else w[kinds[k]][l]
        sems, buf, token = gather_start(place(shard.astype(BF16)), 0, after, f"ag_start_{tag(i)}")
        flight[i] = (sems, buf)
        return token

    def start_second(i, after):
        if i >= 4 * nl:
            return jnp.zeros(_TOKEN.shape, F32)
        arrived = transfer_wait(flight[i][0], [flight[i][1]], after, f"ag_wait_{tag(i)}", GATHER_SLOTS[0])[0]
        sems, buf, token = gather_start(arrived, 1, after, f"ag_pass_{tag(i)}")
        flight[i] = (sems, buf)
        return token

    def gathered(l, k, after):
        i = 4 * l + k
        late = start_second(1, after) if i == 1 else None
        got = transfer_wait(flight[i][0], [flight[i][1]], after if late is None else late, f"ag_done_{tag(i)}",
                            GATHER_SLOTS[1])[0]
        first = start_first(i + 3, got)
        if i == 0:
            return got, first[0, 0]
        return got, (start_second(i + 1, first) + first)[0, 0]

    win, wout, wup, wdown = [None] * nl, [None] * nl, [None] * nl, [None] * nl
    row = lambda a, l: a[l][None, :]
    g1, g2, g3, g4 = w["norm_mix_pre"], w["norm_mix_post"], w["norm_ffn_pre"], w["norm_ffn_post"]

    saved = []
    xl = x_in
    passed = start_second(0, start_first(0, gathered_conv))
    h1 = norm_first(xl, row(g1, 0) + (passed + start_first(1, passed) + start_first(2, passed))[0, 0])
    for l in range(nl):
        buf, tk = gathered(l, 0, h1)
        win[l] = buf.reshape(d, PW)
        p = mm_nn(h1, win[l], F32, "mm_in")
        y_dn, states = dn_fwd(p, dn_cw[l], dn_par[l] + tk)
        y_pool = pool_fwd(p, w["pool_w"][l], row(w["pool_scale"], l))
        y_swa = swa_fwd(p, cos2, sin2, sinks[l])
        c = jnp.concatenate([y_dn, y_swa, y_pool], axis=1)
        buf, tk = gathered(l, 1, c)
        wout[l] = _perm_mix_rows(buf.reshape(MIX_W, d))
        mix = mm_nn(c, wout[l], F32, "mm_out")
        x1, h2 = post_pre(xl, mix, row(g2, l) + tk, row(g3, l))
        wup[l], tk = gathered(l, 2, h2)
        u0 = mm_up(h2, wup[l], "mm_up")
        act = glu_fwd(u0, ffn_cw[l], ffn_cb[l] + tk)
        buf, tk = gathered(l, 3, act)
        wdown[l] = buf.reshape(f, d)
        fo = mm_nn(act, wdown[l], F32, "mm_down")
        saved.append(dict(x=xl, h1=h1, p=p, states=states, c=c, mix=mix, x1=x1, h2=h2, u0=u0, act=act, f=fo))
        if l < nl - 1:
            xl, h1 = post_pre(x1, fo, row(g4, l) + tk, row(g1, l + 1))
        else:
            dx, loss_part = post_loss(x1, fo, row(g4, l) + tk, tgt)

    small_g = [dict() for _ in range(nl)]
    pending = {name: [None] * nl for name in _BIG}

    def exchange(name, l, dw):
        sems, src, land, token = exchange_start(dw, f"xch_start_{name}_{l}")
        pending[name][l] = (sems, src, land)
        return token[0, 0]

    df, small_g[nl - 1]["norm_ffn_post"] = bwd_norms(dx, post=(saved[-1]["f"], row(g4, nl - 1)))
    for l in range(nl - 1, -1, -1):
        s, sg = saved[l], small_g[l]
        dact = mm_nt(df, wdown[l], F32, "mm_down_d")
        tk = exchange("ffn_w_down", l, mm_tn(s["act"], df, BF16, "mm_down_w").reshape(N_DEV, f // N_DEV, d))
        du0, dcw = glu_bwd(dact, s["u0"], ffn_cw[l], ffn_cb[l])
        sg["ffn_conv"] = dcw
        dh2 = mm_up_dgrad(du0, wup[l], "mm_up_d")
        tk = tk + exchange("ffn_w_up", l, mm_up_wgrad(s["h2"], du0, "mm_up_w"))
        dx1, sg["norm_ffn_pre"], dmix, sg["norm_mix_post"] = bwd_norms(
            dx, pre=(dh2, s["x1"], row(g3, l) + tk), post=(s["mix"], row(g2, l)))
        dc = mm_nt(dmix, wout[l], F32, "mm_out_d")
        tk = exchange("w_out", l, _unperm_mix_rows(mm_tn(s["c"], dmix, BF16, "mm_out_w"))
                      .reshape(N_DEV, MIX_W // N_DEV, d))
        dqkv, dz, dgate, sg["dn_conv_w"], sg["dn_par"] = dn_bwd(s["p"], dc, s["states"], dn_cw[l], dn_par[l])
        dpool, sg["pool_w"], sg["pool_scale"] = pool_bwd(s["p"], dc, w["pool_w"][l], row(w["pool_scale"], l))
        dsq, dsk, dsv, sg["swa_sinks"] = swa_bwd(s["p"], dc, cos2, sin2, sinks[l])
        dp = jnp.concatenate([dqkv, dz, dsq, dsk, dsv, dpool, dgate], axis=1)
        dh1 = mm_nt(dp, win[l], F32, "mm_in_d")
        tk = tk + exchange("w_in", l, mm_tn(s["h1"], dp, BF16, "mm_in_w").reshape(N_DEV, d // N_DEV, PW))
        if l > 0:
            dx, sg["norm_mix_pre"], df, small_g[l - 1]["norm_ffn_post"] = bwd_norms(
                dx1, pre=(dh1, s["x"], row(g1, l) + tk), post=(saved[l - 1]["f"], row(g4, l - 1)))
        else:
            grad_x, sg["norm_mix_pre"] = bwd_norms(dx1, pre=(dh1, s["x"], row(g1, 0) + tk))

    me_arr = jnp.reshape(me, (1,)).astype(jnp.int32)
    big = {name: None for name in _BIG}
    w_in_sums = [None] * nl

    def finish(name, l, after):
        sems, src, land = pending[name][l]
        src, land = transfer_wait(sems, [src, land], after, f"xch_wait_{name}_{l}")
        if name == "w_in":
            w_in_sums[l] = sum_slabs(land, "sum_w_in", own=src, me=me_arr)
            return w_in_sums[l]
        big[name] = adamw_layer(l, w[name], m[name], v[name], big[name], "adamw_" + name, land=land, own=src,
                                me=me_arr)
        return big[name][1]

    after = grad_x
    for l in range(nl - 1, 0, -1):
        for name in ("ffn_w_down", "ffn_w_up", "w_out", "w_in"):
            after = finish(name, l, after)

    keys = ("norm_mix_pre", "norm_mix_post", "norm_ffn_pre", "norm_ffn_post", "dn_conv_w", "dn_par", "pool_w",
            "pool_scale", "swa_sinks", "ffn_conv")
    grads = {}
    parts = [small_g[l][k] for l in range(nl) for k in keys] + [loss_part]
    shapes = [a.shape for a in parts]
    ordered = parts + [after[0:SUB, 0:LANE]]
    summed = sum_slabs(all_gather(_pack(ordered), "ag_small"), "sum_small")
    vals = _unpack(summed, shapes)
    loss = vals[-1][0, 0]
    sm = [dict(zip(keys, vals[l * len(keys):(l + 1) * len(keys)])) for l in range(nl)]
    st = lambda fn: jnp.stack([fn(sm[l]) for l in range(nl)])
    for k in ("norm_mix_pre", "norm_mix_post", "norm_ffn_pre", "norm_ffn_post"):
        grads[k] = st(lambda q: q[k][0])
    grads["dn_conv_w"] = lax.dynamic_slice_in_dim(st(lambda q: q["dn_conv_w"][0:DN_K]), me * (_QKV_W // N_DEV),
                                                  _QKV_W // N_DEV, axis=2)
    grads["dn_a_log"] = st(lambda q: q["dn_par"][0, DN_H:2 * DN_H])
    grads["dn_dt_bias"] = st(lambda q: q["dn_par"][1, DN_H:2 * DN_H])
    grads["dn_norm_w"] = st(lambda q: q["dn_par"][2])
    grads["pool_w"] = st(lambda q: q["pool_w"])
    grads["pool_scale"] = st(lambda q: q["pool_scale"][0])
    grads["swa_sinks"] = st(lambda q: q["swa_sinks"][0, 0:SWA_H])
    conv_all = st(lambda q: q["ffn_conv"].reshape(N_DEV, SUB, nb))
    grads["ffn_conv_w"] = lax.dynamic_index_in_dim(conv_all, me, axis=1, keepdims=False)[:, 0:3, :]
    grads["ffn_conv_b"] = conv_all[:, :, 3, :].reshape(nl, 2 * f)

    delta, new_m, new_v = {}, {}, {}
    shapes = [w[k].shape for k in _SMALL]
    pk = lambda tree: _pack([tree[k] for k in _SMALL])
    outs = adamw(pk(w), pk(grads), pk(m), pk(v), "adamw_small")
    for tree, buf in zip((delta, new_m, new_v), outs):
        for k, a in zip(_SMALL, _unpack(buf, shapes)):
            tree[k] = a
    after = outs[0]
    for name in ("ffn_w_down", "ffn_w_up", "w_out"):
        after = finish(name, 0, after)
        grads[name], delta[name], new_m[name], new_v[name] = big[name]
    finish("w_in", 0, after)
    grads["w_in"] = _unalign_in(jnp.stack(w_in_sums))
    flat = lambda a: a.reshape(-1, IN_W)
    delta["w_in"], new_m["w_in"], new_v["w_in"] = (
        o.reshape(w["w_in"].shape) for o in adamw(flat(w["w_in"]), flat(grads["w_in"]), flat(m["w_in"]),
                                                  flat(v["w_in"]), "adamw_w_in"))

    return (loss, grad_x[None], *[grads[k] for k in _ORDER], *[delta[k] for k in _ORDER],
            *[new_m[k] for k in _ORDER], *[new_v[k] for k in _ORDER])


def kernel(x, positions, norm_mix_pre, w_in, dn_conv_w, dn_a_log, dn_dt_bias, dn_norm_w, pool_w, pool_scale, swa_sinks, w_out, norm_mix_post, norm_ffn_pre, ffn_w_up, ffn_conv_w, ffn_conv_b, ffn_w_down, norm_ffn_post, loss_target, m_norm_mix_pre, m_w_in, m_dn_conv_w, m_dn_a_log, m_dn_dt_bias, m_dn_norm_w, m_pool_w, m_pool_scale, m_swa_sinks, m_w_out, m_norm_mix_post, m_norm_ffn_pre, m_ffn_w_up, m_ffn_conv_w, m_ffn_conv_b, m_ffn_w_down, m_norm_ffn_post, v_norm_mix_pre, v_w_in, v_dn_conv_w, v_dn_a_log, v_dn_dt_bias, v_dn_norm_w, v_pool_w, v_pool_scale, v_swa_sinks, v_w_out, v_norm_mix_post, v_norm_ffn_pre, v_ffn_w_up, v_ffn_conv_w, v_ffn_conv_b, v_ffn_w_down, v_norm_ffn_post):
    args = locals()
    w = {k: args[k] for k in _ORDER}
    m = {k: args["m_" + k] for k in _ORDER}
    v = {k: args["v_" + k] for k in _ORDER}
    return _step(x, positions, loss_target, w, m, v)
```

```python
import functools
import math

import jax
import jax.numpy as jnp
from jax import lax
from jax.experimental import pallas as pl
from jax.experimental.pallas import tpu as pltpu

F32 = jnp.float32
BF16 = jnp.bfloat16
MXU_DT = jnp.bfloat16
HI = lax.Precision.HIGHEST

N_DEV = 8
LANE = 128
SUB = 8
VMEM_LIMIT = 56 * 1024 * 1024
ROW_TILE = 512
NORM_TILE = 256
MM_TM, MM_TN, MM_TK = 512, 1664, 2816
MM_TN_NT = 2048
PACK_ROWS = 512

HD = 128
DN_H, DN_W, DN_K, CH = 6, 768, 4, 64
POOL_G = 4
SWA_H, SWA_KV, SWA_G, SWA_BLK = 6, 2, 3, 128
EPS = 1e-6
SCALE = HD ** -0.5
NEG = -1e30

O_QKV, O_Z, O_SQ, O_SK, O_SV, O_POOL, O_GATE, PW = 0, 2304, 3072, 3840, 4096, 4352, 4864, 4992
IN_W = 4876
MIX_W = 2048

ADAM_LR, ADAM_B1, ADAM_B2, ADAM_EPS, ADAM_WD, ADAM_STEP = 0.001, 0.9, 0.999, 1e-08, 0.01, 10


def _pick(n, cap, mult=LANE):
    best = None
    for d in range(mult, min(n, cap) + 1, mult):
        if n % d == 0:
            best = d
    return best if best is not None else n


def _cp(*sem):
    return pltpu.CompilerParams(dimension_semantics=sem, vmem_limit_bytes=VMEM_LIMIT)


def _dot(a, b, dims):
    return lax.dot_general(a.astype(MXU_DT), b.astype(MXU_DT), dims, preferred_element_type=F32)


_NN = (((1,), (0,)), ((), ()))
_NT = (((1,), (1,)), ((), ()))
_TN = (((0,), (0,)), ((), ()))


def _mm(a, b):
    return _dot(a, b, _NN)


def _mm_nt(a, b):
    return _dot(a, b, _NT)


def _mm_tn(a, b):
    return _dot(a, b, _TN)


def _mm_hi(a, b, dims=_NN):
    return lax.dot_general(a, b, dims, precision=HI, preferred_element_type=F32)


def _sigmoid(x):
    return jax.nn.sigmoid(x)


def _softplus(x):
    return jnp.maximum(x, 0.0) + jnp.log(1.0 + jnp.exp(-jnp.abs(x)))


def _align_in(w):
    pad = jnp.zeros(w.shape[:-1] + (PW - IN_W,), w.dtype)
    return jnp.concatenate([w[..., 0:3072], w[..., 3596:4364], w[..., 4364:4620], w[..., 4620:4876],
                            w[..., 3084:3596], w[..., 3072:3084], pad], axis=-1)


def _unalign_in(g):
    return jnp.concatenate([g[..., 0:3072], g[..., O_GATE:O_GATE + 12], g[..., O_POOL:O_POOL + 512],
                            g[..., O_SQ:O_SQ + 768], g[..., O_SK:O_SK + 256], g[..., O_SV:O_SV + 256]], axis=-1)


def _perm_mix_rows(w):
    return jnp.concatenate([w[0:768], w[1280:2048], w[768:1280]], axis=0)


def _unperm_mix_rows(w):
    return jnp.concatenate([w[0:768], w[1536:2048], w[768:1536]], axis=0)


def _mm_call(name, a, b, out_shape, grid, a_spec, b_spec, o_spec, dims, acc_shape):
    nk = grid[2]
    if nk == 1:
        def body_once(a_ref, b_ref, o_ref):
            o_ref[...] = _dot(a_ref[...], b_ref[...], dims).astype(o_ref.dtype)

        return pl.pallas_call(
            body_once, grid=grid, in_specs=[a_spec, b_spec], out_specs=o_spec, out_shape=out_shape,
            compiler_params=_cp("parallel", "parallel", "arbitrary"), name=name)(a, b)

    def body(a_ref, b_ref, o_ref, acc_ref):
        k = pl.program_id(2)

        @pl.when(k == 0)
        def _():
            acc_ref[...] = jnp.zeros_like(acc_ref)

        acc_ref[...] += _dot(a_ref[...], b_ref[...], dims)

        @pl.when(k == nk - 1)
        def _():
            o_ref[...] = acc_ref[...].astype(o_ref.dtype)

    return pl.pallas_call(
        body, grid=grid, in_specs=[a_spec, b_spec], out_specs=o_spec, out_shape=out_shape,
        scratch_shapes=[pltpu.VMEM(acc_shape, F32)],
        compiler_params=_cp("parallel", "parallel", "arbitrary"), name=name)(a, b)


def mm_nn(a, b, out_dtype, name):
    (m, k), n = a.shape, b.shape[1]
    tm, tn, tk = _pick(m, MM_TM, SUB), _pick(n, MM_TN), _pick(k, MM_TK)
    return _mm_call(name, a, b, jax.ShapeDtypeStruct((m, n), out_dtype), (m // tm, n // tn, k // tk),
                    pl.BlockSpec((tm, tk), lambda i, j, kk: (i, kk)),
                    pl.BlockSpec((tk, tn), lambda i, j, kk: (kk, j)),
                    pl.BlockSpec((tm, tn), lambda i, j, kk: (i, j)), _NN, (tm, tn))


def mm_nt(a, b, out_dtype, name):
    (m, k), n = a.shape, b.shape[0]
    tm, tn, tk = _pick(m, MM_TM, SUB), _pick(n, MM_TN_NT), _pick(k, MM_TK)
    return _mm_call(name, a, b, jax.ShapeDtypeStruct((m, n), out_dtype), (m // tm, n // tn, k // tk),
                    pl.BlockSpec((tm, tk), lambda i, j, kk: (i, kk)),
                    pl.BlockSpec((tn, tk), lambda i, j, kk: (j, kk)),
                    pl.BlockSpec((tm, tn), lambda i, j, kk: (i, j)), _NT, (tm, tn))


def mm_tn(a, b, out_dtype, name):
    (k, m), n = a.shape, b.shape[1]
    tm, tn, tk = _pick(m, MM_TM), _pick(n, MM_TN), _pick(k, MM_TK, SUB)
    return _mm_call(name, a, b, jax.ShapeDtypeStruct((m, n), out_dtype), (m // tm, n // tn, k // tk),
                    pl.BlockSpec((tk, tm), lambda i, j, kk: (kk, i)),
                    pl.BlockSpec((tk, tn), lambda i, j, kk: (kk, j)),
                    pl.BlockSpec((tm, tn), lambda i, j, kk: (i, j)), _TN, (tm, tn))


def mm_up(h, wblk, name):
    (t, d), (nblk, _, nb) = h.shape, wblk.shape
    tm, tk = _pick(t, MM_TM, SUB), _pick(d, MM_TK)
    hb = nblk // 2
    return _mm_call(name, h, wblk, jax.ShapeDtypeStruct((2, t, hb * nb), F32), (t // tm, nblk, d // tk),
                    pl.BlockSpec((tm, tk), lambda i, j, kk: (i, kk)),
                    pl.BlockSpec((None, tk, nb), lambda i, j, kk: (j, kk, 0)),
                    pl.BlockSpec((None, tm, nb), lambda i, j, kk: (j // hb, i, j % hb)), _NN, (tm, nb))


def mm_up_dgrad(du0, wblk, name):
    (_, t, _), (nblk, d, nb) = du0.shape, wblk.shape
    tm, tn = _pick(t, MM_TM, SUB), _pick(d, MM_TN_NT)
    hb = nblk // 2
    return _mm_call(name, du0, wblk, jax.ShapeDtypeStruct((t, d), F32), (t // tm, d // tn, nblk),
                    pl.BlockSpec((None, tm, nb), lambda i, j, kk: (kk // hb, i, kk % hb)),
                    pl.BlockSpec((None, tn, nb), lambda i, j, kk: (kk, j, 0)),
                    pl.BlockSpec((tm, tn), lambda i, j, kk: (i, j)), _NT, (tm, tn))


def mm_up_wgrad(h, du0, name):
    (t, d), (_, _, f) = h.shape, du0.shape
    nb = f // (N_DEV // 2)
    hb = N_DEV // 2
    tm, tk = _pick(d, MM_TM), _pick(t, MM_TK, SUB)
    return _mm_call(name, h, du0, jax.ShapeDtypeStruct((N_DEV, d, nb), BF16), (d // tm, N_DEV, t // tk),
                    pl.BlockSpec((tk, tm), lambda i, j, kk: (kk, i)),
                    pl.BlockSpec((None, tk, nb), lambda i, j, kk: (j // hb, kk, j % hb)),
                    pl.BlockSpec((None, tm, nb), lambda i, j, kk: (j, i, 0)), _TN, (tm, nb))


def _rms(x, w):
    r = lax.rsqrt(jnp.mean(x * x, axis=-1, keepdims=True) + EPS)
    return x * r * w


def _rms_bwd(dy, x, w):
    r = lax.rsqrt(jnp.mean(x * x, axis=-1, keepdims=True) + EPS)
    xh = x * r
    dxh = dy * w
    dx = r * (dxh - xh * jnp.mean(dxh * xh, axis=-1, keepdims=True))
    return dx, jnp.sum(dy * xh, axis=0, keepdims=True)


def _row_spec(tb, d):
    return pl.BlockSpec((tb, d), lambda i: (i, 0))


def _fix_spec(r, d):
    return pl.BlockSpec((r, d), lambda i: (0, 0))


def norm_first(x, w):
    t, d = x.shape
    tb = _pick(t, NORM_TILE, SUB)

    def body(x_ref, w_ref, h_ref):
        h_ref[...] = _rms(x_ref[...], w_ref[...]).astype(h_ref.dtype)

    return pl.pallas_call(body, grid=(t // tb,), in_specs=[_row_spec(tb, d), _fix_spec(1, d)],
                          out_specs=_row_spec(tb, d), out_shape=jax.ShapeDtypeStruct((t, d), BF16),
                          compiler_params=_cp("parallel"), name="norm_first")(x, w)


def post_pre(x, y, w_post, w_pre):
    t, d = x.shape
    tb = _pick(t, NORM_TILE, SUB)

    def body(x_ref, y_ref, wp_ref, wq_ref, xn_ref, h_ref):
        xn = x_ref[...] + _rms(y_ref[...], wp_ref[...])
        xn_ref[...] = xn
        h_ref[...] = _rms(xn, wq_ref[...]).astype(h_ref.dtype)

    return pl.pallas_call(
        body, grid=(t // tb,),
        in_specs=[_row_spec(tb, d), _row_spec(tb, d), _fix_spec(1, d), _fix_spec(1, d)],
        out_specs=[_row_spec(tb, d), _row_spec(tb, d)],
        out_shape=(jax.ShapeDtypeStruct((t, d), F32), jax.ShapeDtypeStruct((t, d), BF16)),
        compiler_params=_cp("parallel"), name="post_pre")(x, y, w_post, w_pre)


def post_loss(x, y, w_post, target):
    t, d = x.shape
    tb = _pick(t, NORM_TILE, SUB)

    def body(x_ref, y_ref, wp_ref, t_ref, g_ref, l_ref):
        err = x_ref[...] + _rms(y_ref[...], wp_ref[...]) - t_ref[...]
        g_ref[...] = err * (1.0 / d)

        @pl.when(pl.program_id(0) == 0)
        def _():
            l_ref[...] = jnp.zeros_like(l_ref)

        part = 0.5 * jnp.sum(jnp.mean(err * err, axis=-1, keepdims=True), axis=0, keepdims=True)
        l_ref[...] += jnp.broadcast_to(part, l_ref.shape)

    return pl.pallas_call(
        body, grid=(t // tb,),
        in_specs=[_row_spec(tb, d), _row_spec(tb, d), _fix_spec(1, d), _row_spec(tb, d)],
        out_specs=[_row_spec(tb, d), _fix_spec(1, LANE)],
        out_shape=(jax.ShapeDtypeStruct((t, d), F32), jax.ShapeDtypeStruct((1, LANE), F32)),
        compiler_params=_cp("arbitrary"), name="post_loss")(x, y, w_post, target)


def bwd_norms(dx_in, *, pre=None, post=None):
    t, d = dx_in.shape
    tb = _pick(t, NORM_TILE, SUB)
    has_pre, has_post = pre is not None, post is not None

    def body(*refs):
        refs = list(refs)
        dxi = refs.pop(0)
        if has_pre:
            dh, x, wq = refs.pop(0), refs.pop(0), refs.pop(0)
        if has_post:
            y, wp = refs.pop(0), refs.pop(0)
        first = pl.program_id(0) == 0
        dx = dxi[...]
        if has_pre:
            dxo, dwq = refs.pop(0), refs.pop(0)
            g, dw = _rms_bwd(dh[...], x[...], wq[...])
            dx = dx + g
            dxo[...] = dx

            @pl.when(first)
            def _():
                dwq[...] = jnp.zeros_like(dwq)

            dwq[...] += dw
        if has_post:
            dyo, dwp = refs.pop(0), refs.pop(0)
            g, dw = _rms_bwd(dx, y[...], wp[...])
            dyo[...] = g.astype(dyo.dtype)

            @pl.when(first)
            def _():
                dwp[...] = jnp.zeros_like(dwp)

            dwp[...] += dw

    ins, in_specs, outs, out_specs = [dx_in], [_row_spec(tb, d)], [], []
    if has_pre:
        ins += list(pre)
        in_specs += [_row_spec(tb, d), _row_spec(tb, d), _fix_spec(1, d)]
        outs += [jax.ShapeDtypeStruct((t, d), F32), jax.ShapeDtypeStruct((1, d), F32)]
        out_specs += [_row_spec(tb, d), _fix_spec(1, d)]
    if has_post:
        ins += list(post)
        in_specs += [_row_spec(tb, d), _fix_spec(1, d)]
        outs += [jax.ShapeDtypeStruct((t, d), BF16), jax.ShapeDtypeStruct((1, d), F32)]
        out_specs += [_row_spec(tb, d), _fix_spec(1, d)]
    name = "bwd_norms" + ("_pre" if has_pre else "") + ("_post" if has_post else "")
    return pl.pallas_call(body, grid=(t // tb,), in_specs=in_specs, out_specs=out_specs, out_shape=tuple(outs),
                          compiler_params=_cp("arbitrary"), name=name)(*ins)


GLU_ROWS = 128


def _ffn_conv_blk(blk, cw, cb):
    r = blk.shape[1] - SUB
    x0, x1, x2 = blk[:, 6:6 + r], blk[:, 7:7 + r], blk[:, 8:8 + r]
    return x0, x1, x2, cw[:, 0:1, :] * x0 + cw[:, 1:2, :] * x1 + cw[:, 2:3, :] * x2 + cb


def _glu_specs(tb, nb, hpb, row_of):
    tile = pl.BlockSpec((2, tb, nb), lambda j, i: (0, row_of(i), j))
    halo = pl.BlockSpec((2, SUB, nb), lambda j, i: (0, jnp.maximum(row_of(i) * hpb - 1, 0), j))
    cw = pl.BlockSpec((2, None, 3, nb), lambda j, i: (0, j, 0, 0))
    cb = pl.BlockSpec((2, None, 1, nb), lambda j, i: (0, j, 0, 0))
    return tile, halo, cw, cb


def glu_fwd(u0, cw, cb):
    _, t, f = u0.shape
    nb = cw.shape[-1]
    tb = _pick(t, ROW_TILE, SUB)
    nt, hpb = t // tb, tb // SUB

    def body(u, h, cwr, cbr, o_ref, e):
        i = pl.program_id(1)
        e[:, 0:SUB, :] = jnp.where(i > 0, h[...], 0.0)
        e[:, SUB:, :] = u[...]
        w, bias = cwr[...], cbr[...]

        def rows(g, carry):
            s = pl.multiple_of(g * GLU_ROWS, GLU_ROWS)
            for lg in range(nb // LANE):
                ls = slice(lg * LANE, (lg + 1) * LANE)
                ab = _ffn_conv_blk(e[:, pl.ds(s, GLU_ROWS + SUB), ls], w[:, :, ls], bias[:, :, ls])[3]
                a, b = ab[0], ab[1]
                o_ref[pl.ds(s, GLU_ROWS), ls] = (a * _sigmoid(a) * b).astype(o_ref.dtype)
            return carry

        lax.fori_loop(0, tb // GLU_ROWS, rows, 0)

    return pl.pallas_call(
        body, grid=(f // nb, nt), in_specs=list(_glu_specs(tb, nb, hpb, lambda i: i)),
        out_specs=pl.BlockSpec((tb, nb), lambda j, i: (i, j)),
        out_shape=jax.ShapeDtypeStruct((t, f), BF16),
        scratch_shapes=[pltpu.VMEM((2, tb + SUB, nb), F32)],
        compiler_params=_cp("parallel", "arbitrary"), name="glu_fwd")(u0, u0, cw, cb)


def glu_bwd(dact, u0, cw, cb):
    _, t, f = u0.shape
    nb = cw.shape[-1]
    tb = _pick(t, ROW_TILE, SUB)
    nt, hpb = t // tb, tb // SUB

    def body(d_ref, u, h, cwr, cbr, du_o, dc_o, e, x2):
        i = pl.program_id(1)
        r = nt - 1 - i
        e[:, 0:SUB, :] = jnp.where(r > 0, h[...], 0.0)
        e[:, SUB:, :] = u[...]
        w, bias = cwr[...], cbr[...]

        @pl.when(i == 0)
        def _():
            dc_o[...] = jnp.zeros_like(dc_o)
            x2[:, tb:, :] = jnp.zeros((2, SUB, nb), F32)

        fold = lambda v: jnp.sum(v.reshape(2, GLU_ROWS // SUB, SUB, LANE), axis=1)
        for lg in range(nb // LANE):
            ls = slice(lg * LANE, (lg + 1) * LANE)
            wl, bl = w[:, :, ls], bias[:, :, ls]

            def grads(g, acc):
                s = pl.multiple_of(g * GLU_ROWS, GLU_ROWS)
                x0, x1, xc, ab = _ffn_conv_blk(e[:, pl.ds(s, GLU_ROWS + SUB), ls], wl, bl)
                a, b = ab[0], ab[1]
                sa = _sigmoid(a)
                d = d_ref[pl.ds(s, GLU_ROWS), ls]
                x2[0, pl.ds(s, GLU_ROWS), ls] = d * b * (sa * (1.0 + a * (1.0 - sa)))
                x2[1, pl.ds(s, GLU_ROWS), ls] = d * (a * sa)
                du = x2[:, pl.ds(s, GLU_ROWS), ls]
                return (acc[0] + fold(du * x0), acc[1] + fold(du * x1), acc[2] + fold(du * xc), acc[3] + fold(du))

            zero = jnp.zeros((2, SUB, LANE), F32)
            acc = lax.fori_loop(0, tb // GLU_ROWS, grads, (zero, zero, zero, zero))
            for k in range(4):
                dc_o[:, k:k + 1, ls] += jnp.sum(acc[k], axis=1, keepdims=True)

            def transposed_conv(g, carry):
                s = pl.multiple_of(g * GLU_ROWS, GLU_ROWS)
                blk = x2[:, pl.ds(s, GLU_ROWS + SUB), ls]
                du_o[:, pl.ds(s, GLU_ROWS), ls] = (
                    wl[:, 2:3, :] * blk[:, 0:GLU_ROWS] + wl[:, 1:2, :] * blk[:, 1:1 + GLU_ROWS]
                    + wl[:, 0:1, :] * blk[:, 2:2 + GLU_ROWS]).astype(du_o.dtype)
                return carry

            lax.fori_loop(0, tb // GLU_ROWS, transposed_conv, 0)
        x2[:, tb:, :] = x2[:, 0:SUB, :]

    rev = lambda i: nt - 1 - i
    return pl.pallas_call(
        body, grid=(f // nb, nt),
        in_specs=[pl.BlockSpec((tb, nb), lambda j, i: (rev(i), j))] + list(_glu_specs(tb, nb, hpb, rev)),
        out_specs=[pl.BlockSpec((2, tb, nb), lambda j, i: (0, rev(i), j)),
                   pl.BlockSpec((2, None, SUB, nb), lambda j, i: (0, j, 0, 0))],
        out_shape=(jax.ShapeDtypeStruct((2, t, f), BF16), jax.ShapeDtypeStruct((2, f // nb, SUB, nb), F32)),
        scratch_shapes=[pltpu.VMEM((2, tb + SUB, nb), F32), pltpu.VMEM((2, tb + SUB, nb), F32)],
        compiler_params=_cp("arbitrary", "arbitrary"), name="glu_bwd")(dact, u0, u0, cw, cb)


POOL_HALO = 16
_PCOL = O_POOL // LANE
_CPOOL = 1536 // LANE


def _pool_sel(g, v2, v4, v8, v16):
    return jnp.where(g == 0, v2, jnp.where(g == 1, v4, jnp.where(g == 2, v8, v16)))


def _pool_cnt(g, t0, n):
    win = _pool_sel(g, 2, 4, 8, 16)
    tpos = t0 + lax.broadcasted_iota(jnp.int32, (n, 1), 0)
    return jnp.minimum(tpos + 1, win).astype(F32)


def _pool_core(e, g, t0, tb):
    s2 = e + pltpu.roll(e, 1, 0)
    s4 = s2 + pltpu.roll(s2, 2, 0)
    s8 = s4 + pltpu.roll(s4, 4, 0)
    s16 = s8 + pltpu.roll(s8, 8, 0)
    sw = _pool_sel(g, s2, s4, s8, s16)[POOL_HALO:]
    return sw / _pool_cnt(g, t0, tb) - e[POOL_HALO:]


def pool_fwd(p, pool_w, pool_scale):
    t = p.shape[0]
    tb = _pick(t, ROW_TILE, POOL_HALO)
    nt, hpb = t // tb, tb // POOL_HALO

    def body(x_ref, h_ref, w_ref, s_ref, o_ref):
        i, g = pl.program_id(0), pl.program_id(1)
        e = jnp.concatenate([jnp.where(i > 0, h_ref[...], 0.0), x_ref[...]], axis=0)
        yy = _pool_core(e, g, i * tb, tb)
        o_ref[...] = (_mm(yy, w_ref[...]) * s_ref[...]).astype(o_ref.dtype)

    return pl.pallas_call(
        body, grid=(nt, POOL_G),
        in_specs=[pl.BlockSpec((tb, LANE), lambda i, g: (i, _PCOL + g)),
                  pl.BlockSpec((POOL_HALO, LANE), lambda i, g: (jnp.maximum(i * hpb - 1, 0), _PCOL + g)),
                  pl.BlockSpec((None, LANE, LANE), lambda i, g: (g, 0, 0)),
                  pl.BlockSpec((1, LANE), lambda i, g: (0, g))],
        out_specs=pl.BlockSpec((tb, LANE), lambda i, g: (i, g)),
        out_shape=jax.ShapeDtypeStruct((t, POOL_G * LANE), BF16),
        compiler_params=_cp("parallel", "parallel"), name="pool_fwd")(p, p, pool_w, pool_scale)


def pool_bwd(p, dc, pool_w, pool_scale):
    t = p.shape[0]
    tb = _pick(t, ROW_TILE, POOL_HALO)
    nt, hpb = t // tb, tb // POOL_HALO
    n = tb + POOL_HALO

    def body(x_ref, h_ref, dy_ref, dn_ref, w_ref, s_ref, dx_o, dw_o, ds_o):
        g, i = pl.program_id(0), pl.program_id(1)
        e = jnp.concatenate([jnp.where(i > 0, h_ref[...], 0.0), x_ref[...]], axis=0)
        yy = _pool_core(e, g, i * tb, tb)
        w, sc, dy = w_ref[...], s_ref[...], dy_ref[...]

        @pl.when(i == 0)
        def _():
            dw_o[...] = jnp.zeros_like(dw_o)
            ds_o[...] = jnp.zeros_like(ds_o)

        ds_o[...] += jnp.sum(dy * _mm(yy, w), axis=0, keepdims=True)
        dw_o[...] += _mm_tn(yy, dy * sc)
        dye = jnp.concatenate([dy, jnp.where(i < nt - 1, dn_ref[...], 0.0)], axis=0) * sc
        dyy = _mm_nt(dye, w)
        z = dyy / _pool_cnt(g, i * tb, n)
        r2 = z + pltpu.roll(z, n - 1, 0)
        r4 = r2 + pltpu.roll(r2, n - 2, 0)
        r8 = r4 + pltpu.roll(r4, n - 4, 0)
        r16 = r8 + pltpu.roll(r8, n - 8, 0)
        dx_o[...] = (_pool_sel(g, r2, r4, r8, r16)[:tb] - dyy[:tb]).astype(dx_o.dtype)

    last = t // POOL_HALO - 1
    return pl.pallas_call(
        body, grid=(POOL_G, nt),
        in_specs=[pl.BlockSpec((tb, LANE), lambda g, i: (i, _PCOL + g)),
                  pl.BlockSpec((POOL_HALO, LANE), lambda g, i: (jnp.maximum(i * hpb - 1, 0), _PCOL + g)),
                  pl.BlockSpec((tb, LANE), lambda g, i: (i, _CPOOL + g)),
                  pl.BlockSpec((POOL_HALO, LANE), lambda g, i: (jnp.minimum((i + 1) * hpb, last), _CPOOL + g)),
                  pl.BlockSpec((None, LANE, LANE), lambda g, i: (g, 0, 0)),
                  pl.BlockSpec((1, LANE), lambda g, i: (0, g))],
        out_specs=[pl.BlockSpec((tb, LANE), lambda g, i: (i, g)),
                   pl.BlockSpec((None, LANE, LANE), lambda g, i: (g, 0, 0)),
                   pl.BlockSpec((1, LANE), lambda g, i: (0, g))],
        out_shape=(jax.ShapeDtypeStruct((t, POOL_G * LANE), BF16),
                   jax.ShapeDtypeStruct((POOL_G, LANE, LANE), F32),
                   jax.ShapeDtypeStruct((1, POOL_G * LANE), F32)),
        compiler_params=_cp("arbitrary", "arbitrary"), name="pool_bwd")(p, p, dc, dc, pool_w, pool_scale)


_QCOL, _KCOL, _VCOL = O_SQ // 768, O_SK // 256, O_SV // 256
_GQ = SWA_G * SWA_BLK


def _rope(x, c2, s2):
    return x * c2 + pltpu.roll(x, HD // 2, 1) * s2


def _rope_bwd(d, c2, s2):
    return d * c2 + pltpu.roll(d * s2, HD // 2, 1)


def _hs(x, h):
    return x[:, h * HD:(h + 1) * HD]


def _swa_group(q, kc, kp, vc, vp, c2c, s2c, c2p, s2p, sinks, h, blk):
    kcat = jnp.concatenate([_rope(_hs(kp, h), c2p, s2p), _rope(_hs(kc, h), c2c, s2c)], axis=0)
    vcat = jnp.concatenate([_hs(vp, h), _hs(vc, h)], axis=0)
    qs = jnp.concatenate([_rope(_hs(q, SWA_G * h + g), c2c, s2c) for g in range(SWA_G)], axis=0)
    s = _mm_nt(qs, kcat) * SCALE
    yield
    ii = lax.broadcasted_iota(jnp.int32, (_GQ, 2 * SWA_BLK), 0) & (SWA_BLK - 1)
    jj = lax.broadcasted_iota(jnp.int32, (_GQ, 2 * SWA_BLK), 1)
    lo = jnp.where(blk > 0, 0, SWA_BLK)
    s = jnp.where((jj > ii) & (jj <= ii + SWA_BLK) & (jj >= lo), s, NEG)
    sink = jnp.concatenate(
        [jnp.broadcast_to(sinks[:, SWA_G * h + g:SWA_G * h + g + 1], (SWA_BLK, 1)) for g in range(SWA_G)], axis=0)
    m = jnp.maximum(jnp.max(s, axis=1, keepdims=True), sink)
    p = jnp.exp(s - m)
    ps = jnp.exp(sink - m)
    l = jnp.sum(p, axis=1, keepdims=True) + ps
    return qs, kcat, vcat, p, ps, l


def _swa_specs(blk_of):
    cur = lambda w, c: pl.BlockSpec((SWA_BLK, w), lambda n: (blk_of(n), c))
    prev = lambda w, c: pl.BlockSpec((SWA_BLK, w), lambda n: (jnp.maximum(blk_of(n) - 1, 0), c))
    return [cur(768, _QCOL), cur(256, _KCOL), prev(256, _KCOL), cur(256, _VCOL), prev(256, _VCOL),
            cur(HD, 0), cur(HD, 0), prev(HD, 0), prev(HD, 0), pl.BlockSpec((1, LANE), lambda n: (0, 0))]


def swa_fwd(p, cos2, sin2, sinks):
    t = p.shape[0]

    def body(q_ref, kc, kp, vc, vp, c2c, s2c, c2p, s2p, sk_ref, o_ref):
        n = pl.program_id(0)

        def head(h):
            _, _, vcat, pr, _, l = yield from _swa_group(q_ref[...], kc[...], kp[...], vc[...], vp[...], c2c[...],
                                                         s2c[...], c2p[...], s2p[...], sk_ref[...], h, n)
            o = _mm(pr, vcat)
            yield
            return o / l

        for h, o in enumerate(_lockstep([head(h) for h in range(SWA_KV)])):
            for g in range(SWA_G):
                hh = SWA_G * h + g
                o_ref[:, hh * HD:(hh + 1) * HD] = o[g * SWA_BLK:(g + 1) * SWA_BLK].astype(o_ref.dtype)

    return pl.pallas_call(
        body, grid=(t // SWA_BLK,), in_specs=_swa_specs(lambda n: n),
        out_specs=pl.BlockSpec((SWA_BLK, 768), lambda n: (n, 0)),
        out_shape=jax.ShapeDtypeStruct((t, 768), BF16),
        compiler_params=_cp("parallel"), name="swa_fwd")(p, p, p, p, p, cos2, sin2, cos2, sin2, sinks)


def swa_bwd(p, dc, cos2, sin2, sinks):
    t = p.shape[0]
    nb = t // SWA_BLK

    def body(q_ref, kc, kp, vc, vp, c2c, s2c, c2p, s2p, sk_ref, do_ref, dq_o, dk_o, dv_o, dsk_o, ck, cv):
        i = pl.program_id(0)
        r = nb - 1 - i

        @pl.when(i == 0)
        def _():
            ck[...] = jnp.zeros_like(ck)
            cv[...] = jnp.zeros_like(cv)
            dsk_o[...] = jnp.zeros_like(dsk_o)

        lane = lax.broadcasted_iota(jnp.int32, (1, LANE), 1)
        dsk = jnp.zeros((1, LANE), F32)
        do = do_ref[...]
        def head(h):
            qs, kcat, vcat, pr, ps, l = yield from _swa_group(q_ref[...], kc[...], kp[...], vc[...], vp[...],
                                                              c2c[...], s2c[...], c2p[...], s2p[...], sk_ref[...], h, r)
            pn = pr / l
            dos = jnp.concatenate([_hs(do, SWA_G * h + g) for g in range(SWA_G)], axis=0)
            dp = _mm_nt(dos, vcat)
            dv = _mm_tn(pn, dos)
            yield
            delta = jnp.sum(pn * dp, axis=1, keepdims=True)
            ds = pn * (dp - delta)
            dqs = _mm(ds, kcat) * SCALE
            dk = _mm_tn(ds, qs) * SCALE
            yield
            return dqs, dk, dv, -(ps / l) * delta

        for h, (dqs, dk, dv, dsr) in enumerate(_lockstep([head(h) for h in range(SWA_KV)])):
            for g in range(SWA_G):
                tot = jnp.sum(dsr[g * SWA_BLK:(g + 1) * SWA_BLK], axis=0, keepdims=True)
                dsk = dsk + jnp.where(lane == SWA_G * h + g, tot, 0.0)
            for g in range(SWA_G):
                hh = SWA_G * h + g
                dq_o[:, hh * HD:(hh + 1) * HD] = _rope_bwd(dqs[g * SWA_BLK:(g + 1) * SWA_BLK], c2c[...],
                                                          s2c[...]).astype(dq_o.dtype)
            cs = slice(h * HD, (h + 1) * HD)
            dk_o[:, cs] = (_rope_bwd(dk[SWA_BLK:], c2c[...], s2c[...]) + ck[:, cs]).astype(dk_o.dtype)
            dv_o[:, cs] = (dv[SWA_BLK:] + cv[:, cs]).astype(dv_o.dtype)
            ck[:, cs] = _rope_bwd(dk[:SWA_BLK], c2p[...], s2p[...])
            cv[:, cs] = dv[:SWA_BLK]
        dsk_o[...] += dsk

    rev = lambda n: nb - 1 - n
    return pl.pallas_call(
        body, grid=(nb,),
        in_specs=_swa_specs(rev) + [pl.BlockSpec((SWA_BLK, 768), lambda n: (rev(n), 1))],
        out_specs=[pl.BlockSpec((SWA_BLK, 768), lambda n: (rev(n), 0)),
                   pl.BlockSpec((SWA_BLK, 256), lambda n: (rev(n), 0)),
                   pl.BlockSpec((SWA_BLK, 256), lambda n: (rev(n), 0)),
                   pl.BlockSpec((1, LANE), lambda n: (0, 0))],
        out_shape=(jax.ShapeDtypeStruct((t, 768), BF16), jax.ShapeDtypeStruct((t, 256), BF16),
                   jax.ShapeDtypeStruct((t, 256), BF16), jax.ShapeDtypeStruct((1, LANE), F32)),
        scratch_shapes=[pltpu.VMEM((SWA_BLK, 256), F32), pltpu.VMEM((SWA_BLK, 256), F32)],
        compiler_params=_cp("arbitrary"), name="swa_bwd")(p, p, p, p, p, cos2, sin2, cos2, sin2, sinks, dc)


_ZCOL, _GCOL = O_Z // DN_W, O_GATE // LANE
_QKV_W = 3 * DN_W
_INV_STEPS = int(math.log2(CH)) - 1


class _Bag(dict):
    __getattr__ = dict.__getitem__


DN_CB = 4
_DN_ROWS = DN_CB * CH
_CH_SHIFT = CH.bit_length() - 1


def _dn_consts():
    ii = lax.broadcasted_iota(jnp.int32, (CH, CH), 0)
    jj = lax.broadcasted_iota(jnp.int32, (CH, CH), 1)
    bi = lax.broadcasted_iota(jnp.int32, (_DN_ROWS, _DN_ROWS), 0)
    bj = lax.broadcasted_iota(jnp.int32, (_DN_ROWS, _DN_ROWS), 1)
    same_chunk = jnp.right_shift(bi, _CH_SHIFT) == jnp.right_shift(bj, _CH_SHIFT)
    return _Bag(lower=ii >= jj, strict=ii > jj, diag=ii == jj,
                eye=jnp.where(ii == jj, 1.0, 0.0).astype(F32),
                tril_blk=jnp.where(same_chunk & (bi >= bj), 1.0, 0.0).astype(F32),
                ones=jnp.ones((CH, CH), F32), ones_w=jnp.ones((CH, LANE), F32),
                rows=lax.broadcasted_iota(jnp.int32, (CH, 1), 0),
                lane=lax.broadcasted_iota(jnp.int32, (1, LANE), 1))


def _dn_conv(ext_ref, cw):
    return (cw[0:1, :] * ext_ref[pl.ds(5, _DN_ROWS), :] + cw[1:2, :] * ext_ref[pl.ds(6, _DN_ROWS), :]
            + cw[2:3, :] * ext_ref[pl.ds(7, _DN_ROWS), :] + cw[3:4, :] * ext_ref[pl.ds(8, _DN_ROWS), :])


def _dn_gates(gt, arow, drow, c):
    beta = _sigmoid(gt)
    ea = jnp.exp(arow)
    xa = gt + drow
    g = -ea * _softplus(xa)
    return beta, g, _mm_hi(c.tril_blk, g), ea, _sigmoid(xa)


def _blk(a, ci, j):
    return a[ci * CH:(ci + 1) * CH, j * HD:(j + 1) * HD]


def _lockstep(gens):
    out, live = [None] * len(gens), list(range(len(gens)))
    while live:
        still = []
        for i in live:
            try:
                next(gens[i])
                still.append(i)
            except StopIteration as stop:
                out[i] = stop.value
        live = still
    return out


def _dn_head_a(qh, kh, vh, beta, gc, c):
    rq = lax.rsqrt(jnp.sum(qh * qh, axis=1, keepdims=True) + EPS)
    rk = lax.rsqrt(jnp.sum(kh * kh, axis=1, keepdims=True) + EPS)
    qn = qh * rq * SCALE
    kn = kh * rk
    kb = kn * beta
    vb = vh * beta
    gcol = _mm_hi(c.ones, jnp.where(c.diag, gc, 0.0))
    kk = _mm_nt(kb, kn)
    qk = _mm_nt(qn, kn)
    yield
    gam = jnp.where(c.lower, jnp.exp(jnp.minimum(gc - gcol, 0.0)), 0.0)
    lmat = jnp.where(c.strict, kk * gam, 0.0)
    amat = qk * gam
    nil = -lmat
    inv = c.eye + nil
    powk = nil
    for _ in range(_INV_STEPS):
        powk = _mm(powk, powk)
        yield
        inv = _mm(inv, c.eye + powk)
    eg = jnp.exp(gc)
    kbe = kb * eg
    yield
    u = _mm(inv, vb)
    w = _mm(inv, kbe)
    gl = gc[CH - 1:CH, :]
    e2 = jnp.exp(gl - gc)
    cd = jnp.exp(gl)
    qd = qn * eg
    kd = kn * e2
    return _Bag(rq=rq, rk=rk, qn=qn, kn=kn, kb=kb, vb=vb, gam=gam, lmat=lmat, inv=inv, eg=eg, kbe=kbe, u=u, w=w,
                amat=amat, e2=e2, cd=cd, qd=qd, kd=kd)


def _dn_head_b(f, s0):
    ws = _mm(f.w, s0)
    qs = _mm(f.qd, s0)
    yield
    vnew = f.u - ws
    return vnew, qs + _mm(f.amat, vnew), s0 * f.cd + _mm_tn(f.kd, vnew)


def _dn_post(o, zh, nw):
    ro = lax.rsqrt(jnp.mean(o * o, axis=1, keepdims=True) + EPS)
    oh = o * ro
    sz = _sigmoid(zh)
    return ro, oh, sz, oh * nw * (zh * sz)


def _dn_post_bwd(o, zh, nw, dy):
    ro, oh, sz, _ = _dn_post(o, zh, nw)
    don = dy * (zh * sz)
    dz = dy * (oh * nw) * (sz * (1.0 + zh * (1.0 - sz)))
    doh = don * nw
    return (ro * (doh - oh * jnp.mean(doh * oh, axis=1, keepdims=True)), dz,
            jnp.sum(don * oh, axis=0, keepdims=True))


def _dn_head_bwd_b(f, vnew, do, dsn, s0):
    a_do = _mm_tn(f.amat, do)
    kd_ds = _mm(f.kd, dsn)
    qd_do = _mm_tn(f.qd, do)
    dkd = _mm_nt(vnew, dsn)
    yield
    dvnew = a_do + kd_ds
    ds0 = qd_do + f.cd * dsn - _mm_tn(f.w, dvnew)
    dcd = jnp.sum(jnp.sum(s0 * dsn, axis=1, keepdims=True), axis=0, keepdims=True)
    return dvnew, ds0, dkd, dcd


def _dn_head_bwd_c(f, vnew, do, dvnew, dkd, dcd, s0, qh, vh, beta, c):
    da = jnp.where(c.lower, _mm_nt(do, vnew), 0.0)
    dqd = _mm_nt(do, s0)
    dw = -_mm_nt(dvnew, s0)
    dt_u = _mm_nt(dvnew, f.vb)
    dvb = _mm_tn(f.inv, dvnew)
    yield
    dt = dt_u + _mm_nt(dw, f.kbe)
    dkbe = _mm_tn(f.inv, dw)
    yield
    dt_inv = _mm_nt(dt, f.inv)
    yield
    dl = -jnp.where(c.strict, _mm_tn(f.inv, dt_inv), 0.0)
    yield
    dm = dl * f.gam
    dn = da * f.gam
    dkb = _mm(dm, f.kn) + dkbe * f.eg
    dkn = _mm_tn(dm, f.kb) + _mm_tn(dn, f.qn) + dkd * f.e2 + beta * dkb
    dqn = _mm(dn, f.kn) + dqd * f.eg
    pm = dl * f.lmat + da * f.amat
    colsum = _mm_hi(pm, c.ones_w, _TN)[:, 0:1]
    yield
    tkd = jnp.sum(dkd * f.kn, axis=1, keepdims=True) * f.e2
    dgc = (jnp.sum(pm, axis=1, keepdims=True) - colsum - tkd
           + (jnp.sum(dqd * f.qn, axis=1, keepdims=True) + jnp.sum(dkbe * f.kb, axis=1, keepdims=True)) * f.eg)
    dgl = jnp.sum(tkd, axis=0, keepdims=True) + dcd * f.cd
    dgc = dgc + jnp.where(c.rows == CH - 1, dgl, 0.0)
    dbeta = jnp.sum(dkb * f.kn, axis=1, keepdims=True) + jnp.sum(dvb * vh, axis=1, keepdims=True)
    dvh = beta * dvb
    qhat = qh * f.rq
    dqs = dqn * SCALE
    dqh = f.rq * (dqs - qhat * jnp.sum(qhat * dqs, axis=1, keepdims=True))
    dkh = f.rk * (dkn - f.kn * jnp.sum(f.kn * dkn, axis=1, keepdims=True))
    return dqh, dkh, dvh, dbeta, dgc


def _dn_in_specs(step_of):
    return [pl.BlockSpec((_DN_ROWS, _QKV_W), lambda n: (step_of(n), 0)),
            pl.BlockSpec((SUB, _QKV_W), lambda n: (jnp.maximum(step_of(n) * (_DN_ROWS // SUB) - 1, 0), 0)),
            pl.BlockSpec((_DN_ROWS, DN_W), lambda n: (step_of(n), _ZCOL)),
            pl.BlockSpec((_DN_ROWS, LANE), lambda n: (step_of(n), _GCOL)),
            pl.BlockSpec((DN_K, _QKV_W), lambda n: (0, 0)),
            pl.BlockSpec((SUB, LANE), lambda n: (0, 0))]


def _dn_heads_a(qkv, beta_all, gc_all, c):
    rows = lambda a, ci: a[ci * CH:(ci + 1) * CH]
    flat = _lockstep([_dn_head_a(_blk(qkv, ci, h), _blk(qkv, ci, DN_H + h), _blk(qkv, ci, 2 * DN_H + h),
                                 rows(beta_all, ci)[:, h:h + 1], rows(gc_all, ci)[:, DN_H + h:DN_H + h + 1], c)
                      for ci in range(DN_CB) for h in range(DN_H)])
    return [flat[ci * DN_H:(ci + 1) * DN_H] for ci in range(DN_CB)]


def dn_fwd(p, conv_w, par):
    t = p.shape[0]
    nc = t // CH
    assert t % _DN_ROWS == 0

    def body(x_ref, h_ref, z_ref, g_ref, cw_ref, par_ref, y_o, s_o, ext, st):
        n = pl.program_id(0)
        c = _dn_consts()

        @pl.when(n == 0)
        def _():
            st[...] = jnp.zeros_like(st)

        ext[0:SUB, :] = jnp.where(n > 0, h_ref[...], 0.0)
        ext[SUB:, :] = x_ref[...]
        pre = _dn_conv(ext, cw_ref[...])
        qkv = pre * _sigmoid(pre)
        par = par_ref[...]
        beta_all, _, gc_all, _, _ = _dn_gates(g_ref[...], par[0:1, :], par[1:2, :], c)
        z = z_ref[...]
        fa = _dn_heads_a(qkv, beta_all, gc_all, c)
        s = [st[h] for h in range(DN_H)]
        for ci in range(DN_CB):
            for h in range(DN_H):
                s_o[ci, h] = s[h]
            res = _lockstep([_dn_head_b(fa[ci][h], s[h]) for h in range(DN_H)])
            for h in range(DN_H):
                _, o, s[h] = res[h]
                y_o[ci * CH:(ci + 1) * CH, h * HD:(h + 1) * HD] = _dn_post(o, _blk(z, ci, h),
                                                                           par[2:3, :])[3].astype(y_o.dtype)
        for h in range(DN_H):
            st[h] = s[h]

    return pl.pallas_call(
        body, grid=(t // _DN_ROWS,), in_specs=_dn_in_specs(lambda n: n),
        out_specs=[pl.BlockSpec((_DN_ROWS, DN_W), lambda n: (n, 0)),
                   pl.BlockSpec((DN_CB, DN_H, HD, HD), lambda n: (n, 0, 0, 0))],
        out_shape=(jax.ShapeDtypeStruct((t, DN_W), BF16), jax.ShapeDtypeStruct((nc, DN_H, HD, HD), F32)),
        scratch_shapes=[pltpu.VMEM((_DN_ROWS + SUB, _QKV_W), F32), pltpu.VMEM((DN_H, HD, HD), F32)],
        compiler_params=_cp("arbitrary"), name="dn_fwd")(p, p, p, p, conv_w, par)


def dn_bwd(p, dc, states, conv_w, par):
    t = p.shape[0]
    ns = t // _DN_ROWS

    def body(x_ref, h_ref, z_ref, g_ref, cw_ref, par_ref, s_ref, dy_ref,
             dx_o, dz_o, dg_o, dcw_o, dpar_o, ext, dst, dpost, x2):
        i = pl.program_id(0)
        r = ns - 1 - i
        c = _dn_consts()

        @pl.when(i == 0)
        def _():
            dst[...] = jnp.zeros_like(dst)
            dcw_o[...] = jnp.zeros_like(dcw_o)
            dpar_o[...] = jnp.zeros_like(dpar_o)
            x2[_DN_ROWS:, :] = jnp.zeros((SUB, _QKV_W), F32)

        ext[0:SUB, :] = jnp.where(r > 0, h_ref[...], 0.0)
        ext[SUB:, :] = x_ref[...]
        cw = cw_ref[...]
        pre = _dn_conv(ext, cw)
        sg = _sigmoid(pre)
        qkv = pre * sg
        par = par_ref[...]
        gt = g_ref[...]
        beta_all, g_all, gc_all, ea, sxa = _dn_gates(gt, par[0:1, :], par[1:2, :], c)
        z, dy = z_ref[...], dy_ref[...]
        nw = par[2:3, :]
        dnw = jnp.zeros((1, LANE), F32)
        pairs = [(ci, h) for ci in range(DN_CB) for h in range(DN_H)]
        fa = _dn_heads_a(qkv, beta_all, gc_all, c)
        vnew, do = {}, {}
        fwd = _lockstep([_dn_head_b(fa[ci][h], s_ref[ci, h]) for ci, h in pairs])
        for (ci, h), (vn, o, _) in zip(pairs, fwd):
            vnew[ci, h] = vn
            do[ci, h], dz, dnw_h = _dn_post_bwd(o, _blk(z, ci, h), nw, _blk(dy, ci, h))
            dnw = dnw + dnw_h
            dz_o[ci * CH:(ci + 1) * CH, h * HD:(h + 1) * HD] = dz.astype(dz_o.dtype)
        ds = [dst[h] for h in range(DN_H)]
        seq = {}
        for ci in range(DN_CB - 1, -1, -1):
            res = _lockstep([_dn_head_bwd_b(fa[ci][h], vnew[ci, h], do[ci, h], ds[h], s_ref[ci, h])
                             for h in range(DN_H)])
            for h in range(DN_H):
                dvnew, ds[h], dkd, dcd = res[h]
                seq[ci, h] = (dvnew, dkd, dcd)
        for h in range(DN_H):
            dst[h] = ds[h]
        rest = _lockstep([_dn_head_bwd_c(
            fa[ci][h], vnew[ci, h], do[ci, h], *seq[ci, h], s_ref[ci, h], _blk(qkv, ci, h),
            _blk(qkv, ci, 2 * DN_H + h), beta_all[ci * CH:(ci + 1) * CH, h:h + 1], c) for ci, h in pairs])
        dbeta_rows, dgc_rows = [], []
        for ci in range(DN_CB):
            dbeta_c = jnp.zeros((CH, LANE), F32)
            dgc_c = jnp.zeros((CH, LANE), F32)
            for h in range(DN_H):
                dqh, dkh, dvh, dbeta, dgc = rest[ci * DN_H + h]
                dbeta_c = dbeta_c + jnp.where(c.lane == h, dbeta, 0.0)
                dgc_c = dgc_c + jnp.where(c.lane == DN_H + h, dgc, 0.0)
                rs = slice(ci * CH, (ci + 1) * CH)
                dpost[rs, h * HD:(h + 1) * HD] = dqh
                dpost[rs, (DN_H + h) * HD:(DN_H + h + 1) * HD] = dkh
                dpost[rs, (2 * DN_H + h) * HD:(2 * DN_H + h + 1) * HD] = dvh
            dbeta_rows.append(dbeta_c)
            dgc_rows.append(dgc_c)
        dbeta_all = jnp.concatenate(dbeta_rows, axis=0)
        dgc_all = jnp.concatenate(dgc_rows, axis=0)
        dg_all = _mm_hi(c.tril_blk, dgc_all, _TN)
        dpa = dg_all * (-ea) * sxa
        dpb = dbeta_all * beta_all * (1.0 - beta_all)
        is_b = c.lane < DN_H
        is_a = (c.lane >= DN_H) & (c.lane < 2 * DN_H)
        dg_o[...] = jnp.where(is_b, dpb, jnp.where(is_a, dpa, 0.0)).astype(dg_o.dtype)
        dpar_o[0:1, :] += jnp.where(is_a, jnp.sum(dg_all * g_all, axis=0, keepdims=True), 0.0)
        dpar_o[1:2, :] += jnp.where(is_a, jnp.sum(dpa, axis=0, keepdims=True), 0.0)
        dpar_o[2:3, :] += dnw
        dpre = dpost[...] * (sg * (1.0 + pre * (1.0 - sg)))
        for k in range(DN_K):
            dcw_o[k:k + 1, :] += jnp.sum(dpre * ext[pl.ds(5 + k, _DN_ROWS), :], axis=0, keepdims=True)
        x2[0:_DN_ROWS, :] = dpre
        dx_o[...] = (cw[3:4, :] * dpre + cw[2:3, :] * x2[pl.ds(1, _DN_ROWS), :]
                     + cw[1:2, :] * x2[pl.ds(2, _DN_ROWS), :]
                     + cw[0:1, :] * x2[pl.ds(3, _DN_ROWS), :]).astype(dx_o.dtype)
        x2[_DN_ROWS:, :] = dpre[0:SUB, :]

    rev = lambda n: ns - 1 - n
    return pl.pallas_call(
        body, grid=(ns,),
        in_specs=_dn_in_specs(rev) + [pl.BlockSpec((DN_CB, DN_H, HD, HD), lambda n: (rev(n), 0, 0, 0)),
                                      pl.BlockSpec((_DN_ROWS, DN_W), lambda n: (rev(n), 0))],
        out_specs=[pl.BlockSpec((_DN_ROWS, _QKV_W), lambda n: (rev(n), 0)),
                   pl.BlockSpec((_DN_ROWS, DN_W), lambda n: (rev(n), 0)),
                   pl.BlockSpec((_DN_ROWS, LANE), lambda n: (rev(n), 0)),
                   pl.BlockSpec((SUB, _QKV_W), lambda n: (0, 0)),
                   pl.BlockSpec((SUB, LANE), lambda n: (0, 0))],
        out_shape=(jax.ShapeDtypeStruct((t, _QKV_W), BF16), jax.ShapeDtypeStruct((t, DN_W), BF16),
                   jax.ShapeDtypeStruct((t, LANE), BF16), jax.ShapeDtypeStruct((SUB, _QKV_W), F32),
                   jax.ShapeDtypeStruct((SUB, LANE), F32)),
        scratch_shapes=[pltpu.VMEM((_DN_ROWS + SUB, _QKV_W), F32), pltpu.VMEM((DN_H, HD, HD), F32),
                        pltpu.VMEM((_DN_ROWS, _QKV_W), F32), pltpu.VMEM((_DN_ROWS + SUB, _QKV_W), F32)],
        compiler_params=_cp("arbitrary"), name="dn_bwd")(p, p, p, p, conv_w, par, states, dc)


_ANY = pl.BlockSpec(memory_space=pl.ANY)
_MESH = pl.DeviceIdType.MESH


def _me():
    return lax.axis_index("x"), lax.axis_index("y"), lax.axis_index("c")


def all_gather(x, name):
    def body(x_ref, out_ref, send_sems, recv_sems, local_sem):
        mx, my, mc = _me()
        me, sibling = (mx, my, mc), (mx, my, 1 - mc)
        chips = [(1 - mx, my), (mx, 1 - my), (1 - mx, 1 - my)]

        def slot(px, py, pc):
            return out_ref.at[4 * px + 2 * py + pc]

        def copy(k, block, to, src=None):
            return pltpu.make_async_remote_copy(
                src_ref=slot(*block) if src is None else src, dst_ref=slot(*block),
                send_sem=send_sems.at[k], recv_sem=recv_sems.at[k], device_id=to, device_id_type=_MESH)

        mine = pltpu.make_async_copy(x_ref, slot(*me), local_sem)
        mine.start()
        first = [copy(0, me, sibling, src=x_ref)]
        first += [copy(1 + j, me, (*chip, mc), src=x_ref) for j, chip in enumerate(chips)]
        for cp in first:
            cp.start()
        passed = [copy(4 + j, (*chip, mc), sibling) for j, chip in enumerate(chips)]
        for j, chip in enumerate(chips):
            copy(1 + j, (*chip, mc), me).wait_recv()
            passed[j].start()
        copy(0, sibling, me).wait_recv()
        for j, chip in enumerate(chips):
            copy(4 + j, (*chip, 1 - mc), me).wait_recv()
        for cp in first + passed:
            cp.wait_send()
        mine.wait()

    return pl.pallas_call(
        body, out_shape=jax.ShapeDtypeStruct((N_DEV,) + x.shape, x.dtype), in_specs=[_ANY], out_specs=_ANY,
        scratch_shapes=[pltpu.SemaphoreType.DMA((7,)), pltpu.SemaphoreType.DMA((7,)), pltpu.SemaphoreType.DMA],
        name=name)(x)


_HBM = pl.BlockSpec(memory_space=pltpu.HBM)
_SEM = pl.BlockSpec(memory_space=pltpu.SEMAPHORE)
_EFFECT = pltpu.SideEffectType.DATAFLOW_SIDE_EFFECTING
_TOKEN = jax.ShapeDtypeStruct((SUB, LANE), F32)


def _peers(mx, my, mc):
    for rel in range(1, N_DEV):
        yield (1 - mx if rel & 4 else mx, 1 - my if rel & 2 else my, 1 - mc if rel & 1 else mc)


def _in_hbm(a):
    return pltpu.with_memory_space_constraint(a, pltpu.HBM)


GATHER_SLOTS = (4, 3)


def gather_start(buf, phase, after, name):
    def body(buf_ref, after_ref, send_sem, recv_sem, thru, token):
        mx, my, mc = _me()
        sibling = (mx, my, 1 - mc)
        chips = [(1 - mx, my), (mx, 1 - my), (1 - mx, 1 - my)]
        if phase == 0:
            slot = buf_ref.at[4 * mx + 2 * my + mc]
            copies = [(slot, sibling)] + [(slot, (px, py, mc)) for px, py in chips]
        else:
            copies = [(buf_ref.at[4 * px + 2 * py + mc], sibling) for px, py in chips]
        for slot, peer in copies:
            pltpu.make_async_remote_copy(src_ref=slot, dst_ref=slot, send_sem=send_sem, recv_sem=recv_sem,
                                         device_id=peer, device_id_type=_MESH).start()
        token[...] = jnp.zeros_like(token)

    send_sem, recv_sem, thru, token = pl.pallas_call(
        body, name=name,
        out_shape=(pltpu.SemaphoreType.DMA(()), pltpu.SemaphoreType.DMA(()), pltpu.HBM(buf.shape, buf.dtype), _TOKEN),
        in_specs=[_HBM, _ANY], out_specs=(_SEM, _SEM, _HBM, pl.BlockSpec(memory_space=pltpu.VMEM)),
        input_output_aliases={0: 2},
        compiler_params=pltpu.CompilerParams(has_side_effects=_EFFECT))(_in_hbm(buf), after)
    return (send_sem, recv_sem), thru, token


def exchange_start(src, name):
    def body(src_ref, land_ref, send_sem, recv_sem, src_thru, land_thru, token):
        mx, my, mc = _me()
        me = 4 * mx + 2 * my + mc
        for px, py, pc in _peers(mx, my, mc):
            pltpu.make_async_remote_copy(
                src_ref=src_ref.at[4 * px + 2 * py + pc], dst_ref=land_ref.at[me], send_sem=send_sem,
                recv_sem=recv_sem, device_id=(px, py, pc), device_id_type=_MESH).start()
        token[...] = jnp.zeros_like(token)

    hbm = pltpu.HBM(src.shape, src.dtype)
    send_sem, recv_sem, src_thru, land_thru, token = pl.pallas_call(
        body, name=name,
        out_shape=(pltpu.SemaphoreType.DMA(()), pltpu.SemaphoreType.DMA(()), hbm, hbm, _TOKEN),
        in_specs=[_HBM, _HBM], out_specs=(_SEM, _SEM, _HBM, _HBM, pl.BlockSpec(memory_space=pltpu.VMEM)),
        input_output_aliases={0: 2, 1: 3},
        compiler_params=pltpu.CompilerParams(has_side_effects=_EFFECT))(
            _in_hbm(src), _in_hbm(lax.empty(src.shape, src.dtype)))
    return (send_sem, recv_sem), src_thru, land_thru, token


def transfer_wait(sems, bufs, after, name, slots=N_DEV - 1):
    n = len(bufs)

    def body(*refs):
        seven = refs[0].at[pl.ds(0, slots)]
        cp = pltpu.make_async_remote_copy(src_ref=seven, dst_ref=seven, send_sem=refs[n], recv_sem=refs[n + 1],
                                          device_id=_me(), device_id_type=_MESH)
        cp.wait_send()
        cp.wait_recv()

    outs = pl.pallas_call(
        body, name=name, out_shape=tuple(pltpu.HBM(b.shape, b.dtype) for b in bufs),
        in_specs=[_HBM] * n + [_SEM, _SEM, _ANY], out_specs=tuple([_HBM] * n),
        input_output_aliases={b: b for b in range(n)},
        compiler_params=pltpu.CompilerParams(has_side_effects=_EFFECT))(*bufs, sems[0], sems[1], after)
    return list(outs)


def sum_slabs(x, name, own=None, me=None):
    _, r, c = x.shape
    tr = _pick(r, max(SUB, (1 << 19) // c // SUB * SUB), SUB)
    out_shape = jax.ShapeDtypeStruct((r, c), F32)
    if own is None:
        def body(x_ref, o_ref):
            acc = x_ref[0].astype(F32)
            for s in range(1, N_DEV):
                acc = acc + x_ref[s].astype(F32)
            o_ref[...] = acc

        return pl.pallas_call(
            body, grid=(r // tr,), in_specs=[pl.BlockSpec((N_DEV, tr, c), lambda i: (0, i, 0))],
            out_specs=pl.BlockSpec((tr, c), lambda i: (i, 0)), out_shape=out_shape,
            compiler_params=_cp("parallel"), name=name)(x)

    def body_own(me_ref, x_ref, own_ref, o_ref):
        acc = None
        for s in range(N_DEV):
            val = jnp.where(me_ref[0] == s, own_ref[...], x_ref[s]).astype(F32)
            acc = val if acc is None else acc + val
        o_ref[...] = acc

    return pl.pallas_call(
        body_own, out_shape=out_shape, name=name, compiler_params=_cp("parallel"),
        grid_spec=pltpu.PrefetchScalarGridSpec(
            num_scalar_prefetch=1, grid=(r // tr,),
            in_specs=[pl.BlockSpec((N_DEV, tr, c), lambda i, me_ref: (0, i, 0)),
                      pl.BlockSpec((None, tr, c), lambda i, me_ref: (me_ref[0], i, 0))],
            out_specs=pl.BlockSpec((tr, c), lambda i, me_ref: (i, 0))))(me, x, own)


def adamw(w, g, m, v, name):
    r, c = w.shape
    tr = _pick(r, max(SUB, (1 << 18) // c // SUB * SUB), SUB)
    c1 = 1.0 / (1.0 - ADAM_B1 ** ADAM_STEP)
    c2 = 1.0 / (1.0 - ADAM_B2 ** ADAM_STEP)

    def body(w_ref, g_ref, m_ref, v_ref, d_o, m_o, v_o):
        gg = g_ref[...]
        mn = ADAM_B1 * m_ref[...] + (1.0 - ADAM_B1) * gg
        vn = ADAM_B2 * v_ref[...] + (1.0 - ADAM_B2) * (gg * gg)
        m_o[...] = mn
        v_o[...] = vn
        d_o[...] = -ADAM_LR * ((mn * c1) / (jnp.sqrt(vn * c2) + ADAM_EPS) + ADAM_WD * w_ref[...])

    spec = pl.BlockSpec((tr, c), lambda i: (i, 0))
    sds = jax.ShapeDtypeStruct((r, c), F32)
    return pl.pallas_call(body, grid=(r // tr,), in_specs=[spec] * 4, out_specs=[spec] * 3, out_shape=(sds,) * 3,
                          compiler_params=_cp("parallel"), name=name)(w, g, m, v)


def _adam_update(w, gg, m, v):
    mn = ADAM_B1 * m + (1.0 - ADAM_B1) * gg
    vn = ADAM_B2 * v + (1.0 - ADAM_B2) * (gg * gg)
    c1 = 1.0 / (1.0 - ADAM_B1 ** ADAM_STEP)
    c2 = 1.0 / (1.0 - ADAM_B2 ** ADAM_STEP)
    return -ADAM_LR * ((mn * c1) / (jnp.sqrt(vn * c2) + ADAM_EPS) + ADAM_WD * w), mn, vn


def adamw_layer(layer, w, m, v, prev, name, g=None, land=None, own=None, me=None):
    nl, r, c = w.shape
    tr = _pick(r, max(SUB, (1 << 17) // c // SUB * SUB), SUB)
    from_slabs = g is None
    if prev is None:
        prev = tuple(lax.empty((nl, r, c), F32) for _ in range(4))

    def body(*refs):
        if from_slabs:
            me_ref, land_ref, own_ref, w_ref, m_ref, v_ref = refs[:6]
            gg = None
            for s in range(N_DEV):
                val = jnp.where(me_ref[0] == s, own_ref[...], land_ref[s]).astype(F32)
                gg = val if gg is None else gg + val
        else:
            me_ref, g_ref, w_ref, m_ref, v_ref = refs[:5]
            gg = g_ref[...]
        g_o, d_o, m_o, v_o = refs[-4:]
        g_o[...] = gg
        d_o[...], m_o[...], v_o[...] = _adam_update(w_ref[...], gg, m_ref[...], v_ref[...])

    lay = pl.BlockSpec((None, tr, c), lambda i, me_ref: (layer, i, 0))
    if from_slabs:
        grad_specs = [pl.BlockSpec((N_DEV, tr, c), lambda i, me_ref: (0, i, 0)),
                      pl.BlockSpec((None, tr, c), lambda i, me_ref: (me_ref[0], i, 0))]
        grad_args = [land, own]
    else:
        grad_specs = [pl.BlockSpec((tr, c), lambda i, me_ref: (i, 0))]
        grad_args = [g]
        me = jnp.zeros((1,), jnp.int32)
    n_in = 1 + len(grad_args) + 3
    return pl.pallas_call(
        body, out_shape=tuple(jax.ShapeDtypeStruct((nl, r, c), F32) for _ in range(4)), name=name,
        input_output_aliases={n_in + k: k for k in range(4)}, compiler_params=_cp("parallel"),
        grid_spec=pltpu.PrefetchScalarGridSpec(
            num_scalar_prefetch=1, grid=(r // tr,), in_specs=grad_specs + [lay] * 3 + [_ANY] * 4,
            out_specs=[lay] * 4))(me, *grad_args, w, m, v, *prev)


def _pack(parts):
    flat = jnp.concatenate([a.reshape(-1).astype(F32) for a in parts])
    n = flat.shape[0]
    npad = -n % (PACK_ROWS * LANE)
    return jnp.pad(flat, (0, npad)).reshape(-1, LANE)


def _unpack(buf, shapes, lead=()):
    flat = buf.reshape(lead + (-1,))
    out, off = [], 0
    for s in shapes:
        n = math.prod(s)
        out.append(flat[..., off:off + n].reshape(lead + tuple(s)))
        off += n
    return out


ROPE_THETA = 10000.0

_SMALL = ("norm_mix_pre", "dn_conv_w", "dn_a_log", "dn_dt_bias", "dn_norm_w", "pool_w", "pool_scale", "swa_sinks",
          "norm_mix_post", "norm_ffn_pre", "ffn_conv_w", "ffn_conv_b", "norm_ffn_post")
_BIG = ("w_in", "w_out", "ffn_w_up", "ffn_w_down")
_ORDER = ("norm_mix_pre", "w_in", "dn_conv_w", "dn_a_log", "dn_dt_bias", "dn_norm_w", "pool_w", "pool_scale",
          "swa_sinks", "w_out", "norm_mix_post", "norm_ffn_pre", "ffn_w_up", "ffn_conv_w", "ffn_conv_b",
          "ffn_w_down", "norm_ffn_post")


def _step(x, positions, loss_target, w, m, v):
    nl = w["w_in"].shape[0]
    t, d = x.shape[1], x.shape[2]
    nb = w["ffn_w_up"].shape[2]
    f = nb * N_DEV // 2
    me = 4 * lax.axis_index("x") + 2 * lax.axis_index("y") + lax.axis_index("c")
    x_in, tgt = x[0], loss_target[0]

    inv_freq = 1.0 / (ROPE_THETA ** (jnp.arange(0, HD, 2, dtype=F32) / HD))
    ang = positions[0].astype(F32)[:, None] * inv_freq
    cos, sin = jnp.cos(ang), jnp.sin(ang)
    cos2 = jnp.concatenate([cos, cos], axis=1)
    sin2 = jnp.concatenate([-sin, sin], axis=1)

    conv_shapes = [w["dn_conv_w"].shape, w["ffn_conv_w"].shape]
    gathered_conv = all_gather(_pack([w["dn_conv_w"], w["ffn_conv_w"]]), "ag_conv")
    dn_cw_g, ffn_cw_g = _unpack(gathered_conv, conv_shapes, lead=(N_DEV,))
    dn_cw = jnp.moveaxis(dn_cw_g, 0, 2).reshape(nl, DN_K, _QKV_W)
    ffn_cw = jnp.moveaxis(ffn_cw_g, 0, 1).reshape(nl, 2, N_DEV // 2, 3, nb)
    ffn_cb = w["ffn_conv_b"].reshape(nl, 2, N_DEV // 2, 1, nb)

    def lane_row(vec, off):
        return jnp.zeros((LANE,), F32).at[off:off + vec.shape[0]].set(vec)

    dn_par = jnp.stack([
        jnp.zeros((SUB, LANE), F32).at[0].set(lane_row(w["dn_a_log"][l], DN_H))
        .at[1].set(lane_row(w["dn_dt_bias"][l], DN_H)).at[2].set(w["dn_norm_w"][l]) for l in range(nl)])
    sinks = jnp.stack([lane_row(w["swa_sinks"][l], 0)[None, :] for l in range(nl)])

    def place(shard):
        return lax.dynamic_update_slice(lax.empty((N_DEV,) + shard.shape, shard.dtype), shard[None], (me, 0, 0))

    kinds = ("w_in", "w_out", "ffn_w_up", "ffn_w_down")
    flight = {}
    tag = lambda i: f"{kinds[i % 4]}_{i // 4}"

    def start_first(i, after):
        if i >= 4 * nl:
            return jnp.zeros(_TOKEN.shape, F32)
        l, k = divmod(i, 4)
        shard = _align_in(w[kinds[k]][l]) if k == 0 else w[kinds[k]][l]
        sems, buf, token = gather_start(place(shard.astype(BF16)), 0, after, f"ag_start_{tag(i)}")
        flight[i] = (sems, buf)
        return token

    def start_second(i, after):
        if i >= 4 * nl:
            return jnp.zeros(_TOKEN.shape, F32)
        arrived = transfer_wait(flight[i][0], [flight[i][1]], after, f"ag_wait_{tag(i)}", GATHER_SLOTS[0])[0]
        sems, buf, token = gather_start(arrived, 1, after, f"ag_pass_{tag(i)}")
        flight[i] = (sems, buf)
        return token

    def gathered(l, k, after):
        i = 4 * l + k
        late = start_second(1, after) if i == 1 else None
        got = transfer_wait(flight[i][0], [flight[i][1]], after if late is None else late, f"ag_done_{tag(i)}",
                            GATHER_SLOTS[1])[0]
        first = start_first(i + 3, got)
        if i == 0:
            return got, first[0, 0]
        return got, (start_second(i + 1, first) + first)[0, 0]

    win, wout, wup, wdown = [None] * nl, [None] * nl, [None] * nl, [None] * nl
    row = lambda a, l: a[l][None, :]
    g1, g2, g3, g4 = w["norm_mix_pre"], w["norm_mix_post"], w["norm_ffn_pre"], w["norm_ffn_post"]

    saved = []
    xl = x_in
    passed = start_second(0, start_first(0, gathered_conv))
    h1 = norm_first(xl, row(g1, 0) + (passed + start_first(1, passed) + start_first(2, passed))[0, 0])
    for l in range(nl):
        buf, tk = gathered(l, 0, h1)
        win[l] = buf.reshape(d, PW)
        p = mm_nn(h1, win[l], F32, "mm_in")
        y_dn, states = dn_fwd(p, dn_cw[l], dn_par[l] + tk)
        y_pool = pool_fwd(p, w["pool_w"][l], row(w["pool_scale"], l))
        y_swa = swa_fwd(p, cos2, sin2, sinks[l])
        c = jnp.concatenate([y_dn, y_swa, y_pool], axis=1)
        buf, tk = gathered(l, 1, c)
        wout[l] = _perm_mix_rows(buf.reshape(MIX_W, d))
        mix = mm_nn(c, wout[l], F32, "mm_out")
        x1, h2 = post_pre(xl, mix, row(g2, l) + tk, row(g3, l))
        wup[l], tk = gathered(l, 2, h2)
        u0 = mm_up(h2, wup[l], "mm_up")
        act = glu_fwd(u0, ffn_cw[l], ffn_cb[l] + tk)
        buf, tk = gathered(l, 3, act)
        wdown[l] = buf.reshape(f, d)
        fo = mm_nn(act, wdown[l], F32, "mm_down")
        saved.append(dict(x=xl, h1=h1, p=p, states=states, c=c, mix=mix, x1=x1, h2=h2, u0=u0, act=act, f=fo))
        if l < nl - 1:
            xl, h1 = post_pre(x1, fo, row(g4, l) + tk, row(g1, l + 1))
        else:
            dx, loss_part = post_loss(x1, fo, row(g4, l) + tk, tgt)

    small_g = [dict() for _ in range(nl)]
    pending = {name: [None] * nl for name in _BIG}

    def exchange(name, l, dw):
        sems, src, land, token = exchange_start(dw, f"xch_start_{name}_{l}")
        pending[name][l] = (sems, src, land)
        return token[0, 0]

    df, small_g[nl - 1]["norm_ffn_post"] = bwd_norms(dx, post=(saved[-1]["f"], row(g4, nl - 1)))
    for l in range(nl - 1, -1, -1):
        s, sg = saved[l], small_g[l]
        dact = mm_nt(df, wdown[l], F32, "mm_down_d")
        tk = exchange("ffn_w_down", l, mm_tn(s["act"], df, BF16, "mm_down_w").reshape(N_DEV, f // N_DEV, d))
        du0, dcw = glu_bwd(dact, s["u0"], ffn_cw[l], ffn_cb[l])
        sg["ffn_conv"] = dcw
        dh2 = mm_up_dgrad(du0, wup[l], "mm_up_d")
        tk = tk + exchange("ffn_w_up", l, mm_up_wgrad(s["h2"], du0, "mm_up_w"))
        dx1, sg["norm_ffn_pre"], dmix, sg["norm_mix_post"] = bwd_norms(
            dx, pre=(dh2, s["x1"], row(g3, l) + tk), post=(s["mix"], row(g2, l)))
        dc = mm_nt(dmix, wout[l], F32, "mm_out_d")
        tk = exchange("w_out", l, _unperm_mix_rows(mm_tn(s["c"], dmix, BF16, "mm_out_w"))
                      .reshape(N_DEV, MIX_W // N_DEV, d))
        dqkv, dz, dgate, sg["dn_conv_w"], sg["dn_par"] = dn_bwd(s["p"], dc, s["states"], dn_cw[l], dn_par[l])
        dpool, sg["pool_w"], sg["pool_scale"] = pool_bwd(s["p"], dc, w["pool_w"][l], row(w["pool_scale"], l))
        dsq, dsk, dsv, sg["swa_sinks"] = swa_bwd(s["p"], dc, cos2, sin2, sinks[l])
        dp = jnp.concatenate([dqkv, dz, dsq, dsk, dsv, dpool, dgate], axis=1)
        dh1 = mm_nt(dp, win[l], F32, "mm_in_d")
        tk = tk + exchange("w_in", l, mm_tn(s["h1"], dp, BF16, "mm_in_w").reshape(N_DEV, d // N_DEV, PW))
        if l > 0:
            dx, sg["norm_mix_pre"], df, small_g[l - 1]["norm_ffn_post"] = bwd_norms(
                dx1, pre=(dh1, s["x"], row(g1, l) + tk), post=(saved[l - 1]["f"], row(g4, l - 1)))
        else:
            grad_x, sg["norm_mix_pre"] = bwd_norms(dx1, pre=(dh1, s["x"], row(g1, 0) + tk))

    me_arr = jnp.reshape(me, (1,)).astype(jnp.int32)
    big = {name: None for name in _BIG}
    w_in_sums = [None] * nl

    def finish(name, l, after):
        sems, src, land = pending[name][l]
        src, land = transfer_wait(sems, [src, land], after, f"xch_wait_{name}_{l}")
        if name == "w_in":
            w_in_sums[l] = sum_slabs(land, "sum_w_in", own=src, me=me_arr)
            return w_in_sums[l]
        big[name] = adamw_layer(l, w[name], m[name], v[name], big[name], "adamw_" + name, land=land, own=src,
                                me=me_arr)
        return big[name][1]

    after = grad_x
    for l in range(nl - 1, 0, -1):
        for name in ("ffn_w_down", "ffn_w_up", "w_out", "w_in"):
            after = finish(name, l, after)

    keys = ("norm_mix_pre", "norm_mix_post", "norm_ffn_pre", "norm_ffn_post", "dn_conv_w", "dn_par", "pool_w",
            "pool_scale", "swa_sinks", "ffn_conv")
    grads = {}
    parts = [small_g[l][k] for l in range(nl) for k in keys] + [loss_part]
    shapes = [a.shape for a in parts]
    ordered = parts + [after[0:SUB, 0:LANE]]
    summed = sum_slabs(all_gather(_pack(ordered), "ag_small"), "sum_small")
    vals = _unpack(summed, shapes)
    loss = vals[-1][0, 0]
    sm = [dict(zip(keys, vals[l * len(keys):(l + 1) * len(keys)])) for l in range(nl)]
    st = lambda fn: jnp.stack([fn(sm[l]) for l in range(nl)])
    for k in ("norm_mix_pre", "norm_mix_post", "norm_ffn_pre", "norm_ffn_post"):
        grads[k] = st(lambda q: q[k][0])
    grads["dn_conv_w"] = lax.dynamic_slice_in_dim(st(lambda q: q["dn_conv_w"][0:DN_K]), me * (_QKV_W // N_DEV),
                                                  _QKV_W // N_DEV, axis=2)
    grads["dn_a_log"] = st(lambda q: q["dn_par"][0, DN_H:2 * DN_H])
    grads["dn_dt_bias"] = st(lambda q: q["dn_par"][1, DN_H:2 * DN_H])
    grads["dn_norm_w"] = st(lambda q: q["dn_par"][2])
    grads["pool_w"] = st(lambda q: q["pool_w"])
    grads["pool_scale"] = st(lambda q: q["pool_scale"][0])
    grads["swa_sinks"] = st(lambda q: q["swa_sinks"][0, 0:SWA_H])
    conv_all = st(lambda q: q["ffn_conv"].reshape(N_DEV, SUB, nb))
    grads["ffn_conv_w"] = lax.dynamic_index_in_dim(conv_all, me, axis=1, keepdims=False)[:, 0:3, :]
    grads["ffn_conv_b"] = conv_all[:, :, 3, :].reshape(nl, 2 * f)

    delta, new_m, new_v = {}, {}, {}
    shapes = [w[k].shape for k in _SMALL]
    pk = lambda tree: _pack([tree[k] for k in _SMALL])
    outs = adamw(pk(w), pk(grads), pk(m), pk(v), "adamw_small")
    for tree, buf in zip((delta, new_m, new_v), outs):
        for k, a in zip(_SMALL, _unpack(buf, shapes)):
            tree[k] = a
    after = outs[0]
    for name in ("ffn_w_down", "ffn_w_up", "w_out"):
        after = finish(name, 0, after)
        grads[name], delta[name], new_m[name], new_v[name] = big[name]
    finish("w_in", 0, after)
    grads["w_in"] = _unalign_in(jnp.stack(w_in_sums))
    flat = lambda a: a.reshape(-1, IN_W)
    delta["w_in"], new_m["w_in"], new_v["w_in"] = (
        o.reshape(w["w_in"].shape) for o in adamw(flat(w["w_in"]), flat(grads["w_in"]), flat(m["w_in"]),
                                                  flat(v["w_in"]), "adamw_w_in"))

    return (loss, grad_x[None], *[grads[k] for k in _ORDER], *[delta[k] for k in _ORDER],
            *[new_m[k] for k in _ORDER], *[new_v[k] for k in _ORDER])


def kernel(x, positions, norm_mix_pre, w_in, dn_conv_w, dn_a_log, dn_dt_bias, dn_norm_w, pool_w, pool_scale, swa_sinks, w_out, norm_mix_post, norm_ffn_pre, ffn_w_up, ffn_conv_w, ffn_conv_b, ffn_w_down, norm_ffn_post, loss_target, m_norm_mix_pre, m_w_in, m_dn_conv_w, m_dn_a_log, m_dn_dt_bias, m_dn_norm_w, m_pool_w, m_pool_scale, m_swa_sinks, m_w_out, m_norm_mix_post, m_norm_ffn_pre, m_ffn_w_up, m_ffn_conv_w, m_ffn_conv_b, m_ffn_w_down, m_norm_ffn_post, v_norm_mix_pre, v_w_in, v_dn_conv_w, v_dn_a_log, v_dn_dt_bias, v_dn_norm_w, v_pool_w, v_pool_scale, v_swa_sinks, v_w_out, v_norm_mix_post, v_norm_ffn_pre, v_ffn_w_up, v_ffn_conv_w, v_ffn_conv_b, v_ffn_w_down, v_norm_ffn_post):
    args = locals()
    w = {k: args[k] for k in _ORDER}
    m = {k: args["m_" + k] for k in _ORDER}
    v = {k: args["v_" + k] for k in _ORDER}
    return _step(x, positions, loss_target, w, m, v)
```

```python
import functools
import math

import jax
import jax.numpy as jnp
from jax import lax
from jax.experimental import pallas as pl
from jax.experimental.pallas import tpu as pltpu

F32 = jnp.float32
BF16 = jnp.bfloat16
MXU_DT = jnp.bfloat16
HI = lax.Precision.HIGHEST

N_DEV = 8
LANE = 128
SUB = 8
VMEM_LIMIT = 56 * 1024 * 1024
ROW_TILE = 512
NORM_TILE = 256
MM_TM, MM_TN, MM_TK = 1024, 1664, 2816
MM_TN_NT = 2048
PACK_ROWS = 512

HD = 128
DN_H, DN_W, DN_K, CH = 6, 768, 4, 64
POOL_G = 4
SWA_H, SWA_KV, SWA_G, SWA_BLK = 6, 2, 3, 128
EPS = 1e-6
SCALE = HD ** -0.5
NEG = -1e30

O_QKV, O_Z, O_SQ, O_SK, O_SV, O_POOL, O_GATE, PW = 0, 2304, 3072, 3840, 4096, 4352, 4864, 4992
IN_W = 4876
MIX_W = 2048

ADAM_LR, ADAM_B1, ADAM_B2, ADAM_EPS, ADAM_WD, ADAM_STEP = 0.001, 0.9, 0.999, 1e-08, 0.01, 10


def _pick(n, cap, mult=LANE):
    best = None
    for d in range(mult, min(n, cap) + 1, mult):
        if n % d == 0:
            best = d
    return best if best is not None else n


def _cp(*sem):
    return pltpu.CompilerParams(dimension_semantics=sem, vmem_limit_bytes=VMEM_LIMIT)


def _dot(a, b, dims):
    return lax.dot_general(a.astype(MXU_DT), b.astype(MXU_DT), dims, preferred_element_type=F32)


_NN = (((1,), (0,)), ((), ()))
_NT = (((1,), (1,)), ((), ()))
_TN = (((0,), (0,)), ((), ()))


def _mm(a, b):
    return _dot(a, b, _NN)


def _mm_nt(a, b):
    return _dot(a, b, _NT)


def _mm_tn(a, b):
    return _dot(a, b, _TN)


def _mm_hi(a, b, dims=_NN):
    return lax.dot_general(a, b, dims, precision=HI, preferred_element_type=F32)


def _sigmoid(x):
    return jax.nn.sigmoid(x)


def _softplus(x):
    return jnp.maximum(x, 0.0) + jnp.log(1.0 + jnp.exp(-jnp.abs(x)))


def _align_in(w):
    pad = jnp.zeros(w.shape[:-1] + (PW - IN_W,), w.dtype)
    return jnp.concatenate([w[..., 0:3072], w[..., 3596:4364], w[..., 4364:4620], w[..., 4620:4876],
                            w[..., 3084:3596], w[..., 3072:3084], pad], axis=-1)


def _unalign_in(g):
    return jnp.concatenate([g[..., 0:3072], g[..., O_GATE:O_GATE + 12], g[..., O_POOL:O_POOL + 512],
                            g[..., O_SQ:O_SQ + 768], g[..., O_SK:O_SK + 256], g[..., O_SV:O_SV + 256]], axis=-1)


def _perm_mix_rows(w):
    return jnp.concatenate([w[0:768], w[1280:2048], w[768:1280]], axis=0)


def _unperm_mix_rows(w):
    return jnp.concatenate([w[0:768], w[1536:2048], w[768:1536]], axis=0)


def _mm_call(name, a, b, out_shape, grid, a_spec, b_spec, o_spec, dims, acc_shape):
    nk = grid[2]
    if nk == 1:
        def body_once(a_ref, b_ref, o_ref):
            o_ref[...] = _dot(a_ref[...], b_ref[...], dims).astype(o_ref.dtype)

        return pl.pallas_call(
            body_once, grid=grid, in_specs=[a_spec, b_spec], out_specs=o_spec, out_shape=out_shape,
            compiler_params=_cp("parallel", "parallel", "arbitrary"), name=name)(a, b)

    def body(a_ref, b_ref, o_ref, acc_ref):
        k = pl.program_id(2)

        @pl.when(k == 0)
        def _():
            acc_ref[...] = jnp.zeros_like(acc_ref)

        acc_ref[...] += _dot(a_ref[...], b_ref[...], dims)

        @pl.when(k == nk - 1)
        def _():
            o_ref[...] = acc_ref[...].astype(o_ref.dtype)

    return pl.pallas_call(
        body, grid=grid, in_specs=[a_spec, b_spec], out_specs=o_spec, out_shape=out_shape,
        scratch_shapes=[pltpu.VMEM(acc_shape, F32)],
        compiler_params=_cp("parallel", "parallel", "arbitrary"), name=name)(a, b)


def mm_nn(a, b, out_dtype, name):
    (m, k), n = a.shape, b.shape[1]
    tm, tn, tk = _pick(m, MM_TM, SUB), _pick(n, MM_TN), _pick(k, MM_TK)
    return _mm_call(name, a, b, jax.ShapeDtypeStruct((m, n), out_dtype), (m // tm, n // tn, k // tk),
                    pl.BlockSpec((tm, tk), lambda i, j, kk: (i, kk)),
                    pl.BlockSpec((tk, tn), lambda i, j, kk: (kk, j)),
                    pl.BlockSpec((tm, tn), lambda i, j, kk: (i, j)), _NN, (tm, tn))


def mm_nt(a, b, out_dtype, name):
    (m, k), n = a.shape, b.shape[0]
    tm, tn, tk = _pick(m, MM_TM, SUB), _pick(n, MM_TN_NT), _pick(k, MM_TK)
    return _mm_call(name, a, b, jax.ShapeDtypeStruct((m, n), out_dtype), (m // tm, n // tn, k // tk),
                    pl.BlockSpec((tm, tk), lambda i, j, kk: (i, kk)),
                    pl.BlockSpec((tn, tk), lambda i, j, kk: (j, kk)),
                    pl.BlockSpec((tm, tn), lambda i, j, kk: (i, j)), _NT, (tm, tn))


def mm_tn(a, b, out_dtype, name):
    (k, m), n = a.shape, b.shape[1]
    tm, tn, tk = _pick(m, MM_TM), _pick(n, MM_TN), _pick(k, MM_TK, SUB)
    return _mm_call(name, a, b, jax.ShapeDtypeStruct((m, n), out_dtype), (m // tm, n // tn, k // tk),
                    pl.BlockSpec((tk, tm), lambda i, j, kk: (kk, i)),
                    pl.BlockSpec((tk, tn), lambda i, j, kk: (kk, j)),
                    pl.BlockSpec((tm, tn), lambda i, j, kk: (i, j)), _TN, (tm, tn))


def mm_up(h, wblk, name):
    (t, d), (nblk, _, nb) = h.shape, wblk.shape
    tm, tk = _pick(t, MM_TM, SUB), _pick(d, MM_TK)
    hb = nblk // 2
    return _mm_call(name, h, wblk, jax.ShapeDtypeStruct((2, t, hb * nb), F32), (t // tm, nblk, d // tk),
                    pl.BlockSpec((tm, tk), lambda i, j, kk: (i, kk)),
                    pl.BlockSpec((None, tk, nb), lambda i, j, kk: (j, kk, 0)),
                    pl.BlockSpec((None, tm, nb), lambda i, j, kk: (j // hb, i, j % hb)), _NN, (tm, nb))


def mm_up_dgrad(du0, wblk, name):
    (_, t, _), (nblk, d, nb) = du0.shape, wblk.shape
    tm, tn = _pick(t, MM_TM, SUB), _pick(d, MM_TN_NT)
    hb = nblk // 2
    return _mm_call(name, du0, wblk, jax.ShapeDtypeStruct((t, d), F32), (t // tm, d // tn, nblk),
                    pl.BlockSpec((None, tm, nb), lambda i, j, kk: (kk // hb, i, kk % hb)),
                    pl.BlockSpec((None, tn, nb), lambda i, j, kk: (kk, j, 0)),
                    pl.BlockSpec((tm, tn), lambda i, j, kk: (i, j)), _NT, (tm, tn))


def mm_up_wgrad(h, du0, name):
    (t, d), (_, _, f) = h.shape, du0.shape
    nb = f // (N_DEV // 2)
    hb = N_DEV // 2
    tm, tk = _pick(d, MM_TM), _pick(t, MM_TK, SUB)
    return _mm_call(name, h, du0, jax.ShapeDtypeStruct((N_DEV, d, nb), BF16), (d // tm, N_DEV, t // tk),
                    pl.BlockSpec((tk, tm), lambda i, j, kk: (kk, i)),
                    pl.BlockSpec((None, tk, nb), lambda i, j, kk: (j // hb, kk, j % hb)),
                    pl.BlockSpec((None, tm, nb), lambda i, j, kk: (j, i, 0)), _TN, (tm, nb))


def _rms(x, w):
    r = lax.rsqrt(jnp.mean(x * x, axis=-1, keepdims=True) + EPS)
    return x * r * w


def _rms_bwd(dy, x, w):
    r = lax.rsqrt(jnp.mean(x * x, axis=-1, keepdims=True) + EPS)
    xh = x * r
    dxh = dy * w
    dx = r * (dxh - xh * jnp.mean(dxh * xh, axis=-1, keepdims=True))
    return dx, jnp.sum(dy * xh, axis=0, keepdims=True)


def _row_spec(tb, d):
    return pl.BlockSpec((tb, d), lambda i: (i, 0))


def _fix_spec(r, d):
    return pl.BlockSpec((r, d), lambda i: (0, 0))


def norm_first(x, w):
    t, d = x.shape
    tb = _pick(t, NORM_TILE, SUB)

    def body(x_ref, w_ref, h_ref):
        h_ref[...] = _rms(x_ref[...], w_ref[...]).astype(h_ref.dtype)

    return pl.pallas_call(body, grid=(t // tb,), in_specs=[_row_spec(tb, d), _fix_spec(1, d)],
                          out_specs=_row_spec(tb, d), out_shape=jax.ShapeDtypeStruct((t, d), BF16),
                          compiler_params=_cp("parallel"), name="norm_first")(x, w)


def post_pre(x, y, w_post, w_pre):
    t, d = x.shape
    tb = _pick(t, NORM_TILE, SUB)

    def body(x_ref, y_ref, wp_ref, wq_ref, xn_ref, h_ref):
        xn = x_ref[...] + _rms(y_ref[...], wp_ref[...])
        xn_ref[...] = xn
        h_ref[...] = _rms(xn, wq_ref[...]).astype(h_ref.dtype)

    return pl.pallas_call(
        body, grid=(t // tb,),
        in_specs=[_row_spec(tb, d), _row_spec(tb, d), _fix_spec(1, d), _fix_spec(1, d)],
        out_specs=[_row_spec(tb, d), _row_spec(tb, d)],
        out_shape=(jax.ShapeDtypeStruct((t, d), F32), jax.ShapeDtypeStruct((t, d), BF16)),
        compiler_params=_cp("parallel"), name="post_pre")(x, y, w_post, w_pre)


def post_loss(x, y, w_post, target):
    t, d = x.shape
    tb = _pick(t, NORM_TILE, SUB)

    def body(x_ref, y_ref, wp_ref, t_ref, g_ref, l_ref):
        err = x_ref[...] + _rms(y_ref[...], wp_ref[...]) - t_ref[...]
        g_ref[...] = err * (1.0 / d)

        @pl.when(pl.program_id(0) == 0)
        def _():
            l_ref[...] = jnp.zeros_like(l_ref)

        part = 0.5 * jnp.sum(jnp.mean(err * err, axis=-1, keepdims=True), axis=0, keepdims=True)
        l_ref[...] += jnp.broadcast_to(part, l_ref.shape)

    return pl.pallas_call(
        body, grid=(t // tb,),
        in_specs=[_row_spec(tb, d), _row_spec(tb, d), _fix_spec(1, d), _row_spec(tb, d)],
        out_specs=[_row_spec(tb, d), _fix_spec(1, LANE)],
        out_shape=(jax.ShapeDtypeStruct((t, d), F32), jax.ShapeDtypeStruct((1, LANE), F32)),
        compiler_params=_cp("arbitrary"), name="post_loss")(x, y, w_post, target)


def bwd_norms(dx_in, *, pre=None, post=None):
    t, d = dx_in.shape
    tb = _pick(t, NORM_TILE, SUB)
    has_pre, has_post = pre is not None, post is not None

    def body(*refs):
        refs = list(refs)
        dxi = refs.pop(0)
        if has_pre:
            dh, x, wq = refs.pop(0), refs.pop(0), refs.pop(0)
        if has_post:
            y, wp = refs.pop(0), refs.pop(0)
        first = pl.program_id(0) == 0
        dx = dxi[...]
        if has_pre:
            dxo, dwq = refs.pop(0), refs.pop(0)
            g, dw = _rms_bwd(dh[...], x[...], wq[...])
            dx = dx + g
            dxo[...] = dx

            @pl.when(first)
            def _():
                dwq[...] = jnp.zeros_like(dwq)

            dwq[...] += dw
        if has_post:
            dyo, dwp = refs.pop(0), refs.pop(0)
            g, dw = _rms_bwd(dx, y[...], wp[...])
            dyo[...] = g.astype(dyo.dtype)

            @pl.when(first)
            def _():
                dwp[...] = jnp.zeros_like(dwp)

            dwp[...] += dw

    ins, in_specs, outs, out_specs = [dx_in], [_row_spec(tb, d)], [], []
    if has_pre:
        ins += list(pre)
        in_specs += [_row_spec(tb, d), _row_spec(tb, d), _fix_spec(1, d)]
        outs += [jax.ShapeDtypeStruct((t, d), F32), jax.ShapeDtypeStruct((1, d), F32)]
        out_specs += [_row_spec(tb, d), _fix_spec(1, d)]
    if has_post:
        ins += list(post)
        in_specs += [_row_spec(tb, d), _fix_spec(1, d)]
        outs += [jax.ShapeDtypeStruct((t, d), BF16), jax.ShapeDtypeStruct((1, d), F32)]
        out_specs += [_row_spec(tb, d), _fix_spec(1, d)]
    name = "bwd_norms" + ("_pre" if has_pre else "") + ("_post" if has_post else "")
    return pl.pallas_call(body, grid=(t // tb,), in_specs=in_specs, out_specs=out_specs, out_shape=tuple(outs),
                          compiler_params=_cp("arbitrary"), name=name)(*ins)


GLU_ROWS = 128


def _ffn_conv_blk(blk, cw, cb):
    r = blk.shape[1] - SUB
    x0, x1, x2 = blk[:, 6:6 + r], blk[:, 7:7 + r], blk[:, 8:8 + r]
    return x0, x1, x2, cw[:, 0:1, :] * x0 + cw[:, 1:2, :] * x1 + cw[:, 2:3, :] * x2 + cb


def _glu_specs(tb, nb, hpb, row_of):
    tile = pl.BlockSpec((2, tb, nb), lambda j, i: (0, row_of(i), j))
    halo = pl.BlockSpec((2, SUB, nb), lambda j, i: (0, jnp.maximum(row_of(i) * hpb - 1, 0), j))
    cw = pl.BlockSpec((2, None, 3, nb), lambda j, i: (0, j, 0, 0))
    cb = pl.BlockSpec((2, None, 1, nb), lambda j, i: (0, j, 0, 0))
    return tile, halo, cw, cb


def glu_fwd(u0, cw, cb):
    _, t, f = u0.shape
    nb = cw.shape[-1]
    tb = _pick(t, ROW_TILE, SUB)
    nt, hpb = t // tb, tb // SUB

    def body(u, h, cwr, cbr, o_ref, e):
        i = pl.program_id(1)
        e[:, 0:SUB, :] = jnp.where(i > 0, h[...], 0.0)
        e[:, SUB:, :] = u[...]
        w, bias = cwr[...], cbr[...]

        def rows(g, carry):
            s = pl.multiple_of(g * GLU_ROWS, GLU_ROWS)
            for lg in range(nb // LANE):
                ls = slice(lg * LANE, (lg + 1) * LANE)
                ab = _ffn_conv_blk(e[:, pl.ds(s, GLU_ROWS + SUB), ls], w[:, :, ls], bias[:, :, ls])[3]
                a, b = ab[0], ab[1]
                o_ref[pl.ds(s, GLU_ROWS), ls] = (a * _sigmoid(a) * b).astype(o_ref.dtype)
            return carry

        lax.fori_loop(0, tb // GLU_ROWS, rows, 0)

    return pl.pallas_call(
        body, grid=(f // nb, nt), in_specs=list(_glu_specs(tb, nb, hpb, lambda i: i)),
        out_specs=pl.BlockSpec((tb, nb), lambda j, i: (i, j)),
        out_shape=jax.ShapeDtypeStruct((t, f), BF16),
        scratch_shapes=[pltpu.VMEM((2, tb + SUB, nb), F32)],
        compiler_params=_cp("parallel", "arbitrary"), name="glu_fwd")(u0, u0, cw, cb)


def glu_bwd(dact, u0, cw, cb):
    _, t, f = u0.shape
    nb = cw.shape[-1]
    tb = _pick(t, ROW_TILE, SUB)
    nt, hpb = t // tb, tb // SUB

    def body(d_ref, u, h, cwr, cbr, du_o, dc_o, e, x2):
        i = pl.program_id(1)
        r = nt - 1 - i
        e[:, 0:SUB, :] = jnp.where(r > 0, h[...], 0.0)
        e[:, SUB:, :] = u[...]
        w, bias = cwr[...], cbr[...]

        @pl.when(i == 0)
        def _():
            dc_o[...] = jnp.zeros_like(dc_o)
            x2[:, tb:, :] = jnp.zeros((2, SUB, nb), F32)

        fold = lambda v: jnp.sum(v.reshape(2, GLU_ROWS // SUB, SUB, LANE), axis=1)
        for lg in range(nb // LANE):
            ls = slice(lg * LANE, (lg + 1) * LANE)
            wl, bl = w[:, :, ls], bias[:, :, ls]

            def grads(g, acc):
                s = pl.multiple_of(g * GLU_ROWS, GLU_ROWS)
                x0, x1, xc, ab = _ffn_conv_blk(e[:, pl.ds(s, GLU_ROWS + SUB), ls], wl, bl)
                a, b = ab[0], ab[1]
                sa = _sigmoid(a)
                d = d_ref[pl.ds(s, GLU_ROWS), ls]
                x2[0, pl.ds(s, GLU_ROWS), ls] = d * b * (sa * (1.0 + a * (1.0 - sa)))
                x2[1, pl.ds(s, GLU_ROWS), ls] = d * (a * sa)
                du = x2[:, pl.ds(s, GLU_ROWS), ls]
                return (acc[0] + fold(du * x0), acc[1] + fold(du * x1), acc[2] + fold(du * xc), acc[3] + fold(du))

            zero = jnp.zeros((2, SUB, LANE), F32)
            acc = lax.fori_loop(0, tb // GLU_ROWS, grads, (zero, zero, zero, zero))
            for k in range(4):
                dc_o[:, k:k + 1, ls] += jnp.sum(acc[k], axis=1, keepdims=True)

            def transposed_conv(g, carry):
                s = pl.multiple_of(g * GLU_ROWS, GLU_ROWS)
                blk = x2[:, pl.ds(s, GLU_ROWS + SUB), ls]
                du_o[:, pl.ds(s, GLU_ROWS), ls] = (
                    wl[:, 2:3, :] * blk[:, 0:GLU_ROWS] + wl[:, 1:2, :] * blk[:, 1:1 + GLU_ROWS]
                    + wl[:, 0:1, :] * blk[:, 2:2 + GLU_ROWS]).astype(du_o.dtype)
                return carry

            lax.fori_loop(0, tb // GLU_ROWS, transposed_conv, 0)
        x2[:, tb:, :] = x2[:, 0:SUB, :]

    rev = lambda i: nt - 1 - i
    return pl.pallas_call(
        body, grid=(f // nb, nt),
        in_specs=[pl.BlockSpec((tb, nb), lambda j, i: (rev(i), j))] + list(_glu_specs(tb, nb, hpb, rev)),
        out_specs=[pl.BlockSpec((2, tb, nb), lambda j, i: (0, rev(i), j)),
                   pl.BlockSpec((2, None, SUB, nb), lambda j, i: (0, j, 0, 0))],
        out_shape=(jax.ShapeDtypeStruct((2, t, f), BF16), jax.ShapeDtypeStruct((2, f // nb, SUB, nb), F32)),
        scratch_shapes=[pltpu.VMEM((2, tb + SUB, nb), F32), pltpu.VMEM((2, tb + SUB, nb), F32)],
        compiler_params=_cp("arbitrary", "arbitrary"), name="glu_bwd")(dact, u0, u0, cw, cb)


POOL_HALO = 16
_PCOL = O_POOL // LANE
_CPOOL = 1536 // LANE


def _pool_sel(g, v2, v4, v8, v16):
    return jnp.where(g == 0, v2, jnp.where(g == 1, v4, jnp.where(g == 2, v8, v16)))


def _pool_cnt(g, t0, n):
    win = _pool_sel(g, 2, 4, 8, 16)
    tpos = t0 + lax.broadcasted_iota(jnp.int32, (n, 1), 0)
    return jnp.minimum(tpos + 1, win).astype(F32)


def _pool_core(e, g, t0, tb):
    s2 = e + pltpu.roll(e, 1, 0)
    s4 = s2 + pltpu.roll(s2, 2, 0)
    s8 = s4 + pltpu.roll(s4, 4, 0)
    s16 = s8 + pltpu.roll(s8, 8, 0)
    sw = _pool_sel(g, s2, s4, s8, s16)[POOL_HALO:]
    return sw / _pool_cnt(g, t0, tb) - e[POOL_HALO:]


def pool_fwd(p, pool_w, pool_scale):
    t = p.shape[0]
    tb = _pick(t, ROW_TILE, POOL_HALO)
    nt, hpb = t // tb, tb // POOL_HALO

    def body(x_ref, h_ref, w_ref, s_ref, o_ref):
        i, g = pl.program_id(0), pl.program_id(1)
        e = jnp.concatenate([jnp.where(i > 0, h_ref[...], 0.0), x_ref[...]], axis=0)
        yy = _pool_core(e, g, i * tb, tb)
        o_ref[...] = (_mm(yy, w_ref[...]) * s_ref[...]).astype(o_ref.dtype)

    return pl.pallas_call(
        body, grid=(nt, POOL_G),
        in_specs=[pl.BlockSpec((tb, LANE), lambda i, g: (i, _PCOL + g)),
                  pl.BlockSpec((POOL_HALO, LANE), lambda i, g: (jnp.maximum(i * hpb - 1, 0), _PCOL + g)),
                  pl.BlockSpec((None, LANE, LANE), lambda i, g: (g, 0, 0)),
                  pl.BlockSpec((1, LANE), lambda i, g: (0, g))],
        out_specs=pl.BlockSpec((tb, LANE), lambda i, g: (i, g)),
        out_shape=jax.ShapeDtypeStruct((t, POOL_G * LANE), BF16),
        compiler_params=_cp("parallel", "parallel"), name="pool_fwd")(p, p, pool_w, pool_scale)


def pool_bwd(p, dc, pool_w, pool_scale):
    t = p.shape[0]
    tb = _pick(t, ROW_TILE, POOL_HALO)
    nt, hpb = t // tb, tb // POOL_HALO
    n = tb + POOL_HALO

    def body(x_ref, h_ref, dy_ref, dn_ref, w_ref, s_ref, dx_o, dw_o, ds_o):
        g, i = pl.program_id(0), pl.program_id(1)
        e = jnp.concatenate([jnp.where(i > 0, h_ref[...], 0.0), x_ref[...]], axis=0)
        yy = _pool_core(e, g, i * tb, tb)
        w, sc, dy = w_ref[...], s_ref[...], dy_ref[...]

        @pl.when(i == 0)
        def _():
            dw_o[...] = jnp.zeros_like(dw_o)
            ds_o[...] = jnp.zeros_like(ds_o)

        ds_o[...] += jnp.sum(dy * _mm(yy, w), axis=0, keepdims=True)
        dw_o[...] += _mm_tn(yy, dy * sc)
        dye = jnp.concatenate([dy, jnp.where(i < nt - 1, dn_ref[...], 0.0)], axis=0) * sc
        dyy = _mm_nt(dye, w)
        z = dyy / _pool_cnt(g, i * tb, n)
        r2 = z + pltpu.roll(z, n - 1, 0)
        r4 = r2 + pltpu.roll(r2, n - 2, 0)
        r8 = r4 + pltpu.roll(r4, n - 4, 0)
        r16 = r8 + pltpu.roll(r8, n - 8, 0)
        dx_o[...] = (_pool_sel(g, r2, r4, r8, r16)[:tb] - dyy[:tb]).astype(dx_o.dtype)

    last = t // POOL_HALO - 1
    return pl.pallas_call(
        body, grid=(POOL_G, nt),
        in_specs=[pl.BlockSpec((tb, LANE), lambda g, i: (i, _PCOL + g)),
                  pl.BlockSpec((POOL_HALO, LANE), lambda g, i: (jnp.maximum(i * hpb - 1, 0), _PCOL + g)),
                  pl.BlockSpec((tb, LANE), lambda g, i: (i, _CPOOL + g)),
                  pl.BlockSpec((POOL_HALO, LANE), lambda g, i: (jnp.minimum((i + 1) * hpb, last), _CPOOL + g)),
                  pl.BlockSpec((None, LANE, LANE), lambda g, i: (g, 0, 0)),
                  pl.BlockSpec((1, LANE), lambda g, i: (0, g))],
        out_specs=[pl.BlockSpec((tb, LANE), lambda g, i: (i, g)),
                   pl.BlockSpec((None, LANE, LANE), lambda g, i: (g, 0, 0)),
                   pl.BlockSpec((1, LANE), lambda g, i: (0, g))],
        out_shape=(jax.ShapeDtypeStruct((t, POOL_G * LANE), BF16),
                   jax.ShapeDtypeStruct((POOL_G, LANE, LANE), F32),
                   jax.ShapeDtypeStruct((1, POOL_G * LANE), F32)),
        compiler_params=_cp("arbitrary", "arbitrary"), name="pool_bwd")(p, p, dc, dc, pool_w, pool_scale)


_QCOL, _KCOL, _VCOL = O_SQ // 768, O_SK // 256, O_SV // 256
_GQ = SWA_G * SWA_BLK


def _rope(x, c2, s2):
    return x * c2 + pltpu.roll(x, HD // 2, 1) * s2


def _rope_bwd(d, c2, s2):
    return d * c2 + pltpu.roll(d * s2, HD // 2, 1)


def _hs(x, h):
    return x[:, h * HD:(h + 1) * HD]


def _swa_group(q, kc, kp, vc, vp, c2c, s2c, c2p, s2p, sinks, h, blk):
    kcat = jnp.concatenate([_rope(_hs(kp, h), c2p, s2p), _rope(_hs(kc, h), c2c, s2c)], axis=0)
    vcat = jnp.concatenate([_hs(vp, h), _hs(vc, h)], axis=0)
    qs = jnp.concatenate([_rope(_hs(q, SWA_G * h + g), c2c, s2c) for g in range(SWA_G)], axis=0)
    s = _mm_nt(qs, kcat) * SCALE
    yield
    ii = lax.broadcasted_iota(jnp.int32, (_GQ, 2 * SWA_BLK), 0) & (SWA_BLK - 1)
    jj = lax.broadcasted_iota(jnp.int32, (_GQ, 2 * SWA_BLK), 1)
    lo = jnp.where(blk > 0, 0, SWA_BLK)
    s = jnp.where((jj > ii) & (jj <= ii + SWA_BLK) & (jj >= lo), s, NEG)
    sink = jnp.concatenate(
        [jnp.broadcast_to(sinks[:, SWA_G * h + g:SWA_G * h + g + 1], (SWA_BLK, 1)) for g in range(SWA_G)], axis=0)
    m = jnp.maximum(jnp.max(s, axis=1, keepdims=True), sink)
    p = jnp.exp(s - m)
    ps = jnp.exp(sink - m)
    l = jnp.sum(p, axis=1, keepdims=True) + ps
    return qs, kcat, vcat, p, ps, l


def _swa_specs(blk_of):
    cur = lambda w, c: pl.BlockSpec((SWA_BLK, w), lambda n: (blk_of(n), c))
    prev = lambda w, c: pl.BlockSpec((SWA_BLK, w), lambda n: (jnp.maximum(blk_of(n) - 1, 0), c))
    return [cur(768, _QCOL), cur(256, _KCOL), prev(256, _KCOL), cur(256, _VCOL), prev(256, _VCOL),
            cur(HD, 0), cur(HD, 0), prev(HD, 0), prev(HD, 0), pl.BlockSpec((1, LANE), lambda n: (0, 0))]


def swa_fwd(p, cos2, sin2, sinks):
    t = p.shape[0]

    def body(q_ref, kc, kp, vc, vp, c2c, s2c, c2p, s2p, sk_ref, o_ref):
        n = pl.program_id(0)

        def head(h):
            _, _, vcat, pr, _, l = yield from _swa_group(q_ref[...], kc[...], kp[...], vc[...], vp[...], c2c[...],
                                                         s2c[...], c2p[...], s2p[...], sk_ref[...], h, n)
            o = _mm(pr, vcat)
            yield
            return o / l

        for h, o in enumerate(_lockstep([head(h) for h in range(SWA_KV)])):
            for g in range(SWA_G):
                hh = SWA_G * h + g
                o_ref[:, hh * HD:(hh + 1) * HD] = o[g * SWA_BLK:(g + 1) * SWA_BLK].astype(o_ref.dtype)

    return pl.pallas_call(
        body, grid=(t // SWA_BLK,), in_specs=_swa_specs(lambda n: n),
        out_specs=pl.BlockSpec((SWA_BLK, 768), lambda n: (n, 0)),
        out_shape=jax.ShapeDtypeStruct((t, 768), BF16),
        compiler_params=_cp("parallel"), name="swa_fwd")(p, p, p, p, p, cos2, sin2, cos2, sin2, sinks)


def swa_bwd(p, dc, cos2, sin2, sinks):
    t = p.shape[0]
    nb = t // SWA_BLK

    def body(q_ref, kc, kp, vc, vp, c2c, s2c, c2p, s2p, sk_ref, do_ref, dq_o, dk_o, dv_o, dsk_o, ck, cv):
        i = pl.program_id(0)
        r = nb - 1 - i

        @pl.when(i == 0)
        def _():
            ck[...] = jnp.zeros_like(ck)
            cv[...] = jnp.zeros_like(cv)
            dsk_o[...] = jnp.zeros_like(dsk_o)

        lane = lax.broadcasted_iota(jnp.int32, (1, LANE), 1)
        dsk = jnp.zeros((1, LANE), F32)
        do = do_ref[...]
        def head(h):
            qs, kcat, vcat, pr, ps, l = yield from _swa_group(q_ref[...], kc[...], kp[...], vc[...], vp[...],
                                                              c2c[...], s2c[...], c2p[...], s2p[...], sk_ref[...], h, r)
            pn = pr / l
            dos = jnp.concatenate([_hs(do, SWA_G * h + g) for g in range(SWA_G)], axis=0)
            dp = _mm_nt(dos, vcat)
            dv = _mm_tn(pn, dos)
            yield
            delta = jnp.sum(pn * dp, axis=1, keepdims=True)
            ds = pn * (dp - delta)
            dqs = _mm(ds, kcat) * SCALE
            dk = _mm_tn(ds, qs) * SCALE
            yield
            return dqs, dk, dv, -(ps / l) * delta

        for h, (dqs, dk, dv, dsr) in enumerate(_lockstep([head(h) for h in range(SWA_KV)])):
            for g in range(SWA_G):
                tot = jnp.sum(dsr[g * SWA_BLK:(g + 1) * SWA_BLK], axis=0, keepdims=True)
                dsk = dsk + jnp.where(lane == SWA_G * h + g, tot, 0.0)
            for g in range(SWA_G):
                hh = SWA_G * h + g
                dq_o[:, hh * HD:(hh + 1) * HD] = _rope_bwd(dqs[g * SWA_BLK:(g + 1) * SWA_BLK], c2c[...],
                                                          s2c[...]).astype(dq_o.dtype)
            cs = slice(h * HD, (h + 1) * HD)
            dk_o[:, cs] = (_rope_bwd(dk[SWA_BLK:], c2c[...], s2c[...]) + ck[:, cs]).astype(dk_o.dtype)
            dv_o[:, cs] = (dv[SWA_BLK:] + cv[:, cs]).astype(dv_o.dtype)
            ck[:, cs] = _rope_bwd(dk[:SWA_BLK], c2p[...], s2p[...])
            cv[:, cs] = dv[:SWA_BLK]
        dsk_o[...] += dsk

    rev = lambda n: nb - 1 - n
    return pl.pallas_call(
        body, grid=(nb,),
        in_specs=_swa_specs(rev) + [pl.BlockSpec((SWA_BLK, 768), lambda n: (rev(n), 1))],
        out_specs=[pl.BlockSpec((SWA_BLK, 768), lambda n: (rev(n), 0)),
                   pl.BlockSpec((SWA_BLK, 256), lambda n: (rev(n), 0)),
                   pl.BlockSpec((SWA_BLK, 256), lambda n: (rev(n), 0)),
                   pl.BlockSpec((1, LANE), lambda n: (0, 0))],
        out_shape=(jax.ShapeDtypeStruct((t, 768), BF16), jax.ShapeDtypeStruct((t, 256), BF16),
                   jax.ShapeDtypeStruct((t, 256), BF16), jax.ShapeDtypeStruct((1, LANE), F32)),
        scratch_shapes=[pltpu.VMEM((SWA_BLK, 256), F32), pltpu.VMEM((SWA_BLK, 256), F32)],
        compiler_params=_cp("arbitrary"), name="swa_bwd")(p, p, p, p, p, cos2, sin2, cos2, sin2, sinks, dc)


_ZCOL, _GCOL = O_Z // DN_W, O_GATE // LANE
_QKV_W = 3 * DN_W
_INV_STEPS = int(math.log2(CH)) - 1


class _Bag(dict):
    __getattr__ = dict.__getitem__


DN_CB = 4
_DN_ROWS = DN_CB * CH
_CH_SHIFT = CH.bit_length() - 1


def _dn_consts():
    ii = lax.broadcasted_iota(jnp.int32, (CH, CH), 0)
    jj = lax.broadcasted_iota(jnp.int32, (CH, CH), 1)
    bi = lax.broadcasted_iota(jnp.int32, (_DN_ROWS, _DN_ROWS), 0)
    bj = lax.broadcasted_iota(jnp.int32, (_DN_ROWS, _DN_ROWS), 1)
    same_chunk = jnp.right_shift(bi, _CH_SHIFT) == jnp.right_shift(bj, _CH_SHIFT)
    return _Bag(lower=ii >= jj, strict=ii > jj, diag=ii == jj,
                eye=jnp.where(ii == jj, 1.0, 0.0).astype(F32),
                tril_blk=jnp.where(same_chunk & (bi >= bj), 1.0, 0.0).astype(F32),
                ones=jnp.ones((CH, CH), F32), ones_w=jnp.ones((CH, LANE), F32),
                rows=lax.broadcasted_iota(jnp.int32, (CH, 1), 0),
                lane=lax.broadcasted_iota(jnp.int32, (1, LANE), 1))


def _dn_conv(ext_ref, cw):
    return (cw[0:1, :] * ext_ref[pl.ds(5, _DN_ROWS), :] + cw[1:2, :] * ext_ref[pl.ds(6, _DN_ROWS), :]
            + cw[2:3, :] * ext_ref[pl.ds(7, _DN_ROWS), :] + cw[3:4, :] * ext_ref[pl.ds(8, _DN_ROWS), :])


def _dn_gates(gt, arow, drow, c):
    beta = _sigmoid(gt)
    ea = jnp.exp(arow)
    xa = gt + drow
    g = -ea * _softplus(xa)
    return beta, g, _mm_hi(c.tril_blk, g), ea, _sigmoid(xa)


def _blk(a, ci, j):
    return a[ci * CH:(ci + 1) * CH, j * HD:(j + 1) * HD]


def _lockstep(gens):
    out, live = [None] * len(gens), list(range(len(gens)))
    while live:
        still = []
        for i in live:
            try:
                next(gens[i])
                still.append(i)
            except StopIteration as stop:
                out[i] = stop.value
        live = still
    return out


def _dn_head_a(qh, kh, vh, beta, gc, c):
    rq = lax.rsqrt(jnp.sum(qh * qh, axis=1, keepdims=True) + EPS)
    rk = lax.rsqrt(jnp.sum(kh * kh, axis=1, keepdims=True) + EPS)
    qn = qh * rq * SCALE
    kn = kh * rk
    kb = kn * beta
    vb = vh * beta
    gcol = _mm_hi(c.ones, jnp.where(c.diag, gc, 0.0))
    kk = _mm_nt(kb, kn)
    qk = _mm_nt(qn, kn)
    yield
    gam = jnp.where(c.lower, jnp.exp(jnp.minimum(gc - gcol, 0.0)), 0.0)
    lmat = jnp.where(c.strict, kk * gam, 0.0)
    amat = qk * gam
    nil = -lmat
    inv = c.eye + nil
    powk = nil
    for _ in range(_INV_STEPS):
        powk = _mm(powk, powk)
        yield
        inv = _mm(inv, c.eye + powk)
    eg = jnp.exp(gc)
    kbe = kb * eg
    yield
    u = _mm(inv, vb)
    w = _mm(inv, kbe)
    gl = gc[CH - 1:CH, :]
    e2 = jnp.exp(gl - gc)
    cd = jnp.exp(gl)
    qd = qn * eg
    kd = kn * e2
    return _Bag(rq=rq, rk=rk, qn=qn, kn=kn, kb=kb, vb=vb, gam=gam, lmat=lmat, inv=inv, eg=eg, kbe=kbe, u=u, w=w,
                amat=amat, e2=e2, cd=cd, qd=qd, kd=kd)


def _dn_head_b(f, s0):
    ws = _mm(f.w, s0)
    qs = _mm(f.qd, s0)
    yield
    vnew = f.u - ws
    return vnew, qs + _mm(f.amat, vnew), s0 * f.cd + _mm_tn(f.kd, vnew)


def _dn_post(o, zh, nw):
    ro = lax.rsqrt(jnp.mean(o * o, axis=1, keepdims=True) + EPS)
    oh = o * ro
    sz = _sigmoid(zh)
    return ro, oh, sz, oh * nw * (zh * sz)


def _dn_post_bwd(o, zh, nw, dy):
    ro, oh, sz, _ = _dn_post(o, zh, nw)
    don = dy * (zh * sz)
    dz = dy * (oh * nw) * (sz * (1.0 + zh * (1.0 - sz)))
    doh = don * nw
    return (ro * (doh - oh * jnp.mean(doh * oh, axis=1, keepdims=True)), dz,
            jnp.sum(don * oh, axis=0, keepdims=True))


def _dn_head_bwd_b(f, vnew, do, dsn, s0):
    a_do = _mm_tn(f.amat, do)
    kd_ds = _mm(f.kd, dsn)
    qd_do = _mm_tn(f.qd, do)
    dkd = _mm_nt(vnew, dsn)
    yield
    dvnew = a_do + kd_ds
    ds0 = qd_do + f.cd * dsn - _mm_tn(f.w, dvnew)
    dcd = jnp.sum(jnp.sum(s0 * dsn, axis=1, keepdims=True), axis=0, keepdims=True)
    return dvnew, ds0, dkd, dcd


def _dn_head_bwd_c(f, vnew, do, dvnew, dkd, dcd, s0, qh, vh, beta, c):
    da = jnp.where(c.lower, _mm_nt(do, vnew), 0.0)
    dqd = _mm_nt(do, s0)
    dw = -_mm_nt(dvnew, s0)
    dt_u = _mm_nt(dvnew, f.vb)
    dvb = _mm_tn(f.inv, dvnew)
    yield
    dt = dt_u + _mm_nt(dw, f.kbe)
    dkbe = _mm_tn(f.inv, dw)
    yield
    dt_inv = _mm_nt(dt, f.inv)
    yield
    dl = -jnp.where(c.strict, _mm_tn(f.inv, dt_inv), 0.0)
    yield
    dm = dl * f.gam
    dn = da * f.gam
    dkb = _mm(dm, f.kn) + dkbe * f.eg
    dkn = _mm_tn(dm, f.kb) + _mm_tn(dn, f.qn) + dkd * f.e2 + beta * dkb
    dqn = _mm(dn, f.kn) + dqd * f.eg
    pm = dl * f.lmat + da * f.amat
    colsum = _mm_hi(pm, c.ones_w, _TN)[:, 0:1]
    yield
    tkd = jnp.sum(dkd * f.kn, axis=1, keepdims=True) * f.e2
    dgc = (jnp.sum(pm, axis=1, keepdims=True) - colsum - tkd
           + (jnp.sum(dqd * f.qn, axis=1, keepdims=True) + jnp.sum(dkbe * f.kb, axis=1, keepdims=True)) * f.eg)
    dgl = jnp.sum(tkd, axis=0, keepdims=True) + dcd * f.cd
    dgc = dgc + jnp.where(c.rows == CH - 1, dgl, 0.0)
    dbeta = jnp.sum(dkb * f.kn, axis=1, keepdims=True) + jnp.sum(dvb * vh, axis=1, keepdims=True)
    dvh = beta * dvb
    qhat = qh * f.rq
    dqs = dqn * SCALE
    dqh = f.rq * (dqs - qhat * jnp.sum(qhat * dqs, axis=1, keepdims=True))
    dkh = f.rk * (dkn - f.kn * jnp.sum(f.kn * dkn, axis=1, keepdims=True))
    return dqh, dkh, dvh, dbeta, dgc


def _dn_in_specs(step_of):
    return [pl.BlockSpec((_DN_ROWS, _QKV_W), lambda n: (step_of(n), 0)),
            pl.BlockSpec((SUB, _QKV_W), lambda n: (jnp.maximum(step_of(n) * (_DN_ROWS // SUB) - 1, 0), 0)),
            pl.BlockSpec((_DN_ROWS, DN_W), lambda n: (step_of(n), _ZCOL)),
            pl.BlockSpec((_DN_ROWS, LANE), lambda n: (step_of(n), _GCOL)),
            pl.BlockSpec((DN_K, _QKV_W), lambda n: (0, 0)),
            pl.BlockSpec((SUB, LANE), lambda n: (0, 0))]


def _dn_heads_a(qkv, beta_all, gc_all, c):
    rows = lambda a, ci: a[ci * CH:(ci + 1) * CH]
    flat = _lockstep([_dn_head_a(_blk(qkv, ci, h), _blk(qkv, ci, DN_H + h), _blk(qkv, ci, 2 * DN_H + h),
                                 rows(beta_all, ci)[:, h:h + 1], rows(gc_all, ci)[:, DN_H + h:DN_H + h + 1], c)
                      for ci in range(DN_CB) for h in range(DN_H)])
    return [flat[ci * DN_H:(ci + 1) * DN_H] for ci in range(DN_CB)]


def dn_fwd(p, conv_w, par):
    t = p.shape[0]
    nc = t // CH
    assert t % _DN_ROWS == 0

    def body(x_ref, h_ref, z_ref, g_ref, cw_ref, par_ref, y_o, s_o, ext, st):
        n = pl.program_id(0)
        c = _dn_consts()

        @pl.when(n == 0)
        def _():
            st[...] = jnp.zeros_like(st)

        ext[0:SUB, :] = jnp.where(n > 0, h_ref[...], 0.0)
        ext[SUB:, :] = x_ref[...]
        pre = _dn_conv(ext, cw_ref[...])
        qkv = pre * _sigmoid(pre)
        par = par_ref[...]
        beta_all, _, gc_all, _, _ = _dn_gates(g_ref[...], par[0:1, :], par[1:2, :], c)
        z = z_ref[...]
        fa = _dn_heads_a(qkv, beta_all, gc_all, c)
        s = [st[h] for h in range(DN_H)]
        for ci in range(DN_CB):
            for h in range(DN_H):
                s_o[ci, h] = s[h]
            res = _lockstep([_dn_head_b(fa[ci][h], s[h]) for h in range(DN_H)])
            for h in range(DN_H):
                _, o, s[h] = res[h]
                y_o[ci * CH:(ci + 1) * CH, h * HD:(h + 1) * HD] = _dn_post(o, _blk(z, ci, h),
                                                                           par[2:3, :])[3].astype(y_o.dtype)
        for h in range(DN_H):
            st[h] = s[h]

    return pl.pallas_call(
        body, grid=(t // _DN_ROWS,), in_specs=_dn_in_specs(lambda n: n),
        out_specs=[pl.BlockSpec((_DN_ROWS, DN_W), lambda n: (n, 0)),
                   pl.BlockSpec((DN_CB, DN_H, HD, HD), lambda n: (n, 0, 0, 0))],
        out_shape=(jax.ShapeDtypeStruct((t, DN_W), BF16), jax.ShapeDtypeStruct((nc, DN_H, HD, HD), F32)),
        scratch_shapes=[pltpu.VMEM((_DN_ROWS + SUB, _QKV_W), F32), pltpu.VMEM((DN_H, HD, HD), F32)],
        compiler_params=_cp("arbitrary"), name="dn_fwd")(p, p, p, p, conv_w, par)


def dn_bwd(p, dc, states, conv_w, par):
    t = p.shape[0]
    ns = t // _DN_ROWS

    def body(x_ref, h_ref, z_ref, g_ref, cw_ref, par_ref, s_ref, dy_ref,
             dx_o, dz_o, dg_o, dcw_o, dpar_o, ext, dst, dpost, x2):
        i = pl.program_id(0)
        r = ns - 1 - i
        c = _dn_consts()

        @pl.when(i == 0)
        def _():
            dst[...] = jnp.zeros_like(dst)
            dcw_o[...] = jnp.zeros_like(dcw_o)
            dpar_o[...] = jnp.zeros_like(dpar_o)
            x2[_DN_ROWS:, :] = jnp.zeros((SUB, _QKV_W), F32)

        ext[0:SUB, :] = jnp.where(r > 0, h_ref[...], 0.0)
        ext[SUB:, :] = x_ref[...]
        cw = cw_ref[...]
        pre = _dn_conv(ext, cw)
        sg = _sigmoid(pre)
        qkv = pre * sg
        par = par_ref[...]
        gt = g_ref[...]
        beta_all, g_all, gc_all, ea, sxa = _dn_gates(gt, par[0:1, :], par[1:2, :], c)
        z, dy = z_ref[...], dy_ref[...]
        nw = par[2:3, :]
        dnw = jnp.zeros((1, LANE), F32)
        pairs = [(ci, h) for ci in range(DN_CB) for h in range(DN_H)]
        fa = _dn_heads_a(qkv, beta_all, gc_all, c)
        vnew, do = {}, {}
        fwd = _lockstep([_dn_head_b(fa[ci][h], s_ref[ci, h]) for ci, h in pairs])
        for (ci, h), (vn, o, _) in zip(pairs, fwd):
            vnew[ci, h] = vn
            do[ci, h], dz, dnw_h = _dn_post_bwd(o, _blk(z, ci, h), nw, _blk(dy, ci, h))
            dnw = dnw + dnw_h
            dz_o[ci * CH:(ci + 1) * CH, h * HD:(h + 1) * HD] = dz.astype(dz_o.dtype)
        ds = [dst[h] for h in range(DN_H)]
        seq = {}
        for ci in range(DN_CB - 1, -1, -1):
            res = _lockstep([_dn_head_bwd_b(fa[ci][h], vnew[ci, h], do[ci, h], ds[h], s_ref[ci, h])
                             for h in range(DN_H)])
            for h in range(DN_H):
                dvnew, ds[h], dkd, dcd = res[h]
                seq[ci, h] = (dvnew, dkd, dcd)
        for h in range(DN_H):
            dst[h] = ds[h]
        rest = _lockstep([_dn_head_bwd_c(
            fa[ci][h], vnew[ci, h], do[ci, h], *seq[ci, h], s_ref[ci, h], _blk(qkv, ci, h),
            _blk(qkv, ci, 2 * DN_H + h), beta_all[ci * CH:(ci + 1) * CH, h:h + 1], c) for ci, h in pairs])
        dbeta_rows, dgc_rows = [], []
        for ci in range(DN_CB):
            dbeta_c = jnp.zeros((CH, LANE), F32)
            dgc_c = jnp.zeros((CH, LANE), F32)
            for h in range(DN_H):
                dqh, dkh, dvh, dbeta, dgc = rest[ci * DN_H + h]
                dbeta_c = dbeta_c + jnp.where(c.lane == h, dbeta, 0.0)
                dgc_c = dgc_c + jnp.where(c.lane == DN_H + h, dgc, 0.0)
                rs = slice(ci * CH, (ci + 1) * CH)
                dpost[rs, h * HD:(h + 1) * HD] = dqh
                dpost[rs, (DN_H + h) * HD:(DN_H + h + 1) * HD] = dkh
                dpost[rs, (2 * DN_H + h) * HD:(2 * DN_H + h + 1) * HD] = dvh
            dbeta_rows.append(dbeta_c)
            dgc_rows.append(dgc_c)
        dbeta_all = jnp.concatenate(dbeta_rows, axis=0)
        dgc_all = jnp.concatenate(dgc_rows, axis=0)
        dg_all = _mm_hi(c.tril_blk, dgc_all, _TN)
        dpa = dg_all * (-ea) * sxa
        dpb = dbeta_all * beta_all * (1.0 - beta_all)
        is_b = c.lane < DN_H
        is_a = (c.lane >= DN_H) & (c.lane < 2 * DN_H)
        dg_o[...] = jnp.where(is_b, dpb, jnp.where(is_a, dpa, 0.0)).astype(dg_o.dtype)
        dpar_o[0:1, :] += jnp.where(is_a, jnp.sum(dg_all * g_all, axis=0, keepdims=True), 0.0)
        dpar_o[1:2, :] += jnp.where(is_a, jnp.sum(dpa, axis=0, keepdims=True), 0.0)
        dpar_o[2:3, :] += dnw
        dpre = dpost[...] * (sg * (1.0 + pre * (1.0 - sg)))
        for k in range(DN_K):
            dcw_o[k:k + 1, :] += jnp.sum(dpre * ext[pl.ds(5 + k, _DN_ROWS), :], axis=0, keepdims=True)
        x2[0:_DN_ROWS, :] = dpre
        dx_o[...] = (cw[3:4, :] * dpre + cw[2:3, :] * x2[pl.ds(1, _DN_ROWS), :]
                     + cw[1:2, :] * x2[pl.ds(2, _DN_ROWS), :]
                     + cw[0:1, :] * x2[pl.ds(3, _DN_ROWS), :]).astype(dx_o.dtype)
        x2[_DN_ROWS:, :] = dpre[0:SUB, :]

    rev = lambda n: ns - 1 - n
    return pl.pallas_call(
        body, grid=(ns,),
        in_specs=_dn_in_specs(rev) + [pl.BlockSpec((DN_CB, DN_H, HD, HD), lambda n: (rev(n), 0, 0, 0)),
                                      pl.BlockSpec((_DN_ROWS, DN_W), lambda n: (rev(n), 0))],
        out_specs=[pl.BlockSpec((_DN_ROWS, _QKV_W), lambda n: (rev(n), 0)),
                   pl.BlockSpec((_DN_ROWS, DN_W), lambda n: (rev(n), 0)),
                   pl.BlockSpec((_DN_ROWS, LANE), lambda n: (rev(n), 0)),
                   pl.BlockSpec((SUB, _QKV_W), lambda n: (0, 0)),
                   pl.BlockSpec((SUB, LANE), lambda n: (0, 0))],
        out_shape=(jax.ShapeDtypeStruct((t, _QKV_W), BF16), jax.ShapeDtypeStruct((t, DN_W), BF16),
                   jax.ShapeDtypeStruct((t, LANE), BF16), jax.ShapeDtypeStruct((SUB, _QKV_W), F32),
                   jax.ShapeDtypeStruct((SUB, LANE), F32)),
        scratch_shapes=[pltpu.VMEM((_DN_ROWS + SUB, _QKV_W), F32), pltpu.VMEM((DN_H, HD, HD), F32),
                        pltpu.VMEM((_DN_ROWS, _QKV_W), F32), pltpu.VMEM((_DN_ROWS + SUB, _QKV_W), F32)],
        compiler_params=_cp("arbitrary"), name="dn_bwd")(p, p, p, p, conv_w, par, states, dc)


_ANY = pl.BlockSpec(memory_space=pl.ANY)
_MESH = pl.DeviceIdType.MESH


def _me():
    return lax.axis_index("x"), lax.axis_index("y"), lax.axis_index("c")


def all_gather(x, name):
    def body(x_ref, out_ref, send_sems, recv_sems, local_sem):
        mx, my, mc = _me()
        me, sibling = (mx, my, mc), (mx, my, 1 - mc)
        chips = [(1 - mx, my), (mx, 1 - my), (1 - mx, 1 - my)]

        def slot(px, py, pc):
            return out_ref.at[4 * px + 2 * py + pc]

        def copy(k, block, to, src=None):
            return pltpu.make_async_remote_copy(
                src_ref=slot(*block) if src is None else src, dst_ref=slot(*block),
                send_sem=send_sems.at[k], recv_sem=recv_sems.at[k], device_id=to, device_id_type=_MESH)

        mine = pltpu.make_async_copy(x_ref, slot(*me), local_sem)
        mine.start()
        first = [copy(0, me, sibling, src=x_ref)]
        first += [copy(1 + j, me, (*chip, mc), src=x_ref) for j, chip in enumerate(chips)]
        for cp in first:
            cp.start()
        passed = [copy(4 + j, (*chip, mc), sibling) for j, chip in enumerate(chips)]
        for j, chip in enumerate(chips):
            copy(1 + j, (*chip, mc), me).wait_recv()
            passed[j].start()
        copy(0, sibling, me).wait_recv()
        for j, chip in enumerate(chips):
            copy(4 + j, (*chip, 1 - mc), me).wait_recv()
        for cp in first + passed:
            cp.wait_send()
        mine.wait()

    return pl.pallas_call(
        body, out_shape=jax.ShapeDtypeStruct((N_DEV,) + x.shape, x.dtype), in_specs=[_ANY], out_specs=_ANY,
        scratch_shapes=[pltpu.SemaphoreType.DMA((7,)), pltpu.SemaphoreType.DMA((7,)), pltpu.SemaphoreType.DMA],
        name=name)(x)


_HBM = pl.BlockSpec(memory_space=pltpu.HBM)
_SEM = pl.BlockSpec(memory_space=pltpu.SEMAPHORE)
_EFFECT = pltpu.SideEffectType.DATAFLOW_SIDE_EFFECTING
_TOKEN = jax.ShapeDtypeStruct((SUB, LANE), F32)


def _peers(mx, my, mc):
    for rel in range(1, N_DEV):
        yield (1 - mx if rel & 4 else mx, 1 - my if rel & 2 else my, 1 - mc if rel & 1 else mc)


def _in_hbm(a):
    return pltpu.with_memory_space_constraint(a, pltpu.HBM)


GATHER_SLOTS = (4, 3)


def gather_start(buf, phase, after, name):
    def body(buf_ref, after_ref, send_sem, recv_sem, thru, token):
        mx, my, mc = _me()
        sibling = (mx, my, 1 - mc)
        chips = [(1 - mx, my), (mx, 1 - my), (1 - mx, 1 - my)]
        if phase == 0:
            slot = buf_ref.at[4 * mx + 2 * my + mc]
            copies = [(slot, sibling)] + [(slot, (px, py, mc)) for px, py in chips]
        else:
            copies = [(buf_ref.at[4 * px + 2 * py + mc], sibling) for px, py in chips]
        for slot, peer in copies:
            pltpu.make_async_remote_copy(src_ref=slot, dst_ref=slot, send_sem=send_sem, recv_sem=recv_sem,
                                         device_id=peer, device_id_type=_MESH).start()
        token[...] = jnp.zeros_like(token)

    send_sem, recv_sem, thru, token = pl.pallas_call(
        body, name=name,
        out_shape=(pltpu.SemaphoreType.DMA(()), pltpu.SemaphoreType.DMA(()), pltpu.HBM(buf.shape, buf.dtype), _TOKEN),
        in_specs=[_HBM, _ANY], out_specs=(_SEM, _SEM, _HBM, pl.BlockSpec(memory_space=pltpu.VMEM)),
        input_output_aliases={0: 2},
        compiler_params=pltpu.CompilerParams(has_side_effects=_EFFECT))(_in_hbm(buf), after)
    return (send_sem, recv_sem), thru, token


def exchange_start(src, name):
    def body(src_ref, land_ref, send_sem, recv_sem, src_thru, land_thru, token):
        mx, my, mc = _me()
        me = 4 * mx + 2 * my + mc
        for px, py, pc in _peers(mx, my, mc):
            pltpu.make_async_remote_copy(
                src_ref=src_ref.at[4 * px + 2 * py + pc], dst_ref=land_ref.at[me], send_sem=send_sem,
                recv_sem=recv_sem, device_id=(px, py, pc), device_id_type=_MESH).start()
        token[...] = jnp.zeros_like(token)

    hbm = pltpu.HBM(src.shape, src.dtype)
    send_sem, recv_sem, src_thru, land_thru, token = pl.pallas_call(
        body, name=name,
        out_shape=(pltpu.SemaphoreType.DMA(()), pltpu.SemaphoreType.DMA(()), hbm, hbm, _TOKEN),
        in_specs=[_HBM, _HBM], out_specs=(_SEM, _SEM, _HBM, _HBM, pl.BlockSpec(memory_space=pltpu.VMEM)),
        input_output_aliases={0: 2, 1: 3},
        compiler_params=pltpu.CompilerParams(has_side_effects=_EFFECT))(
            _in_hbm(src), _in_hbm(lax.empty(src.shape, src.dtype)))
    return (send_sem, recv_sem), src_thru, land_thru, token


def transfer_wait(sems, bufs, after, name, slots=N_DEV - 1):
    n = len(bufs)

    def body(*refs):
        seven = refs[0].at[pl.ds(0, slots)]
        cp = pltpu.make_async_remote_copy(src_ref=seven, dst_ref=seven, send_sem=refs[n], recv_sem=refs[n + 1],
                                          device_id=_me(), device_id_type=_MESH)
        cp.wait_send()
        cp.wait_recv()

    outs = pl.pallas_call(
        body, name=name, out_shape=tuple(pltpu.HBM(b.shape, b.dtype) for b in bufs),
        in_specs=[_HBM] * n + [_SEM, _SEM, _ANY], out_specs=tuple([_HBM] * n),
        input_output_aliases={b: b for b in range(n)},
        compiler_params=pltpu.CompilerParams(has_side_effects=_EFFECT))(*bufs, sems[0], sems[1], after)
    return list(outs)


def sum_slabs(x, name, own=None, me=None):
    _, r, c = x.shape
    tr = _pick(r, max(SUB, (1 << 19) // c // SUB * SUB), SUB)
    out_shape = jax.ShapeDtypeStruct((r, c), F32)
    if own is None:
        def body(x_ref, o_ref):
            acc = x_ref[0].astype(F32)
            for s in range(1, N_DEV):
                acc = acc + x_ref[s].astype(F32)
            o_ref[...] = acc

        return pl.pallas_call(
            body, grid=(r // tr,), in_specs=[pl.BlockSpec((N_DEV, tr, c), lambda i: (0, i, 0))],
            out_specs=pl.BlockSpec((tr, c), lambda i: (i, 0)), out_shape=out_shape,
            compiler_params=_cp("parallel"), name=name)(x)

    def body_own(me_ref, x_ref, own_ref, o_ref):
        acc = None
        for s in range(N_DEV):
            val = jnp.where(me_ref[0] == s, own_ref[...], x_ref[s]).astype(F32)
            acc = val if acc is None else acc + val
        o_ref[...] = acc

    return pl.pallas_call(
        body_own, out_shape=out_shape, name=name, compiler_params=_cp("parallel"),
        grid_spec=pltpu.PrefetchScalarGridSpec(
            num_scalar_prefetch=1, grid=(r // tr,),
            in_specs=[pl.BlockSpec((N_DEV, tr, c), lambda i, me_ref: (0, i, 0)),
                      pl.BlockSpec((None, tr, c), lambda i, me_ref: (me_ref[0], i, 0))],
            out_specs=pl.BlockSpec((tr, c), lambda i, me_ref: (i, 0))))(me, x, own)


def adamw(w, g, m, v, name):
    r, c = w.shape
    tr = _pick(r, max(SUB, (1 << 18) // c // SUB * SUB), SUB)
    c1 = 1.0 / (1.0 - ADAM_B1 ** ADAM_STEP)
    c2 = 1.0 / (1.0 - ADAM_B2 ** ADAM_STEP)

    def body(w_ref, g_ref, m_ref, v_ref, d_o, m_o, v_o):
        gg = g_ref[...]
        mn = ADAM_B1 * m_ref[...] + (1.0 - ADAM_B1) * gg
        vn = ADAM_B2 * v_ref[...] + (1.0 - ADAM_B2) * (gg * gg)
        m_o[...] = mn
        v_o[...] = vn
        d_o[...] = -ADAM_LR * ((mn * c1) / (jnp.sqrt(vn * c2) + ADAM_EPS) + ADAM_WD * w_ref[...])

    spec = pl.BlockSpec((tr, c), lambda i: (i, 0))
    sds = jax.ShapeDtypeStruct((r, c), F32)
    return pl.pallas_call(body, grid=(r // tr,), in_specs=[spec] * 4, out_specs=[spec] * 3, out_shape=(sds,) * 3,
                          compiler_params=_cp("parallel"), name=name)(w, g, m, v)


def _adam_update(w, gg, m, v):
    mn = ADAM_B1 * m + (1.0 - ADAM_B1) * gg
    vn = ADAM_B2 * v + (1.0 - ADAM_B2) * (gg * gg)
    c1 = 1.0 / (1.0 - ADAM_B1 ** ADAM_STEP)
    c2 = 1.0 / (1.0 - ADAM_B2 ** ADAM_STEP)
    return -ADAM_LR * ((mn * c1) / (jnp.sqrt(vn * c2) + ADAM_EPS) + ADAM_WD * w), mn, vn


def adamw_layer(layer, w, m, v, prev, name, g=None, land=None, own=None, me=None):
    nl, r, c = w.shape
    tr = _pick(r, max(SUB, (1 << 17) // c // SUB * SUB), SUB)
    from_slabs = g is None
    if prev is None:
        prev = tuple(lax.empty((nl, r, c), F32) for _ in range(4))

    def body(*refs):
        if from_slabs:
            me_ref, land_ref, own_ref, w_ref, m_ref, v_ref = refs[:6]
            gg = None
            for s in range(N_DEV):
                val = jnp.where(me_ref[0] == s, own_ref[...], land_ref[s]).astype(F32)
                gg = val if gg is None else gg + val
        else:
            me_ref, g_ref, w_ref, m_ref, v_ref = refs[:5]
            gg = g_ref[...]
        g_o, d_o, m_o, v_o = refs[-4:]
        g_o[...] = gg
        d_o[...], m_o[...], v_o[...] = _adam_update(w_ref[...], gg, m_ref[...], v_ref[...])

    lay = pl.BlockSpec((None, tr, c), lambda i, me_ref: (layer, i, 0))
    if from_slabs:
        grad_specs = [pl.BlockSpec((N_DEV, tr, c), lambda i, me_ref: (0, i, 0)),
                      pl.BlockSpec((None, tr, c), lambda i, me_ref: (me_ref[0], i, 0))]
        grad_args = [land, own]
    else:
        grad_specs = [pl.BlockSpec((tr, c), lambda i, me_ref: (i, 0))]
        grad_args = [g]
        me = jnp.zeros((1,), jnp.int32)
    n_in = 1 + len(grad_args) + 3
    return pl.pallas_call(
        body, out_shape=tuple(jax.ShapeDtypeStruct((nl, r, c), F32) for _ in range(4)), name=name,
        input_output_aliases={n_in + k: k for k in range(4)}, compiler_params=_cp("parallel"),
        grid_spec=pltpu.PrefetchScalarGridSpec(
            num_scalar_prefetch=1, grid=(r // tr,), in_specs=grad_specs + [lay] * 3 + [_ANY] * 4,
            out_specs=[lay] * 4))(me, *grad_args, w, m, v, *prev)


def _pack(parts):
    flat = jnp.concatenate([a.reshape(-1).astype(F32) for a in parts])
    n = flat.shape[0]
    npad = -n % (PACK_ROWS * LANE)
    return jnp.pad(flat, (0, npad)).reshape(-1, LANE)


def _unpack(buf, shapes, lead=()):
    flat = buf.reshape(lead + (-1,))
    out, off = [], 0
    for s in shapes:
        n = math.prod(s)
        out.append(flat[..., off:off + n].reshape(lead + tuple(s)))
        off += n
    return out


ROPE_THETA = 10000.0

_SMALL = ("norm_mix_pre", "dn_conv_w", "dn_a_log", "dn_dt_bias", "dn_norm_w", "pool_w", "pool_scale", "swa_sinks",
          "norm_mix_post", "norm_ffn_pre", "ffn_conv_w", "ffn_conv_b", "norm_ffn_post")
_BIG = ("w_in", "w_out", "ffn_w_up", "ffn_w_down")
_ORDER = ("norm_mix_pre", "w_in", "dn_conv_w", "dn_a_log", "dn_dt_bias", "dn_norm_w", "pool_w", "pool_scale",
          "swa_sinks", "w_out", "norm_mix_post", "norm_ffn_pre", "ffn_w_up", "ffn_conv_w", "ffn_conv_b",
          "ffn_w_down", "norm_ffn_post")


def _step(x, positions, loss_target, w, m, v):
    nl = w["w_in"].shape[0]
    t, d = x.shape[1], x.shape[2]
    nb = w["ffn_w_up"].shape[2]
    f = nb * N_DEV // 2
    me = 4 * lax.axis_index("x") + 2 * lax.axis_index("y") + lax.axis_index("c")
    x_in, tgt = x[0], loss_target[0]

    inv_freq = 1.0 / (ROPE_THETA ** (jnp.arange(0, HD, 2, dtype=F32) / HD))
    ang = positions[0].astype(F32)[:, None] * inv_freq
    cos, sin = jnp.cos(ang), jnp.sin(ang)
    cos2 = jnp.concatenate([cos, cos], axis=1)
    sin2 = jnp.concatenate([-sin, sin], axis=1)

    conv_shapes = [w["dn_conv_w"].shape, w["ffn_conv_w"].shape]
    gathered_conv = all_gather(_pack([w["dn_conv_w"], w["ffn_conv_w"]]), "ag_conv")
    dn_cw_g, ffn_cw_g = _unpack(gathered_conv, conv_shapes, lead=(N_DEV,))
    dn_cw = jnp.moveaxis(dn_cw_g, 0, 2).reshape(nl, DN_K, _QKV_W)
    ffn_cw = jnp.moveaxis(ffn_cw_g, 0, 1).reshape(nl, 2, N_DEV // 2, 3, nb)
    ffn_cb = w["ffn_conv_b"].reshape(nl, 2, N_DEV // 2, 1, nb)

    def lane_row(vec, off):
        return jnp.zeros((LANE,), F32).at[off:off + vec.shape[0]].set(vec)

    dn_par = jnp.stack([
        jnp.zeros((SUB, LANE), F32).at[0].set(lane_row(w["dn_a_log"][l], DN_H))
        .at[1].set(lane_row(w["dn_dt_bias"][l], DN_H)).at[2].set(w["dn_norm_w"][l]) for l in range(nl)])
    sinks = jnp.stack([lane_row(w["swa_sinks"][l], 0)[None, :] for l in range(nl)])

    def place(shard):
        return lax.dynamic_update_slice(lax.empty((N_DEV,) + shard.shape, shard.dtype), shard[None], (me, 0, 0))

    kinds = ("w_in", "w_out", "ffn_w_up", "ffn_w_down")
    flight = {}
    tag = lambda i: f"{kinds[i % 4]}_{i // 4}"

    def start_first(i, after):
        if i >= 4 * nl:
            return jnp.zeros(_TOKEN.shape, F32)
        l, k = divmod(i, 4)
        shard = _align_in(w[kinds[k]][l]) if k == 0 else w[kinds[k]][l]
        sems, buf, token = gather_start(place(shard.astype(BF16)), 0, after, f"ag_start_{tag(i)}")
        flight[i] = (sems, buf)
        return token

    def start_second(i, after):
        if i >= 4 * nl:
            return jnp.zeros(_TOKEN.shape, F32)
        arrived = transfer_wait(flight[i][0], [flight[i][1]], after, f"ag_wait_{tag(i)}", GATHER_SLOTS[0])[0]
        sems, buf, token = gather_start(arrived, 1, after, f"ag_pass_{tag(i)}")
        flight[i] = (sems, buf)
        return token

    def gathered(l, k, after):
        i = 4 * l + k
        late = start_second(1, after) if i == 1 else None
        got = transfer_wait(flight[i][0], [flight[i][1]], after if late is None else late, f"ag_done_{tag(i)}",
                            GATHER_SLOTS[1])[0]
        first = start_first(i + 3, got)
        if i == 0:
            return got, first[0, 0]
        return got, (start_second(i + 1, first) + first)[0, 0]

    win, wout, wup, wdown = [None] * nl, [None] * nl, [None] * nl, [None] * nl
    row = lambda a, l: a[l][None, :]
    g1, g2, g3, g4 = w["norm_mix_pre"], w["norm_mix_post"], w["norm_ffn_pre"], w["norm_ffn_post"]

    saved = []
    xl = x_in
    passed = start_second(0, start_first(0, gathered_conv))
    h1 = norm_first(xl, row(g1, 0) + (passed + start_first(1, passed) + start_first(2, passed))[0, 0])
    for l in range(nl):
        buf, tk = gathered(l, 0, h1)
        win[l] = buf.reshape(d, PW)
        p = mm_nn(h1, win[l], F32, "mm_in")
        y_dn, states = dn_fwd(p, dn_cw[l], dn_par[l] + tk)
        y_pool = pool_fwd(p, w["pool_w"][l], row(w["pool_scale"], l))
        y_swa = swa_fwd(p, cos2, sin2, sinks[l])
        c = jnp.concatenate([y_dn, y_swa, y_pool], axis=1)
        buf, tk = gathered(l, 1, c)
        wout[l] = _perm_mix_rows(buf.reshape(MIX_W, d))
        mix = mm_nn(c, wout[l], F32, "mm_out")
        x1, h2 = post_pre(xl, mix, row(g2, l) + tk, row(g3, l))
        wup[l], tk = gathered(l, 2, h2)
        u0 = mm_up(h2, wup[l], "mm_up")
        act = glu_fwd(u0, ffn_cw[l], ffn_cb[l] + tk)
        buf, tk = gathered(l, 3, act)
        wdown[l] = buf.reshape(f, d)
        fo = mm_nn(act, wdown[l], F32, "mm_down")
        saved.append(dict(x=xl, h1=h1, p=p, states=states, c=c, mix=mix, x1=x1, h2=h2, u0=u0, act=act, f=fo))
        if l < nl - 1:
            xl, h1 = post_pre(x1, fo, row(g4, l) + tk, row(g1, l + 1))
        else:
            dx, loss_part = post_loss(x1, fo, row(g4, l) + tk, tgt)

    small_g = [dict() for _ in range(nl)]
    pending = {name: [None] * nl for name in _BIG}

    def exchange(name, l, dw):
        sems, src, land, token = exchange_start(dw, f"xch_start_{name}_{l}")
        pending[name][l] = (sems, src, land)
        return token[0, 0]

    df, small_g[nl - 1]["norm_ffn_post"] = bwd_norms(dx, post=(saved[-1]["f"], row(g4, nl - 1)))
    for l in range(nl - 1, -1, -1):
        s, sg = saved[l], small_g[l]
        dact = mm_nt(df, wdown[l], F32, "mm_down_d")
        tk = exchange("ffn_w_down", l, mm_tn(s["act"], df, BF16, "mm_down_w").reshape(N_DEV, f // N_DEV, d))
        du0, dcw = glu_bwd(dact, s["u0"], ffn_cw[l], ffn_cb[l])
        sg["ffn_conv"] = dcw
        dh2 = mm_up_dgrad(du0, wup[l], "mm_up_d")
        tk = tk + exchange("ffn_w_up", l, mm_up_wgrad(s["h2"], du0, "mm_up_w"))
        dx1, sg["norm_ffn_pre"], dmix, sg["norm_mix_post"] = bwd_norms(
            dx, pre=(dh2, s["x1"], row(g3, l) + tk), post=(s["mix"], row(g2, l)))
        dc = mm_nt(dmix, wout[l], F32, "mm_out_d")
        tk = exchange("w_out", l, _unperm_mix_rows(mm_tn(s["c"], dmix, BF16, "mm_out_w"))
                      .reshape(N_DEV, MIX_W // N_DEV, d))
        dqkv, dz, dgate, sg["dn_conv_w"], sg["dn_par"] = dn_bwd(s["p"], dc, s["states"], dn_cw[l], dn_par[l])
        dpool, sg["pool_w"], sg["pool_scale"] = pool_bwd(s["p"], dc, w["pool_w"][l], row(w["pool_scale"], l))
        dsq, dsk, dsv, sg["swa_sinks"] = swa_bwd(s["p"], dc, cos2, sin2, sinks[l])
        dp = jnp.concatenate([dqkv, dz, dsq, dsk, dsv, dpool, dgate], axis=1)
        dh1 = mm_nt(dp, win[l], F32, "mm_in_d")
        tk = tk + exchange("w_in", l, mm_tn(s["h1"], dp, BF16, "mm_in_w").reshape(N_DEV, d // N_DEV, PW))
        if l > 0:
            dx, sg["norm_mix_pre"], df, small_g[l - 1]["norm_ffn_post"] = bwd_norms(
                dx1, pre=(dh1, s["x"], row(g1, l) + tk), post=(saved[l - 1]["f"], row(g4, l - 1)))
        else:
            grad_x, sg["norm_mix_pre"] = bwd_norms(dx1, pre=(dh1, s["x"], row(g1, 0) + tk))

    me_arr = jnp.reshape(me, (1,)).astype(jnp.int32)
    big = {name: None for name in _BIG}
    w_in_sums = [None] * nl

    def finish(name, l, after):
        sems, src, land = pending[name][l]
        src, land = transfer_wait(sems, [src, land], after, f"xch_wait_{name}_{l}")
        if name == "w_in":
            w_in_sums[l] = sum_slabs(land, "sum_w_in", own=src, me=me_arr)
            return w_in_sums[l]
        big[name] = adamw_layer(l, w[name], m[name], v[name], big[name], "adamw_" + name, land=land, own=src,
                                me=me_arr)
        return big[name][1]

    after = grad_x
    for l in range(nl - 1, 0, -1):
        for name in ("ffn_w_down", "ffn_w_up", "w_out", "w_in"):
            after = finish(name, l, after)

    keys = ("norm_mix_pre", "norm_mix_post", "norm_ffn_pre", "norm_ffn_post", "dn_conv_w", "dn_par", "pool_w",
            "pool_scale", "swa_sinks", "ffn_conv")
    grads = {}
    parts = [small_g[l][k] for l in range(nl) for k in keys] + [loss_part]
    shapes = [a.shape for a in parts]
    ordered = parts + [after[0:SUB, 0:LANE]]
    summed = sum_slabs(all_gather(_pack(ordered), "ag_small"), "sum_small")
    vals = _unpack(summed, shapes)
    loss = vals[-1][0, 0]
    sm = [dict(zip(keys, vals[l * len(keys):(l + 1) * len(keys)])) for l in range(nl)]
    st = lambda fn: jnp.stack([fn(sm[l]) for l in range(nl)])
    for k in ("norm_mix_pre", "norm_mix_post", "norm_ffn_pre", "norm_ffn_post"):
        grads[k] = st(lambda q: q[k][0])
    grads["dn_conv_w"] = lax.dynamic_slice_in_dim(st(lambda q: q["dn_conv_w"][0:DN_K]), me * (_QKV_W // N_DEV),
                                                  _QKV_W // N_DEV, axis=2)
    grads["dn_a_log"] = st(lambda q: q["dn_par"][0, DN_H:2 * DN_H])
    grads["dn_dt_bias"] = st(lambda q: q["dn_par"][1, DN_H:2 * DN_H])
    grads["dn_norm_w"] = st(lambda q: q["dn_par"][2])
    grads["pool_w"] = st(lambda q: q["pool_w"])
    grads["pool_scale"] = st(lambda q: q["pool_scale"][0])
    grads["swa_sinks"] = st(lambda q: q["swa_sinks"][0, 0:SWA_H])
    conv_all = st(lambda q: q["ffn_conv"].reshape(N_DEV, SUB, nb))
    grads["ffn_conv_w"] = lax.dynamic_index_in_dim(conv_all, me, axis=1, keepdims=False)[:, 0:3, :]
    grads["ffn_conv_b"] = conv_all[:, :, 3, :].reshape(nl, 2 * f)

    delta, new_m, new_v = {}, {}, {}
    shapes = [w[k].shape for k in _SMALL]
    pk = lambda tree: _pack([tree[k] for k in _SMALL])
    outs = adamw(pk(w), pk(grads), pk(m), pk(v), "adamw_small")
    for tree, buf in zip((delta, new_m, new_v), outs):
        for k, a in zip(_SMALL, _unpack(buf, shapes)):
            tree[k] = a
    after = outs[0]
    for name in ("ffn_w_down", "ffn_w_up", "w_out"):
        after = finish(name, 0, after)
        grads[name], delta[name], new_m[name], new_v[name] = big[name]
    finish("w_in", 0, after)
    grads["w_in"] = _unalign_in(jnp.stack(w_in_sums))
    flat = lambda a: a.reshape(-1, IN_W)
    delta["w_in"], new_m["w_in"], new_v["w_in"] = (
        o.reshape(w["w_in"].shape) for o in adamw(flat(w["w_in"]), flat(grads["w_in"]), flat(m["w_in"]),
                                                  flat(v["w_in"]), "adamw_w_in"))

    return (loss, grad_x[None], *[grads[k] for k in _ORDER], *[delta[k] for k in _ORDER],
            *[new_m[k] for k in _ORDER], *[new_v[k] for k in _ORDER])


def kernel(x, positions, norm_mix_pre, w_in, dn_conv_w, dn_a_log, dn_dt_bias, dn_norm_w, pool_w, pool_scale, swa_sinks, w_out, norm_mix_post, norm_ffn_pre, ffn_w_up, ffn_conv_w, ffn_conv_b, ffn_w_down, norm_ffn_post, loss_target, m_norm_mix_pre, m_w_in, m_dn_conv_w, m_dn_a_log, m_dn_dt_bias, m_dn_norm_w, m_pool_w, m_pool_scale, m_swa_sinks, m_w_out, m_norm_mix_post, m_norm_ffn_pre, m_ffn_w_up, m_ffn_conv_w, m_ffn_conv_b, m_ffn_w_down, m_norm_ffn_post, v_norm_mix_pre, v_w_in, v_dn_conv_w, v_dn_a_log, v_dn_dt_bias, v_dn_norm_w, v_pool_w, v_pool_scale, v_swa_sinks, v_w_out, v_norm_mix_post, v_norm_ffn_pre, v_ffn_w_up, v_ffn_conv_w, v_ffn_conv_b, v_ffn_w_down, v_norm_ffn_post):
    args = locals()
    w = {k: args[k] for k in _ORDER}
    m = {k: args["m_" + k] for k in _ORDER}
    v = {k: args["v_" + k] for k in _ORDER}
    return _step(x, positions, loss_target, w, m, v)
```
